```python
import math
import jax, jax.numpy as jnp
from jax import lax
import numpy as np

D_MODEL = 1024
BATCH = 2
SEQ = 8192
DEPTH = 1
DEC_BATCH = 128
DEC_SEQ = 4
PAST_LEN = 8192
PAGE_SIZE = 128

HEAD_DIM = 64
ATTN_WIDTH = D_MODEL // 2
N_Q_HEADS = ATTN_WIDTH // HEAD_DIM
N_KV_HEADS = N_Q_HEADS // 4
Q_PER_KV = N_Q_HEADS // N_KV_HEADS
KV_WIDTH = N_KV_HEADS * HEAD_DIM
WINDOW = 128
BLOCK = 128
N_BUCKETS = 32
MAX_DISTANCE = 128
SSM_WIDTH = D_MODEL - ATTN_WIDTH
SSM_GROUP = 16
N_SSM_GROUPS = SSM_WIDTH // SSM_GROUP
SSM_STATE = 64
D_FF = 4 * D_MODEL
D_IN_PROJ = ATTN_WIDTH + 2 * KV_WIDTH + SSM_WIDTH
DT_MIN = 0.001
DT_MAX = 0.1
EPS = 1e-6
NEG = -1e30

kernel_name = "hymba_swa_sink_s5_decode_step"


def rmsnorm(x, g):
    xf = x.astype(jnp.float32)
    y = xf * lax.rsqrt(jnp.mean(xf * xf, axis=-1, keepdims=True) + EPS)
    return (y * g.astype(jnp.float32)).astype(x.dtype)


def rel_bucket(dist):
    n = jnp.maximum(dist, 0)
    max_exact = N_BUCKETS // 2
    nf = jnp.maximum(n, max_exact).astype(jnp.float32)
    large = max_exact + (jnp.log(nf / max_exact) / math.log(MAX_DISTANCE / max_exact)
                         * (N_BUCKETS - max_exact)).astype(jnp.int32)
    large = jnp.minimum(large, N_BUCKETS - 1)
    return jnp.where(n < max_exact, n, large)


def band_attention(q, k, v, dist, valid, sinks, rel_table):
    bias = rel_table[rel_bucket(dist)]
    bias = jnp.transpose(bias, (2, 0, 1)).reshape(N_KV_HEADS, Q_PER_KV, *dist.shape).astype(jnp.float32)
    s = jnp.einsum('bnqgrd,bnkgd->bngrqk', q, k).astype(jnp.float32) * (HEAD_DIM ** -0.5) + bias
    s = jnp.where(valid[:, None, None], s, NEG)
    sink = sinks.astype(jnp.float32).reshape(N_KV_HEADS, Q_PER_KV, 1)
    m = jnp.maximum(jnp.max(s, axis=-1), sink)
    p = jnp.exp(s - m[..., None])
    denom = jnp.sum(p, axis=-1) + jnp.exp(sink - m)
    p = (p / denom[..., None]).astype(v.dtype)
    return jnp.einsum('bngrqk,bnkgd->bnqgrd', p, v)


def attn_prompt(q, k, v, sinks, rel_table):
    b, L = q.shape[:2]
    nb = L // BLOCK
    qb = q.reshape(b, nb, BLOCK, N_KV_HEADS, Q_PER_KV, HEAD_DIM)
    kb = k.reshape(b, nb, BLOCK, N_KV_HEADS, HEAD_DIM)
    vb = v.reshape(b, nb, BLOCK, N_KV_HEADS, HEAD_DIM)

    def band(t):
        prev = jnp.concatenate([jnp.zeros_like(t[:, :1]), t[:, :-1]], axis=1)
        return jnp.concatenate([prev, t], axis=2)

    i = jnp.arange(BLOCK)[:, None]
    j = jnp.arange(2 * BLOCK)[None, :]
    dist = i + BLOCK - j
    in_band = (dist >= 0) & (dist < WINDOW)
    first = (jnp.arange(nb) == 0)[:, None, None]
    valid = in_band[None] & ~(first & (j < BLOCK)[None])
    o = band_attention(qb, band(kb), band(vb), dist, valid, sinks, rel_table)
    return o.reshape(b, L, ATTN_WIDTH)


def attn_sample(q, k_new, v_new, k_cache, v_cache, sinks, rel_table):
    b, T = q.shape[:2]
    keys = jnp.concatenate([k_cache, k_new], axis=1)
    vals = jnp.concatenate([v_cache, v_new], axis=1)
    i = jnp.arange(T)[:, None]
    key_pos = jnp.arange(WINDOW + T)[None, :] - WINDOW
    dist = i - key_pos
    valid = ((dist >= 0) & (dist < WINDOW))[None]
    o = band_attention(q[:, None], keys[:, None], vals[:, None], dist, valid, sinks, rel_table)
    return o.reshape(b, T, ATTN_WIDTH), keys[:, T:], vals[:, T:]


def s5_mixer(u, h0, a_re, a_im, log_step, b_re, b_im, c_re, c_im, d, w_glu, b_glu):
    f32 = jnp.float32
    b, L = u.shape[:2]
    uf = u.astype(f32).reshape(b, L, N_SSM_GROUPS, SSM_GROUP)
    lam = lax.complex(a_re.astype(f32), a_im.astype(f32))
    step = jnp.exp(log_step.astype(f32))[:, None]
    lam_bar = jnp.exp(lam * step)
    b_bar = ((lam_bar - 1.0) / lam)[..., None] * lax.complex(b_re.astype(f32), b_im.astype(f32))
    bu = jnp.einsum('blgc,gpc->blgp', uf.astype(jnp.complex64), b_bar)
    if h0 is not None:
        bu = bu.at[:, 0].add(lam_bar * h0)
    a = jnp.broadcast_to(lam_bar, bu.shape)

    def combine(e1, e2):
        a1, b1 = e1
        a2, b2 = e2
        return a1 * a2, a2 * b1 + b2

    _, h = lax.associative_scan(combine, (a, bu), axis=1)
    c = lax.complex(c_re.astype(f32), c_im.astype(f32))
    y = jnp.real(jnp.einsum('blgp,gcp->blgc', h, c)) + d.astype(f32) * uf
    y = y.reshape(b, L, SSM_WIDTH)
    g = jax.nn.gelu(y)
    out = g * jax.nn.sigmoid(g @ w_glu.astype(f32) + b_glu.astype(f32))
    return out.astype(u.dtype), h[:, -1]


def layer_forward(x, lw, rel_table, kv_cache, ssm_h0):
    b, L, _ = x.shape
    h = rmsnorm(x, lw['norm_mix'])
    proj = h @ lw['w_in']
    q, k, v, u = jnp.split(proj, [ATTN_WIDTH, ATTN_WIDTH + KV_WIDTH, ATTN_WIDTH + 2 * KV_WIDTH], axis=-1)
    q = q.reshape(b, L, N_KV_HEADS, Q_PER_KV, HEAD_DIM)
    k = k.reshape(b, L, N_KV_HEADS, HEAD_DIM)
    v = v.reshape(b, L, N_KV_HEADS, HEAD_DIM)
    if kv_cache is None:
        o_attn = attn_prompt(q, k, v, lw['sinks'], rel_table)
        new_k, new_v = k[:, L - WINDOW:], v[:, L - WINDOW:]
    else:
        o_attn, new_k, new_v = attn_sample(q, k, v, kv_cache[0], kv_cache[1], lw['sinks'], rel_table)
    o_ssm, h_last = s5_mixer(u, ssm_h0, lw['a_re'], lw['a_im'], lw['log_step'], lw['b_re'], lw['b_im'],
                             lw['c_re'], lw['c_im'], lw['d'], lw['w_glu'], lw['b_glu'])
    merged = jnp.concatenate([rmsnorm(o_attn, lw['norm_attn_out']), rmsnorm(o_ssm, lw['norm_ssm_out'])], axis=-1)
    x = x + merged @ lw['w_out']
    hm = rmsnorm(x, lw['norm_mlp'])
    x = x + jnp.square(jax.nn.relu(hm @ lw['w_up'])) @ lw['w_down']
    return x, new_k, new_v, jnp.real(h_last), jnp.imag(h_last)


def setup_inputs(seed: int = 0) -> dict:
    key = jax.random.key(seed)
    ks = jax.random.split(key, 32)
    f32 = jnp.float32
    nrm = lambda k, shape, s: jax.random.normal(k, shape, f32) * s
    gain = lambda k, shape: 1.0 + 0.02 * jax.random.normal(k, shape, f32)
    a_im_base = math.pi * jnp.arange(SSM_STATE, dtype=f32)
    return {
        'x_prompt': nrm(ks[0], (BATCH, SEQ, D_MODEL), 1.0),
        'x_sample': nrm(ks[1], (DEC_BATCH, DEC_SEQ, D_MODEL), 1.0),
        'cache_k': nrm(ks[2], (DEPTH, DEC_BATCH, WINDOW, N_KV_HEADS, HEAD_DIM), 1.0),
        'cache_v': nrm(ks[3], (DEPTH, DEC_BATCH, WINDOW, N_KV_HEADS, HEAD_DIM), 1.0),
        'state_ssm_re': nrm(ks[4], (DEPTH, DEC_BATCH, N_SSM_GROUPS, SSM_STATE), 0.5),
        'state_ssm_im': nrm(ks[5], (DEPTH, DEC_BATCH, N_SSM_GROUPS, SSM_STATE), 0.5),
        'rel_bias': nrm(ks[6], (N_BUCKETS, N_Q_HEADS), 0.2),
        'norm_mix': gain(ks[7], (DEPTH, D_MODEL)),
        'w_in': nrm(ks[8], (DEPTH, D_MODEL, D_IN_PROJ), D_MODEL ** -0.5),
        'attn_sinks': nrm(ks[9], (DEPTH, N_Q_HEADS), 0.5),
        'ssm_a_re': -0.5 + nrm(ks[10], (DEPTH, N_SSM_GROUPS, SSM_STATE), 0.01),
        'ssm_a_im': a_im_base + nrm(ks[11], (DEPTH, N_SSM_GROUPS, SSM_STATE), 0.01),
        'ssm_log_step': jax.random.uniform(ks[12], (DEPTH, N_SSM_GROUPS), f32,
                                           minval=math.log(DT_MIN), maxval=math.log(DT_MAX)),
        'ssm_b_re': nrm(ks[13], (DEPTH, N_SSM_GROUPS, SSM_STATE, SSM_GROUP), (2 * SSM_GROUP) ** -0.5),
        'ssm_b_im': nrm(ks[14], (DEPTH, N_SSM_GROUPS, SSM_STATE, SSM_GROUP), (2 * SSM_GROUP) ** -0.5),
        'ssm_c_re': nrm(ks[15], (DEPTH, N_SSM_GROUPS, SSM_GROUP, SSM_STATE), SSM_STATE ** -0.5),
        'ssm_c_im': nrm(ks[16], (DEPTH, N_SSM_GROUPS, SSM_GROUP, SSM_STATE), SSM_STATE ** -0.5),
        'ssm_d': nrm(ks[17], (DEPTH, N_SSM_GROUPS, SSM_GROUP), 1.0),
        'w_glu': nrm(ks[18], (DEPTH, SSM_WIDTH, SSM_WIDTH), SSM_WIDTH ** -0.5),
        'b_glu': nrm(ks[19], (DEPTH, SSM_WIDTH), 0.02),
        'norm_attn_out': gain(ks[20], (DEPTH, ATTN_WIDTH)),
        'norm_ssm_out': gain(ks[21], (DEPTH, SSM_WIDTH)),
        'w_out': nrm(ks[22], (DEPTH, D_MODEL, D_MODEL), D_MODEL ** -0.5),
        'norm_mlp': gain(ks[23], (DEPTH, D_MODEL)),
        'w_up': nrm(ks[24], (DEPTH, D_MODEL, D_FF), D_MODEL ** -0.5),
        'w_down': nrm(ks[25], (DEPTH, D_FF, D_MODEL), D_FF ** -0.5),
        'norm_final': gain(ks[26], (D_MODEL,)),
    }


def reference(x_prompt, x_sample, cache_k, cache_v, state_ssm_re, state_ssm_im, rel_bias,
              norm_mix, w_in, attn_sinks, ssm_a_re, ssm_a_im, ssm_log_step, ssm_b_re, ssm_b_im,
              ssm_c_re, ssm_c_im, ssm_d, w_glu, b_glu, norm_attn_out, norm_ssm_out, w_out,
              norm_mlp, w_up, w_down, norm_final):
    xp, xs = x_prompt, x_sample
    kp, vp, rep, imp = [], [], [], []
    ksm, vsm, res, ims = [], [], [], []
    for l in range(DEPTH):
        lw = {'norm_mix': norm_mix[l], 'w_in': w_in[l], 'sinks': attn_sinks[l],
              'a_re': ssm_a_re[l], 'a_im': ssm_a_im[l], 'log_step': ssm_log_step[l],
              'b_re': ssm_b_re[l], 'b_im': ssm_b_im[l], 'c_re': ssm_c_re[l], 'c_im': ssm_c_im[l],
              'd': ssm_d[l], 'w_glu': w_glu[l], 'b_glu': b_glu[l],
              'norm_attn_out': norm_attn_out[l], 'norm_ssm_out': norm_ssm_out[l], 'w_out': w_out[l],
              'norm_mlp': norm_mlp[l], 'w_up': w_up[l], 'w_down': w_down[l]}
        xp, k1, v1, r1, i1 = layer_forward(xp, lw, rel_bias, None, None)
        h0 = lax.complex(state_ssm_re[l].astype(jnp.float32), state_ssm_im[l].astype(jnp.float32))
        xs, k2, v2, r2, i2 = layer_forward(xs, lw, rel_bias, (cache_k[l], cache_v[l]), h0)
        kp.append(k1); vp.append(v1); rep.append(r1); imp.append(i1)
        ksm.append(k2); vsm.append(v2); res.append(r2); ims.append(i2)
    y_prompt = rmsnorm(xp, norm_final)
    y_sample = rmsnorm(xs, norm_final)
    return (y_prompt, y_sample,
            jnp.stack(kp), jnp.stack(vp), jnp.stack(rep), jnp.stack(imp),
            jnp.stack(ksm), jnp.stack(vsm), jnp.stack(res), jnp.stack(ims))
```

```python
import functools
import math

import numpy as np
import jax
import jax.numpy as jnp
from jax import lax
from jax.experimental import pallas as pl
from jax.experimental.pallas import tpu as pltpu

D_MODEL = 1024
HEAD_DIM = 64
ATTN_WIDTH = 512
N_Q_HEADS = 8
N_KV_HEADS = 2
Q_PER_KV = 4
KV_WIDTH = 128
WINDOW = 128
BLOCK = 128
N_BUCKETS = 32
MAX_DISTANCE = 128
SSM_WIDTH = 512
SSM_GROUP = 16
N_SSM_GROUPS = 32
SSM_STATE = 64
STATE_LANES = N_SSM_GROUPS * SSM_STATE
D_FF = 4096
D_IN_PROJ = 1280
EPS = 1e-6
NEG = -1e30

LANES = 128
SUBLANES = 8
MXU_DIM = 256
VMEM_LIMIT = 56 * 1024 * 1024

F32 = jnp.float32
BF16 = jnp.bfloat16

TM_A = 512
TM_C = 512
TM_B = 512
CHUNK = TM_B // SUBLANES
PITCH = CHUNK + SUBLANES
S_SEQ = 8


def _rms(x, g):
    return x * lax.rsqrt(jnp.mean(x * x, axis=-1, keepdims=True) + EPS) * g


def _rel_bucket_table():
    i = np.arange(BLOCK)[:, None]
    j = np.arange(2 * BLOCK)[None, :]
    n = np.maximum(i + BLOCK - j, 0)
    max_exact = N_BUCKETS // 2
    nf = np.maximum(n, max_exact).astype(np.float32)
    large = max_exact + (np.log(nf / np.float32(max_exact)) / np.float32(math.log(MAX_DISTANCE / max_exact))
                         * np.float32(N_BUCKETS - max_exact)).astype(np.int32)
    large = np.minimum(large, N_BUCKETS - 1)
    return np.where(n < max_exact, n, large).astype(np.int32)


def _build_bias(relb_ref, bucket_ref, bias_scr):
    bucket = bucket_ref[...]
    for h in range(N_Q_HEADS):
        bias_scr[h] = jnp.zeros((BLOCK, 2 * BLOCK), F32)
    for b in range(N_BUCKETS):
        hit = bucket == b
        for h in range(N_Q_HEADS):
            bias_scr[h] = jnp.where(hit, relb_ref[b, h], bias_scr[h])


def _ssm_prep_kernel(are_ref, aim_ref, ls_ref, btr_ref, bti_ref,
                     lr_ref, li_ref, lcr_ref, lci_ref, bbr_ref, bbi_ref):
    ar = are_ref[...]
    ai = aim_ref[...]
    step = jnp.exp(ls_ref[...])
    zr = ar * step
    zi = ai * step
    mag = jnp.exp(zr)
    lr = mag * jnp.cos(zi)
    li = mag * jnp.sin(zi)
    lr_ref[...] = lr
    li_ref[...] = li
    pr, pi = lr, li
    for _ in range(int(math.log2(CHUNK))):
        pr, pi = pr * pr - pi * pi, 2.0 * pr * pi
    lcr_ref[...] = pr
    lci_ref[...] = pi
    nr = lr - 1.0
    ni = li
    den = ar * ar + ai * ai
    cr = (nr * ar + ni * ai) / den
    ci = (ni * ar - nr * ai) / den
    btr = btr_ref[...]
    bti = bti_ref[...]
    bbr_ref[...] = cr * btr - ci * bti
    bbi_ref[...] = cr * bti + ci * btr


def _ssm_prep(a_re, a_im, log_step, b_re, b_im):
    g, p, c = N_SSM_GROUPS, SSM_STATE, SSM_GROUP
    vec = jax.ShapeDtypeStruct((g, 1, p), F32)
    mat = jax.ShapeDtypeStruct((g, c, p), F32)
    return pl.pallas_call(
        _ssm_prep_kernel,
        out_shape=(vec, vec, vec, vec, mat, mat),
        name="ssm_prep",
    )(a_re.reshape(g, 1, p), a_im.reshape(g, 1, p), log_step.reshape(g, 1, 1),
      jnp.transpose(b_re, (0, 2, 1)), jnp.transpose(b_im, (0, 2, 1)))


def _softmax_pv(s, valid, sink, v):
    s = jnp.where(valid, s, NEG)
    m = jnp.maximum(jnp.max(s, axis=-1, keepdims=True), sink)
    p = jnp.exp(s - m)
    denom = jnp.sum(p, axis=-1, keepdims=True) + jnp.exp(sink - m)
    o = jnp.dot(p.astype(BF16), v, preferred_element_type=F32)
    return o / denom


def _in_attn_kernel(relb_ref, sinks_ref, x_ref, gmix_ref, win_ref, bucket_ref, gattn_ref,
                    attn_ref, u_ref, newk_ref, newv_ref,
                    kv_scr, q_scr, bias_scr):
    b = pl.program_id(0)
    t = pl.program_id(1)
    nt = pl.num_programs(1)

    @pl.when(jnp.logical_and(b == 0, t == 0))
    def _():
        _build_bias(relb_ref, bucket_ref, bias_scr)

    @pl.when(t == 0)
    def _():
        kv_scr[0:BLOCK, :] = jnp.zeros((BLOCK, 2 * KV_WIDTH), BF16)

    xn = _rms(x_ref[...], gmix_ref[...]).astype(BF16)
    proj = jnp.dot(xn, win_ref[...], preferred_element_type=F32)
    q_scr[...] = proj[:, :ATTN_WIDTH].astype(BF16)
    kv_scr[BLOCK:BLOCK + TM_A, :] = proj[:, ATTN_WIDTH:ATTN_WIDTH + 2 * KV_WIDTH].astype(BF16)
    u_ref[...] = proj[:, ATTN_WIDTH + 2 * KV_WIDTH:]

    @pl.when(t == nt - 1)
    def _():
        newk_ref[...] = proj[TM_A - WINDOW:, ATTN_WIDTH:ATTN_WIDTH + KV_WIDTH]
        newv_ref[...] = proj[TM_A - WINDOW:, ATTN_WIDTH + KV_WIDTH:ATTN_WIDTH + 2 * KV_WIDTH]

    qi = lax.broadcasted_iota(jnp.int32, (BLOCK, 2 * BLOCK), 0)
    kj = lax.broadcasted_iota(jnp.int32, (BLOCK, 2 * BLOCK), 1)
    dist = qi + BLOCK - kj
    in_band = jnp.logical_and(dist >= 0, dist < WINDOW)
    cur_half = kj >= BLOCK
    gattn = gattn_ref[...]

    def block_body(j, carry):
        r0 = pl.multiple_of(j * BLOCK, BLOCK)
        qb = q_scr[pl.ds(r0, BLOCK), :]
        kvb = kv_scr[pl.ds(r0, 2 * BLOCK), :]
        not_first = jnp.logical_or(t > 0, j > 0)
        valid = jnp.logical_and(in_band, jnp.logical_or(cur_half, not_first))
        outs = []
        for h in range(N_Q_HEADS):
            g = h // Q_PER_KV
            qh = qb[:, h * HEAD_DIM:(h + 1) * HEAD_DIM]
            kg = kvb[:, g * HEAD_DIM:(g + 1) * HEAD_DIM]
            vg = kvb[:, KV_WIDTH + g * HEAD_DIM:KV_WIDTH + (g + 1) * HEAD_DIM]
            s = lax.dot_general(qh, kg, (((1,), (1,)), ((), ())), preferred_element_type=F32)
            s = s * (HEAD_DIM ** -0.5) + bias_scr[h]
            outs.append(_softmax_pv(s, valid, sinks_ref[0, h], vg))
        o = jnp.concatenate(outs, axis=-1)
        attn_ref[pl.ds(r0, BLOCK), :] = _rms(o, gattn).astype(BF16)
        return carry

    lax.fori_loop(0, TM_A // BLOCK, block_body, 0)
    kv_scr[0:BLOCK, :] = kv_scr[TM_A:TM_A + BLOCK, :]


def _const_spec(shape):
    zeros = (0,) * len(shape)
    return pl.BlockSpec(shape, lambda *_: zeros, pipeline_mode=pl.Buffered(1))


def _smem_spec():
    return pl.BlockSpec(memory_space=pltpu.SMEM)


def _in_attn(x, rel_bias, sinks, g_mix, w_in, bucket, g_attn):
    bsz, seq, _ = x.shape
    nt = seq // TM_A
    row = lambda width: pl.BlockSpec((None, TM_A, width), lambda b, t: (b, t, 0))
    win_spec = pl.BlockSpec((None, WINDOW, KV_WIDTH), lambda b, t: (b, 0, 0))
    return pl.pallas_call(
        _in_attn_kernel,
        out_shape=(jax.ShapeDtypeStruct((bsz, seq, ATTN_WIDTH), BF16),
                   jax.ShapeDtypeStruct((bsz, seq, SSM_WIDTH), F32),
                   jax.ShapeDtypeStruct((bsz, WINDOW, KV_WIDTH), F32),
                   jax.ShapeDtypeStruct((bsz, WINDOW, KV_WIDTH), F32)),
        grid=(bsz, nt),
        in_specs=[_smem_spec(), _smem_spec(), row(D_MODEL), _const_spec((1, D_MODEL)),
                  _const_spec((D_MODEL, D_IN_PROJ)), _const_spec((BLOCK, 2 * BLOCK)),
                  _const_spec((1, ATTN_WIDTH))],
        out_specs=(row(ATTN_WIDTH), row(SSM_WIDTH), win_spec, win_spec),
        scratch_shapes=[pltpu.VMEM((TM_A + BLOCK, 2 * KV_WIDTH), BF16),
                        pltpu.VMEM((TM_A, ATTN_WIDTH), BF16),
                        pltpu.VMEM((N_Q_HEADS, BLOCK, 2 * BLOCK), F32)],
        compiler_params=pltpu.CompilerParams(
            dimension_semantics=("arbitrary", "arbitrary"), vmem_limit_bytes=VMEM_LIMIT),
        name="in_attn",
    )(rel_bias, sinks, x, g_mix, w_in, bucket, g_attn)


def _s_in_attn_kernel(relb_ref, sinks_ref, x_ref, gmix_ref, win_ref, bucket_ref, gattn_ref,
                      ck_ref, cv_ref,
                      attn_ref, u_ref, newk_ref, newv_ref,
                      bias_scr):
    steps = x_ref.shape[0] // S_SEQ

    @pl.when(pl.program_id(0) == 0)
    def _():
        _build_bias(relb_ref, bucket_ref, bias_scr)

    xn = _rms(x_ref[...], gmix_ref[...]).astype(BF16)
    proj = jnp.dot(xn, win_ref[...], preferred_element_type=F32)
    u_ref[...] = proj[:, ATTN_WIDTH + 2 * KV_WIDTH:]

    nkeys = WINDOW + SUBLANES
    qi = lax.broadcasted_iota(jnp.int32, (steps, nkeys), 0)
    kj = lax.broadcasted_iota(jnp.int32, (steps, nkeys), 1)
    dist = qi + WINDOW - kj
    valid = jnp.logical_and(jnp.logical_and(dist >= 0, dist < WINDOW), kj < WINDOW + steps)
    gattn = gattn_ref[...]
    pad = jnp.zeros((SUBLANES - steps, KV_WIDTH), F32)

    rows = []
    for sq in range(S_SEQ):
        pr = proj[sq * steps:(sq + 1) * steps, :]
        kn = pr[:, ATTN_WIDTH:ATTN_WIDTH + KV_WIDTH]
        vn = pr[:, ATTN_WIDTH + KV_WIDTH:ATTN_WIDTH + 2 * KV_WIDTH]
        ck = ck_ref[sq]
        cv = cv_ref[sq]
        newk_ref[sq, 0:WINDOW - steps, :] = ck[steps:, :]
        newk_ref[sq, WINDOW - steps:, :] = kn
        newv_ref[sq, 0:WINDOW - steps, :] = cv[steps:, :]
        newv_ref[sq, WINDOW - steps:, :] = vn
        keys = jnp.concatenate([ck, kn, pad], axis=0).astype(BF16)
        vals = jnp.concatenate([cv, vn, pad], axis=0).astype(BF16)
        outs = []
        for h in range(N_Q_HEADS):
            g = h // Q_PER_KV
            qh = pr[:, h * HEAD_DIM:(h + 1) * HEAD_DIM].astype(BF16)
            kg = keys[:, g * HEAD_DIM:(g + 1) * HEAD_DIM]
            vg = vals[:, g * HEAD_DIM:(g + 1) * HEAD_DIM]
            s = lax.dot_general(qh, kg, (((1,), (1,)), ((), ())), preferred_element_type=F32)
            s = s * (HEAD_DIM ** -0.5) + bias_scr[h, 0:steps, 0:nkeys]
            outs.append(_softmax_pv(s, valid, sinks_ref[0, h], vg))
        rows.append(jnp.concatenate(outs, axis=-1))
    o = jnp.concatenate(rows, axis=0)
    attn_ref[...] = _rms(o, gattn).astype(BF16)


def _s_in_attn(x, rel_bias, sinks, g_mix, w_in, bucket, g_attn, cache_k, cache_v):
    nseq, steps, _ = x.shape
    rows = S_SEQ * steps
    xf = x.reshape(nseq * steps, D_MODEL)
    row = lambda width: pl.BlockSpec((rows, width), lambda i: (i, 0))
    cache = pl.BlockSpec((S_SEQ, WINDOW, KV_WIDTH), lambda i: (i, 0, 0))
    return pl.pallas_call(
        _s_in_attn_kernel,
        out_shape=(jax.ShapeDtypeStruct((nseq * steps, ATTN_WIDTH), BF16),
                   jax.ShapeDtypeStruct((nseq * steps, SSM_WIDTH), F32),
                   jax.ShapeDtypeStruct((nseq, WINDOW, KV_WIDTH), F32),
                   jax.ShapeDtypeStruct((nseq, WINDOW, KV_WIDTH), F32)),
        grid=(nseq // S_SEQ,),
        in_specs=[_smem_spec(), _smem_spec(), row(D_MODEL), _const_spec((1, D_MODEL)),
                  _const_spec((D_MODEL, D_IN_PROJ)), _const_spec((BLOCK, 2 * BLOCK)),
                  _const_spec((1, ATTN_WIDTH)), cache, cache],
        out_specs=(row(ATTN_WIDTH), row(SSM_WIDTH), cache, cache),
        scratch_shapes=[pltpu.VMEM((N_Q_HEADS, BLOCK, 2 * BLOCK), F32)],
        compiler_params=pltpu.CompilerParams(
            dimension_semantics=("arbitrary",), vmem_limit_bytes=VMEM_LIMIT),
        name="s_in_attn",
    )(rel_bias, sinks, xf, g_mix, w_in, bucket, g_attn, cache_k, cache_v)


def _input_drive(up, bre_ref, bim_ref, hre, him):
    upb = up.astype(BF16)
    half = STATE_LANES // 2
    for kt in range(SSM_WIDTH // MXU_DIM):
        lhs = upb[:, kt * MXU_DIM:(kt + 1) * MXU_DIM]
        cols = slice(kt * half, (kt + 1) * half)
        rws = slice(kt * MXU_DIM, (kt + 1) * MXU_DIM)
        hre[:, cols] = jnp.dot(lhs, bre_ref[rws, cols], preferred_element_type=F32)
        him[:, cols] = jnp.dot(lhs, bim_ref[rws, cols], preferred_element_type=F32)


def _ssm_output(hre, him, up, cre_ref, cim_ref, d_ref, wglu_ref, bglu_ref, gssm_ref):
    ys = []
    kblk = STATE_LANES // (SSM_WIDTH // MXU_DIM)
    for n in range(SSM_WIDTH // MXU_DIM):
        rws = slice(n * kblk, (n + 1) * kblk)
        cols = slice(n * MXU_DIM, (n + 1) * MXU_DIM)
        yr = jnp.dot(hre[:, rws].astype(BF16), cre_ref[rws, cols], preferred_element_type=F32)
        yi = jnp.dot(him[:, rws].astype(BF16), cim_ref[rws, cols], preferred_element_type=F32)
        ys.append(yr - yi)
    y = jnp.concatenate(ys, axis=-1) + d_ref[...] * up
    g = jax.nn.gelu(y)
    gate = jax.nn.sigmoid(jnp.dot(g.astype(BF16), wglu_ref[...], preferred_element_type=F32) + bglu_ref[...])
    return _rms(g * gate, gssm_ref[...])


def _scan_kernel(u_ref, lr_ref, li_ref, lcr_ref, lci_ref, bre_ref, bim_ref, cre_ref, cim_ref,
                 d_ref, wglu_ref, bglu_ref, gssm_ref,
                 out_ref, hre_out, him_out,
                 pad_scr, up_scr, hre, him, carry_re, carry_im, e_re, e_im, hc_re, hc_im):
    t = pl.program_id(1)
    nslab = SSM_WIDTH // LANES

    @pl.when(t == 0)
    def _():
        carry_re[...] = jnp.zeros_like(carry_re)
        carry_im[...] = jnp.zeros_like(carry_im)

    for c in range(SUBLANES):
        for s in range(nslab):
            pad_scr[s, c * PITCH:c * PITCH + CHUNK, :] = u_ref[c * CHUNK:(c + 1) * CHUNK, s * LANES:(s + 1) * LANES]
    for tau in range(CHUNK):
        for s in range(nslab):
            up_scr[tau * SUBLANES:(tau + 1) * SUBLANES, s * LANES:(s + 1) * LANES] = (
                pad_scr[s, pl.ds(tau, SUBLANES, stride=PITCH), :])
    up = up_scr[...]
    _input_drive(up, bre_ref, bim_ref, hre, him)

    slab = 4 * LANES

    def recur(store):
        for sl in range(STATE_LANES // slab):
            cols = slice(sl * slab, (sl + 1) * slab)
            ar = jnp.broadcast_to(lr_ref[:, cols], (SUBLANES, slab))
            ai = jnp.broadcast_to(li_ref[:, cols], (SUBLANES, slab))
            if store:
                init = (hc_re[:, cols], hc_im[:, cols])
            else:
                init = (jnp.zeros((SUBLANES, slab), F32), jnp.zeros((SUBLANES, slab), F32))

            def body(tau, h):
                hr, hi = h
                r0 = pl.multiple_of(tau * SUBLANES, SUBLANES)
                br = hre[pl.ds(r0, SUBLANES), cols]
                bi = him[pl.ds(r0, SUBLANES), cols]
                nr = ar * hr - ai * hi + br
                ni = ar * hi + ai * hr + bi
                if store:
                    hre[pl.ds(r0, SUBLANES), cols] = nr
                    him[pl.ds(r0, SUBLANES), cols] = ni
                return nr, ni

            hr, hi = lax.fori_loop(0, CHUNK, body, init, unroll=8)
            if not store:
                e_re[:, cols] = hr
                e_im[:, cols] = hi

    recur(store=False)

    cr = carry_re[...]
    ci = carry_im[...]
    lcr = lcr_ref[...]
    lci = lci_ref[...]
    for c in range(SUBLANES):
        hc_re[c:c + 1, :] = cr
        hc_im[c:c + 1, :] = ci
        er = e_re[c:c + 1, :]
        ei = e_im[c:c + 1, :]
        cr, ci = lcr * cr - lci * ci + er, lcr * ci + lci * cr + ei
    carry_re[...] = cr
    carry_im[...] = ci

    recur(store=True)

    o = _ssm_output(hre, him, up, cre_ref, cim_ref, d_ref, wglu_ref, bglu_ref, gssm_ref)

    for tau in range(CHUNK):
        for s in range(nslab):
            pad_scr[s, pl.ds(tau, SUBLANES, stride=PITCH), :] = (
                o[tau * SUBLANES:(tau + 1) * SUBLANES, s * LANES:(s + 1) * LANES])
    for c in range(SUBLANES):
        for s in range(nslab):
            out_ref[c * CHUNK:(c + 1) * CHUNK, s * LANES:(s + 1) * LANES] = (
                pad_scr[s, c * PITCH:c * PITCH + CHUNK, :].astype(BF16))

    @pl.when(t == pl.num_programs(1) - 1)
    def _():
        hre_out[...] = cr
        him_out[...] = ci


def _scan(u, lam, lam_chunk, bblk, cblk, d, w_glu, b_glu, g_ssm):
    bsz, seq, _ = u.shape
    nt = seq // TM_B
    row = pl.BlockSpec((None, TM_B, SSM_WIDTH), lambda b, t: (b, t, 0))
    state = pl.BlockSpec((None, 1, STATE_LANES), lambda b, t: (b, 0, 0))
    vec = _const_spec((1, STATE_LANES))
    return pl.pallas_call(
        _scan_kernel,
        out_shape=(jax.ShapeDtypeStruct((bsz, seq, SSM_WIDTH), BF16),
                   jax.ShapeDtypeStruct((bsz, 1, STATE_LANES), F32),
                   jax.ShapeDtypeStruct((bsz, 1, STATE_LANES), F32)),
        grid=(bsz, nt),
        in_specs=[row, vec, vec, vec, vec,
                  _const_spec((SSM_WIDTH, STATE_LANES)), _const_spec((SSM_WIDTH, STATE_LANES)),
                  _const_spec((STATE_LANES, SSM_WIDTH)), _const_spec((STATE_LANES, SSM_WIDTH)),
                  _const_spec((1, SSM_WIDTH)), _const_spec((SSM_WIDTH, SSM_WIDTH)),
                  _const_spec((1, SSM_WIDTH)), _const_spec((1, SSM_WIDTH))],
        out_specs=(row, state, state),
        scratch_shapes=[pltpu.VMEM((SSM_WIDTH // LANES, SUBLANES * PITCH, LANES), F32),
                        pltpu.VMEM((TM_B, SSM_WIDTH), F32),
                        pltpu.VMEM((TM_B, STATE_LANES), F32),
                        pltpu.VMEM((TM_B, STATE_LANES), F32),
                        pltpu.VMEM((1, STATE_LANES), F32),
                        pltpu.VMEM((1, STATE_LANES), F32),
                        pltpu.VMEM((SUBLANES, STATE_LANES), F32),
                        pltpu.VMEM((SUBLANES, STATE_LANES), F32),
                        pltpu.VMEM((SUBLANES, STATE_LANES), F32),
                        pltpu.VMEM((SUBLANES, STATE_LANES), F32)],
        compiler_params=pltpu.CompilerParams(
            dimension_semantics=("arbitrary", "arbitrary"), vmem_limit_bytes=VMEM_LIMIT),
        name="scan",
    )(u, lam[0], lam[1], lam_chunk[0], lam_chunk[1], bblk[0], bblk[1], cblk[0], cblk[1],
      d, w_glu, b_glu, g_ssm)


def _s_scan_kernel(u_ref, h0r_ref, h0i_ref, lr_ref, li_ref, bre_ref, bim_ref, cre_ref, cim_ref,
                   d_ref, wglu_ref, bglu_ref, gssm_ref,
                   out_ref, hre_out, him_out,
                   pad_scr, up_scr, hre, him):
    nseq = h0r_ref.shape[0]
    steps = u_ref.shape[0] // nseq
    nslab = SSM_WIDTH // LANES

    for s in range(nslab):
        pad_scr[s] = u_ref[:, s * LANES:(s + 1) * LANES]
    for i in range(steps):
        for s in range(nslab):
            up_scr[i * nseq:(i + 1) * nseq, s * LANES:(s + 1) * LANES] = (
                pad_scr[s, pl.ds(i, nseq, stride=steps), :])
    up = up_scr[...]
    _input_drive(up, bre_ref, bim_ref, hre, him)

    slab = 2 * LANES
    for sl in range(STATE_LANES // slab):
        cols = slice(sl * slab, (sl + 1) * slab)
        ar = lr_ref[:, cols]
        ai = li_ref[:, cols]
        hr = h0r_ref[:, cols]
        hi = h0i_ref[:, cols]
        for i in range(steps):
            rws = slice(i * nseq, (i + 1) * nseq)
            hr, hi = ar * hr - ai * hi + hre[rws, cols], ar * hi + ai * hr + him[rws, cols]
            hre[rws, cols] = hr
            him[rws, cols] = hi
        hre_out[:, cols] = hr
        him_out[:, cols] = hi

    o = _ssm_output(hre, him, up, cre_ref, cim_ref, d_ref, wglu_ref, bglu_ref, gssm_ref)

    for i in range(steps):
        for s in range(nslab):
            pad_scr[s, pl.ds(i, nseq, stride=steps), :] = o[i * nseq:(i + 1) * nseq, s * LANES:(s + 1) * LANES]
    for s in range(nslab):
        out_ref[:, s * LANES:(s + 1) * LANES] = pad_scr[s].astype(BF16)


def _s_scan(u, h0_re, h0_im, lam, bblk, cblk, d, w_glu, b_glu, g_ssm):
    rows = u.shape[0]
    nseq = h0_re.shape[0]
    return pl.pallas_call(
        _s_scan_kernel,
        out_shape=(jax.ShapeDtypeStruct((rows, SSM_WIDTH), BF16),
                   jax.ShapeDtypeStruct((nseq, STATE_LANES), F32),
                   jax.ShapeDtypeStruct((nseq, STATE_LANES), F32)),
        scratch_shapes=[pltpu.VMEM((SSM_WIDTH // LANES, rows, LANES), F32),
                        pltpu.VMEM((rows, SSM_WIDTH), F32),
                        pltpu.VMEM((rows, STATE_LANES), F32),
                        pltpu.VMEM((rows, STATE_LANES), F32)],
        compiler_params=pltpu.CompilerParams(vmem_limit_bytes=VMEM_LIMIT),
        name="s_scan",
    )(u, h0_re, h0_im, lam[0], lam[1], bblk[0], bblk[1], cblk[0], cblk[1], d, w_glu, b_glu, g_ssm)


def _out_mlp_kernel(x_ref, a_ref, s_ref, wout_ref, gmlp_ref, wup_ref, wdown_ref, gfin_ref, y_ref):
    x1 = (x_ref[...]
          + jnp.dot(a_ref[...], wout_ref[0:ATTN_WIDTH, :], preferred_element_type=F32)
          + jnp.dot(s_ref[...], wout_ref[ATTN_WIDTH:, :], preferred_element_type=F32))
    hm = _rms(x1, gmlp_ref[...]).astype(BF16)
    y_ref[...] = x1
    ff_blk = 1024
    for c in range(D_FF // ff_blk):
        cols = slice(c * ff_blk, (c + 1) * ff_blk)
        up = jnp.dot(hm, wup_ref[:, cols], preferred_element_type=F32)
        act = jnp.square(jnp.maximum(up, 0.0)).astype(BF16)
        y_ref[...] += jnp.dot(act, wdown_ref[cols, :], preferred_element_type=F32)
    y_ref[...] = _rms(y_ref[...], gfin_ref[...])


def _out_mlp(x, attn_n, ssm_n, w_out, g_mlp, w_up, w_down, g_fin):
    rows = x.shape[0]
    row = lambda width: pl.BlockSpec((TM_C, width), lambda i: (i, 0))
    return pl.pallas_call(
        _out_mlp_kernel,
        out_shape=jax.ShapeDtypeStruct((rows, D_MODEL), F32),
        grid=(rows // TM_C,),
        in_specs=[row(D_MODEL), row(ATTN_WIDTH), row(SSM_WIDTH),
                  _const_spec((D_MODEL, D_MODEL)), _const_spec((1, D_MODEL)),
                  _const_spec((D_MODEL, D_FF)), _const_spec((D_FF, D_MODEL)),
                  _const_spec((1, D_MODEL))],
        out_specs=row(D_MODEL),
        compiler_params=pltpu.CompilerParams(
            dimension_semantics=("arbitrary",), vmem_limit_bytes=VMEM_LIMIT),
        name="out_mlp",
    )(x, attn_n, ssm_n, w_out, g_mlp, w_up, w_down, g_fin)


def _block_diag(bbar, c_gpc):
    eye = jnp.eye(N_SSM_GROUPS, dtype=F32)
    bblk = jnp.einsum('gcp,gh->gchp', bbar, eye).reshape(SSM_WIDTH, STATE_LANES).astype(BF16)
    cblk = jnp.einsum('gpc,gh->gphc', c_gpc, eye).reshape(STATE_LANES, SSM_WIDTH).astype(BF16)
    return bblk, cblk


def kernel(x_prompt, x_sample, cache_k, cache_v, state_ssm_re, state_ssm_im, rel_bias, norm_mix, w_in, attn_sinks, ssm_a_re, ssm_a_im, ssm_log_step, ssm_b_re, ssm_b_im, ssm_c_re, ssm_c_im, ssm_d, w_glu, b_glu, norm_attn_out, norm_ssm_out, w_out, norm_mlp, w_up, w_down, norm_final):
    depth = w_in.shape[0]
    bsz, seq, _ = x_prompt.shape
    nseq, steps, _ = x_sample.shape
    bucket = jnp.asarray(_rel_bucket_table())
    g_fin = norm_final.reshape(1, D_MODEL)

    xp, xs = x_prompt, x_sample
    outs = [[] for _ in range(8)]
    for l in range(depth):
        lr, li, lcr, lci, bbr, bbi = _ssm_prep(ssm_a_re[l], ssm_a_im[l], ssm_log_step[l], ssm_b_re[l], ssm_b_im[l])
        lam = (lr.reshape(1, STATE_LANES), li.reshape(1, STATE_LANES))
        lam_chunk = (lcr.reshape(1, STATE_LANES), lci.reshape(1, STATE_LANES))
        c_re_t = jnp.transpose(ssm_c_re[l], (0, 2, 1))
        c_im_t = jnp.transpose(ssm_c_im[l], (0, 2, 1))
        bre, cre = _block_diag(bbr, c_re_t)
        bim, cim = _block_diag(bbi, c_im_t)
        g_mix = norm_mix[l].reshape(1, D_MODEL)
        g_attn = norm_attn_out[l].reshape(1, ATTN_WIDTH)
        g_ssm = norm_ssm_out[l].reshape(1, SSM_WIDTH)
        g_mlp = norm_mlp[l].reshape(1, D_MODEL)
        sinks = attn_sinks[l].reshape(1, N_Q_HEADS)
        d = ssm_d[l].reshape(1, SSM_WIDTH)
        bg = b_glu[l].reshape(1, SSM_WIDTH)
        w_in_b = w_in[l].astype(BF16)
        w_glu_b = w_glu[l].astype(BF16)
        w_out_b = w_out[l].astype(BF16)
        w_up_b = w_up[l].astype(BF16)
        w_down_b = w_down[l].astype(BF16)
        last = l == depth - 1
        g_last = g_fin if last else jnp.ones_like(g_fin)
        assert last, "the final norm is fused into the last layer's MLP kernel"

        attn_n, u, k1, v1 = _in_attn(xp, rel_bias, sinks, g_mix, w_in_b, bucket, g_attn)
        ssm_n, r1, i1 = _scan(u, lam, lam_chunk, (bre, bim), (cre, cim), d, w_glu_b, bg, g_ssm)
        xp = _out_mlp(xp.reshape(bsz * seq, D_MODEL), attn_n.reshape(bsz * seq, ATTN_WIDTH),
                      ssm_n.reshape(bsz * seq, SSM_WIDTH), w_out_b, g_mlp, w_up_b, w_down_b, g_last
                      ).reshape(bsz, seq, D_MODEL)

        ck = cache_k[l].reshape(nseq, WINDOW, KV_WIDTH)
        cv = cache_v[l].reshape(nseq, WINDOW, KV_WIDTH)
        s_attn, s_u, k2, v2 = _s_in_attn(xs, rel_bias, sinks, g_mix, w_in_b, bucket, g_attn, ck, cv)
        s_ssm, r2, i2 = _s_scan(s_u, state_ssm_re[l].reshape(nseq, STATE_LANES),
                                state_ssm_im[l].reshape(nseq, STATE_LANES),
                                lam, (bre, bim), (cre, cim), d, w_glu_b, bg, g_ssm)
        xs = _out_mlp(xs.reshape(nseq * steps, D_MODEL), s_attn, s_ssm, w_out_b, g_mlp, w_up_b, w_down_b,
                      g_last).reshape(nseq, steps, D_MODEL)

        kv_shape = (WINDOW, N_KV_HEADS, HEAD_DIM)
        st_shape = (N_SSM_GROUPS, SSM_STATE)
        for lst, val in zip(outs, (k1.reshape(bsz, *kv_shape), v1.reshape(bsz, *kv_shape),
                                   r1.reshape(bsz, *st_shape), i1.reshape(bsz, *st_shape),
                                   k2.reshape(nseq, *kv_shape), v2.reshape(nseq, *kv_shape),
                                   r2.reshape(nseq, *st_shape), i2.reshape(nseq, *st_shape))):
            lst.append(val)

    return (xp, xs) + tuple(jnp.stack(o) for o in outs)
```

```python
import functools
import math

import numpy as np
import jax
import jax.numpy as jnp
from jax import lax
from jax.experimental import pallas as pl
from jax.experimental.pallas import tpu as pltpu

D_MODEL = 1024
HEAD_DIM = 64
ATTN_WIDTH = 512
N_Q_HEADS = 8
N_KV_HEADS = 2
Q_PER_KV = 4
KV_WIDTH = 128
WINDOW = 128
BLOCK = 128
N_BUCKETS = 32
MAX_DISTANCE = 128
SSM_WIDTH = 512
SSM_GROUP = 16
N_SSM_GROUPS = 32
SSM_STATE = 64
STATE_LANES = N_SSM_GROUPS * SSM_STATE
D_FF = 4096
D_IN_PROJ = 1280
EPS = 1e-6
NEG = -1e30

LANES = 128
SUBLANES = 8
MXU_DIM = 256
VMEM_LIMIT = 56 * 1024 * 1024

F32 = jnp.float32
BF16 = jnp.bfloat16

TM_A = 512
TM_C = 512
TM_B = 512
CHUNK = TM_B // SUBLANES
PITCH = CHUNK + SUBLANES
S_SEQ = 16


def _rms(x, g):
    return x * lax.rsqrt(jnp.mean(x * x, axis=-1, keepdims=True) + EPS) * g


def _rel_bucket_table():
    i = np.arange(BLOCK)[:, None]
    j = np.arange(2 * BLOCK)[None, :]
    n = np.maximum(i + BLOCK - j, 0)
    max_exact = N_BUCKETS // 2
    nf = np.maximum(n, max_exact).astype(np.float32)
    large = max_exact + (np.log(nf / np.float32(max_exact)) / np.float32(math.log(MAX_DISTANCE / max_exact))
                         * np.float32(N_BUCKETS - max_exact)).astype(np.int32)
    large = np.minimum(large, N_BUCKETS - 1)
    return np.where(n < max_exact, n, large).astype(np.int32)


def _build_bias(relb_ref, bucket_ref, bias_scr):
    bucket = bucket_ref[...]
    for h in range(N_Q_HEADS):
        bias_scr[h] = jnp.zeros((BLOCK, 2 * BLOCK), F32)
    for b in range(N_BUCKETS):
        hit = bucket == b
        for h in range(N_Q_HEADS):
            bias_scr[h] = jnp.where(hit, relb_ref[b, h], bias_scr[h])


def _ssm_prep_kernel(are_ref, aim_ref, ls_ref, btr_ref, bti_ref,
                     lr_ref, li_ref, lcr_ref, lci_ref, bbr_ref, bbi_ref):
    ar = are_ref[...]
    ai = aim_ref[...]
    step = jnp.exp(ls_ref[...])
    zr = ar * step
    zi = ai * step
    mag = jnp.exp(zr)
    lr = mag * jnp.cos(zi)
    li = mag * jnp.sin(zi)
    lr_ref[...] = lr
    li_ref[...] = li
    pr, pi = lr, li
    for _ in range(int(math.log2(CHUNK))):
        pr, pi = pr * pr - pi * pi, 2.0 * pr * pi
    lcr_ref[...] = pr
    lci_ref[...] = pi
    nr = lr - 1.0
    ni = li
    den = ar * ar + ai * ai
    cr = (nr * ar + ni * ai) / den
    ci = (ni * ar - nr * ai) / den
    btr = btr_ref[...]
    bti = bti_ref[...]
    bbr_ref[...] = cr * btr - ci * bti
    bbi_ref[...] = cr * bti + ci * btr


def _ssm_prep(a_re, a_im, log_step, b_re, b_im):
    g, p, c = N_SSM_GROUPS, SSM_STATE, SSM_GROUP
    vec = jax.ShapeDtypeStruct((g, 1, p), F32)
    mat = jax.ShapeDtypeStruct((g, c, p), F32)
    return pl.pallas_call(
        _ssm_prep_kernel,
        out_shape=(vec, vec, vec, vec, mat, mat),
        name="ssm_prep",
    )(a_re.reshape(g, 1, p), a_im.reshape(g, 1, p), log_step.reshape(g, 1, 1),
      jnp.transpose(b_re, (0, 2, 1)), jnp.transpose(b_im, (0, 2, 1)))


def _stack_heads(q_blk, lane_lo):
    zero = jnp.zeros_like(q_blk[:, 0:LANES])
    lo, hi = [], []
    for c in range(ATTN_WIDTH // LANES):
        blk = q_blk[:, c * LANES:(c + 1) * LANES]
        lo.append(jnp.where(lane_lo, blk, zero))
        hi.append(jnp.where(lane_lo, zero, blk))
    return jnp.concatenate(lo + hi, axis=0)


def _unstack_heads(o, rows, lane_lo):
    nblk = ATTN_WIDTH // LANES
    return jnp.concatenate(
        [jnp.where(lane_lo, o[c * rows:(c + 1) * rows], o[(c + nblk) * rows:(c + nblk + 1) * rows])
         for c in range(nblk)], axis=-1)


def _in_attn_kernel(relb_ref, sinks_ref, x_ref, gmix_ref, win_ref, bucket_ref, gattn_ref,
                    attn_ref, u_ref, newk_ref, newv_ref,
                    kv_scr, q_scr, bias_scr):
    b = pl.program_id(0)
    t = pl.program_id(1)
    nt = pl.num_programs(1)

    @pl.when(jnp.logical_and(b == 0, t == 0))
    def _():
        _build_bias(relb_ref, bucket_ref, bias_scr)

    @pl.when(t == 0)
    def _():
        kv_scr[0:BLOCK, :] = jnp.zeros((BLOCK, 2 * KV_WIDTH), BF16)

    xn = _rms(x_ref[...], gmix_ref[...]).astype(BF16)
    proj = jnp.dot(xn, win_ref[...], preferred_element_type=F32)
    q_scr[...] = (proj[:, :ATTN_WIDTH] * (HEAD_DIM ** -0.5)).astype(BF16)
    kv_scr[BLOCK:BLOCK + TM_A, :] = proj[:, ATTN_WIDTH:ATTN_WIDTH + 2 * KV_WIDTH].astype(BF16)
    u_ref[...] = proj[:, ATTN_WIDTH + 2 * KV_WIDTH:]

    @pl.when(t == nt - 1)
    def _():
        newk_ref[...] = proj[TM_A - WINDOW:, ATTN_WIDTH:ATTN_WIDTH + KV_WIDTH]
        newv_ref[...] = proj[TM_A - WINDOW:, ATTN_WIDTH + KV_WIDTH:ATTN_WIDTH + 2 * KV_WIDTH]

    qi = lax.broadcasted_iota(jnp.int32, (BLOCK, 2 * BLOCK), 0)
    kj = lax.broadcasted_iota(jnp.int32, (BLOCK, 2 * BLOCK), 1)
    dist = qi + BLOCK - kj
    in_band = jnp.logical_and(dist >= 0, dist < WINDOW)
    cur_half = kj >= BLOCK
    lane_lo = lax.broadcasted_iota(jnp.int32, (BLOCK, LANES), 1) < HEAD_DIM
    gattn = gattn_ref[...]

    def block_body(j, carry):
        r0 = pl.multiple_of(j * BLOCK, BLOCK)
        q2 = _stack_heads(q_scr[pl.ds(r0, BLOCK), :], lane_lo)
        kvb = kv_scr[pl.ds(r0, 2 * BLOCK), :]
        not_first = jnp.logical_or(t > 0, j > 0)
        valid = jnp.logical_and(in_band, jnp.logical_or(cur_half, not_first))
        s_all = lax.dot_general(q2, kvb[:, 0:KV_WIDTH], (((1,), (1,)), ((), ())),
                                preferred_element_type=F32)
        ps, denoms = [], []
        for h in range(N_Q_HEADS):
            sink = sinks_ref[0, h]
            s = jnp.where(valid, s_all[h * BLOCK:(h + 1) * BLOCK] + bias_scr[h], NEG)
            m = jnp.maximum(jnp.max(s, axis=-1, keepdims=True), sink)
            p = jnp.exp(s - m)
            denoms.append(jnp.sum(p, axis=-1, keepdims=True) + jnp.exp(sink - m))
            ps.append(p.astype(BF16))
        o_all = jnp.dot(jnp.concatenate(ps, axis=0), kvb[:, KV_WIDTH:], preferred_element_type=F32)
        o_all = o_all / jnp.concatenate(denoms, axis=0)
        o = _unstack_heads(o_all, BLOCK, lane_lo)
        attn_ref[pl.ds(r0, BLOCK), :] = _rms(o, gattn).astype(BF16)
        return carry

    lax.fori_loop(0, TM_A // BLOCK, block_body, 0)
    kv_scr[0:BLOCK, :] = kv_scr[TM_A:TM_A + BLOCK, :]


def _const_spec(shape):
    zeros = (0,) * len(shape)
    return pl.BlockSpec(shape, lambda *_: zeros, pipeline_mode=pl.Buffered(1))


def _smem_spec():
    return pl.BlockSpec(memory_space=pltpu.SMEM)


def _in_attn(x, rel_bias, sinks, g_mix, w_in, bucket, g_attn):
    bsz, seq, _ = x.shape
    nt = seq // TM_A
    row = lambda width: pl.BlockSpec((None, TM_A, width), lambda b, t: (b, t, 0))
    win_spec = pl.BlockSpec((None, WINDOW, KV_WIDTH), lambda b, t: (b, 0, 0))
    return pl.pallas_call(
        _in_attn_kernel,
        out_shape=(jax.ShapeDtypeStruct((bsz, seq, ATTN_WIDTH), BF16),
                   jax.ShapeDtypeStruct((bsz, seq, SSM_WIDTH), F32),
                   jax.ShapeDtypeStruct((bsz, WINDOW, KV_WIDTH), F32),
                   jax.ShapeDtypeStruct((bsz, WINDOW, KV_WIDTH), F32)),
        grid=(bsz, nt),
        in_specs=[_smem_spec(), _smem_spec(), row(D_MODEL), _const_spec((1, D_MODEL)),
                  _const_spec((D_MODEL, D_IN_PROJ)), _const_spec((BLOCK, 2 * BLOCK)),
                  _const_spec((1, ATTN_WIDTH))],
        out_specs=(row(ATTN_WIDTH), row(SSM_WIDTH), win_spec, win_spec),
        scratch_shapes=[pltpu.VMEM((TM_A + BLOCK, 2 * KV_WIDTH), BF16),
                        pltpu.VMEM((TM_A, ATTN_WIDTH), BF16),
                        pltpu.VMEM((N_Q_HEADS, BLOCK, 2 * BLOCK), F32)],
        compiler_params=pltpu.CompilerParams(
            dimension_semantics=("arbitrary", "arbitrary"), vmem_limit_bytes=VMEM_LIMIT),
        name="in_attn",
    )(rel_bias, sinks, x, g_mix, w_in, bucket, g_attn)


def _s_in_attn_kernel(relb_ref, sinks_ref, x_ref, gmix_ref, win_ref, bucket_ref, gattn_ref,
                      ck_ref, cv_ref,
                      attn_ref, u_ref, newk_ref, newv_ref,
                      bias_scr, biasd_scr):
    steps = x_ref.shape[0] // S_SEQ
    pair = SUBLANES // steps
    prow = N_Q_HEADS * SUBLANES

    @pl.when(pl.program_id(0) == 0)
    def _():
        _build_bias(relb_ref, bucket_ref, bias_scr)
        for h in range(N_Q_HEADS):
            for k in range(pair):
                biasd_scr[h * SUBLANES + k * steps:h * SUBLANES + (k + 1) * steps, :] = bias_scr[h, 0:steps, :]

    xn = _rms(x_ref[...], gmix_ref[...]).astype(BF16)
    proj = jnp.dot(xn, win_ref[...], preferred_element_type=F32)
    u_ref[...] = proj[:, ATTN_WIDTH + 2 * KV_WIDTH:]
    qs = proj[:, :ATTN_WIDTH] * (HEAD_DIM ** -0.5)

    nkeys = 2 * WINDOW
    ri = lax.broadcasted_iota(jnp.int32, (prow, nkeys), 0)
    kj = lax.broadcasted_iota(jnp.int32, (prow, nkeys), 1)
    dist = ri % steps + WINDOW - kj
    valid = jnp.logical_and(dist >= 0, dist < WINDOW)
    head = lax.broadcasted_iota(jnp.int32, (prow, 1), 0) // SUBLANES
    sink = jnp.zeros((prow, 1), F32)
    for h in range(N_Q_HEADS):
        sink = jnp.where(head == h, sinks_ref[0, h], sink)
    seq_in_tile = (lax.broadcasted_iota(jnp.int32, (prow, LANES), 0) % SUBLANES) // steps
    lane_lo = lax.broadcasted_iota(jnp.int32, (SUBLANES, LANES), 1) < HEAD_DIM
    row8 = lax.broadcasted_iota(jnp.int32, (SUBLANES, KV_WIDTH), 0)
    gattn = gattn_ref[...]
    tail = jnp.zeros((nkeys - WINDOW - SUBLANES, KV_WIDTH), F32)
    bias = biasd_scr[...]

    tiles = []
    for tl in range(S_SEQ // pair):
        rws = slice(tl * SUBLANES, (tl + 1) * SUBLANES)
        q2 = _stack_heads(qs[rws], lane_lo).astype(BF16)
        kn8 = proj[rws, ATTN_WIDTH:ATTN_WIDTH + KV_WIDTH]
        vn8 = proj[rws, ATTN_WIDTH + KV_WIDTH:ATTN_WIDTH + 2 * KV_WIDTH]
        o_tile = jnp.zeros((prow, LANES), F32)
        for k in range(pair):
            sq = tl * pair + k
            shift = (SUBLANES - k * steps) % SUBLANES
            kn = jnp.where(row8 < steps, pltpu.roll(kn8, shift, axis=0) if shift else kn8, 0.0)
            vn = jnp.where(row8 < steps, pltpu.roll(vn8, shift, axis=0) if shift else vn8, 0.0)
            ck = ck_ref[sq]
            cv = cv_ref[sq]
            newk_ref[sq, 0:WINDOW - steps, :] = ck[steps:, :]
            newk_ref[sq, WINDOW - steps:, :] = kn[0:steps]
            newv_ref[sq, 0:WINDOW - steps, :] = cv[steps:, :]
            newv_ref[sq, WINDOW - steps:, :] = vn[0:steps]
            keys = jnp.concatenate([ck, kn, tail], axis=0).astype(BF16)
            vals = jnp.concatenate([cv, vn, tail], axis=0).astype(BF16)
            s = lax.dot_general(q2, keys, (((1,), (1,)), ((), ())), preferred_element_type=F32)
            s = jnp.where(valid, s + bias, NEG)
            m = jnp.maximum(jnp.max(s, axis=-1, keepdims=True), sink)
            p = jnp.exp(s - m)
            denom = jnp.sum(p, axis=-1, keepdims=True) + jnp.exp(sink - m)
            o = jnp.dot(p.astype(BF16), vals, preferred_element_type=F32) / denom
            o_tile = jnp.where(seq_in_tile == k, o, o_tile)
        tiles.append(_unstack_heads(o_tile, SUBLANES, lane_lo))
    attn_ref[...] = _rms(jnp.concatenate(tiles, axis=0), gattn).astype(BF16)


def _s_in_attn(x, rel_bias, sinks, g_mix, w_in, bucket, g_attn, cache_k, cache_v):
    nseq, steps, _ = x.shape
    rows = S_SEQ * steps
    xf = x.reshape(nseq * steps, D_MODEL)
    row = lambda width: pl.BlockSpec((rows, width), lambda i: (i, 0))
    cache = pl.BlockSpec((S_SEQ, WINDOW, KV_WIDTH), lambda i: (i, 0, 0))
    return pl.pallas_call(
        _s_in_attn_kernel,
        out_shape=(jax.ShapeDtypeStruct((nseq * steps, ATTN_WIDTH), BF16),
                   jax.ShapeDtypeStruct((nseq * steps, SSM_WIDTH), F32),
                   jax.ShapeDtypeStruct((nseq, WINDOW, KV_WIDTH), F32),
                   jax.ShapeDtypeStruct((nseq, WINDOW, KV_WIDTH), F32)),
        grid=(nseq // S_SEQ,),
        in_specs=[_smem_spec(), _smem_spec(), row(D_MODEL), _const_spec((1, D_MODEL)),
                  _const_spec((D_MODEL, D_IN_PROJ)), _const_spec((BLOCK, 2 * BLOCK)),
                  _const_spec((1, ATTN_WIDTH)), cache, cache],
        out_specs=(row(ATTN_WIDTH), row(SSM_WIDTH), cache, cache),
        scratch_shapes=[pltpu.VMEM((N_Q_HEADS, BLOCK, 2 * BLOCK), F32),
                        pltpu.VMEM((N_Q_HEADS * SUBLANES, 2 * WINDOW), F32)],
        compiler_params=pltpu.CompilerParams(
            dimension_semantics=("arbitrary",), vmem_limit_bytes=VMEM_LIMIT),
        name="s_in_attn",
    )(rel_bias, sinks, xf, g_mix, w_in, bucket, g_attn, cache_k, cache_v)


def _input_drive(up, bre_ref, bim_ref, hre, him):
    upb = up.astype(BF16)
    half = STATE_LANES // 2
    for kt in range(SSM_WIDTH // MXU_DIM):
        lhs = upb[:, kt * MXU_DIM:(kt + 1) * MXU_DIM]
        cols = slice(kt * half, (kt + 1) * half)
        rws = slice(kt * MXU_DIM, (kt + 1) * MXU_DIM)
        hre[:, cols] = jnp.dot(lhs, bre_ref[rws, cols], preferred_element_type=F32)
        him[:, cols] = jnp.dot(lhs, bim_ref[rws, cols], preferred_element_type=F32)


def _ssm_output(hre, him, up, cre_ref, cim_ref, d_ref, wglu_ref, bglu_ref, gssm_ref):
    ys = []
    kblk = STATE_LANES // (SSM_WIDTH // MXU_DIM)
    for n in range(SSM_WIDTH // MXU_DIM):
        rws = slice(n * kblk, (n + 1) * kblk)
        cols = slice(n * MXU_DIM, (n + 1) * MXU_DIM)
        yr = jnp.dot(hre[:, rws].astype(BF16), cre_ref[rws, cols], preferred_element_type=F32)
        yi = jnp.dot(him[:, rws].astype(BF16), cim_ref[rws, cols], preferred_element_type=F32)
        ys.append(yr - yi)
    y = jnp.concatenate(ys, axis=-1) + d_ref[...] * up
    g = jax.nn.gelu(y)
    gate = jax.nn.sigmoid(jnp.dot(g.astype(BF16), wglu_ref[...], preferred_element_type=F32) + bglu_ref[...])
    return _rms(g * gate, gssm_ref[...])


def _scan_kernel(u_ref, lr_ref, li_ref, lcr_ref, lci_ref, bre_ref, bim_ref, cre_ref, cim_ref,
                 d_ref, wglu_ref, bglu_ref, gssm_ref,
                 out_ref, hre_out, him_out,
                 pad_scr, up_scr, hre, him, carry_re, carry_im, e_re, e_im, hc_re, hc_im):
    t = pl.program_id(1)
    nslab = SSM_WIDTH // LANES

    @pl.when(t == 0)
    def _():
        carry_re[...] = jnp.zeros_like(carry_re)
        carry_im[...] = jnp.zeros_like(carry_im)

    for c in range(SUBLANES):
        for s in range(nslab):
            pad_scr[s, c * PITCH:c * PITCH + CHUNK, :] = u_ref[c * CHUNK:(c + 1) * CHUNK, s * LANES:(s + 1) * LANES]
    for tau in range(CHUNK):
        for s in range(nslab):
            up_scr[tau * SUBLANES:(tau + 1) * SUBLANES, s * LANES:(s + 1) * LANES] = (
                pad_scr[s, pl.ds(tau, SUBLANES, stride=PITCH), :])
    up = up_scr[...]
    _input_drive(up, bre_ref, bim_ref, hre, him)

    slab = 4 * LANES

    def recur(store):
        for sl in range(STATE_LANES // slab):
            cols = slice(sl * slab, (sl + 1) * slab)
            ar = jnp.broadcast_to(lr_ref[:, cols], (SUBLANES, slab))
            ai = jnp.broadcast_to(li_ref[:, cols], (SUBLANES, slab))
            if store:
                init = (hc_re[:, cols], hc_im[:, cols])
            else:
                init = (jnp.zeros((SUBLANES, slab), F32), jnp.zeros((SUBLANES, slab), F32))

            def body(tau, h):
                hr, hi = h
                r0 = pl.multiple_of(tau * SUBLANES, SUBLANES)
                br = hre[pl.ds(r0, SUBLANES), cols]
                bi = him[pl.ds(r0, SUBLANES), cols]
                nr = ar * hr - ai * hi + br
                ni = ar * hi + ai * hr + bi
                if store:
                    hre[pl.ds(r0, SUBLANES), cols] = nr
                    him[pl.ds(r0, SUBLANES), cols] = ni
                return nr, ni

            hr, hi = lax.fori_loop(0, CHUNK, body, init, unroll=8)
            if not store:
                e_re[:, cols] = hr
                e_im[:, cols] = hi

    recur(store=False)

    cr = carry_re[...]
    ci = carry_im[...]
    lcr = lcr_ref[...]
    lci = lci_ref[...]
    for c in range(SUBLANES):
        hc_re[c:c + 1, :] = cr
        hc_im[c:c + 1, :] = ci
        er = e_re[c:c + 1, :]
        ei = e_im[c:c + 1, :]
        cr, ci = lcr * cr - lci * ci + er, lcr * ci + lci * cr + ei
    carry_re[...] = cr
    carry_im[...] = ci

    recur(store=True)

    o = _ssm_output(hre, him, up, cre_ref, cim_ref, d_ref, wglu_ref, bglu_ref, gssm_ref)

    for tau in range(CHUNK):
        for s in range(nslab):
            pad_scr[s, pl.ds(tau, SUBLANES, stride=PITCH), :] = (
                o[tau * SUBLANES:(tau + 1) * SUBLANES, s * LANES:(s + 1) * LANES])
    for c in range(SUBLANES):
        for s in range(nslab):
            out_ref[c * CHUNK:(c + 1) * CHUNK, s * LANES:(s + 1) * LANES] = (
                pad_scr[s, c * PITCH:c * PITCH + CHUNK, :].astype(BF16))

    @pl.when(t == pl.num_programs(1) - 1)
    def _():
        hre_out[...] = cr
        him_out[...] = ci


def _scan(u, lam, lam_chunk, bblk, cblk, d, w_glu, b_glu, g_ssm):
    bsz, seq, _ = u.shape
    nt = seq // TM_B
    row = pl.BlockSpec((None, TM_B, SSM_WIDTH), lambda b, t: (b, t, 0))
    state = pl.BlockSpec((None, 1, STATE_LANES), lambda b, t: (b, 0, 0))
    vec = _const_spec((1, STATE_LANES))
    return pl.pallas_call(
        _scan_kernel,
        out_shape=(jax.ShapeDtypeStruct((bsz, seq, SSM_WIDTH), BF16),
                   jax.ShapeDtypeStruct((bsz, 1, STATE_LANES), F32),
                   jax.ShapeDtypeStruct((bsz, 1, STATE_LANES), F32)),
        grid=(bsz, nt),
        in_specs=[row, vec, vec, vec, vec,
                  _const_spec((SSM_WIDTH, STATE_LANES)), _const_spec((SSM_WIDTH, STATE_LANES)),
                  _const_spec((STATE_LANES, SSM_WIDTH)), _const_spec((STATE_LANES, SSM_WIDTH)),
                  _const_spec((1, SSM_WIDTH)), _const_spec((SSM_WIDTH, SSM_WIDTH)),
                  _const_spec((1, SSM_WIDTH)), _const_spec((1, SSM_WIDTH))],
        out_specs=(row, state, state),
        scratch_shapes=[pltpu.VMEM((SSM_WIDTH // LANES, SUBLANES * PITCH, LANES), F32),
                        pltpu.VMEM((TM_B, SSM_WIDTH), F32),
                        pltpu.VMEM((TM_B, STATE_LANES), F32),
                        pltpu.VMEM((TM_B, STATE_LANES), F32),
                        pltpu.VMEM((1, STATE_LANES), F32),
                        pltpu.VMEM((1, STATE_LANES), F32),
                        pltpu.VMEM((SUBLANES, STATE_LANES), F32),
                        pltpu.VMEM((SUBLANES, STATE_LANES), F32),
                        pltpu.VMEM((SUBLANES, STATE_LANES), F32),
                        pltpu.VMEM((SUBLANES, STATE_LANES), F32)],
        compiler_params=pltpu.CompilerParams(
            dimension_semantics=("arbitrary", "arbitrary"), vmem_limit_bytes=VMEM_LIMIT),
        name="scan",
    )(u, lam[0], lam[1], lam_chunk[0], lam_chunk[1], bblk[0], bblk[1], cblk[0], cblk[1],
      d, w_glu, b_glu, g_ssm)


def _s_scan_kernel(u_ref, h0r_ref, h0i_ref, lr_ref, li_ref, bre_ref, bim_ref, cre_ref, cim_ref,
                   d_ref, wglu_ref, bglu_ref, gssm_ref,
                   out_ref, hre_out, him_out,
                   pad_scr, up_scr, hre, him):
    nseq = h0r_ref.shape[0]
    steps = u_ref.shape[0] // nseq
    nslab = SSM_WIDTH // LANES

    for s in range(nslab):
        pad_scr[s] = u_ref[:, s * LANES:(s + 1) * LANES]
    for i in range(steps):
        for s in range(nslab):
            up_scr[i * nseq:(i + 1) * nseq, s * LANES:(s + 1) * LANES] = (
                pad_scr[s, pl.ds(i, nseq, stride=steps), :])
    up = up_scr[...]
    _input_drive(up, bre_ref, bim_ref, hre, him)

    slab = 2 * LANES
    for sl in range(STATE_LANES // slab):
        cols = slice(sl * slab, (sl + 1) * slab)
        ar = lr_ref[:, cols]
        ai = li_ref[:, cols]
        hr = h0r_ref[:, cols]
        hi = h0i_ref[:, cols]
        for i in range(steps):
            rws = slice(i * nseq, (i + 1) * nseq)
            hr, hi = ar * hr - ai * hi + hre[rws, cols], ar * hi + ai * hr + him[rws, cols]
            hre[rws, cols] = hr
            him[rws, cols] = hi
        hre_out[:, cols] = hr
        him_out[:, cols] = hi

    o = _ssm_output(hre, him, up, cre_ref, cim_ref, d_ref, wglu_ref, bglu_ref, gssm_ref)

    for i in range(steps):
        for s in range(nslab):
            pad_scr[s, pl.ds(i, nseq, stride=steps), :] = o[i * nseq:(i + 1) * nseq, s * LANES:(s + 1) * LANES]
    for s in range(nslab):
        out_ref[:, s * LANES:(s + 1) * LANES] = pad_scr[s].astype(BF16)


def _s_scan(u, h0_re, h0_im, lam, bblk, cblk, d, w_glu, b_glu, g_ssm):
    rows = u.shape[0]
    nseq = h0_re.shape[0]
    return pl.pallas_call(
        _s_scan_kernel,
        out_shape=(jax.ShapeDtypeStruct((rows, SSM_WIDTH), BF16),
                   jax.ShapeDtypeStruct((nseq, STATE_LANES), F32),
                   jax.ShapeDtypeStruct((nseq, STATE_LANES), F32)),
        scratch_shapes=[pltpu.VMEM((SSM_WIDTH // LANES, rows, LANES), F32),
                        pltpu.VMEM((rows, SSM_WIDTH), F32),
                        pltpu.VMEM((rows, STATE_LANES), F32),
                        pltpu.VMEM((rows, STATE_LANES), F32)],
        compiler_params=pltpu.CompilerParams(vmem_limit_bytes=VMEM_LIMIT),
        name="s_scan",
    )(u, h0_re, h0_im, lam[0], lam[1], bblk[0], bblk[1], cblk[0], cblk[1], d, w_glu, b_glu, g_ssm)


def _out_mlp_kernel(x_ref, a_ref, s_ref, wout_ref, gmlp_ref, wup_ref, wdown_ref, gfin_ref, y_ref):
    x1 = (x_ref[...]
          + jnp.dot(a_ref[...], wout_ref[0:ATTN_WIDTH, :], preferred_element_type=F32)
          + jnp.dot(s_ref[...], wout_ref[ATTN_WIDTH:, :], preferred_element_type=F32))
    hm = _rms(x1, gmlp_ref[...]).astype(BF16)
    y_ref[...] = x1
    ff_blk = 1024
    for c in range(D_FF // ff_blk):
        cols = slice(c * ff_blk, (c + 1) * ff_blk)
        up = jnp.dot(hm, wup_ref[:, cols], preferred_element_type=F32)
        act = jnp.square(jnp.maximum(up, 0.0)).astype(BF16)
        y_ref[...] += jnp.dot(act, wdown_ref[cols, :], preferred_element_type=F32)
    y_ref[...] = _rms(y_ref[...], gfin_ref[...])


def _out_mlp(x, attn_n, ssm_n, w_out, g_mlp, w_up, w_down, g_fin):
    rows = x.shape[0]
    row = lambda width: pl.BlockSpec((TM_C, width), lambda i: (i, 0))
    return pl.pallas_call(
        _out_mlp_kernel,
        out_shape=jax.ShapeDtypeStruct((rows, D_MODEL), F32),
        grid=(rows // TM_C,),
        in_specs=[row(D_MODEL), row(ATTN_WIDTH), row(SSM_WIDTH),
                  _const_spec((D_MODEL, D_MODEL)), _const_spec((1, D_MODEL)),
                  _const_spec((D_MODEL, D_FF)), _const_spec((D_FF, D_MODEL)),
                  _const_spec((1, D_MODEL))],
        out_specs=row(D_MODEL),
        compiler_params=pltpu.CompilerParams(
            dimension_semantics=("arbitrary",), vmem_limit_bytes=VMEM_LIMIT),
        name="out_mlp",
    )(x, attn_n, ssm_n, w_out, g_mlp, w_up, w_down, g_fin)


def _pair_heads(w, axis):
    shape = w.shape
    split = shape[:axis] + (N_KV_HEADS, Q_PER_KV, HEAD_DIM) + shape[axis + 1:]
    return jnp.swapaxes(w.reshape(split), axis, axis + 1).reshape(shape)


def _block_diag(bbar, c_gpc):
    eye = jnp.eye(N_SSM_GROUPS, dtype=F32)
    bblk = jnp.einsum('gcp,gh->gchp', bbar, eye).reshape(SSM_WIDTH, STATE_LANES).astype(BF16)
    cblk = jnp.einsum('gpc,gh->gphc', c_gpc, eye).reshape(STATE_LANES, SSM_WIDTH).astype(BF16)
    return bblk, cblk


def kernel(x_prompt, x_sample, cache_k, cache_v, state_ssm_re, state_ssm_im, rel_bias, norm_mix, w_in, attn_sinks, ssm_a_re, ssm_a_im, ssm_log_step, ssm_b_re, ssm_b_im, ssm_c_re, ssm_c_im, ssm_d, w_glu, b_glu, norm_attn_out, norm_ssm_out, w_out, norm_mlp, w_up, w_down, norm_final):
    depth = w_in.shape[0]
    bsz, seq, _ = x_prompt.shape
    nseq, steps, _ = x_sample.shape
    bucket = jnp.asarray(_rel_bucket_table())
    g_fin = norm_final.reshape(1, D_MODEL)

    xp, xs = x_prompt, x_sample
    outs = [[] for _ in range(8)]
    for l in range(depth):
        lr, li, lcr, lci, bbr, bbi = _ssm_prep(ssm_a_re[l], ssm_a_im[l], ssm_log_step[l], ssm_b_re[l], ssm_b_im[l])
        lam = (lr.reshape(1, STATE_LANES), li.reshape(1, STATE_LANES))
        lam_chunk = (lcr.reshape(1, STATE_LANES), lci.reshape(1, STATE_LANES))
        c_re_t = jnp.transpose(ssm_c_re[l], (0, 2, 1))
        c_im_t = jnp.transpose(ssm_c_im[l], (0, 2, 1))
        bre, cre = _block_diag(bbr, c_re_t)
        bim, cim = _block_diag(bbi, c_im_t)
        g_mix = norm_mix[l].reshape(1, D_MODEL)
        g_attn = _pair_heads(norm_attn_out[l].reshape(1, ATTN_WIDTH), axis=1)
        g_ssm = norm_ssm_out[l].reshape(1, SSM_WIDTH)
        g_mlp = norm_mlp[l].reshape(1, D_MODEL)
        sinks = attn_sinks[l].reshape(1, N_Q_HEADS)
        d = ssm_d[l].reshape(1, SSM_WIDTH)
        bg = b_glu[l].reshape(1, SSM_WIDTH)
        w_in_b = jnp.concatenate([_pair_heads(w_in[l][:, :ATTN_WIDTH], axis=1), w_in[l][:, ATTN_WIDTH:]],
                                 axis=1).astype(BF16)
        w_glu_b = w_glu[l].astype(BF16)
        w_out_b = jnp.concatenate([_pair_heads(w_out[l][:ATTN_WIDTH], axis=0), w_out[l][ATTN_WIDTH:]],
                                  axis=0).astype(BF16)
        w_up_b = w_up[l].astype(BF16)
        w_down_b = w_down[l].astype(BF16)
        last = l == depth - 1
        g_last = g_fin if last else jnp.ones_like(g_fin)
        assert last, "the final norm is fused into the last layer's MLP kernel"

        attn_n, u, k1, v1 = _in_attn(xp, rel_bias, sinks, g_mix, w_in_b, bucket, g_attn)
        ssm_n, r1, i1 = _scan(u, lam, lam_chunk, (bre, bim), (cre, cim), d, w_glu_b, bg, g_ssm)
        xp = _out_mlp(xp.reshape(bsz * seq, D_MODEL), attn_n.reshape(bsz * seq, ATTN_WIDTH),
                      ssm_n.reshape(bsz * seq, SSM_WIDTH), w_out_b, g_mlp, w_up_b, w_down_b, g_last
                      ).reshape(bsz, seq, D_MODEL)

        ck = cache_k[l].reshape(nseq, WINDOW, KV_WIDTH)
        cv = cache_v[l].reshape(nseq, WINDOW, KV_WIDTH)
        s_attn, s_u, k2, v2 = _s_in_attn(xs, rel_bias, sinks, g_mix, w_in_b, bucket, g_attn, ck, cv)
        s_ssm, r2, i2 = _s_scan(s_u, state_ssm_re[l].reshape(nseq, STATE_LANES),
                                state_ssm_im[l].reshape(nseq, STATE_LANES),
                                lam, (bre, bim), (cre, cim), d, w_glu_b, bg, g_ssm)
        xs = _out_mlp(xs.reshape(nseq * steps, D_MODEL), s_attn, s_ssm, w_out_b, g_mlp, w_up_b, w_down_b,
                      g_last).reshape(nseq, steps, D_MODEL)

        kv_shape = (WINDOW, N_KV_HEADS, HEAD_DIM)
        st_shape = (N_SSM_GROUPS, SSM_STATE)
        for lst, val in zip(outs, (k1.reshape(bsz, *kv_shape), v1.reshape(bsz, *kv_shape),
                                   r1.reshape(bsz, *st_shape), i1.reshape(bsz, *st_shape),
                                   k2.reshape(nseq, *kv_shape), v2.reshape(nseq, *kv_shape),
                                   r2.reshape(nseq, *st_shape), i2.reshape(nseq, *st_shape))):
            lst.append(val)

    return (xp, xs) + tuple(jnp.stack(o) for o in outs)
```

```python
import functools
import math

import numpy as np
import jax
import jax.numpy as jnp
from jax import lax
from jax.experimental import pallas as pl
from jax.experimental.pallas import tpu as pltpu

D_MODEL = 1024
HEAD_DIM = 64
ATTN_WIDTH = 512
N_Q_HEADS = 8
N_KV_HEADS = 2
Q_PER_KV = 4
KV_WIDTH = 128
WINDOW = 128
BLOCK = 128
N_BUCKETS = 32
MAX_DISTANCE = 128
SSM_WIDTH = 512
SSM_GROUP = 16
N_SSM_GROUPS = 32
SSM_STATE = 64
STATE_LANES = N_SSM_GROUPS * SSM_STATE
D_FF = 4096
D_IN_PROJ = 1280
EPS = 1e-6
NEG = -1e30

LANES = 128
SUBLANES = 8
MXU_DIM = 256
VMEM_LIMIT = 56 * 1024 * 1024

F32 = jnp.float32
BF16 = jnp.bfloat16

TM_A = 512
TM_C = 512
TM_B = 512
CHUNK = TM_B // SUBLANES
PITCH = CHUNK + SUBLANES
S_SEQ = 16


def _rms(x, g):
    return x * lax.rsqrt(jnp.mean(x * x, axis=-1, keepdims=True) + EPS) * g


def _rel_bucket_table():
    i = np.arange(BLOCK)[:, None]
    j = np.arange(2 * BLOCK)[None, :]
    n = np.maximum(i + BLOCK - j, 0)
    max_exact = N_BUCKETS // 2
    nf = np.maximum(n, max_exact).astype(np.float32)
    large = max_exact + (np.log(nf / np.float32(max_exact)) / np.float32(math.log(MAX_DISTANCE / max_exact))
                         * np.float32(N_BUCKETS - max_exact)).astype(np.int32)
    large = np.minimum(large, N_BUCKETS - 1)
    return np.where(n < max_exact, n, large).astype(np.int32)


def _build_bias(relb_ref, bucket_ref, bias_scr):
    bucket = bucket_ref[...]
    for h in range(N_Q_HEADS):
        bias_scr[h] = jnp.zeros((BLOCK, 2 * BLOCK), F32)
    for b in range(N_BUCKETS):
        hit = bucket == b
        for h in range(N_Q_HEADS):
            bias_scr[h] = jnp.where(hit, relb_ref[b, h], bias_scr[h])


def _build_bias_t(relb_ref, bucket_t_ref, bias_t_scr):
    bucket_t = bucket_t_ref[...]
    bias_t_scr[...] = jnp.zeros_like(bias_t_scr)
    for b in range(N_BUCKETS):
        hit = bucket_t == b
        for h in range(N_Q_HEADS):
            cols = slice(h * BLOCK, (h + 1) * BLOCK)
            bias_t_scr[:, cols] = jnp.where(hit, relb_ref[b, h], bias_t_scr[:, cols])


def _ssm_prep_kernel(are_ref, aim_ref, ls_ref, btr_ref, bti_ref,
                     lr_ref, li_ref, lcr_ref, lci_ref, bbr_ref, bbi_ref):
    ar = are_ref[...]
    ai = aim_ref[...]
    step = jnp.exp(ls_ref[...])
    zr = ar * step
    zi = ai * step
    mag = jnp.exp(zr)
    lr = mag * jnp.cos(zi)
    li = mag * jnp.sin(zi)
    lr_ref[...] = lr
    li_ref[...] = li
    pr, pi = lr, li
    for _ in range(int(math.log2(CHUNK))):
        pr, pi = pr * pr - pi * pi, 2.0 * pr * pi
    lcr_ref[...] = pr
    lci_ref[...] = pi
    nr = lr - 1.0
    ni = li
    den = ar * ar + ai * ai
    cr = (nr * ar + ni * ai) / den
    ci = (ni * ar - nr * ai) / den
    btr = btr_ref[...]
    bti = bti_ref[...]
    bbr_ref[...] = cr * btr - ci * bti
    bbi_ref[...] = cr * bti + ci * btr


def _ssm_prep(a_re, a_im, log_step, b_re, b_im):
    g, p, c = N_SSM_GROUPS, SSM_STATE, SSM_GROUP
    vec = jax.ShapeDtypeStruct((g, 1, p), F32)
    mat = jax.ShapeDtypeStruct((g, c, p), F32)
    return pl.pallas_call(
        _ssm_prep_kernel,
        out_shape=(vec, vec, vec, vec, mat, mat),
        name="ssm_prep",
    )(a_re.reshape(g, 1, p), a_im.reshape(g, 1, p), log_step.reshape(g, 1, 1),
      jnp.transpose(b_re, (0, 2, 1)), jnp.transpose(b_im, (0, 2, 1)))


def _stack_heads(q_blk, lane_lo):
    zero = jnp.zeros_like(q_blk[:, 0:LANES])
    lo, hi = [], []
    for c in range(ATTN_WIDTH // LANES):
        blk = q_blk[:, c * LANES:(c + 1) * LANES]
        lo.append(jnp.where(lane_lo, blk, zero))
        hi.append(jnp.where(lane_lo, zero, blk))
    return jnp.concatenate(lo + hi, axis=0)


def _unstack_heads(o, rows, lane_lo):
    nblk = ATTN_WIDTH // LANES
    return jnp.concatenate(
        [jnp.where(lane_lo, o[c * rows:(c + 1) * rows], o[(c + nblk) * rows:(c + nblk + 1) * rows])
         for c in range(nblk)], axis=-1)


def _in_attn_kernel(relb_ref, sinks_ref, x_ref, gmix_ref, win_ref, bucket_t_ref, gattn_ref,
                    attn_ref, u_ref, newk_ref, newv_ref,
                    k_scr, vt_scr, q_scr, bias_t_scr, gt_scr):
    b = pl.program_id(0)
    t = pl.program_id(1)
    nt = pl.num_programs(1)
    nstk = N_Q_HEADS * BLOCK

    @pl.when(jnp.logical_and(b == 0, t == 0))
    def _():
        _build_bias_t(relb_ref, bucket_t_ref, bias_t_scr)
        gt_scr[...] = jnp.broadcast_to(gattn_ref[...], (ATTN_WIDTH, BLOCK))

    @pl.when(t == 0)
    def _():
        k_scr[0:BLOCK, :] = jnp.zeros((BLOCK, KV_WIDTH), BF16)
        vt_scr[:, 0:BLOCK] = jnp.zeros((KV_WIDTH, BLOCK), BF16)

    xn = _rms(x_ref[...], gmix_ref[...]).astype(BF16)
    proj = jnp.dot(xn, win_ref[...], preferred_element_type=F32)
    q_scr[...] = (proj[:, :ATTN_WIDTH] * (HEAD_DIM ** -0.5)).astype(BF16)
    v_new = proj[:, ATTN_WIDTH + KV_WIDTH:ATTN_WIDTH + 2 * KV_WIDTH]
    k_scr[BLOCK:BLOCK + TM_A, :] = proj[:, ATTN_WIDTH:ATTN_WIDTH + KV_WIDTH].astype(BF16)
    vt_scr[:, BLOCK:BLOCK + TM_A] = v_new.T.astype(BF16)
    u_ref[...] = proj[:, ATTN_WIDTH + 2 * KV_WIDTH:]

    @pl.when(t == nt - 1)
    def _():
        newk_ref[...] = proj[TM_A - WINDOW:, ATTN_WIDTH:ATTN_WIDTH + KV_WIDTH]
        newv_ref[...] = v_new[TM_A - WINDOW:, :]

    kj = lax.broadcasted_iota(jnp.int32, (2 * BLOCK, nstk), 0)
    qi = lax.broadcasted_iota(jnp.int32, (2 * BLOCK, nstk), 1) % BLOCK
    dist = qi + BLOCK - kj
    in_band = jnp.logical_and(dist >= 0, dist < WINDOW)
    in_band_first = jnp.logical_and(in_band, jnp.logical_or(kj >= BLOCK, t > 0))
    lane_lo = lax.broadcasted_iota(jnp.int32, (BLOCK, LANES), 1) < HEAD_DIM
    row_lo = lax.broadcasted_iota(jnp.int32, (KV_WIDTH, BLOCK), 0) < HEAD_DIM
    head = lax.broadcasted_iota(jnp.int32, (1, nstk), 1) // BLOCK
    sink = jnp.zeros((1, nstk), F32)
    for h in range(N_Q_HEADS):
        sink = jnp.where(head == h, sinks_ref[0, h], sink)
    nblk = ATTN_WIDTH // LANES

    for j in range(TM_A // BLOCK):
        r0 = j * BLOCK
        q2 = _stack_heads(q_scr[r0:r0 + BLOCK, :], lane_lo)
        keys = k_scr[r0:r0 + 2 * BLOCK, :]
        s = lax.dot_general(keys, q2, (((1,), (1,)), ((), ())), preferred_element_type=F32)
        s = jnp.where(in_band_first if j == 0 else in_band, s + bias_t_scr[...], NEG)
        m = jnp.maximum(jnp.max(s, axis=0, keepdims=True), sink)
        p = jnp.exp(s - m)
        denom = jnp.sum(p, axis=0, keepdims=True) + jnp.exp(sink - m)
        o = jnp.dot(vt_scr[:, r0:r0 + 2 * BLOCK], p.astype(BF16), preferred_element_type=F32) / denom
        a = jnp.concatenate(
            [jnp.where(row_lo, o[:, c * BLOCK:(c + 1) * BLOCK], o[:, (c + nblk) * BLOCK:(c + nblk + 1) * BLOCK])
             for c in range(nblk)], axis=0)
        a = a * lax.rsqrt(jnp.mean(a * a, axis=0, keepdims=True) + EPS) * gt_scr[...]
        attn_ref[r0:r0 + BLOCK, :] = a.T.astype(BF16)

    k_scr[0:BLOCK, :] = k_scr[TM_A:TM_A + BLOCK, :]
    vt_scr[:, 0:BLOCK] = vt_scr[:, TM_A:TM_A + BLOCK]


def _const_spec(shape):
    zeros = (0,) * len(shape)
    return pl.BlockSpec(shape, lambda *_: zeros, pipeline_mode=pl.Buffered(1))


def _smem_spec():
    return pl.BlockSpec(memory_space=pltpu.SMEM)


def _in_attn(x, rel_bias, sinks, g_mix, w_in, bucket_t, g_attn_col):
    bsz, seq, _ = x.shape
    nt = seq // TM_A
    row = lambda width: pl.BlockSpec((None, TM_A, width), lambda b, t: (b, t, 0))
    win_spec = pl.BlockSpec((None, WINDOW, KV_WIDTH), lambda b, t: (b, 0, 0))
    return pl.pallas_call(
        _in_attn_kernel,
        out_shape=(jax.ShapeDtypeStruct((bsz, seq, ATTN_WIDTH), BF16),
                   jax.ShapeDtypeStruct((bsz, seq, SSM_WIDTH), F32),
                   jax.ShapeDtypeStruct((bsz, WINDOW, KV_WIDTH), F32),
                   jax.ShapeDtypeStruct((bsz, WINDOW, KV_WIDTH), F32)),
        grid=(bsz, nt),
        in_specs=[_smem_spec(), _smem_spec(), row(D_MODEL), _const_spec((1, D_MODEL)),
                  _const_spec((D_MODEL, D_IN_PROJ)), _const_spec((2 * BLOCK, BLOCK)),
                  _const_spec((ATTN_WIDTH, 1))],
        out_specs=(row(ATTN_WIDTH), row(SSM_WIDTH), win_spec, win_spec),
        scratch_shapes=[pltpu.VMEM((TM_A + BLOCK, KV_WIDTH), BF16),
                        pltpu.VMEM((KV_WIDTH, TM_A + BLOCK), BF16),
                        pltpu.VMEM((TM_A, ATTN_WIDTH), BF16),
                        pltpu.VMEM((2 * BLOCK, N_Q_HEADS * BLOCK), F32),
                        pltpu.VMEM((ATTN_WIDTH, BLOCK), F32)],
        compiler_params=pltpu.CompilerParams(
            dimension_semantics=("arbitrary", "arbitrary"), vmem_limit_bytes=VMEM_LIMIT),
        name="in_attn",
    )(rel_bias, sinks, x, g_mix, w_in, bucket_t, g_attn_col)


def _s_in_attn_kernel(relb_ref, sinks_ref, x_ref, gmix_ref, win_ref, bucket_ref, gattn_ref,
                      ck_ref, cv_ref,
                      attn_ref, u_ref, newk_ref, newv_ref,
                      bias_scr, biasd_scr):
    steps = x_ref.shape[0] // S_SEQ
    pair = SUBLANES // steps
    prow = N_Q_HEADS * SUBLANES

    @pl.when(pl.program_id(0) == 0)
    def _():
        _build_bias(relb_ref, bucket_ref, bias_scr)
        for h in range(N_Q_HEADS):
            for k in range(pair):
                biasd_scr[h * SUBLANES + k * steps:h * SUBLANES + (k + 1) * steps, :] = bias_scr[h, 0:steps, :]

    xn = _rms(x_ref[...], gmix_ref[...]).astype(BF16)
    proj = jnp.dot(xn, win_ref[...], preferred_element_type=F32)
    u_ref[...] = proj[:, ATTN_WIDTH + 2 * KV_WIDTH:]
    qs = proj[:, :ATTN_WIDTH] * (HEAD_DIM ** -0.5)

    nkeys = 2 * WINDOW
    ri = lax.broadcasted_iota(jnp.int32, (prow, nkeys), 0)
    kj = lax.broadcasted_iota(jnp.int32, (prow, nkeys), 1)
    dist = ri % steps + WINDOW - kj
    valid = jnp.logical_and(dist >= 0, dist < WINDOW)
    head = lax.broadcasted_iota(jnp.int32, (prow, 1), 0) // SUBLANES
    sink = jnp.zeros((prow, 1), F32)
    for h in range(N_Q_HEADS):
        sink = jnp.where(head == h, sinks_ref[0, h], sink)
    seq_in_tile = (lax.broadcasted_iota(jnp.int32, (prow, LANES), 0) % SUBLANES) // steps
    lane_lo = lax.broadcasted_iota(jnp.int32, (SUBLANES, LANES), 1) < HEAD_DIM
    row8 = lax.broadcasted_iota(jnp.int32, (SUBLANES, KV_WIDTH), 0)
    gattn = gattn_ref[...]
    tail = jnp.zeros((nkeys - WINDOW - SUBLANES, KV_WIDTH), F32)
    bias = biasd_scr[...]

    tiles = []
    for tl in range(S_SEQ // pair):
        rws = slice(tl * SUBLANES, (tl + 1) * SUBLANES)
        q2 = _stack_heads(qs[rws], lane_lo).astype(BF16)
        kn8 = proj[rws, ATTN_WIDTH:ATTN_WIDTH + KV_WIDTH]
        vn8 = proj[rws, ATTN_WIDTH + KV_WIDTH:ATTN_WIDTH + 2 * KV_WIDTH]
        o_tile = jnp.zeros((prow, LANES), F32)
        for k in range(pair):
            sq = tl * pair + k
            shift = (SUBLANES - k * steps) % SUBLANES
            kn = jnp.where(row8 < steps, pltpu.roll(kn8, shift, axis=0) if shift else kn8, 0.0)
            vn = jnp.where(row8 < steps, pltpu.roll(vn8, shift, axis=0) if shift else vn8, 0.0)
            ck = ck_ref[sq]
            cv = cv_ref[sq]
            newk_ref[sq, 0:WINDOW - steps, :] = ck[steps:, :]
            newk_ref[sq, WINDOW - steps:, :] = kn[0:steps]
            newv_ref[sq, 0:WINDOW - steps, :] = cv[steps:, :]
            newv_ref[sq, WINDOW - steps:, :] = vn[0:steps]
            keys = jnp.concatenate([ck, kn, tail], axis=0).astype(BF16)
            vals = jnp.concatenate([cv, vn, tail], axis=0).astype(BF16)
            s = lax.dot_general(q2, keys, (((1,), (1,)), ((), ())), preferred_element_type=F32)
            s = jnp.where(valid, s + bias, NEG)
            m = jnp.maximum(jnp.max(s, axis=-1, keepdims=True), sink)
            p = jnp.exp(s - m)
            denom = jnp.sum(p, axis=-1, keepdims=True) + jnp.exp(sink - m)
            o = jnp.dot(p.astype(BF16), vals, preferred_element_type=F32) / denom
            o_tile = jnp.where(seq_in_tile == k, o, o_tile)
        tiles.append(_unstack_heads(o_tile, SUBLANES, lane_lo))
    attn_ref[...] = _rms(jnp.concatenate(tiles, axis=0), gattn).astype(BF16)


def _s_in_attn(x, rel_bias, sinks, g_mix, w_in, bucket, g_attn, cache_k, cache_v):
    nseq, steps, _ = x.shape
    rows = S_SEQ * steps
    xf = x.reshape(nseq * steps, D_MODEL)
    row = lambda width: pl.BlockSpec((rows, width), lambda i: (i, 0))
    cache = pl.BlockSpec((S_SEQ, WINDOW, KV_WIDTH), lambda i: (i, 0, 0))
    return pl.pallas_call(
        _s_in_attn_kernel,
        out_shape=(jax.ShapeDtypeStruct((nseq * steps, ATTN_WIDTH), BF16),
                   jax.ShapeDtypeStruct((nseq * steps, SSM_WIDTH), F32),
                   jax.ShapeDtypeStruct((nseq, WINDOW, KV_WIDTH), F32),
                   jax.ShapeDtypeStruct((nseq, WINDOW, KV_WIDTH), F32)),
        grid=(nseq // S_SEQ,),
        in_specs=[_smem_spec(), _smem_spec(), row(D_MODEL), _const_spec((1, D_MODEL)),
                  _const_spec((D_MODEL, D_IN_PROJ)), _const_spec((BLOCK, 2 * BLOCK)),
                  _const_spec((1, ATTN_WIDTH)), cache, cache],
        out_specs=(row(ATTN_WIDTH), row(SSM_WIDTH), cache, cache),
        scratch_shapes=[pltpu.VMEM((N_Q_HEADS, BLOCK, 2 * BLOCK), F32),
                        pltpu.VMEM((N_Q_HEADS * SUBLANES, 2 * WINDOW), F32)],
        compiler_params=pltpu.CompilerParams(
            dimension_semantics=("arbitrary",), vmem_limit_bytes=VMEM_LIMIT),
        name="s_in_attn",
    )(rel_bias, sinks, xf, g_mix, w_in, bucket, g_attn, cache_k, cache_v)


def _input_drive(up, bre_ref, bim_ref, hre, him):
    upb = up.astype(BF16)
    half = STATE_LANES // 2
    for kt in range(SSM_WIDTH // MXU_DIM):
        lhs = upb[:, kt * MXU_DIM:(kt + 1) * MXU_DIM]
        cols = slice(kt * half, (kt + 1) * half)
        rws = slice(kt * MXU_DIM, (kt + 1) * MXU_DIM)
        hre[:, cols] = jnp.dot(lhs, bre_ref[rws, cols], preferred_element_type=F32)
        him[:, cols] = jnp.dot(lhs, bim_ref[rws, cols], preferred_element_type=F32)


def _ssm_output(hre, him, up, cre_ref, cim_ref, d_ref, wglu_ref, bglu_ref, gssm_ref):
    ys = []
    kblk = STATE_LANES // (SSM_WIDTH // MXU_DIM)
    for n in range(SSM_WIDTH // MXU_DIM):
        rws = slice(n * kblk, (n + 1) * kblk)
        cols = slice(n * MXU_DIM, (n + 1) * MXU_DIM)
        yr = jnp.dot(hre[:, rws].astype(BF16), cre_ref[rws, cols], preferred_element_type=F32)
        yi = jnp.dot(him[:, rws].astype(BF16), cim_ref[rws, cols], preferred_element_type=F32)
        ys.append(yr - yi)
    y = jnp.concatenate(ys, axis=-1) + d_ref[...] * up
    g = jax.nn.gelu(y)
    gate = jax.nn.sigmoid(jnp.dot(g.astype(BF16), wglu_ref[...], preferred_element_type=F32) + bglu_ref[...])
    return _rms(g * gate, gssm_ref[...])


def _scan_kernel(u_ref, lr_ref, li_ref, lcr_ref, lci_ref, bre_ref, bim_ref, cre_ref, cim_ref,
                 d_ref, wglu_ref, bglu_ref, gssm_ref,
                 out_ref, hre_out, him_out,
                 pad_scr, up_scr, hre, him, carry_re, carry_im, e_re, e_im, hc_re, hc_im):
    t = pl.program_id(1)
    nslab = SSM_WIDTH // LANES

    @pl.when(t == 0)
    def _():
        carry_re[...] = jnp.zeros_like(carry_re)
        carry_im[...] = jnp.zeros_like(carry_im)

    for c in range(SUBLANES):
        for s in range(nslab):
            pad_scr[s, c * PITCH:c * PITCH + CHUNK, :] = u_ref[c * CHUNK:(c + 1) * CHUNK, s * LANES:(s + 1) * LANES]
    for tau in range(CHUNK):
        for s in range(nslab):
            up_scr[tau * SUBLANES:(tau + 1) * SUBLANES, s * LANES:(s + 1) * LANES] = (
                pad_scr[s, pl.ds(tau, SUBLANES, stride=PITCH), :])
    up = up_scr[...]
    _input_drive(up, bre_ref, bim_ref, hre, him)

    slab = 4 * LANES

    def recur(store):
        for sl in range(STATE_LANES // slab):
            cols = slice(sl * slab, (sl + 1) * slab)
            ar = jnp.broadcast_to(lr_ref[:, cols], (SUBLANES, slab))
            ai = jnp.broadcast_to(li_ref[:, cols], (SUBLANES, slab))
            if store:
                init = (hc_re[:, cols], hc_im[:, cols])
            else:
                init = (jnp.zeros((SUBLANES, slab), F32), jnp.zeros((SUBLANES, slab), F32))

            def body(tau, h):
                hr, hi = h
                r0 = pl.multiple_of(tau * SUBLANES, SUBLANES)
                br = hre[pl.ds(r0, SUBLANES), cols]
                bi = him[pl.ds(r0, SUBLANES), cols]
                nr = ar * hr - ai * hi + br
                ni = ar * hi + ai * hr + bi
                if store:
                    hre[pl.ds(r0, SUBLANES), cols] = nr
                    him[pl.ds(r0, SUBLANES), cols] = ni
                return nr, ni

            hr, hi = lax.fori_loop(0, CHUNK, body, init, unroll=8)
            if not store:
                e_re[:, cols] = hr
                e_im[:, cols] = hi

    recur(store=False)

    cr = carry_re[...]
    ci = carry_im[...]
    lcr = lcr_ref[...]
    lci = lci_ref[...]
    for c in range(SUBLANES):
        hc_re[c:c + 1, :] = cr
        hc_im[c:c + 1, :] = ci
        er = e_re[c:c + 1, :]
        ei = e_im[c:c + 1, :]
        cr, ci = lcr * cr - lci * ci + er, lcr * ci + lci * cr + ei
    carry_re[...] = cr
    carry_im[...] = ci

    recur(store=True)

    o = _ssm_output(hre, him, up, cre_ref, cim_ref, d_ref, wglu_ref, bglu_ref, gssm_ref)

    for tau in range(CHUNK):
        for s in range(nslab):
            pad_scr[s, pl.ds(tau, SUBLANES, stride=PITCH), :] = (
                o[tau * SUBLANES:(tau + 1) * SUBLANES, s * LANES:(s + 1) * LANES])
    for c in range(SUBLANES):
        for s in range(nslab):
            out_ref[c * CHUNK:(c + 1) * CHUNK, s * LANES:(s + 1) * LANES] = (
                pad_scr[s, c * PITCH:c * PITCH + CHUNK, :].astype(BF16))

    @pl.when(t == pl.num_programs(1) - 1)
    def _():
        hre_out[...] = cr
        him_out[...] = ci


def _scan(u, lam, lam_chunk, bblk, cblk, d, w_glu, b_glu, g_ssm):
    bsz, seq, _ = u.shape
    nt = seq // TM_B
    row = pl.BlockSpec((None, TM_B, SSM_WIDTH), lambda b, t: (b, t, 0))
    state = pl.BlockSpec((None, 1, STATE_LANES), lambda b, t: (b, 0, 0))
    vec = _const_spec((1, STATE_LANES))
    return pl.pallas_call(
        _scan_kernel,
        out_shape=(jax.ShapeDtypeStruct((bsz, seq, SSM_WIDTH), BF16),
                   jax.ShapeDtypeStruct((bsz, 1, STATE_LANES), F32),
                   jax.ShapeDtypeStruct((bsz, 1, STATE_LANES), F32)),
        grid=(bsz, nt),
        in_specs=[row, vec, vec, vec, vec,
                  _const_spec((SSM_WIDTH, STATE_LANES)), _const_spec((SSM_WIDTH, STATE_LANES)),
                  _const_spec((STATE_LANES, SSM_WIDTH)), _const_spec((STATE_LANES, SSM_WIDTH)),
                  _const_spec((1, SSM_WIDTH)), _const_spec((SSM_WIDTH, SSM_WIDTH)),
                  _const_spec((1, SSM_WIDTH)), _const_spec((1, SSM_WIDTH))],
        out_specs=(row, state, state),
        scratch_shapes=[pltpu.VMEM((SSM_WIDTH // LANES, SUBLANES * PITCH, LANES), F32),
                        pltpu.VMEM((TM_B, SSM_WIDTH), F32),
                        pltpu.VMEM((TM_B, STATE_LANES), F32),
                        pltpu.VMEM((TM_B, STATE_LANES), F32),
                        pltpu.VMEM((1, STATE_LANES), F32),
                        pltpu.VMEM((1, STATE_LANES), F32),
                        pltpu.VMEM((SUBLANES, STATE_LANES), F32),
                        pltpu.VMEM((SUBLANES, STATE_LANES), F32),
                        pltpu.VMEM((SUBLANES, STATE_LANES), F32),
                        pltpu.VMEM((SUBLANES, STATE_LANES), F32)],
        compiler_params=pltpu.CompilerParams(
            dimension_semantics=("arbitrary", "arbitrary"), vmem_limit_bytes=VMEM_LIMIT),
        name="scan",
    )(u, lam[0], lam[1], lam_chunk[0], lam_chunk[1], bblk[0], bblk[1], cblk[0], cblk[1],
      d, w_glu, b_glu, g_ssm)


def _s_scan_kernel(u_ref, h0r_ref, h0i_ref, lr_ref, li_ref, bre_ref, bim_ref, cre_ref, cim_ref,
                   d_ref, wglu_ref, bglu_ref, gssm_ref,
                   out_ref, hre_out, him_out,
                   pad_scr, up_scr, hre, him):
    nseq = h0r_ref.shape[0]
    steps = u_ref.shape[0] // nseq
    nslab = SSM_WIDTH // LANES

    for s in range(nslab):
        pad_scr[s] = u_ref[:, s * LANES:(s + 1) * LANES]
    for i in range(steps):
        for s in range(nslab):
            up_scr[i * nseq:(i + 1) * nseq, s * LANES:(s + 1) * LANES] = (
                pad_scr[s, pl.ds(i, nseq, stride=steps), :])
    up = up_scr[...]
    _input_drive(up, bre_ref, bim_ref, hre, him)

    slab = 2 * LANES
    for sl in range(STATE_LANES // slab):
        cols = slice(sl * slab, (sl + 1) * slab)
        ar = lr_ref[:, cols]
        ai = li_ref[:, cols]
        hr = h0r_ref[:, cols]
        hi = h0i_ref[:, cols]
        for i in range(steps):
            rws = slice(i * nseq, (i + 1) * nseq)
            hr, hi = ar * hr - ai * hi + hre[rws, cols], ar * hi + ai * hr + him[rws, cols]
            hre[rws, cols] = hr
            him[rws, cols] = hi
        hre_out[:, cols] = hr
        him_out[:, cols] = hi

    o = _ssm_output(hre, him, up, cre_ref, cim_ref, d_ref, wglu_ref, bglu_ref, gssm_ref)

    for i in range(steps):
        for s in range(nslab):
            pad_scr[s, pl.ds(i, nseq, stride=steps), :] = o[i * nseq:(i + 1) * nseq, s * LANES:(s + 1) * LANES]
    for s in range(nslab):
        out_ref[:, s * LANES:(s + 1) * LANES] = pad_scr[s].astype(BF16)


def _s_scan(u, h0_re, h0_im, lam, bblk, cblk, d, w_glu, b_glu, g_ssm):
    rows = u.shape[0]
    nseq = h0_re.shape[0]
    return pl.pallas_call(
        _s_scan_kernel,
        out_shape=(jax.ShapeDtypeStruct((rows, SSM_WIDTH), BF16),
                   jax.ShapeDtypeStruct((nseq, STATE_LANES), F32),
                   jax.ShapeDtypeStruct((nseq, STATE_LANES), F32)),
        scratch_shapes=[pltpu.VMEM((SSM_WIDTH // LANES, rows, LANES), F32),
                        pltpu.VMEM((rows, SSM_WIDTH), F32),
                        pltpu.VMEM((rows, STATE_LANES), F32),
                        pltpu.VMEM((rows, STATE_LANES), F32)],
        compiler_params=pltpu.CompilerParams(vmem_limit_bytes=VMEM_LIMIT),
        name="s_scan",
    )(u, h0_re, h0_im, lam[0], lam[1], bblk[0], bblk[1], cblk[0], cblk[1], d, w_glu, b_glu, g_ssm)


def _out_mlp_kernel(x_ref, a_ref, s_ref, wout_ref, gmlp_ref, wup_ref, wdown_ref, gfin_ref, y_ref):
    x1 = (x_ref[...]
          + jnp.dot(a_ref[...], wout_ref[0:ATTN_WIDTH, :], preferred_element_type=F32)
          + jnp.dot(s_ref[...], wout_ref[ATTN_WIDTH:, :], preferred_element_type=F32))
    hm = _rms(x1, gmlp_ref[...]).astype(BF16)
    y_ref[...] = x1
    ff_blk = 1024
    for c in range(D_FF // ff_blk):
        cols = slice(c * ff_blk, (c + 1) * ff_blk)
        up = jnp.dot(hm, wup_ref[:, cols], preferred_element_type=F32)
        act = jnp.square(jnp.maximum(up, 0.0)).astype(BF16)
        y_ref[...] += jnp.dot(act, wdown_ref[cols, :], preferred_element_type=F32)
    y_ref[...] = _rms(y_ref[...], gfin_ref[...])


def _out_mlp(x, attn_n, ssm_n, w_out, g_mlp, w_up, w_down, g_fin):
    rows = x.shape[0]
    row = lambda width: pl.BlockSpec((TM_C, width), lambda i: (i, 0))
    return pl.pallas_call(
        _out_mlp_kernel,
        out_shape=jax.ShapeDtypeStruct((rows, D_MODEL), F32),
        grid=(rows // TM_C,),
        in_specs=[row(D_MODEL), row(ATTN_WIDTH), row(SSM_WIDTH),
                  _const_spec((D_MODEL, D_MODEL)), _const_spec((1, D_MODEL)),
                  _const_spec((D_MODEL, D_FF)), _const_spec((D_FF, D_MODEL)),
                  _const_spec((1, D_MODEL))],
        out_specs=row(D_MODEL),
        compiler_params=pltpu.CompilerParams(
            dimension_semantics=("arbitrary",), vmem_limit_bytes=VMEM_LIMIT),
        name="out_mlp",
    )(x, attn_n, ssm_n, w_out, g_mlp, w_up, w_down, g_fin)


def _pair_heads(w, axis):
    shape = w.shape
    split = shape[:axis] + (N_KV_HEADS, Q_PER_KV, HEAD_DIM) + shape[axis + 1:]
    return jnp.swapaxes(w.reshape(split), axis, axis + 1).reshape(shape)


def _block_diag(bbar, c_gpc):
    eye = jnp.eye(N_SSM_GROUPS, dtype=F32)
    bblk = jnp.einsum('gcp,gh->gchp', bbar, eye).reshape(SSM_WIDTH, STATE_LANES).astype(BF16)
    cblk = jnp.einsum('gpc,gh->gphc', c_gpc, eye).reshape(STATE_LANES, SSM_WIDTH).astype(BF16)
    return bblk, cblk


def kernel(x_prompt, x_sample, cache_k, cache_v, state_ssm_re, state_ssm_im, rel_bias, norm_mix, w_in, attn_sinks, ssm_a_re, ssm_a_im, ssm_log_step, ssm_b_re, ssm_b_im, ssm_c_re, ssm_c_im, ssm_d, w_glu, b_glu, norm_attn_out, norm_ssm_out, w_out, norm_mlp, w_up, w_down, norm_final):
    depth = w_in.shape[0]
    bsz, seq, _ = x_prompt.shape
    nseq, steps, _ = x_sample.shape
    bucket = jnp.asarray(_rel_bucket_table())
    g_fin = norm_final.reshape(1, D_MODEL)

    xp, xs = x_prompt, x_sample
    outs = [[] for _ in range(8)]
    for l in range(depth):
        lr, li, lcr, lci, bbr, bbi = _ssm_prep(ssm_a_re[l], ssm_a_im[l], ssm_log_step[l], ssm_b_re[l], ssm_b_im[l])
        lam = (lr.reshape(1, STATE_LANES), li.reshape(1, STATE_LANES))
        lam_chunk = (lcr.reshape(1, STATE_LANES), lci.reshape(1, STATE_LANES))
        c_re_t = jnp.transpose(ssm_c_re[l], (0, 2, 1))
        c_im_t = jnp.transpose(ssm_c_im[l], (0, 2, 1))
        bre, cre = _block_diag(bbr, c_re_t)
        bim, cim = _block_diag(bbi, c_im_t)
        g_mix = norm_mix[l].reshape(1, D_MODEL)
        g_attn = _pair_heads(norm_attn_out[l].reshape(1, ATTN_WIDTH), axis=1)
        g_ssm = norm_ssm_out[l].reshape(1, SSM_WIDTH)
        g_mlp = norm_mlp[l].reshape(1, D_MODEL)
        sinks = attn_sinks[l].reshape(1, N_Q_HEADS)
        d = ssm_d[l].reshape(1, SSM_WIDTH)
        bg = b_glu[l].reshape(1, SSM_WIDTH)
        w_in_b = jnp.concatenate([_pair_heads(w_in[l][:, :ATTN_WIDTH], axis=1), w_in[l][:, ATTN_WIDTH:]],
                                 axis=1).astype(BF16)
        w_glu_b = w_glu[l].astype(BF16)
        w_out_b = jnp.concatenate([_pair_heads(w_out[l][:ATTN_WIDTH], axis=0), w_out[l][ATTN_WIDTH:]],
                                  axis=0).astype(BF16)
        w_up_b = w_up[l].astype(BF16)
        w_down_b = w_down[l].astype(BF16)
        last = l == depth - 1
        g_last = g_fin if last else jnp.ones_like(g_fin)
        assert last, "the final norm is fused into the last layer's MLP kernel"

        attn_n, u, k1, v1 = _in_attn(xp, rel_bias, sinks, g_mix, w_in_b, bucket.T, g_attn.reshape(ATTN_WIDTH, 1))
        ssm_n, r1, i1 = _scan(u, lam, lam_chunk, (bre, bim), (cre, cim), d, w_glu_b, bg, g_ssm)
        xp = _out_mlp(xp.reshape(bsz * seq, D_MODEL), attn_n.reshape(bsz * seq, ATTN_WIDTH),
                      ssm_n.reshape(bsz * seq, SSM_WIDTH), w_out_b, g_mlp, w_up_b, w_down_b, g_last
                      ).reshape(bsz, seq, D_MODEL)

        ck = cache_k[l].reshape(nseq, WINDOW, KV_WIDTH)
        cv = cache_v[l].reshape(nseq, WINDOW, KV_WIDTH)
        s_attn, s_u, k2, v2 = _s_in_attn(xs, rel_bias, sinks, g_mix, w_in_b, bucket, g_attn, ck, cv)
        s_ssm, r2, i2 = _s_scan(s_u, state_ssm_re[l].reshape(nseq, STATE_LANES),
                                state_ssm_im[l].reshape(nseq, STATE_LANES),
                                lam, (bre, bim), (cre, cim), d, w_glu_b, bg, g_ssm)
        xs = _out_mlp(xs.reshape(nseq * steps, D_MODEL), s_attn, s_ssm, w_out_b, g_mlp, w_up_b, w_down_b,
                      g_last).reshape(nseq, steps, D_MODEL)

        kv_shape = (WINDOW, N_KV_HEADS, HEAD_DIM)
        st_shape = (N_SSM_GROUPS, SSM_STATE)
        for lst, val in zip(outs, (k1.reshape(bsz, *kv_shape), v1.reshape(bsz, *kv_shape),
                                   r1.reshape(bsz, *st_shape), i1.reshape(bsz, *st_shape),
                                   k2.reshape(nseq, *kv_shape), v2.reshape(nseq, *kv_shape),
                                   r2.reshape(nseq, *st_shape), i2.reshape(nseq, *st_shape))):
            lst.append(val)

    return (xp, xs) + tuple(jnp.stack(o) for o in outs)
```

```python
import functools
import math

import numpy as np
import jax
import jax.numpy as jnp
from jax import lax
from jax.experimental import pallas as pl
from jax.experimental.pallas import tpu as pltpu

D_MODEL = 1024
HEAD_DIM = 64
ATTN_WIDTH = 512
N_Q_HEADS = 8
N_KV_HEADS = 2
Q_PER_KV = 4
KV_WIDTH = 128
WINDOW = 128
BLOCK = 128
N_BUCKETS = 32
MAX_DISTANCE = 128
SSM_WIDTH = 512
SSM_GROUP = 16
N_SSM_GROUPS = 32
SSM_STATE = 64
STATE_LANES = N_SSM_GROUPS * SSM_STATE
D_FF = 4096
D_IN_PROJ = 1280
EPS = 1e-6
NEG = -1e30

LANES = 128
SUBLANES = 8
MXU_DIM = 256
VMEM_LIMIT = 56 * 1024 * 1024

F32 = jnp.float32
BF16 = jnp.bfloat16

TM_A = 512
TM_C = 512
TM_B = 512
CHUNK = TM_B // SUBLANES
PITCH = CHUNK + SUBLANES
S_SEQ = 16


def _rms(x, g):
    return x * lax.rsqrt(jnp.mean(x * x, axis=-1, keepdims=True) + EPS) * g


def _rel_bucket_table():
    i = np.arange(BLOCK)[:, None]
    j = np.arange(2 * BLOCK)[None, :]
    n = np.maximum(i + BLOCK - j, 0)
    max_exact = N_BUCKETS // 2
    nf = np.maximum(n, max_exact).astype(np.float32)
    large = max_exact + (np.log(nf / np.float32(max_exact)) / np.float32(math.log(MAX_DISTANCE / max_exact))
                         * np.float32(N_BUCKETS - max_exact)).astype(np.int32)
    large = np.minimum(large, N_BUCKETS - 1)
    return np.where(n < max_exact, n, large).astype(np.int32)


def _build_bias(relb_ref, bucket_ref, bias_scr):
    bucket = bucket_ref[...]
    for h in range(N_Q_HEADS):
        bias_scr[h] = jnp.zeros((BLOCK, 2 * BLOCK), F32)
    for b in range(N_BUCKETS):
        hit = bucket == b
        for h in range(N_Q_HEADS):
            bias_scr[h] = jnp.where(hit, relb_ref[b, h], bias_scr[h])


def _build_bias_t(relb_ref, bucket_t_ref, bias_t_scr):
    bucket_t = bucket_t_ref[...]
    bias_t_scr[...] = jnp.zeros_like(bias_t_scr)
    for b in range(N_BUCKETS):
        hit = bucket_t == b
        for h in range(N_Q_HEADS):
            cols = slice(h * BLOCK, (h + 1) * BLOCK)
            bias_t_scr[:, cols] = jnp.where(hit, relb_ref[b, h], bias_t_scr[:, cols])


def _ssm_prep_kernel(are_ref, aim_ref, ls_ref, btr_ref, bti_ref,
                     lr_ref, li_ref, lcr_ref, lci_ref, bbr_ref, bbi_ref):
    ar = are_ref[...]
    ai = aim_ref[...]
    step = jnp.exp(ls_ref[...])
    zr = ar * step
    zi = ai * step
    mag = jnp.exp(zr)
    lr = mag * jnp.cos(zi)
    li = mag * jnp.sin(zi)
    lr_ref[...] = lr
    li_ref[...] = li
    pr, pi = lr, li
    for _ in range(int(math.log2(CHUNK))):
        pr, pi = pr * pr - pi * pi, 2.0 * pr * pi
    lcr_ref[...] = pr
    lci_ref[...] = pi
    nr = lr - 1.0
    ni = li
    den = ar * ar + ai * ai
    cr = (nr * ar + ni * ai) / den
    ci = (ni * ar - nr * ai) / den
    btr = btr_ref[...]
    bti = bti_ref[...]
    bbr_ref[...] = cr * btr - ci * bti
    bbi_ref[...] = cr * bti + ci * btr


def _ssm_prep(a_re, a_im, log_step, b_re, b_im):
    g, p, c = N_SSM_GROUPS, SSM_STATE, SSM_GROUP
    vec = jax.ShapeDtypeStruct((g, 1, p), F32)
    mat = jax.ShapeDtypeStruct((g, c, p), F32)
    return pl.pallas_call(
        _ssm_prep_kernel,
        out_shape=(vec, vec, vec, vec, mat, mat),
        name="ssm_prep",
    )(a_re.reshape(g, 1, p), a_im.reshape(g, 1, p), log_step.reshape(g, 1, 1),
      jnp.transpose(b_re, (0, 2, 1)), jnp.transpose(b_im, (0, 2, 1)))


def _stack_heads(q_blk, lane_lo):
    zero = jnp.zeros_like(q_blk[:, 0:LANES])
    lo, hi = [], []
    for c in range(ATTN_WIDTH // LANES):
        blk = q_blk[:, c * LANES:(c + 1) * LANES]
        lo.append(jnp.where(lane_lo, blk, zero))
        hi.append(jnp.where(lane_lo, zero, blk))
    return jnp.concatenate(lo + hi, axis=0)


def _unstack_heads(o, rows, lane_lo):
    nblk = ATTN_WIDTH // LANES
    return jnp.concatenate(
        [jnp.where(lane_lo, o[c * rows:(c + 1) * rows], o[(c + nblk) * rows:(c + nblk + 1) * rows])
         for c in range(nblk)], axis=-1)


def _in_attn_kernel(relb_ref, sinks_ref, x_ref, gmix_ref, win_ref, bucket_t_ref, gattn_ref,
                    attn_ref, u_ref, newk_ref, newv_ref,
                    k_scr, vt_scr, q_scr, bias_t_scr, gt_scr):
    b = pl.program_id(0)
    t = pl.program_id(1)
    nt = pl.num_programs(1)
    nstk = N_Q_HEADS * BLOCK

    @pl.when(jnp.logical_and(b == 0, t == 0))
    def _():
        _build_bias_t(relb_ref, bucket_t_ref, bias_t_scr)
        gt_scr[...] = jnp.broadcast_to(gattn_ref[...], (ATTN_WIDTH, BLOCK))

    @pl.when(t == 0)
    def _():
        k_scr[0:BLOCK, :] = jnp.zeros((BLOCK, KV_WIDTH), BF16)
        vt_scr[:, 0:BLOCK] = jnp.zeros((KV_WIDTH, BLOCK), BF16)

    xn = _rms(x_ref[...], gmix_ref[...]).astype(BF16)
    proj = jnp.dot(xn, win_ref[...], preferred_element_type=F32)
    q_scr[...] = (proj[:, :ATTN_WIDTH] * (HEAD_DIM ** -0.5)).astype(BF16)
    v_new = proj[:, ATTN_WIDTH + KV_WIDTH:ATTN_WIDTH + 2 * KV_WIDTH]
    k_scr[BLOCK:BLOCK + TM_A, :] = proj[:, ATTN_WIDTH:ATTN_WIDTH + KV_WIDTH].astype(BF16)
    vt_scr[:, BLOCK:BLOCK + TM_A] = v_new.T.astype(BF16)
    u_ref[...] = proj[:, ATTN_WIDTH + 2 * KV_WIDTH:]

    @pl.when(t == nt - 1)
    def _():
        newk_ref[...] = proj[TM_A - WINDOW:, ATTN_WIDTH:ATTN_WIDTH + KV_WIDTH]
        newv_ref[...] = v_new[TM_A - WINDOW:, :]

    kj = lax.broadcasted_iota(jnp.int32, (2 * BLOCK, nstk), 0)
    qi = lax.broadcasted_iota(jnp.int32, (2 * BLOCK, nstk), 1) % BLOCK
    dist = qi + BLOCK - kj
    in_band = jnp.logical_and(dist >= 0, dist < WINDOW)
    in_band_first = jnp.logical_and(in_band, jnp.logical_or(kj >= BLOCK, t > 0))
    lane_lo = lax.broadcasted_iota(jnp.int32, (BLOCK, LANES), 1) < HEAD_DIM
    row_lo = lax.broadcasted_iota(jnp.int32, (KV_WIDTH, BLOCK), 0) < HEAD_DIM
    head = lax.broadcasted_iota(jnp.int32, (1, nstk), 1) // BLOCK
    sink = jnp.zeros((1, nstk), F32)
    for h in range(N_Q_HEADS):
        sink = jnp.where(head == h, sinks_ref[0, h], sink)
    nblk = ATTN_WIDTH // LANES

    for j in range(TM_A // BLOCK):
        r0 = j * BLOCK
        q2 = _stack_heads(q_scr[r0:r0 + BLOCK, :], lane_lo)
        keys = k_scr[r0:r0 + 2 * BLOCK, :]
        s = lax.dot_general(keys, q2, (((1,), (1,)), ((), ())), preferred_element_type=F32)
        s = jnp.where(in_band_first if j == 0 else in_band, s + bias_t_scr[...], NEG)
        m = jnp.maximum(jnp.max(s, axis=0, keepdims=True), sink)
        p = jnp.exp(s - m)
        denom = jnp.sum(p, axis=0, keepdims=True) + jnp.exp(sink - m)
        o = jnp.dot(vt_scr[:, r0:r0 + 2 * BLOCK], p.astype(BF16), preferred_element_type=F32) / denom
        a = jnp.concatenate(
            [jnp.where(row_lo, o[:, c * BLOCK:(c + 1) * BLOCK], o[:, (c + nblk) * BLOCK:(c + nblk + 1) * BLOCK])
             for c in range(nblk)], axis=0)
        a = a * lax.rsqrt(jnp.mean(a * a, axis=0, keepdims=True) + EPS) * gt_scr[...]
        attn_ref[r0:r0 + BLOCK, :] = a.T.astype(BF16)

    k_scr[0:BLOCK, :] = k_scr[TM_A:TM_A + BLOCK, :]
    vt_scr[:, 0:BLOCK] = vt_scr[:, TM_A:TM_A + BLOCK]


def _const_spec(shape):
    zeros = (0,) * len(shape)
    return pl.BlockSpec(shape, lambda *_: zeros, pipeline_mode=pl.Buffered(1))


def _smem_spec():
    return pl.BlockSpec(memory_space=pltpu.SMEM)


def _in_attn(x, rel_bias, sinks, g_mix, w_in, bucket_t, g_attn_col):
    bsz, seq, _ = x.shape
    nt = seq // TM_A
    row = lambda width: pl.BlockSpec((None, TM_A, width), lambda b, t: (b, t, 0))
    win_spec = pl.BlockSpec((None, WINDOW, KV_WIDTH), lambda b, t: (b, 0, 0))
    return pl.pallas_call(
        _in_attn_kernel,
        out_shape=(jax.ShapeDtypeStruct((bsz, seq, ATTN_WIDTH), BF16),
                   jax.ShapeDtypeStruct((bsz, seq, SSM_WIDTH), F32),
                   jax.ShapeDtypeStruct((bsz, WINDOW, KV_WIDTH), F32),
                   jax.ShapeDtypeStruct((bsz, WINDOW, KV_WIDTH), F32)),
        grid=(bsz, nt),
        in_specs=[_smem_spec(), _smem_spec(), row(D_MODEL), _const_spec((1, D_MODEL)),
                  _const_spec((D_MODEL, D_IN_PROJ)), _const_spec((2 * BLOCK, BLOCK)),
                  _const_spec((ATTN_WIDTH, 1))],
        out_specs=(row(ATTN_WIDTH), row(SSM_WIDTH), win_spec, win_spec),
        scratch_shapes=[pltpu.VMEM((TM_A + BLOCK, KV_WIDTH), BF16),
                        pltpu.VMEM((KV_WIDTH, TM_A + BLOCK), BF16),
                        pltpu.VMEM((TM_A, ATTN_WIDTH), BF16),
                        pltpu.VMEM((2 * BLOCK, N_Q_HEADS * BLOCK), F32),
                        pltpu.VMEM((ATTN_WIDTH, BLOCK), F32)],
        compiler_params=pltpu.CompilerParams(
            dimension_semantics=("arbitrary", "arbitrary"), vmem_limit_bytes=VMEM_LIMIT),
        name="in_attn",
    )(rel_bias, sinks, x, g_mix, w_in, bucket_t, g_attn_col)


def _s_in_attn_kernel(relb_ref, sinks_ref, x_ref, gmix_ref, win_ref, bucket_ref, gattn_ref,
                      ck_ref, cv_ref,
                      attn_ref, u_ref, newk_ref, newv_ref,
                      bias_scr, biasd_scr):
    steps = x_ref.shape[0] // S_SEQ
    pair = SUBLANES // steps
    prow = N_Q_HEADS * SUBLANES

    @pl.when(pl.program_id(0) == 0)
    def _():
        _build_bias(relb_ref, bucket_ref, bias_scr)
        for h in range(N_Q_HEADS):
            for k in range(pair):
                biasd_scr[h * SUBLANES + k * steps:h * SUBLANES + (k + 1) * steps, :] = bias_scr[h, 0:steps, :]

    xn = _rms(x_ref[...], gmix_ref[...]).astype(BF16)
    proj = jnp.dot(xn, win_ref[...], preferred_element_type=F32)
    u_ref[...] = proj[:, ATTN_WIDTH + 2 * KV_WIDTH:]
    qs = proj[:, :ATTN_WIDTH] * (HEAD_DIM ** -0.5)

    nkeys = 2 * WINDOW
    ri = lax.broadcasted_iota(jnp.int32, (prow, nkeys), 0)
    kj = lax.broadcasted_iota(jnp.int32, (prow, nkeys), 1)
    dist = ri % steps + WINDOW - kj
    valid = jnp.logical_and(dist >= 0, dist < WINDOW)
    head = lax.broadcasted_iota(jnp.int32, (prow, 1), 0) // SUBLANES
    sink = jnp.zeros((prow, 1), F32)
    for h in range(N_Q_HEADS):
        sink = jnp.where(head == h, sinks_ref[0, h], sink)
    seq_in_tile = (lax.broadcasted_iota(jnp.int32, (prow, LANES), 0) % SUBLANES) // steps
    lane_lo = lax.broadcasted_iota(jnp.int32, (SUBLANES, LANES), 1) < HEAD_DIM
    row8 = lax.broadcasted_iota(jnp.int32, (SUBLANES, KV_WIDTH), 0)
    gattn = gattn_ref[...]
    tail = jnp.zeros((nkeys - WINDOW - SUBLANES, KV_WIDTH), F32)
    bias = biasd_scr[...]

    tiles = []
    for tl in range(S_SEQ // pair):
        rws = slice(tl * SUBLANES, (tl + 1) * SUBLANES)
        q2 = _stack_heads(qs[rws], lane_lo).astype(BF16)
        kn8 = proj[rws, ATTN_WIDTH:ATTN_WIDTH + KV_WIDTH]
        vn8 = proj[rws, ATTN_WIDTH + KV_WIDTH:ATTN_WIDTH + 2 * KV_WIDTH]
        o_tile = jnp.zeros((prow, LANES), F32)
        for k in range(pair):
            sq = tl * pair + k
            shift = (SUBLANES - k * steps) % SUBLANES
            kn = jnp.where(row8 < steps, pltpu.roll(kn8, shift, axis=0) if shift else kn8, 0.0)
            vn = jnp.where(row8 < steps, pltpu.roll(vn8, shift, axis=0) if shift else vn8, 0.0)
            ck = ck_ref[sq]
            cv = cv_ref[sq]
            newk_ref[sq, 0:WINDOW - steps, :] = ck[steps:, :]
            newk_ref[sq, WINDOW - steps:, :] = kn[0:steps]
            newv_ref[sq, 0:WINDOW - steps, :] = cv[steps:, :]
            newv_ref[sq, WINDOW - steps:, :] = vn[0:steps]
            keys = jnp.concatenate([ck, kn, tail], axis=0).astype(BF16)
            vals = jnp.concatenate([cv, vn, tail], axis=0).astype(BF16)
            s = lax.dot_general(q2, keys, (((1,), (1,)), ((), ())), preferred_element_type=F32)
            s = jnp.where(valid, s + bias, NEG)
            m = jnp.maximum(jnp.max(s, axis=-1, keepdims=True), sink)
            p = jnp.exp(s - m)
            denom = jnp.sum(p, axis=-1, keepdims=True) + jnp.exp(sink - m)
            o = jnp.dot(p.astype(BF16), vals, preferred_element_type=F32) / denom
            o_tile = jnp.where(seq_in_tile == k, o, o_tile)
        tiles.append(_unstack_heads(o_tile, SUBLANES, lane_lo))
    attn_ref[...] = _rms(jnp.concatenate(tiles, axis=0), gattn).astype(BF16)


def _s_in_attn(x, rel_bias, sinks, g_mix, w_in, bucket, g_attn, cache_k, cache_v):
    nseq, steps, _ = x.shape
    rows = S_SEQ * steps
    xf = x.reshape(nseq * steps, D_MODEL)
    row = lambda width: pl.BlockSpec((rows, width), lambda i: (i, 0))
    cache = pl.BlockSpec((S_SEQ, WINDOW, KV_WIDTH), lambda i: (i, 0, 0))
    return pl.pallas_call(
        _s_in_attn_kernel,
        out_shape=(jax.ShapeDtypeStruct((nseq * steps, ATTN_WIDTH), BF16),
                   jax.ShapeDtypeStruct((nseq * steps, SSM_WIDTH), F32),
                   jax.ShapeDtypeStruct((nseq, WINDOW, KV_WIDTH), F32),
                   jax.ShapeDtypeStruct((nseq, WINDOW, KV_WIDTH), F32)),
        grid=(nseq // S_SEQ,),
        in_specs=[_smem_spec(), _smem_spec(), row(D_MODEL), _const_spec((1, D_MODEL)),
                  _const_spec((D_MODEL, D_IN_PROJ)), _const_spec((BLOCK, 2 * BLOCK)),
                  _const_spec((1, ATTN_WIDTH)), cache, cache],
        out_specs=(row(ATTN_WIDTH), row(SSM_WIDTH), cache, cache),
        scratch_shapes=[pltpu.VMEM((N_Q_HEADS, BLOCK, 2 * BLOCK), F32),
                        pltpu.VMEM((N_Q_HEADS * SUBLANES, 2 * WINDOW), F32)],
        compiler_params=pltpu.CompilerParams(
            dimension_semantics=("arbitrary",), vmem_limit_bytes=VMEM_LIMIT),
        name="s_in_attn",
    )(rel_bias, sinks, xf, g_mix, w_in, bucket, g_attn, cache_k, cache_v)


def _input_drive(up, bre_ref, bim_ref, hre, him):
    upb = up.astype(BF16)
    half = STATE_LANES // 2
    for kt in range(SSM_WIDTH // MXU_DIM):
        lhs = upb[:, kt * MXU_DIM:(kt + 1) * MXU_DIM]
        cols = slice(kt * half, (kt + 1) * half)
        rws = slice(kt * MXU_DIM, (kt + 1) * MXU_DIM)
        hre[:, cols] = jnp.dot(lhs, bre_ref[rws, cols], preferred_element_type=F32)
        him[:, cols] = jnp.dot(lhs, bim_ref[rws, cols], preferred_element_type=F32)


def _ssm_output(hre, him, up, cre_ref, cim_ref, d_ref, wglu_ref, bglu_ref, gssm_ref):
    ys = []
    kblk = STATE_LANES // (SSM_WIDTH // LANES)
    for n in range(SSM_WIDTH // LANES):
        rws = slice(n * kblk, (n + 1) * kblk)
        cols = slice(n * LANES, (n + 1) * LANES)
        yr = jnp.dot(hre[:, rws].astype(BF16), cre_ref[rws, cols], preferred_element_type=F32)
        yi = jnp.dot(him[:, rws].astype(BF16), cim_ref[rws, cols], preferred_element_type=F32)
        ys.append(yr - yi)
    y = jnp.concatenate(ys, axis=-1) + d_ref[...] * up
    g = jax.nn.gelu(y)
    gate = jax.nn.sigmoid(jnp.dot(g.astype(BF16), wglu_ref[...], preferred_element_type=F32) + bglu_ref[...])
    return _rms(g * gate, gssm_ref[...])


def _scan_kernel(u_ref, lr_ref, li_ref, lcr_ref, lci_ref, bre_ref, bim_ref, cre_ref, cim_ref,
                 d_ref, wglu_ref, bglu_ref, gssm_ref,
                 out_ref, hre_out, him_out,
                 pad_scr, up_scr, hre, him, carry_re, carry_im, e_re, e_im, hc_re, hc_im):
    t = pl.program_id(1)
    nslab = SSM_WIDTH // LANES

    @pl.when(t == 0)
    def _():
        carry_re[...] = jnp.zeros_like(carry_re)
        carry_im[...] = jnp.zeros_like(carry_im)

    for c in range(SUBLANES):
        for s in range(nslab):
            pad_scr[s, c * PITCH:c * PITCH + CHUNK, :] = u_ref[c * CHUNK:(c + 1) * CHUNK, s * LANES:(s + 1) * LANES]
    for tau in range(CHUNK):
        for s in range(nslab):
            up_scr[tau * SUBLANES:(tau + 1) * SUBLANES, s * LANES:(s + 1) * LANES] = (
                pad_scr[s, pl.ds(tau, SUBLANES, stride=PITCH), :])
    up = up_scr[...]
    upb = up.astype(BF16)

    slab = 4 * LANES
    chans = slab // SSM_STATE * SSM_GROUP
    ys = []
    for sl in range(STATE_LANES // slab):
        cols = slice(sl * slab, (sl + 1) * slab)
        ch = slice(sl * chans, (sl + 1) * chans)
        hre[:, cols] = jnp.dot(upb[:, ch], bre_ref[ch, cols], preferred_element_type=F32)
        him[:, cols] = jnp.dot(upb[:, ch], bim_ref[ch, cols], preferred_element_type=F32)
        ar = jnp.broadcast_to(lr_ref[:, cols], (SUBLANES, slab))
        ai = jnp.broadcast_to(li_ref[:, cols], (SUBLANES, slab))

        hr = jnp.zeros((SUBLANES, slab), F32)
        hi = jnp.zeros((SUBLANES, slab), F32)
        for tau in range(CHUNK):
            rws = slice(tau * SUBLANES, (tau + 1) * SUBLANES)
            hr, hi = ar * hr - ai * hi + hre[rws, cols], ar * hi + ai * hr + him[rws, cols]
        e_re[:, cols] = hr
        e_im[:, cols] = hi

        cr = carry_re[:, cols]
        ci = carry_im[:, cols]
        lcr = lcr_ref[:, cols]
        lci = lci_ref[:, cols]
        for c in range(SUBLANES):
            hc_re[c:c + 1, cols] = cr
            hc_im[c:c + 1, cols] = ci
            er = e_re[c:c + 1, cols]
            ei = e_im[c:c + 1, cols]
            cr, ci = lcr * cr - lci * ci + er, lcr * ci + lci * cr + ei
        carry_re[:, cols] = cr
        carry_im[:, cols] = ci

        hr = hc_re[:, cols]
        hi = hc_im[:, cols]
        for tau in range(CHUNK):
            rws = slice(tau * SUBLANES, (tau + 1) * SUBLANES)
            hr, hi = ar * hr - ai * hi + hre[rws, cols], ar * hi + ai * hr + him[rws, cols]
            hre[rws, cols] = hr
            him[rws, cols] = hi

        yr = jnp.dot(hre[:, cols].astype(BF16), cre_ref[cols, ch], preferred_element_type=F32)
        yi = jnp.dot(him[:, cols].astype(BF16), cim_ref[cols, ch], preferred_element_type=F32)
        ys.append(yr - yi)

    y = jnp.concatenate(ys, axis=-1) + d_ref[...] * up
    g = jax.nn.gelu(y)
    gate = jax.nn.sigmoid(jnp.dot(g.astype(BF16), wglu_ref[...], preferred_element_type=F32) + bglu_ref[...])
    o = _rms(g * gate, gssm_ref[...])

    for tau in range(CHUNK):
        for s in range(nslab):
            pad_scr[s, pl.ds(tau, SUBLANES, stride=PITCH), :] = (
                o[tau * SUBLANES:(tau + 1) * SUBLANES, s * LANES:(s + 1) * LANES])
    for c in range(SUBLANES):
        for s in range(nslab):
            out_ref[c * CHUNK:(c + 1) * CHUNK, s * LANES:(s + 1) * LANES] = (
                pad_scr[s, c * PITCH:c * PITCH + CHUNK, :].astype(BF16))

    @pl.when(t == pl.num_programs(1) - 1)
    def _():
        hre_out[...] = carry_re[...]
        him_out[...] = carry_im[...]


def _scan(u, lam, lam_chunk, bblk, cblk, d, w_glu, b_glu, g_ssm):
    bsz, seq, _ = u.shape
    nt = seq // TM_B
    row = pl.BlockSpec((None, TM_B, SSM_WIDTH), lambda b, t: (b, t, 0))
    state = pl.BlockSpec((None, 1, STATE_LANES), lambda b, t: (b, 0, 0))
    vec = _const_spec((1, STATE_LANES))
    return pl.pallas_call(
        _scan_kernel,
        out_shape=(jax.ShapeDtypeStruct((bsz, seq, SSM_WIDTH), BF16),
                   jax.ShapeDtypeStruct((bsz, 1, STATE_LANES), F32),
                   jax.ShapeDtypeStruct((bsz, 1, STATE_LANES), F32)),
        grid=(bsz, nt),
        in_specs=[row, vec, vec, vec, vec,
                  _const_spec((SSM_WIDTH, STATE_LANES)), _const_spec((SSM_WIDTH, STATE_LANES)),
                  _const_spec((STATE_LANES, SSM_WIDTH)), _const_spec((STATE_LANES, SSM_WIDTH)),
                  _const_spec((1, SSM_WIDTH)), _const_spec((SSM_WIDTH, SSM_WIDTH)),
                  _const_spec((1, SSM_WIDTH)), _const_spec((1, SSM_WIDTH))],
        out_specs=(row, state, state),
        scratch_shapes=[pltpu.VMEM((SSM_WIDTH // LANES, SUBLANES * PITCH, LANES), F32),
                        pltpu.VMEM((TM_B, SSM_WIDTH), F32),
                        pltpu.VMEM((TM_B, STATE_LANES), F32),
                        pltpu.VMEM((TM_B, STATE_LANES), F32),
                        pltpu.VMEM((1, STATE_LANES), F32),
                        pltpu.VMEM((1, STATE_LANES), F32),
                        pltpu.VMEM((SUBLANES, STATE_LANES), F32),
                        pltpu.VMEM((SUBLANES, STATE_LANES), F32),
                        pltpu.VMEM((SUBLANES, STATE_LANES), F32),
                        pltpu.VMEM((SUBLANES, STATE_LANES), F32)],
        compiler_params=pltpu.CompilerParams(
            dimension_semantics=("arbitrary", "arbitrary"), vmem_limit_bytes=VMEM_LIMIT),
        name="scan",
    )(u, lam[0], lam[1], lam_chunk[0], lam_chunk[1], bblk[0], bblk[1], cblk[0], cblk[1],
      d, w_glu, b_glu, g_ssm)


def _s_scan_kernel(u_ref, h0r_ref, h0i_ref, lr_ref, li_ref, bre_ref, bim_ref, cre_ref, cim_ref,
                   d_ref, wglu_ref, bglu_ref, gssm_ref,
                   out_ref, hre_out, him_out,
                   pad_scr, up_scr, hre, him):
    nseq = h0r_ref.shape[0]
    steps = u_ref.shape[0] // nseq
    nslab = SSM_WIDTH // LANES

    for s in range(nslab):
        pad_scr[s] = u_ref[:, s * LANES:(s + 1) * LANES]
    for i in range(steps):
        for s in range(nslab):
            up_scr[i * nseq:(i + 1) * nseq, s * LANES:(s + 1) * LANES] = (
                pad_scr[s, pl.ds(i, nseq, stride=steps), :])
    up = up_scr[...]
    _input_drive(up, bre_ref, bim_ref, hre, him)

    slab = 2 * LANES
    for sl in range(STATE_LANES // slab):
        cols = slice(sl * slab, (sl + 1) * slab)
        ar = lr_ref[:, cols]
        ai = li_ref[:, cols]
        hr = h0r_ref[:, cols]
        hi = h0i_ref[:, cols]
        for i in range(steps):
            rws = slice(i * nseq, (i + 1) * nseq)
            hr, hi = ar * hr - ai * hi + hre[rws, cols], ar * hi + ai * hr + him[rws, cols]
            hre[rws, cols] = hr
            him[rws, cols] = hi
        hre_out[:, cols] = hr
        him_out[:, cols] = hi

    o = _ssm_output(hre, him, up, cre_ref, cim_ref, d_ref, wglu_ref, bglu_ref, gssm_ref)

    for i in range(steps):
        for s in range(nslab):
            pad_scr[s, pl.ds(i, nseq, stride=steps), :] = o[i * nseq:(i + 1) * nseq, s * LANES:(s + 1) * LANES]
    for s in range(nslab):
        out_ref[:, s * LANES:(s + 1) * LANES] = pad_scr[s].astype(BF16)


def _s_scan(u, h0_re, h0_im, lam, bblk, cblk, d, w_glu, b_glu, g_ssm):
    rows = u.shape[0]
    nseq = h0_re.shape[0]
    return pl.pallas_call(
        _s_scan_kernel,
        out_shape=(jax.ShapeDtypeStruct((rows, SSM_WIDTH), BF16),
                   jax.ShapeDtypeStruct((nseq, STATE_LANES), F32),
                   jax.ShapeDtypeStruct((nseq, STATE_LANES), F32)),
        scratch_shapes=[pltpu.VMEM((SSM_WIDTH // LANES, rows, LANES), F32),
                        pltpu.VMEM((rows, SSM_WIDTH), F32),
                        pltpu.VMEM((rows, STATE_LANES), F32),
                        pltpu.VMEM((rows, STATE_LANES), F32)],
        compiler_params=pltpu.CompilerParams(vmem_limit_bytes=VMEM_LIMIT),
        name="s_scan",
    )(u, h0_re, h0_im, lam[0], lam[1], bblk[0], bblk[1], cblk[0], cblk[1], d, w_glu, b_glu, g_ssm)


def _out_mlp_kernel(x_ref, a_ref, s_ref, wout_ref, gmlp_ref, wup_ref, wdown_ref, gfin_ref, y_ref):
    x1 = (x_ref[...]
          + jnp.dot(a_ref[...], wout_ref[0:ATTN_WIDTH, :], preferred_element_type=F32)
          + jnp.dot(s_ref[...], wout_ref[ATTN_WIDTH:, :], preferred_element_type=F32))
    hm = _rms(x1, gmlp_ref[...]).astype(BF16)
    y_ref[...] = x1
    ff_blk = 1024
    for c in range(D_FF // ff_blk):
        cols = slice(c * ff_blk, (c + 1) * ff_blk)
        up = jnp.dot(hm, wup_ref[:, cols], preferred_element_type=F32)
        act = jnp.square(jnp.maximum(up, 0.0)).astype(BF16)
        y_ref[...] += jnp.dot(act, wdown_ref[cols, :], preferred_element_type=F32)
    y_ref[...] = _rms(y_ref[...], gfin_ref[...])


def _out_mlp(x, attn_n, ssm_n, w_out, g_mlp, w_up, w_down, g_fin):
    rows = x.shape[0]
    row = lambda width: pl.BlockSpec((TM_C, width), lambda i: (i, 0))
    return pl.pallas_call(
        _out_mlp_kernel,
        out_shape=jax.ShapeDtypeStruct((rows, D_MODEL), F32),
        grid=(rows // TM_C,),
        in_specs=[row(D_MODEL), row(ATTN_WIDTH), row(SSM_WIDTH),
                  _const_spec((D_MODEL, D_MODEL)), _const_spec((1, D_MODEL)),
                  _const_spec((D_MODEL, D_FF)), _const_spec((D_FF, D_MODEL)),
                  _const_spec((1, D_MODEL))],
        out_specs=row(D_MODEL),
        compiler_params=pltpu.CompilerParams(
            dimension_semantics=("arbitrary",), vmem_limit_bytes=VMEM_LIMIT),
        name="out_mlp",
    )(x, attn_n, ssm_n, w_out, g_mlp, w_up, w_down, g_fin)


def _pair_heads(w, axis):
    shape = w.shape
    split = shape[:axis] + (N_KV_HEADS, Q_PER_KV, HEAD_DIM) + shape[axis + 1:]
    return jnp.swapaxes(w.reshape(split), axis, axis + 1).reshape(shape)


def _block_diag(bbar, c_gpc):
    eye = jnp.eye(N_SSM_GROUPS, dtype=F32)
    bblk = jnp.einsum('gcp,gh->gchp', bbar, eye).reshape(SSM_WIDTH, STATE_LANES).astype(BF16)
    cblk = jnp.einsum('gpc,gh->gphc', c_gpc, eye).reshape(STATE_LANES, SSM_WIDTH).astype(BF16)
    return bblk, cblk


def kernel(x_prompt, x_sample, cache_k, cache_v, state_ssm_re, state_ssm_im, rel_bias, norm_mix, w_in, attn_sinks, ssm_a_re, ssm_a_im, ssm_log_step, ssm_b_re, ssm_b_im, ssm_c_re, ssm_c_im, ssm_d, w_glu, b_glu, norm_attn_out, norm_ssm_out, w_out, norm_mlp, w_up, w_down, norm_final):
    depth = w_in.shape[0]
    bsz, seq, _ = x_prompt.shape
    nseq, steps, _ = x_sample.shape
    bucket = jnp.asarray(_rel_bucket_table())
    g_fin = norm_final.reshape(1, D_MODEL)

    xp, xs = x_prompt, x_sample
    outs = [[] for _ in range(8)]
    for l in range(depth):
        lr, li, lcr, lci, bbr, bbi = _ssm_prep(ssm_a_re[l], ssm_a_im[l], ssm_log_step[l], ssm_b_re[l], ssm_b_im[l])
        lam = (lr.reshape(1, STATE_LANES), li.reshape(1, STATE_LANES))
        lam_chunk = (lcr.reshape(1, STATE_LANES), lci.reshape(1, STATE_LANES))
        c_re_t = jnp.transpose(ssm_c_re[l], (0, 2, 1))
        c_im_t = jnp.transpose(ssm_c_im[l], (0, 2, 1))
        bre, cre = _block_diag(bbr, c_re_t)
        bim, cim = _block_diag(bbi, c_im_t)
        g_mix = norm_mix[l].reshape(1, D_MODEL)
        g_attn = _pair_heads(norm_attn_out[l].reshape(1, ATTN_WIDTH), axis=1)
        g_ssm = norm_ssm_out[l].reshape(1, SSM_WIDTH)
        g_mlp = norm_mlp[l].reshape(1, D_MODEL)
        sinks = attn_sinks[l].reshape(1, N_Q_HEADS)
        d = ssm_d[l].reshape(1, SSM_WIDTH)
        bg = b_glu[l].reshape(1, SSM_WIDTH)
        w_in_b = jnp.concatenate([_pair_heads(w_in[l][:, :ATTN_WIDTH], axis=1), w_in[l][:, ATTN_WIDTH:]],
                                 axis=1).astype(BF16)
        w_glu_b = w_glu[l].astype(BF16)
        w_out_b = jnp.concatenate([_pair_heads(w_out[l][:ATTN_WIDTH], axis=0), w_out[l][ATTN_WIDTH:]],
                                  axis=0).astype(BF16)
        w_up_b = w_up[l].astype(BF16)
        w_down_b = w_down[l].astype(BF16)
        last = l == depth - 1
        g_last = g_fin if last else jnp.ones_like(g_fin)
        assert last, "the final norm is fused into the last layer's MLP kernel"

        attn_n, u, k1, v1 = _in_attn(xp, rel_bias, sinks, g_mix, w_in_b, bucket.T, g_attn.reshape(ATTN_WIDTH, 1))
        ssm_n, r1, i1 = _scan(u, lam, lam_chunk, (bre, bim), (cre, cim), d, w_glu_b, bg, g_ssm)
        xp = _out_mlp(xp.reshape(bsz * seq, D_MODEL), attn_n.reshape(bsz * seq, ATTN_WIDTH),
                      ssm_n.reshape(bsz * seq, SSM_WIDTH), w_out_b, g_mlp, w_up_b, w_down_b, g_last
                      ).reshape(bsz, seq, D_MODEL)

        ck = cache_k[l].reshape(nseq, WINDOW, KV_WIDTH)
        cv = cache_v[l].reshape(nseq, WINDOW, KV_WIDTH)
        s_attn, s_u, k2, v2 = _s_in_attn(xs, rel_bias, sinks, g_mix, w_in_b, bucket, g_attn, ck, cv)
        s_ssm, r2, i2 = _s_scan(s_u, state_ssm_re[l].reshape(nseq, STATE_LANES),
                                state_ssm_im[l].reshape(nseq, STATE_LANES),
                                lam, (bre, bim), (cre, cim), d, w_glu_b, bg, g_ssm)
        xs = _out_mlp(xs.reshape(nseq * steps, D_MODEL), s_attn, s_ssm, w_out_b, g_mlp, w_up_b, w_down_b,
                      g_last).reshape(nseq, steps, D_MODEL)

        kv_shape = (WINDOW, N_KV_HEADS, HEAD_DIM)
        st_shape = (N_SSM_GROUPS, SSM_STATE)
        for lst, val in zip(outs, (k1.reshape(bsz, *kv_shape), v1.reshape(bsz, *kv_shape),
                                   r1.reshape(bsz, *st_shape), i1.reshape(bsz, *st_shape),
                                   k2.reshape(nseq, *kv_shape), v2.reshape(nseq, *kv_shape),
                                   r2.reshape(nseq, *st_shape), i2.reshape(nseq, *st_shape))):
            lst.append(val)

    return (xp, xs) + tuple(jnp.stack(o) for o in outs)
```

```python
import functools
import math

import numpy as np
import jax
import jax.numpy as jnp
from jax import lax
from jax.experimental import pallas as pl
from jax.experimental.pallas import tpu as pltpu

D_MODEL = 1024
HEAD_DIM = 64
ATTN_WIDTH = 512
N_Q_HEADS = 8
N_KV_HEADS = 2
Q_PER_KV = 4
KV_WIDTH = 128
WINDOW = 128
BLOCK = 128
N_BUCKETS = 32
MAX_DISTANCE = 128
SSM_WIDTH = 512
SSM_GROUP = 16
N_SSM_GROUPS = 32
SSM_STATE = 64
STATE_LANES = N_SSM_GROUPS * SSM_STATE
D_FF = 4096
D_IN_PROJ = 1280
EPS = 1e-6
NEG = -1e30

LANES = 128
SUBLANES = 8
MXU_DIM = 256
VMEM_LIMIT = 56 * 1024 * 1024

F32 = jnp.float32
BF16 = jnp.bfloat16

TM_A = 512
TM_C = 512
TM_B = 512
CHUNK = TM_B // SUBLANES
PITCH = CHUNK + SUBLANES
S_SEQ = 16


def _rms(x, g):
    return x * lax.rsqrt(jnp.mean(x * x, axis=-1, keepdims=True) + EPS) * g


def _rel_bucket_table():
    i = np.arange(BLOCK)[:, None]
    j = np.arange(2 * BLOCK)[None, :]
    n = np.maximum(i + BLOCK - j, 0)
    max_exact = N_BUCKETS // 2
    nf = np.maximum(n, max_exact).astype(np.float32)
    large = max_exact + (np.log(nf / np.float32(max_exact)) / np.float32(math.log(MAX_DISTANCE / max_exact))
                         * np.float32(N_BUCKETS - max_exact)).astype(np.int32)
    large = np.minimum(large, N_BUCKETS - 1)
    return np.where(n < max_exact, n, large).astype(np.int32)


def _build_bias(relb_ref, bucket_ref, bias_scr):
    bucket = bucket_ref[...]
    for h in range(N_Q_HEADS):
        bias_scr[h] = jnp.zeros((BLOCK, 2 * BLOCK), F32)
    for b in range(N_BUCKETS):
        hit = bucket == b
        for h in range(N_Q_HEADS):
            bias_scr[h] = jnp.where(hit, relb_ref[b, h], bias_scr[h])


def _build_bias_t(relb_ref, bucket_t_ref, bias_t_scr):
    bucket_t = bucket_t_ref[...]
    bias_t_scr[...] = jnp.zeros_like(bias_t_scr)
    for b in range(N_BUCKETS):
        hit = bucket_t == b
        for h in range(N_Q_HEADS):
            cols = slice(h * BLOCK, (h + 1) * BLOCK)
            bias_t_scr[:, cols] = jnp.where(hit, relb_ref[b, h], bias_t_scr[:, cols])


def _ssm_prep_kernel(are_ref, aim_ref, ls_ref, btr_ref, bti_ref,
                     lr_ref, li_ref, lcr_ref, lci_ref, bbr_ref, bbi_ref):
    ar = are_ref[...]
    ai = aim_ref[...]
    step = jnp.exp(ls_ref[...])
    zr = ar * step
    zi = ai * step
    mag = jnp.exp(zr)
    lr = mag * jnp.cos(zi)
    li = mag * jnp.sin(zi)
    lr_ref[...] = lr
    li_ref[...] = li
    pr, pi = lr, li
    for _ in range(int(math.log2(CHUNK))):
        pr, pi = pr * pr - pi * pi, 2.0 * pr * pi
    lcr_ref[...] = pr
    lci_ref[...] = pi
    nr = lr - 1.0
    ni = li
    den = ar * ar + ai * ai
    cr = (nr * ar + ni * ai) / den
    ci = (ni * ar - nr * ai) / den
    btr = btr_ref[...]
    bti = bti_ref[...]
    bbr_ref[...] = cr * btr - ci * bti
    bbi_ref[...] = cr * bti + ci * btr


def _ssm_prep(a_re, a_im, log_step, b_re, b_im):
    g, p, c = N_SSM_GROUPS, SSM_STATE, SSM_GROUP
    vec = jax.ShapeDtypeStruct((g, 1, p), F32)
    mat = jax.ShapeDtypeStruct((g, c, p), F32)
    return pl.pallas_call(
        _ssm_prep_kernel,
        out_shape=(vec, vec, vec, vec, mat, mat),
        name="ssm_prep",
    )(a_re.reshape(g, 1, p), a_im.reshape(g, 1, p), log_step.reshape(g, 1, 1),
      jnp.transpose(b_re, (0, 2, 1)), jnp.transpose(b_im, (0, 2, 1)))


def _stack_heads(q_blk, lane_lo):
    zero = jnp.zeros_like(q_blk[:, 0:LANES])
    lo, hi = [], []
    for c in range(ATTN_WIDTH // LANES):
        blk = q_blk[:, c * LANES:(c + 1) * LANES]
        lo.append(jnp.where(lane_lo, blk, zero))
        hi.append(jnp.where(lane_lo, zero, blk))
    return jnp.concatenate(lo + hi, axis=0)


def _unstack_heads(o, rows, lane_lo):
    nblk = ATTN_WIDTH // LANES
    return jnp.concatenate(
        [jnp.where(lane_lo, o[c * rows:(c + 1) * rows], o[(c + nblk) * rows:(c + nblk + 1) * rows])
         for c in range(nblk)], axis=-1)


def _in_attn_kernel(relb_ref, sinks_ref, x_ref, gmix_ref, win_ref, bucket_t_ref, gattn_ref,
                    attn_ref, u_ref, newk_ref, newv_ref,
                    k_scr, vt_scr, q_scr, bias_t_scr, gt_scr):
    b = pl.program_id(0)
    t = pl.program_id(1)
    nt = pl.num_programs(1)
    nstk = N_Q_HEADS * BLOCK

    @pl.when(jnp.logical_and(b == 0, t == 0))
    def _():
        _build_bias_t(relb_ref, bucket_t_ref, bias_t_scr)
        gt_scr[...] = jnp.broadcast_to(gattn_ref[...], (ATTN_WIDTH, BLOCK))

    @pl.when(t == 0)
    def _():
        k_scr[0:BLOCK, :] = jnp.zeros((BLOCK, KV_WIDTH), BF16)
        vt_scr[:, 0:BLOCK] = jnp.zeros((KV_WIDTH, BLOCK), BF16)

    xn = _rms(x_ref[...], gmix_ref[...]).astype(BF16)
    proj = jnp.dot(xn, win_ref[...], preferred_element_type=F32)
    q_scr[...] = (proj[:, :ATTN_WIDTH] * (HEAD_DIM ** -0.5)).astype(BF16)
    v_new = proj[:, ATTN_WIDTH + KV_WIDTH:ATTN_WIDTH + 2 * KV_WIDTH]
    k_scr[BLOCK:BLOCK + TM_A, :] = proj[:, ATTN_WIDTH:ATTN_WIDTH + KV_WIDTH].astype(BF16)
    vt_scr[:, BLOCK:BLOCK + TM_A] = v_new.T.astype(BF16)
    u_ref[...] = proj[:, ATTN_WIDTH + 2 * KV_WIDTH:]

    @pl.when(t == nt - 1)
    def _():
        newk_ref[...] = proj[TM_A - WINDOW:, ATTN_WIDTH:ATTN_WIDTH + KV_WIDTH]
        newv_ref[...] = v_new[TM_A - WINDOW:, :]

    kj = lax.broadcasted_iota(jnp.int32, (2 * BLOCK, nstk), 0)
    qi = lax.broadcasted_iota(jnp.int32, (2 * BLOCK, nstk), 1) % BLOCK
    dist = qi + BLOCK - kj
    in_band = jnp.logical_and(dist >= 0, dist < WINDOW)
    in_band_first = jnp.logical_and(in_band, jnp.logical_or(kj >= BLOCK, t > 0))
    lane_lo = lax.broadcasted_iota(jnp.int32, (BLOCK, LANES), 1) < HEAD_DIM
    row_lo = lax.broadcasted_iota(jnp.int32, (KV_WIDTH, BLOCK), 0) < HEAD_DIM
    head = lax.broadcasted_iota(jnp.int32, (1, nstk), 1) // BLOCK
    sink = jnp.zeros((1, nstk), F32)
    for h in range(N_Q_HEADS):
        sink = jnp.where(head == h, sinks_ref[0, h], sink)
    nblk = ATTN_WIDTH // LANES

    for j in range(TM_A // BLOCK):
        r0 = j * BLOCK
        q2 = _stack_heads(q_scr[r0:r0 + BLOCK, :], lane_lo)
        keys = k_scr[r0:r0 + 2 * BLOCK, :]
        s = lax.dot_general(keys, q2, (((1,), (1,)), ((), ())), preferred_element_type=F32)
        s = jnp.where(in_band_first if j == 0 else in_band, s + bias_t_scr[...], NEG)
        m = jnp.maximum(jnp.max(s, axis=0, keepdims=True), sink)
        p = jnp.exp(s - m)
        denom = jnp.sum(p, axis=0, keepdims=True) + jnp.exp(sink - m)
        o = jnp.dot(vt_scr[:, r0:r0 + 2 * BLOCK], p.astype(BF16), preferred_element_type=F32) / denom
        a = jnp.concatenate(
            [jnp.where(row_lo, o[:, c * BLOCK:(c + 1) * BLOCK], o[:, (c + nblk) * BLOCK:(c + nblk + 1) * BLOCK])
             for c in range(nblk)], axis=0)
        a = a * lax.rsqrt(jnp.mean(a * a, axis=0, keepdims=True) + EPS) * gt_scr[...]
        attn_ref[r0:r0 + BLOCK, :] = a.T.astype(BF16)

    k_scr[0:BLOCK, :] = k_scr[TM_A:TM_A + BLOCK, :]
    vt_scr[:, 0:BLOCK] = vt_scr[:, TM_A:TM_A + BLOCK]


def _const_spec(shape):
    zeros = (0,) * len(shape)
    return pl.BlockSpec(shape, lambda *_: zeros, pipeline_mode=pl.Buffered(1))


def _smem_spec():
    return pl.BlockSpec(memory_space=pltpu.SMEM)


def _in_attn(x, rel_bias, sinks, g_mix, w_in, bucket_t, g_attn_col):
    bsz, seq, _ = x.shape
    nt = seq // TM_A
    row = lambda width: pl.BlockSpec((None, TM_A, width), lambda b, t: (b, t, 0))
    win_spec = pl.BlockSpec((None, WINDOW, KV_WIDTH), lambda b, t: (b, 0, 0))
    return pl.pallas_call(
        _in_attn_kernel,
        out_shape=(jax.ShapeDtypeStruct((bsz, seq, ATTN_WIDTH), BF16),
                   jax.ShapeDtypeStruct((bsz, seq, SSM_WIDTH), F32),
                   jax.ShapeDtypeStruct((bsz, WINDOW, KV_WIDTH), F32),
                   jax.ShapeDtypeStruct((bsz, WINDOW, KV_WIDTH), F32)),
        grid=(bsz, nt),
        in_specs=[_smem_spec(), _smem_spec(), row(D_MODEL), _const_spec((1, D_MODEL)),
                  _const_spec((D_MODEL, D_IN_PROJ)), _const_spec((2 * BLOCK, BLOCK)),
                  _const_spec((ATTN_WIDTH, 1))],
        out_specs=(row(ATTN_WIDTH), row(SSM_WIDTH), win_spec, win_spec),
        scratch_shapes=[pltpu.VMEM((TM_A + BLOCK, KV_WIDTH), BF16),
                        pltpu.VMEM((KV_WIDTH, TM_A + BLOCK), BF16),
                        pltpu.VMEM((TM_A, ATTN_WIDTH), BF16),
                        pltpu.VMEM((2 * BLOCK, N_Q_HEADS * BLOCK), F32),
                        pltpu.VMEM((ATTN_WIDTH, BLOCK), F32)],
        compiler_params=pltpu.CompilerParams(
            dimension_semantics=("arbitrary", "arbitrary"), vmem_limit_bytes=VMEM_LIMIT),
        name="in_attn",
    )(rel_bias, sinks, x, g_mix, w_in, bucket_t, g_attn_col)


def _s_in_attn_kernel(relb_ref, sinks_ref, x_ref, gmix_ref, win_ref, bucket_ref, gattn_ref,
                      ck_ref, cv_ref,
                      attn_ref, u_ref, newk_ref, newv_ref,
                      bias_scr, biasd_scr):
    steps = x_ref.shape[0] // S_SEQ
    pair = SUBLANES // steps
    prow = N_Q_HEADS * SUBLANES

    @pl.when(pl.program_id(0) == 0)
    def _():
        _build_bias(relb_ref, bucket_ref, bias_scr)
        for h in range(N_Q_HEADS):
            for k in range(pair):
                biasd_scr[h * SUBLANES + k * steps:h * SUBLANES + (k + 1) * steps, :] = bias_scr[h, 0:steps, :]

    xn = _rms(x_ref[...], gmix_ref[...]).astype(BF16)
    proj = jnp.dot(xn, win_ref[...], preferred_element_type=F32)
    u_ref[...] = proj[:, ATTN_WIDTH + 2 * KV_WIDTH:]
    qs = proj[:, :ATTN_WIDTH] * (HEAD_DIM ** -0.5)

    nkeys = 2 * WINDOW
    ri = lax.broadcasted_iota(jnp.int32, (prow, nkeys), 0)
    kj = lax.broadcasted_iota(jnp.int32, (prow, nkeys), 1)
    dist = ri % steps + WINDOW - kj
    valid = jnp.logical_and(dist >= 0, dist < WINDOW)
    head = lax.broadcasted_iota(jnp.int32, (prow, 1), 0) // SUBLANES
    sink = jnp.zeros((prow, 1), F32)
    for h in range(N_Q_HEADS):
        sink = jnp.where(head == h, sinks_ref[0, h], sink)
    seq_in_tile = (lax.broadcasted_iota(jnp.int32, (prow, LANES), 0) % SUBLANES) // steps
    lane_lo = lax.broadcasted_iota(jnp.int32, (SUBLANES, LANES), 1) < HEAD_DIM
    row8 = lax.broadcasted_iota(jnp.int32, (SUBLANES, KV_WIDTH), 0)
    gattn = gattn_ref[...]
    tail = jnp.zeros((nkeys - WINDOW - SUBLANES, KV_WIDTH), F32)
    bias = biasd_scr[...]

    tiles = []
    for tl in range(S_SEQ // pair):
        rws = slice(tl * SUBLANES, (tl + 1) * SUBLANES)
        q2 = _stack_heads(qs[rws], lane_lo).astype(BF16)
        kn8 = proj[rws, ATTN_WIDTH:ATTN_WIDTH + KV_WIDTH]
        vn8 = proj[rws, ATTN_WIDTH + KV_WIDTH:ATTN_WIDTH + 2 * KV_WIDTH]
        o_tile = jnp.zeros((prow, LANES), F32)
        for k in range(pair):
            sq = tl * pair + k
            shift = (SUBLANES - k * steps) % SUBLANES
            kn = jnp.where(row8 < steps, pltpu.roll(kn8, shift, axis=0) if shift else kn8, 0.0)
            vn = jnp.where(row8 < steps, pltpu.roll(vn8, shift, axis=0) if shift else vn8, 0.0)
            ck = ck_ref[sq]
            cv = cv_ref[sq]
            newk_ref[sq, 0:WINDOW - steps, :] = ck[steps:, :]
            newk_ref[sq, WINDOW - steps:, :] = kn[0:steps]
            newv_ref[sq, 0:WINDOW - steps, :] = cv[steps:, :]
            newv_ref[sq, WINDOW - steps:, :] = vn[0:steps]
            keys = jnp.concatenate([ck, kn, tail], axis=0).astype(BF16)
            vals = jnp.concatenate([cv, vn, tail], axis=0).astype(BF16)
            s = lax.dot_general(q2, keys, (((1,), (1,)), ((), ())), preferred_element_type=F32)
            s = jnp.where(valid, s + bias, NEG)
            m = jnp.maximum(jnp.max(s, axis=-1, keepdims=True), sink)
            p = jnp.exp(s - m)
            denom = jnp.sum(p, axis=-1, keepdims=True) + jnp.exp(sink - m)
            o = jnp.dot(p.astype(BF16), vals, preferred_element_type=F32) / denom
            o_tile = jnp.where(seq_in_tile == k, o, o_tile)
        tiles.append(_unstack_heads(o_tile, SUBLANES, lane_lo))
    attn_ref[...] = _rms(jnp.concatenate(tiles, axis=0), gattn).astype(BF16)


def _s_in_attn(x, rel_bias, sinks, g_mix, w_in, bucket, g_attn, cache_k, cache_v):
    nseq, steps, _ = x.shape
    rows = S_SEQ * steps
    xf = x.reshape(nseq * steps, D_MODEL)
    row = lambda width: pl.BlockSpec((rows, width), lambda i: (i, 0))
    cache = pl.BlockSpec((S_SEQ, WINDOW, KV_WIDTH), lambda i: (i, 0, 0))
    return pl.pallas_call(
        _s_in_attn_kernel,
        out_shape=(jax.ShapeDtypeStruct((nseq * steps, ATTN_WIDTH), BF16),
                   jax.ShapeDtypeStruct((nseq * steps, SSM_WIDTH), F32),
                   jax.ShapeDtypeStruct((nseq, WINDOW, KV_WIDTH), F32),
                   jax.ShapeDtypeStruct((nseq, WINDOW, KV_WIDTH), F32)),
        grid=(nseq // S_SEQ,),
        in_specs=[_smem_spec(), _smem_spec(), row(D_MODEL), _const_spec((1, D_MODEL)),
                  _const_spec((D_MODEL, D_IN_PROJ)), _const_spec((BLOCK, 2 * BLOCK)),
                  _const_spec((1, ATTN_WIDTH)), cache, cache],
        out_specs=(row(ATTN_WIDTH), row(SSM_WIDTH), cache, cache),
        scratch_shapes=[pltpu.VMEM((N_Q_HEADS, BLOCK, 2 * BLOCK), F32),
                        pltpu.VMEM((N_Q_HEADS * SUBLANES, 2 * WINDOW), F32)],
        compiler_params=pltpu.CompilerParams(
            dimension_semantics=("arbitrary",), vmem_limit_bytes=VMEM_LIMIT),
        name="s_in_attn",
    )(rel_bias, sinks, xf, g_mix, w_in, bucket, g_attn, cache_k, cache_v)


def _embed_blocks(bbr_ref, bbi_ref, cgr_ref, cgi_ref, bre, bim, cre, cim):
    for ref in (bre, bim, cre, cim):
        ref[...] = jnp.zeros_like(ref)
    for g in range(N_SSM_GROUPS):
        ch = slice(g * SSM_GROUP, (g + 1) * SSM_GROUP)
        st = slice(g * SSM_STATE, (g + 1) * SSM_STATE)
        bre[ch, st] = bbr_ref[g].astype(BF16)
        bim[ch, st] = bbi_ref[g].astype(BF16)
        cre[st, ch] = cgr_ref[g].astype(BF16)
        cim[st, ch] = cgi_ref[g].astype(BF16)


def _input_drive(up, bre_ref, bim_ref, hre, him):
    upb = up.astype(BF16)
    half = STATE_LANES // 2
    for kt in range(SSM_WIDTH // MXU_DIM):
        lhs = upb[:, kt * MXU_DIM:(kt + 1) * MXU_DIM]
        cols = slice(kt * half, (kt + 1) * half)
        rws = slice(kt * MXU_DIM, (kt + 1) * MXU_DIM)
        hre[:, cols] = jnp.dot(lhs, bre_ref[rws, cols], preferred_element_type=F32)
        him[:, cols] = jnp.dot(lhs, bim_ref[rws, cols], preferred_element_type=F32)


def _ssm_output(hre, him, up, cre_ref, cim_ref, d_ref, wglu_ref, bglu_ref, gssm_ref):
    ys = []
    kblk = STATE_LANES // (SSM_WIDTH // LANES)
    for n in range(SSM_WIDTH // LANES):
        rws = slice(n * kblk, (n + 1) * kblk)
        cols = slice(n * LANES, (n + 1) * LANES)
        yr = jnp.dot(hre[:, rws].astype(BF16), cre_ref[rws, cols], preferred_element_type=F32)
        yi = jnp.dot(him[:, rws].astype(BF16), cim_ref[rws, cols], preferred_element_type=F32)
        ys.append(yr - yi)
    y = jnp.concatenate(ys, axis=-1) + d_ref[...] * up
    g = jax.nn.gelu(y)
    gate = jax.nn.sigmoid(jnp.dot(g.astype(BF16), wglu_ref[...], preferred_element_type=F32) + bglu_ref[...])
    return _rms(g * gate, gssm_ref[...])


def _scan_kernel(u_ref, lr_ref, li_ref, lcr_ref, lci_ref, bbr_ref, bbi_ref, cgr_ref, cgi_ref,
                 d_ref, wglu_ref, bglu_ref, gssm_ref,
                 out_ref, hre_out, him_out,
                 bre_ref, bim_ref, cre_ref, cim_ref,
                 pad_scr, up_scr, hre, him, carry_re, carry_im, e_re, e_im, hc_re, hc_im):
    t = pl.program_id(1)
    nslab = SSM_WIDTH // LANES

    @pl.when(jnp.logical_and(pl.program_id(0) == 0, t == 0))
    def _():
        _embed_blocks(bbr_ref, bbi_ref, cgr_ref, cgi_ref, bre_ref, bim_ref, cre_ref, cim_ref)

    @pl.when(t == 0)
    def _():
        carry_re[...] = jnp.zeros_like(carry_re)
        carry_im[...] = jnp.zeros_like(carry_im)

    for c in range(SUBLANES):
        for s in range(nslab):
            pad_scr[s, c * PITCH:c * PITCH + CHUNK, :] = u_ref[c * CHUNK:(c + 1) * CHUNK, s * LANES:(s + 1) * LANES]
    for tau in range(CHUNK):
        for s in range(nslab):
            up_scr[tau * SUBLANES:(tau + 1) * SUBLANES, s * LANES:(s + 1) * LANES] = (
                pad_scr[s, pl.ds(tau, SUBLANES, stride=PITCH), :])
    up = up_scr[...]
    upb = up.astype(BF16)

    slab = 4 * LANES
    chans = slab // SSM_STATE * SSM_GROUP
    ys = []
    for sl in range(STATE_LANES // slab):
        cols = slice(sl * slab, (sl + 1) * slab)
        ch = slice(sl * chans, (sl + 1) * chans)
        hre[:, cols] = jnp.dot(upb[:, ch], bre_ref[ch, cols], preferred_element_type=F32)
        him[:, cols] = jnp.dot(upb[:, ch], bim_ref[ch, cols], preferred_element_type=F32)
        ar = jnp.broadcast_to(lr_ref[:, cols], (SUBLANES, slab))
        ai = jnp.broadcast_to(li_ref[:, cols], (SUBLANES, slab))

        hr = jnp.zeros((SUBLANES, slab), F32)
        hi = jnp.zeros((SUBLANES, slab), F32)
        for tau in range(CHUNK):
            rws = slice(tau * SUBLANES, (tau + 1) * SUBLANES)
            hr, hi = ar * hr - ai * hi + hre[rws, cols], ar * hi + ai * hr + him[rws, cols]
        e_re[:, cols] = hr
        e_im[:, cols] = hi

        cr = carry_re[:, cols]
        ci = carry_im[:, cols]
        lcr = lcr_ref[:, cols]
        lci = lci_ref[:, cols]
        for c in range(SUBLANES):
            hc_re[c:c + 1, cols] = cr
            hc_im[c:c + 1, cols] = ci
            er = e_re[c:c + 1, cols]
            ei = e_im[c:c + 1, cols]
            cr, ci = lcr * cr - lci * ci + er, lcr * ci + lci * cr + ei
        carry_re[:, cols] = cr
        carry_im[:, cols] = ci

        hr = hc_re[:, cols]
        hi = hc_im[:, cols]
        for tau in range(CHUNK):
            rws = slice(tau * SUBLANES, (tau + 1) * SUBLANES)
            hr, hi = ar * hr - ai * hi + hre[rws, cols], ar * hi + ai * hr + him[rws, cols]
            hre[rws, cols] = hr
            him[rws, cols] = hi

        yr = jnp.dot(hre[:, cols].astype(BF16), cre_ref[cols, ch], preferred_element_type=F32)
        yi = jnp.dot(him[:, cols].astype(BF16), cim_ref[cols, ch], preferred_element_type=F32)
        ys.append(yr - yi)

    y = jnp.concatenate(ys, axis=-1) + d_ref[...] * up
    g = jax.nn.gelu(y)
    gate = jax.nn.sigmoid(jnp.dot(g.astype(BF16), wglu_ref[...], preferred_element_type=F32) + bglu_ref[...])
    o = _rms(g * gate, gssm_ref[...])

    for tau in range(CHUNK):
        for s in range(nslab):
            pad_scr[s, pl.ds(tau, SUBLANES, stride=PITCH), :] = (
                o[tau * SUBLANES:(tau + 1) * SUBLANES, s * LANES:(s + 1) * LANES])
    for c in range(SUBLANES):
        for s in range(nslab):
            out_ref[c * CHUNK:(c + 1) * CHUNK, s * LANES:(s + 1) * LANES] = (
                pad_scr[s, c * PITCH:c * PITCH + CHUNK, :].astype(BF16))

    @pl.when(t == pl.num_programs(1) - 1)
    def _():
        hre_out[...] = carry_re[...]
        him_out[...] = carry_im[...]


def _block_operand_scratch():
    return [pltpu.VMEM((SSM_WIDTH, STATE_LANES), BF16), pltpu.VMEM((SSM_WIDTH, STATE_LANES), BF16),
            pltpu.VMEM((STATE_LANES, SSM_WIDTH), BF16), pltpu.VMEM((STATE_LANES, SSM_WIDTH), BF16)]


def _scan(u, lam, lam_chunk, bbar, c_gpc, d, w_glu, b_glu, g_ssm):
    bsz, seq, _ = u.shape
    nt = seq // TM_B
    row = pl.BlockSpec((None, TM_B, SSM_WIDTH), lambda b, t: (b, t, 0))
    state = pl.BlockSpec((None, 1, STATE_LANES), lambda b, t: (b, 0, 0))
    vec = _const_spec((1, STATE_LANES))
    return pl.pallas_call(
        _scan_kernel,
        out_shape=(jax.ShapeDtypeStruct((bsz, seq, SSM_WIDTH), BF16),
                   jax.ShapeDtypeStruct((bsz, 1, STATE_LANES), F32),
                   jax.ShapeDtypeStruct((bsz, 1, STATE_LANES), F32)),
        grid=(bsz, nt),
        in_specs=[row, vec, vec, vec, vec,
                  _const_spec((N_SSM_GROUPS, SSM_GROUP, SSM_STATE)), _const_spec((N_SSM_GROUPS, SSM_GROUP, SSM_STATE)),
                  _const_spec((N_SSM_GROUPS, SSM_STATE, SSM_GROUP)), _const_spec((N_SSM_GROUPS, SSM_STATE, SSM_GROUP)),
                  _const_spec((1, SSM_WIDTH)), _const_spec((SSM_WIDTH, SSM_WIDTH)),
                  _const_spec((1, SSM_WIDTH)), _const_spec((1, SSM_WIDTH))],
        out_specs=(row, state, state),
        scratch_shapes=_block_operand_scratch() + [
                        pltpu.VMEM((SSM_WIDTH // LANES, SUBLANES * PITCH, LANES), F32),
                        pltpu.VMEM((TM_B, SSM_WIDTH), F32),
                        pltpu.VMEM((TM_B, STATE_LANES), F32),
                        pltpu.VMEM((TM_B, STATE_LANES), F32),
                        pltpu.VMEM((1, STATE_LANES), F32),
                        pltpu.VMEM((1, STATE_LANES), F32),
                        pltpu.VMEM((SUBLANES, STATE_LANES), F32),
                        pltpu.VMEM((SUBLANES, STATE_LANES), F32),
                        pltpu.VMEM((SUBLANES, STATE_LANES), F32),
                        pltpu.VMEM((SUBLANES, STATE_LANES), F32)],
        compiler_params=pltpu.CompilerParams(
            dimension_semantics=("arbitrary", "arbitrary"), vmem_limit_bytes=VMEM_LIMIT),
        name="scan",
    )(u, lam[0], lam[1], lam_chunk[0], lam_chunk[1], bbar[0], bbar[1], c_gpc[0], c_gpc[1],
      d, w_glu, b_glu, g_ssm)


def _s_scan_kernel(u_ref, h0r_ref, h0i_ref, lr_ref, li_ref, bbr_ref, bbi_ref, cgr_ref, cgi_ref,
                   d_ref, wglu_ref, bglu_ref, gssm_ref,
                   out_ref, hre_out, him_out,
                   bre_ref, bim_ref, cre_ref, cim_ref,
                   pad_scr, up_scr, hre, him):
    nseq = h0r_ref.shape[0]
    steps = u_ref.shape[0] // nseq
    nslab = SSM_WIDTH // LANES
    _embed_blocks(bbr_ref, bbi_ref, cgr_ref, cgi_ref, bre_ref, bim_ref, cre_ref, cim_ref)

    for s in range(nslab):
        pad_scr[s] = u_ref[:, s * LANES:(s + 1) * LANES]
    for i in range(steps):
        for s in range(nslab):
            up_scr[i * nseq:(i + 1) * nseq, s * LANES:(s + 1) * LANES] = (
                pad_scr[s, pl.ds(i, nseq, stride=steps), :])
    up = up_scr[...]
    _input_drive(up, bre_ref, bim_ref, hre, him)

    slab = 2 * LANES
    for sl in range(STATE_LANES // slab):
        cols = slice(sl * slab, (sl + 1) * slab)
        ar = lr_ref[:, cols]
        ai = li_ref[:, cols]
        hr = h0r_ref[:, cols]
        hi = h0i_ref[:, cols]
        for i in range(steps):
            rws = slice(i * nseq, (i + 1) * nseq)
            hr, hi = ar * hr - ai * hi + hre[rws, cols], ar * hi + ai * hr + him[rws, cols]
            hre[rws, cols] = hr
            him[rws, cols] = hi
        hre_out[:, cols] = hr
        him_out[:, cols] = hi

    o = _ssm_output(hre, him, up, cre_ref, cim_ref, d_ref, wglu_ref, bglu_ref, gssm_ref)

    for i in range(steps):
        for s in range(nslab):
            pad_scr[s, pl.ds(i, nseq, stride=steps), :] = o[i * nseq:(i + 1) * nseq, s * LANES:(s + 1) * LANES]
    for s in range(nslab):
        out_ref[:, s * LANES:(s + 1) * LANES] = pad_scr[s].astype(BF16)


def _s_scan(u, h0_re, h0_im, lam, bbar, c_gpc, d, w_glu, b_glu, g_ssm):
    rows = u.shape[0]
    nseq = h0_re.shape[0]
    return pl.pallas_call(
        _s_scan_kernel,
        out_shape=(jax.ShapeDtypeStruct((rows, SSM_WIDTH), BF16),
                   jax.ShapeDtypeStruct((nseq, STATE_LANES), F32),
                   jax.ShapeDtypeStruct((nseq, STATE_LANES), F32)),
        scratch_shapes=_block_operand_scratch() + [
                        pltpu.VMEM((SSM_WIDTH // LANES, rows, LANES), F32),
                        pltpu.VMEM((rows, SSM_WIDTH), F32),
                        pltpu.VMEM((rows, STATE_LANES), F32),
                        pltpu.VMEM((rows, STATE_LANES), F32)],
        compiler_params=pltpu.CompilerParams(vmem_limit_bytes=VMEM_LIMIT),
        name="s_scan",
    )(u, h0_re, h0_im, lam[0], lam[1], bbar[0], bbar[1], c_gpc[0], c_gpc[1], d, w_glu, b_glu, g_ssm)


def _out_mlp_kernel(x_ref, a_ref, s_ref, wout_ref, gmlp_ref, wup_ref, wdown_ref, gfin_ref, y_ref):
    x1 = (x_ref[...]
          + jnp.dot(a_ref[...], wout_ref[0:ATTN_WIDTH, :], preferred_element_type=F32)
          + jnp.dot(s_ref[...], wout_ref[ATTN_WIDTH:, :], preferred_element_type=F32))
    hm = _rms(x1, gmlp_ref[...]).astype(BF16)
    y_ref[...] = x1
    ff_blk = 1024
    for c in range(D_FF // ff_blk):
        cols = slice(c * ff_blk, (c + 1) * ff_blk)
        up = jnp.dot(hm, wup_ref[:, cols], preferred_element_type=F32)
        act = jnp.square(jnp.maximum(up, 0.0)).astype(BF16)
        y_ref[...] += jnp.dot(act, wdown_ref[cols, :], preferred_element_type=F32)
    y_ref[...] = _rms(y_ref[...], gfin_ref[...])


def _out_mlp(x, attn_n, ssm_n, w_out, g_mlp, w_up, w_down, g_fin):
    rows = x.shape[0]
    row = lambda width: pl.BlockSpec((TM_C, width), lambda i: (i, 0))
    return pl.pallas_call(
        _out_mlp_kernel,
        out_shape=jax.ShapeDtypeStruct((rows, D_MODEL), F32),
        grid=(rows // TM_C,),
        in_specs=[row(D_MODEL), row(ATTN_WIDTH), row(SSM_WIDTH),
                  _const_spec((D_MODEL, D_MODEL)), _const_spec((1, D_MODEL)),
                  _const_spec((D_MODEL, D_FF)), _const_spec((D_FF, D_MODEL)),
                  _const_spec((1, D_MODEL))],
        out_specs=row(D_MODEL),
        compiler_params=pltpu.CompilerParams(
            dimension_semantics=("arbitrary",), vmem_limit_bytes=VMEM_LIMIT),
        name="out_mlp",
    )(x, attn_n, ssm_n, w_out, g_mlp, w_up, w_down, g_fin)


def _pair_heads(w, axis):
    shape = w.shape
    split = shape[:axis] + (N_KV_HEADS, Q_PER_KV, HEAD_DIM) + shape[axis + 1:]
    return jnp.swapaxes(w.reshape(split), axis, axis + 1).reshape(shape)


def kernel(x_prompt, x_sample, cache_k, cache_v, state_ssm_re, state_ssm_im, rel_bias, norm_mix, w_in, attn_sinks, ssm_a_re, ssm_a_im, ssm_log_step, ssm_b_re, ssm_b_im, ssm_c_re, ssm_c_im, ssm_d, w_glu, b_glu, norm_attn_out, norm_ssm_out, w_out, norm_mlp, w_up, w_down, norm_final):
    depth = w_in.shape[0]
    bsz, seq, _ = x_prompt.shape
    nseq, steps, _ = x_sample.shape
    bucket = jnp.asarray(_rel_bucket_table())
    g_fin = norm_final.reshape(1, D_MODEL)

    xp, xs = x_prompt, x_sample
    outs = [[] for _ in range(8)]
    for l in range(depth):
        lr, li, lcr, lci, bbr, bbi = _ssm_prep(ssm_a_re[l], ssm_a_im[l], ssm_log_step[l], ssm_b_re[l], ssm_b_im[l])
        lam = (lr.reshape(1, STATE_LANES), li.reshape(1, STATE_LANES))
        lam_chunk = (lcr.reshape(1, STATE_LANES), lci.reshape(1, STATE_LANES))
        bbar = (bbr, bbi)
        c_gpc = (jnp.transpose(ssm_c_re[l], (0, 2, 1)), jnp.transpose(ssm_c_im[l], (0, 2, 1)))
        g_mix = norm_mix[l].reshape(1, D_MODEL)
        g_attn = _pair_heads(norm_attn_out[l].reshape(1, ATTN_WIDTH), axis=1)
        g_ssm = norm_ssm_out[l].reshape(1, SSM_WIDTH)
        g_mlp = norm_mlp[l].reshape(1, D_MODEL)
        sinks = attn_sinks[l].reshape(1, N_Q_HEADS)
        d = ssm_d[l].reshape(1, SSM_WIDTH)
        bg = b_glu[l].reshape(1, SSM_WIDTH)
        w_in_b = jnp.concatenate([_pair_heads(w_in[l][:, :ATTN_WIDTH], axis=1), w_in[l][:, ATTN_WIDTH:]],
                                 axis=1).astype(BF16)
        w_glu_b = w_glu[l].astype(BF16)
        w_out_b = jnp.concatenate([_pair_heads(w_out[l][:ATTN_WIDTH], axis=0), w_out[l][ATTN_WIDTH:]],
                                  axis=0).astype(BF16)
        w_up_b = w_up[l].astype(BF16)
        w_down_b = w_down[l].astype(BF16)
        last = l == depth - 1
        g_last = g_fin if last else jnp.ones_like(g_fin)
        assert last, "the final norm is fused into the last layer's MLP kernel"

        attn_n, u, k1, v1 = _in_attn(xp, rel_bias, sinks, g_mix, w_in_b, bucket.T, g_attn.reshape(ATTN_WIDTH, 1))
        ssm_n, r1, i1 = _scan(u, lam, lam_chunk, bbar, c_gpc, d, w_glu_b, bg, g_ssm)
        xp = _out_mlp(xp.reshape(bsz * seq, D_MODEL), attn_n.reshape(bsz * seq, ATTN_WIDTH),
                      ssm_n.reshape(bsz * seq, SSM_WIDTH), w_out_b, g_mlp, w_up_b, w_down_b, g_last
                      ).reshape(bsz, seq, D_MODEL)

        ck = cache_k[l].reshape(nseq, WINDOW, KV_WIDTH)
        cv = cache_v[l].reshape(nseq, WINDOW, KV_WIDTH)
        s_attn, s_u, k2, v2 = _s_in_attn(xs, rel_bias, sinks, g_mix, w_in_b, bucket, g_attn, ck, cv)
        s_ssm, r2, i2 = _s_scan(s_u, state_ssm_re[l].reshape(nseq, STATE_LANES),
                                state_ssm_im[l].reshape(nseq, STATE_LANES),
                                lam, bbar, c_gpc, d, w_glu_b, bg, g_ssm)
        xs = _out_mlp(xs.reshape(nseq * steps, D_MODEL), s_attn, s_ssm, w_out_b, g_mlp, w_up_b, w_down_b,
                      g_last).reshape(nseq, steps, D_MODEL)

        kv_shape = (WINDOW, N_KV_HEADS, HEAD_DIM)
        st_shape = (N_SSM_GROUPS, SSM_STATE)
        for lst, val in zip(outs, (k1.reshape(bsz, *kv_shape), v1.reshape(bsz, *kv_shape),
                                   r1.reshape(bsz, *st_shape), i1.reshape(bsz, *st_shape),
                                   k2.reshape(nseq, *kv_shape), v2.reshape(nseq, *kv_shape),
                                   r2.reshape(nseq, *st_shape), i2.reshape(nseq, *st_shape))):
            lst.append(val)

    return (xp, xs) + tuple(jnp.stack(o) for o in outs)
```

```python
import functools
import math

import numpy as np
import jax
import jax.numpy as jnp
from jax import lax
from jax.experimental import pallas as pl
from jax.experimental.pallas import tpu as pltpu

D_MODEL = 1024
HEAD_DIM = 64
ATTN_WIDTH = 512
N_Q_HEADS = 8
N_KV_HEADS = 2
Q_PER_KV = 4
KV_WIDTH = 128
WINDOW = 128
BLOCK = 128
N_BUCKETS = 32
MAX_DISTANCE = 128
SSM_WIDTH = 512
SSM_GROUP = 16
N_SSM_GROUPS = 32
SSM_STATE = 64
STATE_LANES = N_SSM_GROUPS * SSM_STATE
D_FF = 4096
D_IN_PROJ = 1280
EPS = 1e-6
NEG = -1e30

LANES = 128
SUBLANES = 8
MXU_DIM = 256
VMEM_LIMIT = 56 * 1024 * 1024

F32 = jnp.float32
BF16 = jnp.bfloat16

TM_A = 512
TM_C = 512
TM_B = 512
CHUNK = TM_B // SUBLANES
PITCH = CHUNK + SUBLANES
S_SEQ = 16


def _rms(x, g):
    return x * lax.rsqrt(jnp.mean(x * x, axis=-1, keepdims=True) + EPS) * g


def _rel_bucket_table():
    i = np.arange(BLOCK)[:, None]
    j = np.arange(2 * BLOCK)[None, :]
    n = np.maximum(i + BLOCK - j, 0)
    max_exact = N_BUCKETS // 2
    nf = np.maximum(n, max_exact).astype(np.float32)
    large = max_exact + (np.log(nf / np.float32(max_exact)) / np.float32(math.log(MAX_DISTANCE / max_exact))
                         * np.float32(N_BUCKETS - max_exact)).astype(np.int32)
    large = np.minimum(large, N_BUCKETS - 1)
    return np.where(n < max_exact, n, large).astype(np.int32)


def _build_bias(relb_ref, bucket_ref, bias_scr):
    bucket = bucket_ref[...]
    for h in range(N_Q_HEADS):
        bias_scr[h] = jnp.zeros((BLOCK, 2 * BLOCK), F32)
    for b in range(N_BUCKETS):
        hit = bucket == b
        for h in range(N_Q_HEADS):
            bias_scr[h] = jnp.where(hit, relb_ref[b, h], bias_scr[h])


def _build_bias_t(relb_ref, bucket_t_ref, bias_t_scr):
    bucket_t = bucket_t_ref[...]
    bias_t_scr[...] = jnp.zeros_like(bias_t_scr)
    for b in range(N_BUCKETS):
        hit = bucket_t == b
        for h in range(N_Q_HEADS):
            cols = slice(h * BLOCK, (h + 1) * BLOCK)
            bias_t_scr[:, cols] = jnp.where(hit, relb_ref[b, h], bias_t_scr[:, cols])


def _ssm_prep_kernel(are_ref, aim_ref, ls_ref, btr_ref, bti_ref,
                     lr_ref, li_ref, lcr_ref, lci_ref, bbr_ref, bbi_ref):
    ar = are_ref[...]
    ai = aim_ref[...]
    step = jnp.exp(ls_ref[...])
    zr = ar * step
    zi = ai * step
    mag = jnp.exp(zr)
    lr = mag * jnp.cos(zi)
    li = mag * jnp.sin(zi)
    lr_ref[...] = lr
    li_ref[...] = li
    pr, pi = lr, li
    for _ in range(int(math.log2(CHUNK))):
        pr, pi = pr * pr - pi * pi, 2.0 * pr * pi
    lcr_ref[...] = pr
    lci_ref[...] = pi
    nr = lr - 1.0
    ni = li
    den = ar * ar + ai * ai
    cr = (nr * ar + ni * ai) / den
    ci = (ni * ar - nr * ai) / den
    btr = btr_ref[...]
    bti = bti_ref[...]
    bbr_ref[...] = cr * btr - ci * bti
    bbi_ref[...] = cr * bti + ci * btr


def _ssm_prep(a_re, a_im, log_step, b_re, b_im):
    g, p, c = N_SSM_GROUPS, SSM_STATE, SSM_GROUP
    vec = jax.ShapeDtypeStruct((g, 1, p), F32)
    mat = jax.ShapeDtypeStruct((g, c, p), F32)
    return pl.pallas_call(
        _ssm_prep_kernel,
        out_shape=(vec, vec, vec, vec, mat, mat),
        name="ssm_prep",
    )(a_re.reshape(g, 1, p), a_im.reshape(g, 1, p), log_step.reshape(g, 1, 1),
      jnp.transpose(b_re, (0, 2, 1)), jnp.transpose(b_im, (0, 2, 1)))


def _stack_heads(q_blk, lane_lo):
    zero = jnp.zeros_like(q_blk[:, 0:LANES])
    lo, hi = [], []
    for c in range(ATTN_WIDTH // LANES):
        blk = q_blk[:, c * LANES:(c + 1) * LANES]
        lo.append(jnp.where(lane_lo, blk, zero))
        hi.append(jnp.where(lane_lo, zero, blk))
    return jnp.concatenate(lo + hi, axis=0)


def _unstack_heads(o, rows, lane_lo):
    nblk = ATTN_WIDTH // LANES
    return jnp.concatenate(
        [jnp.where(lane_lo, o[c * rows:(c + 1) * rows], o[(c + nblk) * rows:(c + nblk + 1) * rows])
         for c in range(nblk)], axis=-1)


def _in_attn_kernel(relb_ref, sinks_ref, x_ref, gmix_ref, win_ref, bucket_t_ref, gattn_ref,
                    attn_ref, u_ref, newk_ref, newv_ref,
                    k_scr, vt_scr, q_scr, bias_t_scr, gt_scr):
    b = pl.program_id(0)
    t = pl.program_id(1)
    nt = pl.num_programs(1)
    nstk = N_Q_HEADS * BLOCK

    @pl.when(jnp.logical_and(b == 0, t == 0))
    def _():
        _build_bias_t(relb_ref, bucket_t_ref, bias_t_scr)
        gt_scr[...] = jnp.broadcast_to(gattn_ref[...], (ATTN_WIDTH, BLOCK))

    @pl.when(t == 0)
    def _():
        k_scr[0:BLOCK, :] = jnp.zeros((BLOCK, KV_WIDTH), BF16)
        vt_scr[:, 0:BLOCK] = jnp.zeros((KV_WIDTH, BLOCK), BF16)

    xn = _rms(x_ref[...], gmix_ref[...]).astype(BF16)
    proj = jnp.dot(xn, win_ref[...], preferred_element_type=F32)
    q_scr[...] = (proj[:, :ATTN_WIDTH] * (HEAD_DIM ** -0.5)).astype(BF16)
    v_new = proj[:, ATTN_WIDTH + KV_WIDTH:ATTN_WIDTH + 2 * KV_WIDTH]
    k_scr[BLOCK:BLOCK + TM_A, :] = proj[:, ATTN_WIDTH:ATTN_WIDTH + KV_WIDTH].astype(BF16)
    vt_scr[:, BLOCK:BLOCK + TM_A] = v_new.T.astype(BF16)
    u_ref[...] = proj[:, ATTN_WIDTH + 2 * KV_WIDTH:]

    @pl.when(t == nt - 1)
    def _():
        newk_ref[...] = proj[TM_A - WINDOW:, ATTN_WIDTH:ATTN_WIDTH + KV_WIDTH]
        newv_ref[...] = v_new[TM_A - WINDOW:, :]

    kj = lax.broadcasted_iota(jnp.int32, (2 * BLOCK, nstk), 0)
    qi = lax.broadcasted_iota(jnp.int32, (2 * BLOCK, nstk), 1) % BLOCK
    dist = qi + BLOCK - kj
    in_band = jnp.logical_and(dist >= 0, dist < WINDOW)
    in_band_first = jnp.logical_and(in_band, jnp.logical_or(kj >= BLOCK, t > 0))
    lane_lo = lax.broadcasted_iota(jnp.int32, (BLOCK, LANES), 1) < HEAD_DIM
    row_lo = lax.broadcasted_iota(jnp.int32, (KV_WIDTH, BLOCK), 0) < HEAD_DIM
    head = lax.broadcasted_iota(jnp.int32, (1, nstk), 1) // BLOCK
    sink = jnp.zeros((1, nstk), F32)
    for h in range(N_Q_HEADS):
        sink = jnp.where(head == h, sinks_ref[0, h], sink)
    nblk = ATTN_WIDTH // LANES

    for j in range(TM_A // BLOCK):
        r0 = j * BLOCK
        q2 = _stack_heads(q_scr[r0:r0 + BLOCK, :], lane_lo)
        keys = k_scr[r0:r0 + 2 * BLOCK, :]
        s = lax.dot_general(keys, q2, (((1,), (1,)), ((), ())), preferred_element_type=F32)
        s = jnp.where(in_band_first if j == 0 else in_band, s + bias_t_scr[...], NEG)
        m = jnp.maximum(jnp.max(s, axis=0, keepdims=True), sink)
        p = jnp.exp(s - m)
        denom = jnp.sum(p, axis=0, keepdims=True) + jnp.exp(sink - m)
        o = jnp.dot(vt_scr[:, r0:r0 + 2 * BLOCK], p.astype(BF16), preferred_element_type=F32) / denom
        a = jnp.concatenate(
            [jnp.where(row_lo, o[:, c * BLOCK:(c + 1) * BLOCK], o[:, (c + nblk) * BLOCK:(c + nblk + 1) * BLOCK])
             for c in range(nblk)], axis=0)
        a = a * lax.rsqrt(jnp.mean(a * a, axis=0, keepdims=True) + EPS) * gt_scr[...]
        attn_ref[r0:r0 + BLOCK, :] = a.T.astype(BF16)

    k_scr[0:BLOCK, :] = k_scr[TM_A:TM_A + BLOCK, :]
    vt_scr[:, 0:BLOCK] = vt_scr[:, TM_A:TM_A + BLOCK]


def _const_spec(shape):
    zeros = (0,) * len(shape)
    return pl.BlockSpec(shape, lambda *_: zeros, pipeline_mode=pl.Buffered(1))


def _smem_spec():
    return pl.BlockSpec(memory_space=pltpu.SMEM)


def _in_attn(x, rel_bias, sinks, g_mix, w_in, bucket_t, g_attn_col):
    bsz, seq, _ = x.shape
    nt = seq // TM_A
    row = lambda width: pl.BlockSpec((None, TM_A, width), lambda b, t: (b, t, 0))
    win_spec = pl.BlockSpec((None, WINDOW, KV_WIDTH), lambda b, t: (b, 0, 0))
    return pl.pallas_call(
        _in_attn_kernel,
        out_shape=(jax.ShapeDtypeStruct((bsz, seq, ATTN_WIDTH), BF16),
                   jax.ShapeDtypeStruct((bsz, seq, SSM_WIDTH), F32),
                   jax.ShapeDtypeStruct((bsz, WINDOW, KV_WIDTH), F32),
                   jax.ShapeDtypeStruct((bsz, WINDOW, KV_WIDTH), F32)),
        grid=(bsz, nt),
        in_specs=[_smem_spec(), _smem_spec(), row(D_MODEL), _const_spec((1, D_MODEL)),
                  _const_spec((D_MODEL, D_IN_PROJ)), _const_spec((2 * BLOCK, BLOCK)),
                  _const_spec((ATTN_WIDTH, 1))],
        out_specs=(row(ATTN_WIDTH), row(SSM_WIDTH), win_spec, win_spec),
        scratch_shapes=[pltpu.VMEM((TM_A + BLOCK, KV_WIDTH), BF16),
                        pltpu.VMEM((KV_WIDTH, TM_A + BLOCK), BF16),
                        pltpu.VMEM((TM_A, ATTN_WIDTH), BF16),
                        pltpu.VMEM((2 * BLOCK, N_Q_HEADS * BLOCK), F32),
                        pltpu.VMEM((ATTN_WIDTH, BLOCK), F32)],
        compiler_params=pltpu.CompilerParams(
            dimension_semantics=("arbitrary", "arbitrary"), vmem_limit_bytes=VMEM_LIMIT),
        name="in_attn",
    )(rel_bias, sinks, x, g_mix, w_in, bucket_t, g_attn_col)


def _s_in_attn_kernel(relb_ref, sinks_ref, x_ref, gmix_ref, win_ref, bucket_ref, gattn_ref,
                      ck_ref, cv_ref,
                      attn_ref, u_ref, newk_ref, newv_ref,
                      bias_scr, biasd_scr):
    steps = x_ref.shape[0] // S_SEQ
    pair = SUBLANES // steps
    prow = N_Q_HEADS * SUBLANES

    @pl.when(pl.program_id(0) == 0)
    def _():
        _build_bias(relb_ref, bucket_ref, bias_scr)
        for h in range(N_Q_HEADS):
            for k in range(pair):
                biasd_scr[h * SUBLANES + k * steps:h * SUBLANES + (k + 1) * steps, :] = bias_scr[h, 0:steps, :]

    xn = _rms(x_ref[...], gmix_ref[...]).astype(BF16)
    proj = jnp.dot(xn, win_ref[...], preferred_element_type=F32)
    u_ref[...] = proj[:, ATTN_WIDTH + 2 * KV_WIDTH:]
    qs = proj[:, :ATTN_WIDTH] * (HEAD_DIM ** -0.5)

    nkeys = 2 * WINDOW
    ri = lax.broadcasted_iota(jnp.int32, (prow, nkeys), 0)
    kj = lax.broadcasted_iota(jnp.int32, (prow, nkeys), 1)
    dist = ri % steps + WINDOW - kj
    valid = jnp.logical_and(dist >= 0, dist < WINDOW)
    head = lax.broadcasted_iota(jnp.int32, (prow, 1), 0) // SUBLANES
    sink = jnp.zeros((prow, 1), F32)
    for h in range(N_Q_HEADS):
        sink = jnp.where(head == h, sinks_ref[0, h], sink)
    seq_in_tile = (lax.broadcasted_iota(jnp.int32, (prow, LANES), 0) % SUBLANES) // steps
    lane_lo = lax.broadcasted_iota(jnp.int32, (SUBLANES, LANES), 1) < HEAD_DIM
    row8 = lax.broadcasted_iota(jnp.int32, (SUBLANES, KV_WIDTH), 0)
    gattn = gattn_ref[...]
    tail = jnp.zeros((nkeys - WINDOW - SUBLANES, KV_WIDTH), F32)
    bias = biasd_scr[...]

    tiles = []
    for tl in range(S_SEQ // pair):
        rws = slice(tl * SUBLANES, (tl + 1) * SUBLANES)
        q2 = _stack_heads(qs[rws], lane_lo).astype(BF16)
        kn8 = proj[rws, ATTN_WIDTH:ATTN_WIDTH + KV_WIDTH]
        vn8 = proj[rws, ATTN_WIDTH + KV_WIDTH:ATTN_WIDTH + 2 * KV_WIDTH]
        o_tile = jnp.zeros((prow, LANES), F32)
        for k in range(pair):
            sq = tl * pair + k
            shift = (SUBLANES - k * steps) % SUBLANES
            kn = jnp.where(row8 < steps, pltpu.roll(kn8, shift, axis=0) if shift else kn8, 0.0)
            vn = jnp.where(row8 < steps, pltpu.roll(vn8, shift, axis=0) if shift else vn8, 0.0)
            ck = ck_ref[sq]
            cv = cv_ref[sq]
            newk_ref[sq, 0:WINDOW - steps, :] = ck[steps:, :]
            newk_ref[sq, WINDOW - steps:, :] = kn[0:steps]
            newv_ref[sq, 0:WINDOW - steps, :] = cv[steps:, :]
            newv_ref[sq, WINDOW - steps:, :] = vn[0:steps]
            keys = jnp.concatenate([ck, kn, tail], axis=0).astype(BF16)
            vals = jnp.concatenate([cv, vn, tail], axis=0).astype(BF16)
            s = lax.dot_general(q2, keys, (((1,), (1,)), ((), ())), preferred_element_type=F32)
            s = jnp.where(valid, s + bias, NEG)
            m = jnp.maximum(jnp.max(s, axis=-1, keepdims=True), sink)
            p = jnp.exp(s - m)
            denom = jnp.sum(p, axis=-1, keepdims=True) + jnp.exp(sink - m)
            o = jnp.dot(p.astype(BF16), vals, preferred_element_type=F32) / denom
            o_tile = jnp.where(seq_in_tile == k, o, o_tile)
        tiles.append(_unstack_heads(o_tile, SUBLANES, lane_lo))
    attn_ref[...] = _rms(jnp.concatenate(tiles, axis=0), gattn).astype(BF16)


def _s_in_attn(x, rel_bias, sinks, g_mix, w_in, bucket, g_attn, cache_k, cache_v):
    nseq, steps, _ = x.shape
    rows = S_SEQ * steps
    xf = x.reshape(nseq * steps, D_MODEL)
    row = lambda width: pl.BlockSpec((rows, width), lambda i: (i, 0))
    cache = pl.BlockSpec((S_SEQ, WINDOW, KV_WIDTH), lambda i: (i, 0, 0))
    return pl.pallas_call(
        _s_in_attn_kernel,
        out_shape=(jax.ShapeDtypeStruct((nseq * steps, ATTN_WIDTH), BF16),
                   jax.ShapeDtypeStruct((nseq * steps, SSM_WIDTH), F32),
                   jax.ShapeDtypeStruct((nseq, WINDOW, KV_WIDTH), F32),
                   jax.ShapeDtypeStruct((nseq, WINDOW, KV_WIDTH), F32)),
        grid=(nseq // S_SEQ,),
        in_specs=[_smem_spec(), _smem_spec(), row(D_MODEL), _const_spec((1, D_MODEL)),
                  _const_spec((D_MODEL, D_IN_PROJ)), _const_spec((BLOCK, 2 * BLOCK)),
                  _const_spec((1, ATTN_WIDTH)), cache, cache],
        out_specs=(row(ATTN_WIDTH), row(SSM_WIDTH), cache, cache),
        scratch_shapes=[pltpu.VMEM((N_Q_HEADS, BLOCK, 2 * BLOCK), F32),
                        pltpu.VMEM((N_Q_HEADS * SUBLANES, 2 * WINDOW), F32)],
        compiler_params=pltpu.CompilerParams(
            dimension_semantics=("arbitrary",), vmem_limit_bytes=VMEM_LIMIT),
        name="s_in_attn",
    )(rel_bias, sinks, xf, g_mix, w_in, bucket, g_attn, cache_k, cache_v)


def _embed_blocks(bbr_ref, bbi_ref, cgr_ref, cgi_ref, bre, bim, cre, cim):
    for ref in (bre, bim, cre, cim):
        ref[...] = jnp.zeros_like(ref)
    for g in range(N_SSM_GROUPS):
        ch = slice(g * SSM_GROUP, (g + 1) * SSM_GROUP)
        st = slice(g * SSM_STATE, (g + 1) * SSM_STATE)
        bre[ch, st] = bbr_ref[g].astype(BF16)
        bim[ch, st] = bbi_ref[g].astype(BF16)
        cre[st, ch] = cgr_ref[g].astype(BF16)
        cim[st, ch] = cgi_ref[g].astype(BF16)


def _input_drive(up, bre_ref, bim_ref, hre, him):
    upb = up.astype(BF16)
    half = STATE_LANES // 2
    for kt in range(SSM_WIDTH // MXU_DIM):
        lhs = upb[:, kt * MXU_DIM:(kt + 1) * MXU_DIM]
        cols = slice(kt * half, (kt + 1) * half)
        rws = slice(kt * MXU_DIM, (kt + 1) * MXU_DIM)
        hre[:, cols] = jnp.dot(lhs, bre_ref[rws, cols], preferred_element_type=F32)
        him[:, cols] = jnp.dot(lhs, bim_ref[rws, cols], preferred_element_type=F32)


def _ssm_output(hre, him, up, cre_ref, cim_ref, d_ref, wglu_ref, bglu_ref, gssm_ref):
    ys = []
    kblk = STATE_LANES // (SSM_WIDTH // LANES)
    for n in range(SSM_WIDTH // LANES):
        rws = slice(n * kblk, (n + 1) * kblk)
        cols = slice(n * LANES, (n + 1) * LANES)
        yr = jnp.dot(hre[:, rws].astype(BF16), cre_ref[rws, cols], preferred_element_type=F32)
        yi = jnp.dot(him[:, rws].astype(BF16), cim_ref[rws, cols], preferred_element_type=F32)
        ys.append(yr - yi)
    y = jnp.concatenate(ys, axis=-1) + d_ref[...] * up
    g = jax.nn.gelu(y)
    gate = jax.nn.sigmoid(jnp.dot(g.astype(BF16), wglu_ref[...], preferred_element_type=F32) + bglu_ref[...])
    return _rms(g * gate, gssm_ref[...])


def _scan_stages(u_ref, lr_ref, li_ref, lcr_ref, lci_ref, d_ref, wglu_ref, bglu_ref, gssm_ref, out_ref,
                 bre_ref, bim_ref, cre_ref, cim_ref,
                 pad_scr, up_scr, hre, him, carry_re, carry_im, e_re, e_im, hc_re, hc_im):
    nslab = SSM_WIDTH // LANES

    for c in range(SUBLANES):
        for s in range(nslab):
            pad_scr[s, c * PITCH:c * PITCH + CHUNK, :] = u_ref[c * CHUNK:(c + 1) * CHUNK, s * LANES:(s + 1) * LANES]
    for tau in range(CHUNK):
        for s in range(nslab):
            up_scr[tau * SUBLANES:(tau + 1) * SUBLANES, s * LANES:(s + 1) * LANES] = (
                pad_scr[s, pl.ds(tau, SUBLANES, stride=PITCH), :])
    up = up_scr[...]
    upb = up.astype(BF16)

    slab = 4 * LANES
    chans = slab // SSM_STATE * SSM_GROUP
    ys = []
    for sl in range(STATE_LANES // slab):
        cols = slice(sl * slab, (sl + 1) * slab)
        ch = slice(sl * chans, (sl + 1) * chans)
        hre[...] = jnp.dot(upb[:, ch], bre_ref[ch, cols], preferred_element_type=F32)
        him[...] = jnp.dot(upb[:, ch], bim_ref[ch, cols], preferred_element_type=F32)
        yield
        ar = jnp.broadcast_to(lr_ref[:, cols], (SUBLANES, slab))
        ai = jnp.broadcast_to(li_ref[:, cols], (SUBLANES, slab))

        hr = jnp.zeros((SUBLANES, slab), F32)
        hi = jnp.zeros((SUBLANES, slab), F32)
        for tau in range(CHUNK):
            rws = slice(tau * SUBLANES, (tau + 1) * SUBLANES)
            hr, hi = ar * hr - ai * hi + hre[rws, :], ar * hi + ai * hr + him[rws, :]
        e_re[:, cols] = hr
        e_im[:, cols] = hi

        cr = carry_re[:, cols]
        ci = carry_im[:, cols]
        lcr = lcr_ref[:, cols]
        lci = lci_ref[:, cols]
        for c in range(SUBLANES):
            hc_re[c:c + 1, cols] = cr
            hc_im[c:c + 1, cols] = ci
            er = e_re[c:c + 1, cols]
            ei = e_im[c:c + 1, cols]
            cr, ci = lcr * cr - lci * ci + er, lcr * ci + lci * cr + ei
        carry_re[:, cols] = cr
        carry_im[:, cols] = ci

        hr = hc_re[:, cols]
        hi = hc_im[:, cols]
        for tau in range(CHUNK):
            rws = slice(tau * SUBLANES, (tau + 1) * SUBLANES)
            hr, hi = ar * hr - ai * hi + hre[rws, :], ar * hi + ai * hr + him[rws, :]
            hre[rws, :] = hr
            him[rws, :] = hi

        yr = jnp.dot(hre[...].astype(BF16), cre_ref[cols, ch], preferred_element_type=F32)
        yi = jnp.dot(him[...].astype(BF16), cim_ref[cols, ch], preferred_element_type=F32)
        ys.append(yr - yi)
        yield

    y = jnp.concatenate(ys, axis=-1) + d_ref[...] * up
    g = jax.nn.gelu(y)
    gate = jax.nn.sigmoid(jnp.dot(g.astype(BF16), wglu_ref[...], preferred_element_type=F32) + bglu_ref[...])
    o = _rms(g * gate, gssm_ref[...])

    for tau in range(CHUNK):
        for s in range(nslab):
            pad_scr[s, pl.ds(tau, SUBLANES, stride=PITCH), :] = (
                o[tau * SUBLANES:(tau + 1) * SUBLANES, s * LANES:(s + 1) * LANES])
    for c in range(SUBLANES):
        for s in range(nslab):
            out_ref[c * CHUNK:(c + 1) * CHUNK, s * LANES:(s + 1) * LANES] = (
                pad_scr[s, c * PITCH:c * PITCH + CHUNK, :].astype(BF16))
    yield


def _scan_mlp_kernel(x_ref, a_ref, u_ref, lr_ref, li_ref, lcr_ref, lci_ref, bbr_ref, bbi_ref, cgr_ref, cgi_ref,
                     d_ref, wglu_ref, bglu_ref, gssm_ref, wout_ref, gmlp_ref, wup_ref, wdown_ref, gfin_ref,
                     y_ref, hre_out, him_out,
                     ssm_scr, bre_ref, bim_ref, cre_ref, cim_ref,
                     pad_scr, up_scr, hre, him, carry_re, carry_im, e_re, e_im, hc_re, hc_im):
    first = jnp.logical_and(pl.program_id(0) == 0, pl.program_id(1) == 0)
    t = pl.program_id(1)
    nt = pl.num_programs(1) - 1

    @pl.when(first)
    def _():
        _embed_blocks(bbr_ref, bbi_ref, cgr_ref, cgi_ref, bre_ref, bim_ref, cre_ref, cim_ref)
        ssm_scr[...] = jnp.zeros_like(ssm_scr)

    @pl.when(t == 0)
    def _():
        carry_re[...] = jnp.zeros_like(carry_re)
        carry_im[...] = jnp.zeros_like(carry_im)

    mlp = _mlp_stages(x_ref, a_ref, ssm_scr, wout_ref, gmlp_ref, wup_ref, wdown_ref, gfin_ref, y_ref)
    scan = _scan_stages(u_ref, lr_ref, li_ref, lcr_ref, lci_ref, d_ref, wglu_ref, bglu_ref, gssm_ref, ssm_scr,
                        bre_ref, bim_ref, cre_ref, cim_ref,
                        pad_scr, up_scr, hre, him, carry_re, carry_im, e_re, e_im, hc_re, hc_im)
    for stage in "ms" + "mss" * 3 + "ms" * 2 + "m" * 4:
        next(scan if stage == "s" else mlp)
    assert next(scan, None) is None and next(mlp, None) is None

    @pl.when(t == nt - 1)
    def _():
        hre_out[...] = carry_re[...]
        him_out[...] = carry_im[...]


def _block_operand_scratch():
    return [pltpu.VMEM((SSM_WIDTH, STATE_LANES), BF16), pltpu.VMEM((SSM_WIDTH, STATE_LANES), BF16),
            pltpu.VMEM((STATE_LANES, SSM_WIDTH), BF16), pltpu.VMEM((STATE_LANES, SSM_WIDTH), BF16)]


def _scan_mlp(x, attn_n, u, lam, lam_chunk, bbar, c_gpc, d, w_glu, b_glu, g_ssm, w_out, g_mlp, w_up, w_down, g_fin):
    bsz, seq, _ = u.shape
    nt = seq // TM_B
    cur = lambda width: pl.BlockSpec((None, TM_B, width), lambda b, t: (b, jnp.minimum(t, nt - 1), 0))
    prev = lambda width: pl.BlockSpec((None, TM_B, width), lambda b, t: (b, jnp.maximum(t - 1, 0), 0))
    state = pl.BlockSpec((None, 1, STATE_LANES), lambda b, t: (b, 0, 0))
    vec = _const_spec((1, STATE_LANES))
    blk_b = _const_spec((N_SSM_GROUPS, SSM_GROUP, SSM_STATE))
    blk_c = _const_spec((N_SSM_GROUPS, SSM_STATE, SSM_GROUP))
    chan = _const_spec((1, SSM_WIDTH))
    model = _const_spec((1, D_MODEL))
    return pl.pallas_call(
        _scan_mlp_kernel,
        out_shape=(jax.ShapeDtypeStruct((bsz, seq, D_MODEL), F32),
                   jax.ShapeDtypeStruct((bsz, 1, STATE_LANES), F32),
                   jax.ShapeDtypeStruct((bsz, 1, STATE_LANES), F32)),
        grid=(bsz, nt + 1),
        in_specs=[prev(D_MODEL), prev(ATTN_WIDTH), cur(SSM_WIDTH), vec, vec, vec, vec,
                  blk_b, blk_b, blk_c, blk_c, chan, _const_spec((SSM_WIDTH, SSM_WIDTH)), chan, chan,
                  _const_spec((D_MODEL, D_MODEL)), model, _const_spec((D_MODEL, D_FF)),
                  _const_spec((D_FF, D_MODEL)), model],
        out_specs=(prev(D_MODEL), state, state),
        scratch_shapes=[pltpu.VMEM((TM_B, SSM_WIDTH), BF16)] + _block_operand_scratch() + [
                        pltpu.VMEM((SSM_WIDTH // LANES, SUBLANES * PITCH, LANES), F32),
                        pltpu.VMEM((TM_B, SSM_WIDTH), F32),
                        pltpu.VMEM((TM_B, 4 * LANES), F32),
                        pltpu.VMEM((TM_B, 4 * LANES), F32),
                        pltpu.VMEM((1, STATE_LANES), F32),
                        pltpu.VMEM((1, STATE_LANES), F32),
                        pltpu.VMEM((SUBLANES, STATE_LANES), F32),
                        pltpu.VMEM((SUBLANES, STATE_LANES), F32),
                        pltpu.VMEM((SUBLANES, STATE_LANES), F32),
                        pltpu.VMEM((SUBLANES, STATE_LANES), F32)],
        compiler_params=pltpu.CompilerParams(
            dimension_semantics=("arbitrary", "arbitrary"), vmem_limit_bytes=VMEM_LIMIT),
        name="scan_mlp",
    )(x, attn_n, u, lam[0], lam[1], lam_chunk[0], lam_chunk[1], bbar[0], bbar[1], c_gpc[0], c_gpc[1],
      d, w_glu, b_glu, g_ssm, w_out, g_mlp, w_up, w_down, g_fin)


def _s_scan_kernel(u_ref, h0r_ref, h0i_ref, lr_ref, li_ref, bbr_ref, bbi_ref, cgr_ref, cgi_ref,
                   d_ref, wglu_ref, bglu_ref, gssm_ref,
                   out_ref, hre_out, him_out,
                   bre_ref, bim_ref, cre_ref, cim_ref,
                   pad_scr, up_scr, hre, him):
    nseq = h0r_ref.shape[0]
    steps = u_ref.shape[0] // nseq
    nslab = SSM_WIDTH // LANES
    _embed_blocks(bbr_ref, bbi_ref, cgr_ref, cgi_ref, bre_ref, bim_ref, cre_ref, cim_ref)

    for s in range(nslab):
        pad_scr[s] = u_ref[:, s * LANES:(s + 1) * LANES]
    for i in range(steps):
        for s in range(nslab):
            up_scr[i * nseq:(i + 1) * nseq, s * LANES:(s + 1) * LANES] = (
                pad_scr[s, pl.ds(i, nseq, stride=steps), :])
    up = up_scr[...]
    _input_drive(up, bre_ref, bim_ref, hre, him)

    slab = 2 * LANES
    for sl in range(STATE_LANES // slab):
        cols = slice(sl * slab, (sl + 1) * slab)
        ar = lr_ref[:, cols]
        ai = li_ref[:, cols]
        hr = h0r_ref[:, cols]
        hi = h0i_ref[:, cols]
        for i in range(steps):
            rws = slice(i * nseq, (i + 1) * nseq)
            hr, hi = ar * hr - ai * hi + hre[rws, cols], ar * hi + ai * hr + him[rws, cols]
            hre[rws, cols] = hr
            him[rws, cols] = hi
        hre_out[:, cols] = hr
        him_out[:, cols] = hi

    o = _ssm_output(hre, him, up, cre_ref, cim_ref, d_ref, wglu_ref, bglu_ref, gssm_ref)

    for i in range(steps):
        for s in range(nslab):
            pad_scr[s, pl.ds(i, nseq, stride=steps), :] = o[i * nseq:(i + 1) * nseq, s * LANES:(s + 1) * LANES]
    for s in range(nslab):
        out_ref[:, s * LANES:(s + 1) * LANES] = pad_scr[s].astype(BF16)


def _s_scan(u, h0_re, h0_im, lam, bbar, c_gpc, d, w_glu, b_glu, g_ssm):
    rows = u.shape[0]
    nseq = h0_re.shape[0]
    return pl.pallas_call(
        _s_scan_kernel,
        out_shape=(jax.ShapeDtypeStruct((rows, SSM_WIDTH), BF16),
                   jax.ShapeDtypeStruct((nseq, STATE_LANES), F32),
                   jax.ShapeDtypeStruct((nseq, STATE_LANES), F32)),
        scratch_shapes=_block_operand_scratch() + [
                        pltpu.VMEM((SSM_WIDTH // LANES, rows, LANES), F32),
                        pltpu.VMEM((rows, SSM_WIDTH), F32),
                        pltpu.VMEM((rows, STATE_LANES), F32),
                        pltpu.VMEM((rows, STATE_LANES), F32)],
        compiler_params=pltpu.CompilerParams(vmem_limit_bytes=VMEM_LIMIT),
        name="s_scan",
    )(u, h0_re, h0_im, lam[0], lam[1], bbar[0], bbar[1], c_gpc[0], c_gpc[1], d, w_glu, b_glu, g_ssm)


def _mlp_stages(x_ref, a_ref, s_ref, wout_ref, gmlp_ref, wup_ref, wdown_ref, gfin_ref, y_ref):
    x1 = (x_ref[...]
          + jnp.dot(a_ref[...], wout_ref[0:ATTN_WIDTH, :], preferred_element_type=F32)
          + jnp.dot(s_ref[...], wout_ref[ATTN_WIDTH:, :], preferred_element_type=F32))
    hm = _rms(x1, gmlp_ref[...]).astype(BF16)
    yield
    ff_blk = D_FF // 4
    parts = []
    for c in range(4):
        cols = slice(c * ff_blk, (c + 1) * ff_blk)
        up = jnp.dot(hm, wup_ref[:, cols], preferred_element_type=F32)
        act = jnp.square(jnp.maximum(up, 0.0)).astype(BF16)
        yield
        parts.append(jnp.dot(act, wdown_ref[cols, :], preferred_element_type=F32))
        yield
    y_ref[...] = _rms(x1 + ((parts[0] + parts[1]) + (parts[2] + parts[3])), gfin_ref[...])
    yield


def _out_mlp_kernel(*refs):
    for _ in _mlp_stages(*refs):
        pass


def _out_mlp(x, attn_n, ssm_n, w_out, g_mlp, w_up, w_down, g_fin):
    rows = x.shape[0]
    row = lambda width: pl.BlockSpec((TM_C, width), lambda i: (i, 0))
    return pl.pallas_call(
        _out_mlp_kernel,
        out_shape=jax.ShapeDtypeStruct((rows, D_MODEL), F32),
        grid=(rows // TM_C,),
        in_specs=[row(D_MODEL), row(ATTN_WIDTH), row(SSM_WIDTH),
                  _const_spec((D_MODEL, D_MODEL)), _const_spec((1, D_MODEL)),
                  _const_spec((D_MODEL, D_FF)), _const_spec((D_FF, D_MODEL)),
                  _const_spec((1, D_MODEL))],
        out_specs=row(D_MODEL),
        compiler_params=pltpu.CompilerParams(
            dimension_semantics=("arbitrary",), vmem_limit_bytes=VMEM_LIMIT),
        name="out_mlp",
    )(x, attn_n, ssm_n, w_out, g_mlp, w_up, w_down, g_fin)


def _pair_heads(w, axis):
    shape = w.shape
    split = shape[:axis] + (N_KV_HEADS, Q_PER_KV, HEAD_DIM) + shape[axis + 1:]
    return jnp.swapaxes(w.reshape(split), axis, axis + 1).reshape(shape)


def kernel(x_prompt, x_sample, cache_k, cache_v, state_ssm_re, state_ssm_im, rel_bias, norm_mix, w_in, attn_sinks, ssm_a_re, ssm_a_im, ssm_log_step, ssm_b_re, ssm_b_im, ssm_c_re, ssm_c_im, ssm_d, w_glu, b_glu, norm_attn_out, norm_ssm_out, w_out, norm_mlp, w_up, w_down, norm_final):
    depth = w_in.shape[0]
    bsz, seq, _ = x_prompt.shape
    nseq, steps, _ = x_sample.shape
    bucket = jnp.asarray(_rel_bucket_table())
    g_fin = norm_final.reshape(1, D_MODEL)

    xp, xs = x_prompt, x_sample
    outs = [[] for _ in range(8)]
    for l in range(depth):
        lr, li, lcr, lci, bbr, bbi = _ssm_prep(ssm_a_re[l], ssm_a_im[l], ssm_log_step[l], ssm_b_re[l], ssm_b_im[l])
        lam = (lr.reshape(1, STATE_LANES), li.reshape(1, STATE_LANES))
        lam_chunk = (lcr.reshape(1, STATE_LANES), lci.reshape(1, STATE_LANES))
        bbar = (bbr, bbi)
        c_gpc = (jnp.transpose(ssm_c_re[l], (0, 2, 1)), jnp.transpose(ssm_c_im[l], (0, 2, 1)))
        g_mix = norm_mix[l].reshape(1, D_MODEL)
        g_attn = _pair_heads(norm_attn_out[l].reshape(1, ATTN_WIDTH), axis=1)
        g_ssm = norm_ssm_out[l].reshape(1, SSM_WIDTH)
        g_mlp = norm_mlp[l].reshape(1, D_MODEL)
        sinks = attn_sinks[l].reshape(1, N_Q_HEADS)
        d = ssm_d[l].reshape(1, SSM_WIDTH)
        bg = b_glu[l].reshape(1, SSM_WIDTH)
        w_in_b = jnp.concatenate([_pair_heads(w_in[l][:, :ATTN_WIDTH], axis=1), w_in[l][:, ATTN_WIDTH:]],
                                 axis=1).astype(BF16)
        w_glu_b = w_glu[l].astype(BF16)
        w_out_b = jnp.concatenate([_pair_heads(w_out[l][:ATTN_WIDTH], axis=0), w_out[l][ATTN_WIDTH:]],
                                  axis=0).astype(BF16)
        w_up_b = w_up[l].astype(BF16)
        w_down_b = w_down[l].astype(BF16)
        last = l == depth - 1
        g_last = g_fin if last else jnp.ones_like(g_fin)
        assert last, "the final norm is fused into the last layer's MLP kernel"

        attn_n, u, k1, v1 = _in_attn(xp, rel_bias, sinks, g_mix, w_in_b, bucket.T, g_attn.reshape(ATTN_WIDTH, 1))
        xp, r1, i1 = _scan_mlp(xp, attn_n, u, lam, lam_chunk, bbar, c_gpc, d, w_glu_b, bg, g_ssm,
                               w_out_b, g_mlp, w_up_b, w_down_b, g_last)

        ck = cache_k[l].reshape(nseq, WINDOW, KV_WIDTH)
        cv = cache_v[l].reshape(nseq, WINDOW, KV_WIDTH)
        s_attn, s_u, k2, v2 = _s_in_attn(xs, rel_bias, sinks, g_mix, w_in_b, bucket, g_attn, ck, cv)
        s_ssm, r2, i2 = _s_scan(s_u, state_ssm_re[l].reshape(nseq, STATE_LANES),
                                state_ssm_im[l].reshape(nseq, STATE_LANES),
                                lam, bbar, c_gpc, d, w_glu_b, bg, g_ssm)
        xs = _out_mlp(xs.reshape(nseq * steps, D_MODEL), s_attn, s_ssm, w_out_b, g_mlp, w_up_b, w_down_b,
                      g_last).reshape(nseq, steps, D_MODEL)

        kv_shape = (WINDOW, N_KV_HEADS, HEAD_DIM)
        st_shape = (N_SSM_GROUPS, SSM_STATE)
        for lst, val in zip(outs, (k1.reshape(bsz, *kv_shape), v1.reshape(bsz, *kv_shape),
                                   r1.reshape(bsz, *st_shape), i1.reshape(bsz, *st_shape),
                                   k2.reshape(nseq, *kv_shape), v2.reshape(nseq, *kv_shape),
                                   r2.reshape(nseq, *st_shape), i2.reshape(nseq, *st_shape))):
            lst.append(val)

    return (xp, xs) + tuple(jnp.stack(o) for o in outs)
```

```python
import math

import numpy as np
import jax
import jax.numpy as jnp
from jax import lax
from jax.experimental import pallas as pl
from jax.experimental.pallas import tpu as pltpu

D_MODEL = 1024
HEAD_DIM = 64
ATTN_WIDTH = 512
N_Q_HEADS = 8
N_KV_HEADS = 2
Q_PER_KV = 4
KV_WIDTH = 128
WINDOW = 128
BLOCK = 128
N_BUCKETS = 32
MAX_DISTANCE = 128
SSM_WIDTH = 512
SSM_GROUP = 16
N_SSM_GROUPS = 32
SSM_STATE = 64
STATE_LANES = N_SSM_GROUPS * SSM_STATE
D_FF = 4096
D_IN_PROJ = 1280
EPS = 1e-6
NEG = -1e30

LANES = 128
SUBLANES = 8
MXU_DIM = 256
VMEM_LIMIT = 56 * 1024 * 1024

F32 = jnp.float32
BF16 = jnp.bfloat16

TM_A = 512
TM_C = 512
TM_B = 512
CHUNK = TM_B // SUBLANES
PITCH = CHUNK + SUBLANES
S_SEQ = 16


def _rms(x, g):
    return x * lax.rsqrt(jnp.mean(x * x, axis=-1, keepdims=True) + EPS) * g


def _rel_bucket_table():
    i = np.arange(BLOCK)[:, None]
    j = np.arange(2 * BLOCK)[None, :]
    n = np.maximum(i + BLOCK - j, 0)
    max_exact = N_BUCKETS // 2
    nf = np.maximum(n, max_exact).astype(np.float32)
    large = max_exact + (np.log(nf / np.float32(max_exact)) / np.float32(math.log(MAX_DISTANCE / max_exact))
                         * np.float32(N_BUCKETS - max_exact)).astype(np.int32)
    large = np.minimum(large, N_BUCKETS - 1)
    return np.where(n < max_exact, n, large).astype(np.int32)


def _build_bias(relb_ref, bucket_ref, bias_scr):
    bucket = bucket_ref[...]
    for h in range(N_Q_HEADS):
        bias_scr[h] = jnp.zeros((BLOCK, 2 * BLOCK), F32)
    for b in range(N_BUCKETS):
        hit = bucket == b
        for h in range(N_Q_HEADS):
            bias_scr[h] = jnp.where(hit, relb_ref[b, h], bias_scr[h])


def _build_bias_t(relb_ref, bucket_t_ref, bias_t_scr):
    bucket_t = bucket_t_ref[...]
    bias_t_scr[...] = jnp.zeros_like(bias_t_scr)
    for b in range(N_BUCKETS):
        hit = bucket_t == b
        for h in range(N_Q_HEADS):
            cols = slice(h * BLOCK, (h + 1) * BLOCK)
            bias_t_scr[:, cols] = jnp.where(hit, relb_ref[b, h], bias_t_scr[:, cols])


def _ssm_prep_kernel(are_ref, aim_ref, ls_ref, btr_ref, bti_ref,
                     lr_ref, li_ref, lcr_ref, lci_ref, bbr_ref, bbi_ref):
    ar = are_ref[...]
    ai = aim_ref[...]
    step = jnp.exp(ls_ref[...])
    zr = ar * step
    zi = ai * step
    mag = jnp.exp(zr)
    lr = mag * jnp.cos(zi)
    li = mag * jnp.sin(zi)
    pr, pi = lr, li
    for _ in range(int(math.log2(CHUNK))):
        pr, pi = pr * pr - pi * pi, 2.0 * pr * pi
    for g in range(N_SSM_GROUPS):
        st = slice(g * SSM_STATE, (g + 1) * SSM_STATE)
        lr_ref[:, st] = lr[g]
        li_ref[:, st] = li[g]
        lcr_ref[:, st] = pr[g]
        lci_ref[:, st] = pi[g]
    nr = lr - 1.0
    ni = li
    den = ar * ar + ai * ai
    cr = (nr * ar + ni * ai) / den
    ci = (ni * ar - nr * ai) / den
    btr = btr_ref[...]
    bti = bti_ref[...]
    bbr_ref[...] = cr * btr - ci * bti
    bbi_ref[...] = cr * bti + ci * btr


def _ssm_prep(a_re, a_im, log_step, b_re, b_im):
    g, p, c = N_SSM_GROUPS, SSM_STATE, SSM_GROUP
    vec = jax.ShapeDtypeStruct((1, STATE_LANES), F32)
    mat = jax.ShapeDtypeStruct((g, c, p), F32)
    return pl.pallas_call(
        _ssm_prep_kernel,
        out_shape=(vec, vec, vec, vec, mat, mat),
        name="ssm_prep",
    )(a_re.reshape(g, 1, p), a_im.reshape(g, 1, p), log_step.reshape(g, 1, 1),
      jnp.transpose(b_re, (0, 2, 1)), jnp.transpose(b_im, (0, 2, 1)))


def _stack_heads(q_blk, lane_lo):
    zero = jnp.zeros_like(q_blk[:, 0:LANES])
    lo, hi = [], []
    for c in range(ATTN_WIDTH // LANES):
        blk = q_blk[:, c * LANES:(c + 1) * LANES]
        lo.append(jnp.where(lane_lo, blk, zero))
        hi.append(jnp.where(lane_lo, zero, blk))
    return jnp.concatenate(lo + hi, axis=0)


def _unstack_heads(o, rows, lane_lo):
    nblk = ATTN_WIDTH // LANES
    return jnp.concatenate(
        [jnp.where(lane_lo, o[c * rows:(c + 1) * rows], o[(c + nblk) * rows:(c + nblk + 1) * rows])
         for c in range(nblk)], axis=-1)


def _in_attn_kernel(relb_ref, sinks_ref, x_ref, gmix_ref, win_ref, bucket_t_ref, gattn_ref,
                    wout_f, wglu_f, wup_f, wdown_f,
                    attn_ref, u_ref, newk_ref, newv_ref, wout_b, wglu_b, wup_b, wdown_b,
                    k_scr, vt_scr, q_scr, bias_t_scr, gt_scr):
    b = pl.program_id(0)
    t = pl.program_id(1)
    nt = pl.num_programs(1)
    nstk = N_Q_HEADS * BLOCK

    wout_b[...] = wout_f[...].astype(BF16)
    wglu_b[...] = wglu_f[...].astype(BF16)
    wup_b[...] = wup_f[...].astype(BF16)
    wdown_b[...] = wdown_f[...].astype(BF16)

    @pl.when(jnp.logical_and(b == 0, t == 0))
    def _():
        _build_bias_t(relb_ref, bucket_t_ref, bias_t_scr)
        gt_scr[...] = jnp.broadcast_to(gattn_ref[...], (ATTN_WIDTH, BLOCK))

    @pl.when(t == 0)
    def _():
        k_scr[0:BLOCK, :] = jnp.zeros((BLOCK, KV_WIDTH), BF16)
        vt_scr[:, 0:BLOCK] = jnp.zeros((KV_WIDTH, BLOCK), BF16)

    xn = _rms(x_ref[...], gmix_ref[...]).astype(BF16)
    proj = jnp.dot(xn, win_ref[...], preferred_element_type=F32)
    q_scr[...] = (proj[:, :ATTN_WIDTH] * (HEAD_DIM ** -0.5)).astype(BF16)
    v_new = proj[:, ATTN_WIDTH + KV_WIDTH:ATTN_WIDTH + 2 * KV_WIDTH]
    k_scr[BLOCK:BLOCK + TM_A, :] = proj[:, ATTN_WIDTH:ATTN_WIDTH + KV_WIDTH].astype(BF16)
    vt_scr[:, BLOCK:BLOCK + TM_A] = v_new.T.astype(BF16)
    u_ref[...] = proj[:, ATTN_WIDTH + 2 * KV_WIDTH:]

    @pl.when(t == nt - 1)
    def _():
        newk_ref[...] = proj[TM_A - WINDOW:, ATTN_WIDTH:ATTN_WIDTH + KV_WIDTH]
        newv_ref[...] = v_new[TM_A - WINDOW:, :]

    kj = lax.broadcasted_iota(jnp.int32, (2 * BLOCK, nstk), 0)
    qi = lax.broadcasted_iota(jnp.int32, (2 * BLOCK, nstk), 1) % BLOCK
    dist = qi + BLOCK - kj
    in_band = jnp.logical_and(dist >= 0, dist < WINDOW)
    in_band_first = jnp.logical_and(in_band, jnp.logical_or(kj >= BLOCK, t > 0))
    lane_lo = lax.broadcasted_iota(jnp.int32, (BLOCK, LANES), 1) < HEAD_DIM
    row_lo = lax.broadcasted_iota(jnp.int32, (KV_WIDTH, BLOCK), 0) < HEAD_DIM
    head = lax.broadcasted_iota(jnp.int32, (1, nstk), 1) // BLOCK
    sink = jnp.zeros((1, nstk), F32)
    for h in range(N_Q_HEADS):
        sink = jnp.where(head == h, sinks_ref[0, h], sink)
    nblk = ATTN_WIDTH // LANES

    for j in range(TM_A // BLOCK):
        r0 = j * BLOCK
        q2 = _stack_heads(q_scr[r0:r0 + BLOCK, :], lane_lo)
        keys = k_scr[r0:r0 + 2 * BLOCK, :]
        s = lax.dot_general(keys, q2, (((1,), (1,)), ((), ())), preferred_element_type=F32)
        s = jnp.where(in_band_first if j == 0 else in_band, s + bias_t_scr[...], NEG)
        m = jnp.maximum(jnp.max(s, axis=0, keepdims=True), sink)
        p = jnp.exp(s - m)
        denom = jnp.sum(p, axis=0, keepdims=True) + jnp.exp(sink - m)
        o = jnp.dot(vt_scr[:, r0:r0 + 2 * BLOCK], p.astype(BF16), preferred_element_type=F32) / denom
        a = jnp.concatenate(
            [jnp.where(row_lo, o[:, c * BLOCK:(c + 1) * BLOCK], o[:, (c + nblk) * BLOCK:(c + nblk + 1) * BLOCK])
             for c in range(nblk)], axis=0)
        a = a * lax.rsqrt(jnp.mean(a * a, axis=0, keepdims=True) + EPS) * gt_scr[...]
        attn_ref[r0:r0 + BLOCK, :] = a.T.astype(BF16)

    k_scr[0:BLOCK, :] = k_scr[TM_A:TM_A + BLOCK, :]
    vt_scr[:, 0:BLOCK] = vt_scr[:, TM_A:TM_A + BLOCK]


def _const_spec(shape):
    zeros = (0,) * len(shape)
    return pl.BlockSpec(shape, lambda *_: zeros, pipeline_mode=pl.Buffered(1))


def _smem_spec():
    return pl.BlockSpec(memory_space=pltpu.SMEM)


def _in_attn(x, rel_bias, sinks, g_mix, w_in, bucket_t, g_attn_col, w_out, w_glu, w_up, w_down):
    bsz, seq, _ = x.shape
    nt = seq // TM_A
    steps = bsz * nt
    row = lambda width: pl.BlockSpec((None, TM_A, width), lambda b, t: (b, t, 0))
    win_spec = pl.BlockSpec((None, WINDOW, KV_WIDTH), lambda b, t: (b, 0, 0))

    def rows_of(w):
        blk = w.shape[0] // steps
        return pl.BlockSpec((blk, w.shape[1]), lambda b, t: (b * nt + t, 0))

    nhead_blk = D_MODEL // HEAD_DIM

    def wout_src(b, t):
        return (jnp.minimum(b * nt + t, nhead_blk - 1), 0)

    def wout_dst(b, t):
        j = jnp.minimum(b * nt + t, nhead_blk - 1)
        return (jnp.where(j < N_Q_HEADS, (j % Q_PER_KV) * N_KV_HEADS + j // Q_PER_KV, j), 0)

    assert steps >= nhead_blk and w_up.shape[0] % steps == 0 and w_down.shape[0] % steps == 0
    wout_in = pl.BlockSpec((HEAD_DIM, D_MODEL), wout_src)
    wout_out = pl.BlockSpec((HEAD_DIM, D_MODEL), wout_dst)
    cast_shape = lambda w: jax.ShapeDtypeStruct(w.shape, BF16)
    return pl.pallas_call(
        _in_attn_kernel,
        out_shape=(jax.ShapeDtypeStruct((bsz, seq, ATTN_WIDTH), BF16),
                   jax.ShapeDtypeStruct((bsz, seq, SSM_WIDTH), F32),
                   jax.ShapeDtypeStruct((bsz, WINDOW, KV_WIDTH), F32),
                   jax.ShapeDtypeStruct((bsz, WINDOW, KV_WIDTH), F32),
                   cast_shape(w_out), cast_shape(w_glu), cast_shape(w_up), cast_shape(w_down)),
        grid=(bsz, nt),
        in_specs=[_smem_spec(), _smem_spec(), row(D_MODEL), _const_spec((1, D_MODEL)),
                  _const_spec((D_MODEL, D_IN_PROJ)), _const_spec((2 * BLOCK, BLOCK)),
                  _const_spec((ATTN_WIDTH, 1)),
                  wout_in, rows_of(w_glu), rows_of(w_up), rows_of(w_down)],
        out_specs=(row(ATTN_WIDTH), row(SSM_WIDTH), win_spec, win_spec,
                   wout_out, rows_of(w_glu), rows_of(w_up), rows_of(w_down)),
        scratch_shapes=[pltpu.VMEM((TM_A + BLOCK, KV_WIDTH), BF16),
                        pltpu.VMEM((KV_WIDTH, TM_A + BLOCK), BF16),
                        pltpu.VMEM((TM_A, ATTN_WIDTH), BF16),
                        pltpu.VMEM((2 * BLOCK, N_Q_HEADS * BLOCK), F32),
                        pltpu.VMEM((ATTN_WIDTH, BLOCK), F32)],
        compiler_params=pltpu.CompilerParams(
            dimension_semantics=("arbitrary", "arbitrary"), vmem_limit_bytes=VMEM_LIMIT),
        name="in_attn",
    )(rel_bias, sinks, x, g_mix, w_in, bucket_t, g_attn_col, w_out, w_glu, w_up, w_down)


def _s_in_attn_kernel(relb_ref, sinks_ref, x_ref, gmix_ref, win_ref, bucket_ref, gattn_ref,
                      ck_ref, cv_ref,
                      attn_ref, u_ref, newk_ref, newv_ref,
                      bias_scr, biasd_scr):
    steps = x_ref.shape[0] // S_SEQ
    pair = SUBLANES // steps
    prow = N_Q_HEADS * SUBLANES

    @pl.when(pl.program_id(0) == 0)
    def _():
        _build_bias(relb_ref, bucket_ref, bias_scr)
        for h in range(N_Q_HEADS):
            for k in range(pair):
                biasd_scr[h * SUBLANES + k * steps:h * SUBLANES + (k + 1) * steps, :] = bias_scr[h, 0:steps, :]

    xn = _rms(x_ref[...], gmix_ref[...]).astype(BF16)
    proj = jnp.dot(xn, win_ref[...], preferred_element_type=F32)
    u_ref[...] = proj[:, ATTN_WIDTH + 2 * KV_WIDTH:]
    qs = proj[:, :ATTN_WIDTH] * (HEAD_DIM ** -0.5)

    nkeys = 2 * WINDOW
    ri = lax.broadcasted_iota(jnp.int32, (prow, nkeys), 0)
    kj = lax.broadcasted_iota(jnp.int32, (prow, nkeys), 1)
    dist = ri % steps + WINDOW - kj
    valid = jnp.logical_and(dist >= 0, dist < WINDOW)
    head = lax.broadcasted_iota(jnp.int32, (prow, 1), 0) // SUBLANES
    sink = jnp.zeros((prow, 1), F32)
    for h in range(N_Q_HEADS):
        sink = jnp.where(head == h, sinks_ref[0, h], sink)
    seq_in_tile = (lax.broadcasted_iota(jnp.int32, (prow, LANES), 0) % SUBLANES) // steps
    lane_lo = lax.broadcasted_iota(jnp.int32, (SUBLANES, LANES), 1) < HEAD_DIM
    row8 = lax.broadcasted_iota(jnp.int32, (SUBLANES, KV_WIDTH), 0)
    gattn = gattn_ref[...]
    tail = jnp.zeros((nkeys - WINDOW - SUBLANES, KV_WIDTH), F32)
    bias = biasd_scr[...]

    tiles = []
    for tl in range(S_SEQ // pair):
        rws = slice(tl * SUBLANES, (tl + 1) * SUBLANES)
        q2 = _stack_heads(qs[rws], lane_lo).astype(BF16)
        kn8 = proj[rws, ATTN_WIDTH:ATTN_WIDTH + KV_WIDTH]
        vn8 = proj[rws, ATTN_WIDTH + KV_WIDTH:ATTN_WIDTH + 2 * KV_WIDTH]
        o_tile = jnp.zeros((prow, LANES), F32)
        for k in range(pair):
            sq = tl * pair + k
            shift = (SUBLANES - k * steps) % SUBLANES
            kn = jnp.where(row8 < steps, pltpu.roll(kn8, shift, axis=0) if shift else kn8, 0.0)
            vn = jnp.where(row8 < steps, pltpu.roll(vn8, shift, axis=0) if shift else vn8, 0.0)
            ck = ck_ref[sq]
            cv = cv_ref[sq]
            newk_ref[sq, 0:WINDOW - steps, :] = ck[steps:, :]
            newk_ref[sq, WINDOW - steps:, :] = kn[0:steps]
            newv_ref[sq, 0:WINDOW - steps, :] = cv[steps:, :]
            newv_ref[sq, WINDOW - steps:, :] = vn[0:steps]
            keys = jnp.concatenate([ck, kn, tail], axis=0).astype(BF16)
            vals = jnp.concatenate([cv, vn, tail], axis=0).astype(BF16)
            s = lax.dot_general(q2, keys, (((1,), (1,)), ((), ())), preferred_element_type=F32)
            s = jnp.where(valid, s + bias, NEG)
            m = jnp.maximum(jnp.max(s, axis=-1, keepdims=True), sink)
            p = jnp.exp(s - m)
            denom = jnp.sum(p, axis=-1, keepdims=True) + jnp.exp(sink - m)
            o = jnp.dot(p.astype(BF16), vals, preferred_element_type=F32) / denom
            o_tile = jnp.where(seq_in_tile == k, o, o_tile)
        tiles.append(_unstack_heads(o_tile, SUBLANES, lane_lo))
    attn_ref[...] = _rms(jnp.concatenate(tiles, axis=0), gattn).astype(BF16)


def _s_in_attn(x, rel_bias, sinks, g_mix, w_in, bucket, g_attn, cache_k, cache_v):
    nseq, steps, _ = x.shape
    rows = S_SEQ * steps
    xf = x.reshape(nseq * steps, D_MODEL)
    row = lambda width: pl.BlockSpec((rows, width), lambda i: (i, 0))
    cache = pl.BlockSpec((S_SEQ, WINDOW, KV_WIDTH), lambda i: (i, 0, 0))
    return pl.pallas_call(
        _s_in_attn_kernel,
        out_shape=(jax.ShapeDtypeStruct((nseq * steps, ATTN_WIDTH), BF16),
                   jax.ShapeDtypeStruct((nseq * steps, SSM_WIDTH), F32),
                   jax.ShapeDtypeStruct((nseq, WINDOW, KV_WIDTH), F32),
                   jax.ShapeDtypeStruct((nseq, WINDOW, KV_WIDTH), F32)),
        grid=(nseq // S_SEQ,),
        in_specs=[_smem_spec(), _smem_spec(), row(D_MODEL), _const_spec((1, D_MODEL)),
                  _const_spec((D_MODEL, D_IN_PROJ)), _const_spec((BLOCK, 2 * BLOCK)),
                  _const_spec((1, ATTN_WIDTH)), cache, cache],
        out_specs=(row(ATTN_WIDTH), row(SSM_WIDTH), cache, cache),
        scratch_shapes=[pltpu.VMEM((N_Q_HEADS, BLOCK, 2 * BLOCK), F32),
                        pltpu.VMEM((N_Q_HEADS * SUBLANES, 2 * WINDOW), F32)],
        compiler_params=pltpu.CompilerParams(
            dimension_semantics=("arbitrary",), vmem_limit_bytes=VMEM_LIMIT),
        name="s_in_attn",
    )(rel_bias, sinks, xf, g_mix, w_in, bucket, g_attn, cache_k, cache_v)


def _embed_blocks(bbr_ref, bbi_ref, cre_in, cim_in, bre, bim, cre, cim):
    for ref in (bre, bim, cre, cim):
        ref[...] = jnp.zeros_like(ref)
    for g in range(N_SSM_GROUPS):
        ch = slice(g * SSM_GROUP, (g + 1) * SSM_GROUP)
        st = slice(g * SSM_STATE, (g + 1) * SSM_STATE)
        bre[ch, st] = bbr_ref[g].astype(BF16)
        bim[ch, st] = bbi_ref[g].astype(BF16)
        cre[ch, st] = cre_in[g].astype(BF16)
        cim[ch, st] = cim_in[g].astype(BF16)


def _output_map(h_slab_re, h_slab_im, cre_ref, cim_ref, ch, cols):
    nt_dims = (((1,), (1,)), ((), ()))
    yr = lax.dot_general(h_slab_re.astype(BF16), cre_ref[ch, cols], nt_dims, preferred_element_type=F32)
    yi = lax.dot_general(h_slab_im.astype(BF16), cim_ref[ch, cols], nt_dims, preferred_element_type=F32)
    return yr - yi


def _glu_norm(y, wglu_ref, bglu_ref, gssm_ref):
    g = jax.nn.gelu(y)
    gate = jax.nn.sigmoid(jnp.dot(g.astype(BF16), wglu_ref[...], preferred_element_type=F32) + bglu_ref[...])
    return _rms(g * gate, gssm_ref[...])


SLAB = 4 * LANES
SLAB_CH = SLAB // SSM_STATE * SSM_GROUP


def _scan_kernel(u_ref, lr_ref, li_ref, lcr_ref, lci_ref, bbr_ref, bbi_ref, cre_in, cim_in,
                 d_ref, wglu_ref, bglu_ref, gssm_ref,
                 out_ref, hre_out, him_out,
                 bre_ref, bim_ref, cre_ref, cim_ref,
                 pad_scr, up_scr, hre, him, carry_re, carry_im, e_re, e_im, hc_re, hc_im):
    t = pl.program_id(1)
    nslab = SSM_WIDTH // LANES

    @pl.when(jnp.logical_and(pl.program_id(0) == 0, t == 0))
    def _():
        _embed_blocks(bbr_ref, bbi_ref, cre_in, cim_in, bre_ref, bim_ref, cre_ref, cim_ref)

    @pl.when(t == 0)
    def _():
        carry_re[...] = jnp.zeros_like(carry_re)
        carry_im[...] = jnp.zeros_like(carry_im)

    for c in range(SUBLANES):
        for s in range(nslab):
            pad_scr[s, c * PITCH:c * PITCH + CHUNK, :] = u_ref[c * CHUNK:(c + 1) * CHUNK, s * LANES:(s + 1) * LANES]
    for tau in range(CHUNK):
        for s in range(nslab):
            up_scr[tau * SUBLANES:(tau + 1) * SUBLANES, s * LANES:(s + 1) * LANES] = (
                pad_scr[s, pl.ds(tau, SUBLANES, stride=PITCH), :])
    up = up_scr[...]
    upb = up.astype(BF16)

    ys = []
    for sl in range(STATE_LANES // SLAB):
        cols = slice(sl * SLAB, (sl + 1) * SLAB)
        ch = slice(sl * SLAB_CH, (sl + 1) * SLAB_CH)
        hre[:, cols] = jnp.dot(upb[:, ch], bre_ref[ch, cols], preferred_element_type=F32)
        him[:, cols] = jnp.dot(upb[:, ch], bim_ref[ch, cols], preferred_element_type=F32)
        ar = jnp.broadcast_to(lr_ref[:, cols], (SUBLANES, SLAB))
        ai = jnp.broadcast_to(li_ref[:, cols], (SUBLANES, SLAB))

        hr = jnp.zeros((SUBLANES, SLAB), F32)
        hi = jnp.zeros((SUBLANES, SLAB), F32)
        for tau in range(CHUNK):
            rws = slice(tau * SUBLANES, (tau + 1) * SUBLANES)
            hr, hi = ar * hr - ai * hi + hre[rws, cols], ar * hi + ai * hr + him[rws, cols]
        e_re[:, cols] = hr
        e_im[:, cols] = hi

        cr = carry_re[:, cols]
        ci = carry_im[:, cols]
        lcr = lcr_ref[:, cols]
        lci = lci_ref[:, cols]
        for c in range(SUBLANES):
            hc_re[c:c + 1, cols] = cr
            hc_im[c:c + 1, cols] = ci
            er = e_re[c:c + 1, cols]
            ei = e_im[c:c + 1, cols]
            cr, ci = lcr * cr - lci * ci + er, lcr * ci + lci * cr + ei
        carry_re[:, cols] = cr
        carry_im[:, cols] = ci

        hr = hc_re[:, cols]
        hi = hc_im[:, cols]
        for tau in range(CHUNK):
            rws = slice(tau * SUBLANES, (tau + 1) * SUBLANES)
            hr, hi = ar * hr - ai * hi + hre[rws, cols], ar * hi + ai * hr + him[rws, cols]
            hre[rws, cols] = hr
            him[rws, cols] = hi

        ys.append(_output_map(hre[:, cols], him[:, cols], cre_ref, cim_ref, ch, cols))

    y = jnp.concatenate(ys, axis=-1) + d_ref[...] * up
    o = _glu_norm(y, wglu_ref, bglu_ref, gssm_ref)

    for tau in range(CHUNK):
        for s in range(nslab):
            pad_scr[s, pl.ds(tau, SUBLANES, stride=PITCH), :] = (
                o[tau * SUBLANES:(tau + 1) * SUBLANES, s * LANES:(s + 1) * LANES])
    for c in range(SUBLANES):
        for s in range(nslab):
            out_ref[c * CHUNK:(c + 1) * CHUNK, s * LANES:(s + 1) * LANES] = (
                pad_scr[s, c * PITCH:c * PITCH + CHUNK, :].astype(BF16))

    @pl.when(t == pl.num_programs(1) - 1)
    def _():
        hre_out[...] = carry_re[...]
        him_out[...] = carry_im[...]


def _block_operand_scratch():
    return [pltpu.VMEM((SSM_WIDTH, STATE_LANES), BF16) for _ in range(4)]


def _group_blocks():
    return _const_spec((N_SSM_GROUPS, SSM_GROUP, SSM_STATE))


def _scan(u, lam, lam_chunk, bbar, c_gcp, d, w_glu, b_glu, g_ssm):
    bsz, seq, _ = u.shape
    nt = seq // TM_B
    row = pl.BlockSpec((None, TM_B, SSM_WIDTH), lambda b, t: (b, t, 0))
    state = pl.BlockSpec((None, 1, STATE_LANES), lambda b, t: (b, 0, 0))
    vec = _const_spec((1, STATE_LANES))
    chan = _const_spec((1, SSM_WIDTH))
    return pl.pallas_call(
        _scan_kernel,
        out_shape=(jax.ShapeDtypeStruct((bsz, seq, SSM_WIDTH), BF16),
                   jax.ShapeDtypeStruct((bsz, 1, STATE_LANES), F32),
                   jax.ShapeDtypeStruct((bsz, 1, STATE_LANES), F32)),
        grid=(bsz, nt),
        in_specs=[row, vec, vec, vec, vec,
                  _group_blocks(), _group_blocks(), _group_blocks(), _group_blocks(),
                  chan, _const_spec((SSM_WIDTH, SSM_WIDTH)), chan, chan],
        out_specs=(row, state, state),
        scratch_shapes=_block_operand_scratch() + [
                        pltpu.VMEM((SSM_WIDTH // LANES, SUBLANES * PITCH, LANES), F32),
                        pltpu.VMEM((TM_B, SSM_WIDTH), F32),
                        pltpu.VMEM((TM_B, STATE_LANES), F32),
                        pltpu.VMEM((TM_B, STATE_LANES), F32),
                        pltpu.VMEM((1, STATE_LANES), F32),
                        pltpu.VMEM((1, STATE_LANES), F32),
                        pltpu.VMEM((SUBLANES, STATE_LANES), F32),
                        pltpu.VMEM((SUBLANES, STATE_LANES), F32),
                        pltpu.VMEM((SUBLANES, STATE_LANES), F32),
                        pltpu.VMEM((SUBLANES, STATE_LANES), F32)],
        compiler_params=pltpu.CompilerParams(
            dimension_semantics=("arbitrary", "arbitrary"), vmem_limit_bytes=VMEM_LIMIT),
        name="scan",
    )(u, lam[0], lam[1], lam_chunk[0], lam_chunk[1], bbar[0], bbar[1], c_gcp[0], c_gcp[1],
      d, w_glu, b_glu, g_ssm)


def _s_scan_kernel(u_ref, h0r_ref, h0i_ref, lr_ref, li_ref, bbr_ref, bbi_ref, cre_in, cim_in,
                   d_ref, wglu_ref, bglu_ref, gssm_ref,
                   out_ref, hre_out, him_out,
                   bre_ref, bim_ref, cre_ref, cim_ref,
                   pad_scr, up_scr, hre, him):
    nseq = h0r_ref.shape[0]
    steps = u_ref.shape[0] // nseq
    nslab = SSM_WIDTH // LANES
    _embed_blocks(bbr_ref, bbi_ref, cre_in, cim_in, bre_ref, bim_ref, cre_ref, cim_ref)

    for s in range(nslab):
        pad_scr[s] = u_ref[:, s * LANES:(s + 1) * LANES]
    for i in range(steps):
        for s in range(nslab):
            up_scr[i * nseq:(i + 1) * nseq, s * LANES:(s + 1) * LANES] = (
                pad_scr[s, pl.ds(i, nseq, stride=steps), :])
    up = up_scr[...]
    upb = up.astype(BF16)

    ys = []
    for sl in range(STATE_LANES // SLAB):
        cols = slice(sl * SLAB, (sl + 1) * SLAB)
        ch = slice(sl * SLAB_CH, (sl + 1) * SLAB_CH)
        hre[:, cols] = jnp.dot(upb[:, ch], bre_ref[ch, cols], preferred_element_type=F32)
        him[:, cols] = jnp.dot(upb[:, ch], bim_ref[ch, cols], preferred_element_type=F32)
        ar = lr_ref[:, cols]
        ai = li_ref[:, cols]
        hr = h0r_ref[:, cols]
        hi = h0i_ref[:, cols]
        for i in range(steps):
            rws = slice(i * nseq, (i + 1) * nseq)
            hr, hi = ar * hr - ai * hi + hre[rws, cols], ar * hi + ai * hr + him[rws, cols]
            hre[rws, cols] = hr
            him[rws, cols] = hi
        hre_out[:, cols] = hr
        him_out[:, cols] = hi
        ys.append(_output_map(hre[:, cols], him[:, cols], cre_ref, cim_ref, ch, cols))

    y = jnp.concatenate(ys, axis=-1) + d_ref[...] * up
    o = _glu_norm(y, wglu_ref, bglu_ref, gssm_ref)

    for i in range(steps):
        for s in range(nslab):
            pad_scr[s, pl.ds(i, nseq, stride=steps), :] = o[i * nseq:(i + 1) * nseq, s * LANES:(s + 1) * LANES]
    for s in range(nslab):
        out_ref[:, s * LANES:(s + 1) * LANES] = pad_scr[s].astype(BF16)


def _s_scan(u, h0_re, h0_im, lam, bbar, c_gcp, d, w_glu, b_glu, g_ssm):
    rows = u.shape[0]
    nseq = h0_re.shape[0]
    return pl.pallas_call(
        _s_scan_kernel,
        out_shape=(jax.ShapeDtypeStruct((rows, SSM_WIDTH), BF16),
                   jax.ShapeDtypeStruct((nseq, STATE_LANES), F32),
                   jax.ShapeDtypeStruct((nseq, STATE_LANES), F32)),
        scratch_shapes=_block_operand_scratch() + [
                        pltpu.VMEM((SSM_WIDTH // LANES, rows, LANES), F32),
                        pltpu.VMEM((rows, SSM_WIDTH), F32),
                        pltpu.VMEM((rows, STATE_LANES), F32),
                        pltpu.VMEM((rows, STATE_LANES), F32)],
        compiler_params=pltpu.CompilerParams(vmem_limit_bytes=VMEM_LIMIT),
        name="s_scan",
    )(u, h0_re, h0_im, lam[0], lam[1], bbar[0], bbar[1], c_gcp[0], c_gcp[1], d, w_glu, b_glu, g_ssm)


def _out_mlp_kernel(x_ref, a_ref, s_ref, wout_ref, gmlp_ref, wup_ref, wdown_ref, gfin_ref, y_ref):
    x1 = (x_ref[...]
          + jnp.dot(a_ref[...], wout_ref[0:ATTN_WIDTH, :], preferred_element_type=F32)
          + jnp.dot(s_ref[...], wout_ref[ATTN_WIDTH:, :], preferred_element_type=F32))
    hm = _rms(x1, gmlp_ref[...]).astype(BF16)
    y_ref[...] = x1
    ff_blk = 1024
    for c in range(D_FF // ff_blk):
        cols = slice(c * ff_blk, (c + 1) * ff_blk)
        up = jnp.dot(hm, wup_ref[:, cols], preferred_element_type=F32)
        act = jnp.square(jnp.maximum(up, 0.0)).astype(BF16)
        y_ref[...] += jnp.dot(act, wdown_ref[cols, :], preferred_element_type=F32)
    y_ref[...] = _rms(y_ref[...], gfin_ref[...])


def _out_mlp(x, attn_n, ssm_n, w_out, g_mlp, w_up, w_down, g_fin):
    rows = x.shape[0]
    row = lambda width: pl.BlockSpec((TM_C, width), lambda i: (i, 0))
    return pl.pallas_call(
        _out_mlp_kernel,
        out_shape=jax.ShapeDtypeStruct((rows, D_MODEL), F32),
        grid=(rows // TM_C,),
        in_specs=[row(D_MODEL), row(ATTN_WIDTH), row(SSM_WIDTH),
                  _const_spec((D_MODEL, D_MODEL)), _const_spec((1, D_MODEL)),
                  _const_spec((D_MODEL, D_FF)), _const_spec((D_FF, D_MODEL)),
                  _const_spec((1, D_MODEL))],
        out_specs=row(D_MODEL),
        compiler_params=pltpu.CompilerParams(
            dimension_semantics=("arbitrary",), vmem_limit_bytes=VMEM_LIMIT),
        name="out_mlp",
    )(x, attn_n, ssm_n, w_out, g_mlp, w_up, w_down, g_fin)


def _pair_heads(w, axis):
    shape = w.shape
    split = shape[:axis] + (N_KV_HEADS, Q_PER_KV, HEAD_DIM) + shape[axis + 1:]
    return jnp.swapaxes(w.reshape(split), axis, axis + 1).reshape(shape)


def kernel(x_prompt, x_sample, cache_k, cache_v, state_ssm_re, state_ssm_im, rel_bias, norm_mix, w_in, attn_sinks, ssm_a_re, ssm_a_im, ssm_log_step, ssm_b_re, ssm_b_im, ssm_c_re, ssm_c_im, ssm_d, w_glu, b_glu, norm_attn_out, norm_ssm_out, w_out, norm_mlp, w_up, w_down, norm_final):
    depth = w_in.shape[0]
    bsz, seq, _ = x_prompt.shape
    nseq, steps, _ = x_sample.shape
    bucket = jnp.asarray(_rel_bucket_table())
    g_fin = norm_final.reshape(1, D_MODEL)
    assert depth == 1, "the final norm is fused into the (single) layer's MLP kernel"

    xp, xs = x_prompt, x_sample
    outs = [[] for _ in range(8)]
    for l in range(depth):
        lr, li, lcr, lci, bbr, bbi = _ssm_prep(ssm_a_re[l], ssm_a_im[l], ssm_log_step[l], ssm_b_re[l], ssm_b_im[l])
        lam = (lr, li)
        lam_chunk = (lcr, lci)
        bbar = (bbr, bbi)
        c_gcp = (ssm_c_re[l], ssm_c_im[l])
        g_mix = norm_mix[l].reshape(1, D_MODEL)
        g_attn = _pair_heads(norm_attn_out[l].reshape(1, ATTN_WIDTH), axis=1)
        g_ssm = norm_ssm_out[l].reshape(1, SSM_WIDTH)
        g_mlp = norm_mlp[l].reshape(1, D_MODEL)
        sinks = attn_sinks[l].reshape(1, N_Q_HEADS)
        d = ssm_d[l].reshape(1, SSM_WIDTH)
        bg = b_glu[l].reshape(1, SSM_WIDTH)
        w_in_b = jnp.concatenate([_pair_heads(w_in[l][:, :ATTN_WIDTH], axis=1), w_in[l][:, ATTN_WIDTH:]],
                                 axis=1).astype(BF16)

        attn_n, u, k1, v1, w_out_b, w_glu_b, w_up_b, w_down_b = _in_attn(
            xp, rel_bias, sinks, g_mix, w_in_b, bucket.T, g_attn.reshape(ATTN_WIDTH, 1),
            w_out[l], w_glu[l], w_up[l], w_down[l])
        ssm_n, r1, i1 = _scan(u, lam, lam_chunk, bbar, c_gcp, d, w_glu_b, bg, g_ssm)
        xp = _out_mlp(xp.reshape(bsz * seq, D_MODEL), attn_n.reshape(bsz * seq, ATTN_WIDTH),
                      ssm_n.reshape(bsz * seq, SSM_WIDTH), w_out_b, g_mlp, w_up_b, w_down_b, g_fin
                      ).reshape(bsz, seq, D_MODEL)

        ck = cache_k[l].reshape(nseq, WINDOW, KV_WIDTH)
        cv = cache_v[l].reshape(nseq, WINDOW, KV_WIDTH)
        s_attn, s_u, k2, v2 = _s_in_attn(xs, rel_bias, sinks, g_mix, w_in_b, bucket, g_attn, ck, cv)
        s_ssm, r2, i2 = _s_scan(s_u, state_ssm_re[l].reshape(nseq, STATE_LANES),
                                state_ssm_im[l].reshape(nseq, STATE_LANES),
                                lam, bbar, c_gcp, d, w_glu_b, bg, g_ssm)
        xs = _out_mlp(xs.reshape(nseq * steps, D_MODEL), s_attn, s_ssm, w_out_b, g_mlp, w_up_b, w_down_b,
                      g_fin).reshape(nseq, steps, D_MODEL)

        kv_shape = (WINDOW, N_KV_HEADS, HEAD_DIM)
        st_shape = (N_SSM_GROUPS, SSM_STATE)
        for lst, val in zip(outs, (k1.reshape(bsz, *kv_shape), v1.reshape(bsz, *kv_shape),
                                   r1.reshape(bsz, *st_shape), i1.reshape(bsz, *st_shape),
                                   k2.reshape(nseq, *kv_shape), v2.reshape(nseq, *kv_shape),
                                   r2.reshape(nseq, *st_shape), i2.reshape(nseq, *st_shape))):
            lst.append(val)

    return (xp, xs) + tuple(jnp.stack(o) for o in outs)
```

```python
import math

import numpy as np
import jax
import jax.numpy as jnp
from jax import lax
from jax.experimental import pallas as pl
from jax.experimental.pallas import tpu as pltpu

D_MODEL = 1024
HEAD_DIM = 64
ATTN_WIDTH = 512
N_Q_HEADS = 8
N_KV_HEADS = 2
Q_PER_KV = 4
KV_WIDTH = 128
WINDOW = 128
BLOCK = 128
N_BUCKETS = 32
MAX_DISTANCE = 128
SSM_WIDTH = 512
SSM_GROUP = 16
N_SSM_GROUPS = 32
SSM_STATE = 64
STATE_LANES = N_SSM_GROUPS * SSM_STATE
D_FF = 4096
D_IN_PROJ = 1280
EPS = 1e-6
NEG = -1e30
LOG2E = math.log2(math.e)

LANES = 128
SUBLANES = 8
MXU_DIM = 256
VMEM_LIMIT = 56 * 1024 * 1024

F32 = jnp.float32
BF16 = jnp.bfloat16

TM_A = 512
TM_C = 512
TM_B = 512
CHUNK = TM_B // SUBLANES
PITCH = CHUNK + SUBLANES
S_SEQ = 16


def _rms(x, g):
    return x * lax.rsqrt(jnp.mean(x * x, axis=-1, keepdims=True) + EPS) * g


def _rel_bucket_table():
    i = np.arange(BLOCK)[:, None]
    j = np.arange(2 * BLOCK)[None, :]
    n = np.maximum(i + BLOCK - j, 0)
    max_exact = N_BUCKETS // 2
    nf = np.maximum(n, max_exact).astype(np.float32)
    large = max_exact + (np.log(nf / np.float32(max_exact)) / np.float32(math.log(MAX_DISTANCE / max_exact))
                         * np.float32(N_BUCKETS - max_exact)).astype(np.int32)
    large = np.minimum(large, N_BUCKETS - 1)
    return np.where(n < max_exact, n, large).astype(np.int32)


def _build_bias(relb_ref, bucket_ref, bias_scr):
    bucket = bucket_ref[...]
    for h in range(N_Q_HEADS):
        bias_scr[h] = jnp.zeros((BLOCK, 2 * BLOCK), F32)
    for b in range(N_BUCKETS):
        hit = bucket == b
        for h in range(N_Q_HEADS):
            bias_scr[h] = jnp.where(hit, relb_ref[b, h], bias_scr[h])


def _build_bias_t(relb_ref, bucket_t_ref, bias_t_scr):
    bucket_t = bucket_t_ref[...]
    bias_t_scr[...] = jnp.zeros_like(bias_t_scr)
    for b in range(N_BUCKETS):
        hit = bucket_t == b
        for h in range(N_Q_HEADS):
            cols = slice(h * BLOCK, (h + 1) * BLOCK)
            bias_t_scr[:, cols] = jnp.where(hit, relb_ref[b, h] * LOG2E, bias_t_scr[:, cols])
    kj = lax.broadcasted_iota(jnp.int32, (2 * BLOCK, BLOCK), 0)
    qi = lax.broadcasted_iota(jnp.int32, (2 * BLOCK, BLOCK), 1)
    dist = qi + BLOCK - kj
    in_band = jnp.logical_and(dist >= 0, dist < WINDOW)
    for h in range(N_Q_HEADS):
        cols = slice(h * BLOCK, (h + 1) * BLOCK)
        bias_t_scr[:, cols] = jnp.where(in_band, bias_t_scr[:, cols], NEG)


def _ssm_prep_kernel(are_ref, aim_ref, ls_ref, btr_ref, bti_ref,
                     lr_ref, li_ref, lcr_ref, lci_ref, bbr_ref, bbi_ref):
    ar = are_ref[...]
    ai = aim_ref[...]
    step = jnp.exp(ls_ref[...])
    zr = ar * step
    zi = ai * step
    mag = jnp.exp(zr)
    lr = mag * jnp.cos(zi)
    li = mag * jnp.sin(zi)
    pr, pi = lr, li
    for _ in range(int(math.log2(CHUNK))):
        pr, pi = pr * pr - pi * pi, 2.0 * pr * pi
    for g in range(N_SSM_GROUPS):
        st = slice(g * SSM_STATE, (g + 1) * SSM_STATE)
        lr_ref[:, st] = lr[g]
        li_ref[:, st] = li[g]
        lcr_ref[:, st] = pr[g]
        lci_ref[:, st] = pi[g]
    nr = lr - 1.0
    ni = li
    den = ar * ar + ai * ai
    cr = (nr * ar + ni * ai) / den
    ci = (ni * ar - nr * ai) / den
    btr = btr_ref[...]
    bti = bti_ref[...]
    bbr_ref[...] = cr * btr - ci * bti
    bbi_ref[...] = cr * bti + ci * btr


def _ssm_prep(a_re, a_im, log_step, b_re, b_im):
    g, p, c = N_SSM_GROUPS, SSM_STATE, SSM_GROUP
    vec = jax.ShapeDtypeStruct((1, STATE_LANES), F32)
    mat = jax.ShapeDtypeStruct((g, c, p), F32)
    return pl.pallas_call(
        _ssm_prep_kernel,
        out_shape=(vec, vec, vec, vec, mat, mat),
        name="ssm_prep",
    )(a_re.reshape(g, 1, p), a_im.reshape(g, 1, p), log_step.reshape(g, 1, 1),
      jnp.transpose(b_re, (0, 2, 1)), jnp.transpose(b_im, (0, 2, 1)))


def _stack_heads(q_blk, lane_lo):
    zero = jnp.zeros_like(q_blk[:, 0:LANES])
    lo, hi = [], []
    for c in range(ATTN_WIDTH // LANES):
        blk = q_blk[:, c * LANES:(c + 1) * LANES]
        lo.append(jnp.where(lane_lo, blk, zero))
        hi.append(jnp.where(lane_lo, zero, blk))
    return jnp.concatenate(lo + hi, axis=0)


def _unstack_heads(o, rows, lane_lo):
    nblk = ATTN_WIDTH // LANES
    return jnp.concatenate(
        [jnp.where(lane_lo, o[c * rows:(c + 1) * rows], o[(c + nblk) * rows:(c + nblk + 1) * rows])
         for c in range(nblk)], axis=-1)


Q_COLS = 256


def _proj_stages(x_ref, gmix_ref, win_ref, q_stage, k_stage, vt_stage, u_ref, newk_ref, newv_ref):
    xn = _rms(x_ref[...], gmix_ref[...]).astype(BF16)
    q_scale = HEAD_DIM ** -0.5 * LOG2E
    for c in range(ATTN_WIDTH // Q_COLS):
        cols = slice(c * Q_COLS, (c + 1) * Q_COLS)
        q = jnp.dot(xn, win_ref[:, cols], preferred_element_type=F32)
        q_stage[:, cols] = (q * q_scale).astype(BF16)
        yield
    kv = jnp.dot(xn, win_ref[:, ATTN_WIDTH:ATTN_WIDTH + 2 * KV_WIDTH], preferred_element_type=F32)
    k_stage[...] = kv[:, :KV_WIDTH].astype(BF16)
    vt_stage[...] = kv[:, KV_WIDTH:].T.astype(BF16)
    newk_ref[...] = kv[TM_A - WINDOW:, :KV_WIDTH]
    newv_ref[...] = kv[TM_A - WINDOW:, KV_WIDTH:]
    yield
    u0 = ATTN_WIDTH + 2 * KV_WIDTH
    for c in range(SSM_WIDTH // Q_COLS):
        cols = slice(u0 + c * Q_COLS, u0 + (c + 1) * Q_COLS)
        u_ref[:, c * Q_COLS:(c + 1) * Q_COLS] = jnp.dot(xn, win_ref[:, cols], preferred_element_type=F32)
        yield


def _attn_stages(sinks_ref, attn_ref, k_scr, vt_scr, q_scr, bias_t_scr, gt_scr, first_tile):
    nstk = N_Q_HEADS * BLOCK
    nblk = ATTN_WIDTH // LANES
    lane_lo = lax.broadcasted_iota(jnp.int32, (BLOCK, LANES), 1) < HEAD_DIM
    row_lo = lax.broadcasted_iota(jnp.int32, (KV_WIDTH, BLOCK), 0) < HEAD_DIM
    kj = lax.broadcasted_iota(jnp.int32, (2 * BLOCK, nstk), 0)
    has_prev = jnp.logical_or(kj >= BLOCK, jnp.logical_not(first_tile))
    head = lax.broadcasted_iota(jnp.int32, (1, nstk), 1) // BLOCK
    sink = jnp.zeros((1, nstk), F32)
    for h in range(N_Q_HEADS):
        sink = jnp.where(head == h, sinks_ref[0, h] * LOG2E, sink)

    scores = []
    for j in range(TM_A // BLOCK):
        r0 = j * BLOCK
        q2 = _stack_heads(q_scr[r0:r0 + BLOCK, :], lane_lo)
        keys = k_scr[r0:r0 + 2 * BLOCK, :]
        scores.append(lax.dot_general(keys, q2, (((1,), (1,)), ((), ())), preferred_element_type=F32))
        yield
    for j in range(TM_A // BLOCK):
        r0 = j * BLOCK
        s = scores[j] + bias_t_scr[...]
        if j == 0:
            s = jnp.where(has_prev, s, NEG)
        m = jnp.maximum(jnp.max(s, axis=0, keepdims=True), sink)
        p = jnp.exp2(s - m)
        denom = jnp.sum(p, axis=0, keepdims=True) + jnp.exp2(sink - m)
        o = jnp.dot(vt_scr[:, r0:r0 + 2 * BLOCK], p.astype(BF16), preferred_element_type=F32) / denom
        a = jnp.concatenate(
            [jnp.where(row_lo, o[:, c * BLOCK:(c + 1) * BLOCK], o[:, (c + nblk) * BLOCK:(c + nblk + 1) * BLOCK])
             for c in range(nblk)], axis=0)
        a = a * lax.rsqrt(jnp.mean(a * a, axis=0, keepdims=True) + EPS) * gt_scr[...]
        attn_ref[r0:r0 + BLOCK, :] = a.T.astype(BF16)
        yield


def _in_attn_kernel(relb_ref, sinks_ref, x_ref, gmix_ref, win_ref, bucket_t_ref, gattn_ref,
                    wout_f, wglu_f, wup_f, wdown_f,
                    attn_ref, u_ref, newk_ref, newv_ref, wout_b, wglu_b, wup_b, wdown_b,
                    k_scr, vt_scr, q_scr, k_stage, vt_stage, q_stage, bias_t_scr, gt_scr):
    b = pl.program_id(0)
    t = pl.program_id(1)

    @pl.when(jnp.logical_and(b == 0, t == 0))
    def _():
        _build_bias_t(relb_ref, bucket_t_ref, bias_t_scr)
        gt_scr[...] = jnp.broadcast_to(gattn_ref[...], (ATTN_WIDTH, BLOCK))
        k_scr[...] = jnp.zeros_like(k_scr)
        vt_scr[...] = jnp.zeros_like(vt_scr)
        k_stage[...] = jnp.zeros_like(k_stage)
        vt_stage[...] = jnp.zeros_like(vt_stage)
        q_stage[...] = jnp.zeros_like(q_stage)

    k_scr[0:BLOCK, :] = k_scr[TM_A:TM_A + BLOCK, :]
    vt_scr[:, 0:BLOCK] = vt_scr[:, TM_A:TM_A + BLOCK]
    k_scr[BLOCK:BLOCK + TM_A, :] = k_stage[...]
    vt_scr[:, BLOCK:BLOCK + TM_A] = vt_stage[...]
    q_scr[...] = q_stage[...]

    wout_b[...] = wout_f[...].astype(BF16)
    wglu_b[...] = wglu_f[...].astype(BF16)
    wup_b[...] = wup_f[...].astype(BF16)
    wdown_b[...] = wdown_f[...].astype(BF16)

    proj = _proj_stages(x_ref, gmix_ref, win_ref, q_stage, k_stage, vt_stage, u_ref, newk_ref, newv_ref)
    attn = _attn_stages(sinks_ref, attn_ref, k_scr, vt_scr, q_scr, bias_t_scr, gt_scr, t <= 1)
    for stage in "paapaapapaapa":
        next(proj if stage == "p" else attn)
    assert next(proj, None) is None and next(attn, None) is None


def _const_spec(shape):
    zeros = (0,) * len(shape)
    return pl.BlockSpec(shape, lambda *_: zeros, pipeline_mode=pl.Buffered(1))


def _smem_spec():
    return pl.BlockSpec(memory_space=pltpu.SMEM)


def _in_attn(x, rel_bias, sinks, g_mix, w_in, bucket_t, g_attn_col, w_out, w_glu, w_up, w_down):
    bsz, seq, _ = x.shape
    nt = seq // TM_A
    steps = bsz * nt
    cur = lambda width: pl.BlockSpec((None, TM_A, width), lambda b, t: (b, jnp.minimum(t, nt - 1), 0))
    prev = lambda width: pl.BlockSpec((None, TM_A, width), lambda b, t: (b, jnp.maximum(t - 1, 0), 0))
    win_spec = pl.BlockSpec((None, WINDOW, KV_WIDTH), lambda b, t: (b, 0, 0))

    def tile(b, t):
        return b * nt + jnp.minimum(t, nt - 1)

    def rows_of(w):
        blk = w.shape[0] // steps
        return pl.BlockSpec((blk, w.shape[1]), lambda b, t: (tile(b, t), 0))

    nhead_blk = D_MODEL // HEAD_DIM

    def wout_src(b, t):
        return (jnp.minimum(tile(b, t), nhead_blk - 1), 0)

    def wout_dst(b, t):
        j = jnp.minimum(tile(b, t), nhead_blk - 1)
        return (jnp.where(j < N_Q_HEADS, (j % Q_PER_KV) * N_KV_HEADS + j // Q_PER_KV, j), 0)

    assert steps >= nhead_blk and w_up.shape[0] % steps == 0 and w_down.shape[0] % steps == 0
    wout_in = pl.BlockSpec((HEAD_DIM, D_MODEL), wout_src)
    wout_out = pl.BlockSpec((HEAD_DIM, D_MODEL), wout_dst)
    cast_shape = lambda w: jax.ShapeDtypeStruct(w.shape, BF16)
    return pl.pallas_call(
        _in_attn_kernel,
        out_shape=(jax.ShapeDtypeStruct((bsz, seq, ATTN_WIDTH), BF16),
                   jax.ShapeDtypeStruct((bsz, seq, SSM_WIDTH), F32),
                   jax.ShapeDtypeStruct((bsz, WINDOW, KV_WIDTH), F32),
                   jax.ShapeDtypeStruct((bsz, WINDOW, KV_WIDTH), F32),
                   cast_shape(w_out), cast_shape(w_glu), cast_shape(w_up), cast_shape(w_down)),
        grid=(bsz, nt + 1),
        in_specs=[_smem_spec(), _smem_spec(), cur(D_MODEL), _const_spec((1, D_MODEL)),
                  _const_spec((D_MODEL, D_IN_PROJ)), _const_spec((2 * BLOCK, BLOCK)),
                  _const_spec((ATTN_WIDTH, 1)),
                  wout_in, rows_of(w_glu), rows_of(w_up), rows_of(w_down)],
        out_specs=(prev(ATTN_WIDTH), cur(SSM_WIDTH), win_spec, win_spec,
                   wout_out, rows_of(w_glu), rows_of(w_up), rows_of(w_down)),
        scratch_shapes=[pltpu.VMEM((TM_A + BLOCK, KV_WIDTH), BF16),
                        pltpu.VMEM((KV_WIDTH, TM_A + BLOCK), BF16),
                        pltpu.VMEM((TM_A, ATTN_WIDTH), BF16),
                        pltpu.VMEM((TM_A, KV_WIDTH), BF16),
                        pltpu.VMEM((KV_WIDTH, TM_A), BF16),
                        pltpu.VMEM((TM_A, ATTN_WIDTH), BF16),
                        pltpu.VMEM((2 * BLOCK, N_Q_HEADS * BLOCK), F32),
                        pltpu.VMEM((ATTN_WIDTH, BLOCK), F32)],
        compiler_params=pltpu.CompilerParams(
            dimension_semantics=("arbitrary", "arbitrary"), vmem_limit_bytes=VMEM_LIMIT),
        name="in_attn",
    )(rel_bias, sinks, x, g_mix, w_in, bucket_t, g_attn_col, w_out, w_glu, w_up, w_down)


def _s_in_attn_kernel(relb_ref, sinks_ref, x_ref, gmix_ref, win_ref, bucket_ref, gattn_ref,
                      ck_ref, cv_ref,
                      attn_ref, u_ref, newk_ref, newv_ref,
                      bias_scr, biasd_scr):
    steps = x_ref.shape[0] // S_SEQ
    pair = SUBLANES // steps
    prow = N_Q_HEADS * SUBLANES

    @pl.when(pl.program_id(0) == 0)
    def _():
        _build_bias(relb_ref, bucket_ref, bias_scr)
        for h in range(N_Q_HEADS):
            for k in range(pair):
                biasd_scr[h * SUBLANES + k * steps:h * SUBLANES + (k + 1) * steps, :] = bias_scr[h, 0:steps, :]

    xn = _rms(x_ref[...], gmix_ref[...]).astype(BF16)
    proj = jnp.dot(xn, win_ref[...], preferred_element_type=F32)
    u_ref[...] = proj[:, ATTN_WIDTH + 2 * KV_WIDTH:]
    qs = proj[:, :ATTN_WIDTH] * (HEAD_DIM ** -0.5)

    nkeys = 2 * WINDOW
    ri = lax.broadcasted_iota(jnp.int32, (prow, nkeys), 0)
    kj = lax.broadcasted_iota(jnp.int32, (prow, nkeys), 1)
    dist = ri % steps + WINDOW - kj
    valid = jnp.logical_and(dist >= 0, dist < WINDOW)
    head = lax.broadcasted_iota(jnp.int32, (prow, 1), 0) // SUBLANES
    sink = jnp.zeros((prow, 1), F32)
    for h in range(N_Q_HEADS):
        sink = jnp.where(head == h, sinks_ref[0, h], sink)
    seq_in_tile = (lax.broadcasted_iota(jnp.int32, (prow, LANES), 0) % SUBLANES) // steps
    lane_lo = lax.broadcasted_iota(jnp.int32, (SUBLANES, LANES), 1) < HEAD_DIM
    row8 = lax.broadcasted_iota(jnp.int32, (SUBLANES, KV_WIDTH), 0)
    gattn = gattn_ref[...]
    tail = jnp.zeros((nkeys - WINDOW - SUBLANES, KV_WIDTH), F32)
    bias = biasd_scr[...]

    tiles = []
    for tl in range(S_SEQ // pair):
        rws = slice(tl * SUBLANES, (tl + 1) * SUBLANES)
        q2 = _stack_heads(qs[rws], lane_lo).astype(BF16)
        kn8 = proj[rws, ATTN_WIDTH:ATTN_WIDTH + KV_WIDTH]
        vn8 = proj[rws, ATTN_WIDTH + KV_WIDTH:ATTN_WIDTH + 2 * KV_WIDTH]
        o_tile = jnp.zeros((prow, LANES), F32)
        for k in range(pair):
            sq = tl * pair + k
            shift = (SUBLANES - k * steps) % SUBLANES
            kn = jnp.where(row8 < steps, pltpu.roll(kn8, shift, axis=0) if shift else kn8, 0.0)
            vn = jnp.where(row8 < steps, pltpu.roll(vn8, shift, axis=0) if shift else vn8, 0.0)
            ck = ck_ref[sq]
            cv = cv_ref[sq]
            newk_ref[sq, 0:WINDOW - steps, :] = ck[steps:, :]
            newk_ref[sq, WINDOW - steps:, :] = kn[0:steps]
            newv_ref[sq, 0:WINDOW - steps, :] = cv[steps:, :]
            newv_ref[sq, WINDOW - steps:, :] = vn[0:steps]
            keys = jnp.concatenate([ck, kn, tail], axis=0).astype(BF16)
            vals = jnp.concatenate([cv, vn, tail], axis=0).astype(BF16)
            s = lax.dot_general(q2, keys, (((1,), (1,)), ((), ())), preferred_element_type=F32)
            s = jnp.where(valid, s + bias, NEG)
            m = jnp.maximum(jnp.max(s, axis=-1, keepdims=True), sink)
            p = jnp.exp(s - m)
            denom = jnp.sum(p, axis=-1, keepdims=True) + jnp.exp(sink - m)
            o = jnp.dot(p.astype(BF16), vals, preferred_element_type=F32) / denom
            o_tile = jnp.where(seq_in_tile == k, o, o_tile)
        tiles.append(_unstack_heads(o_tile, SUBLANES, lane_lo))
    attn_ref[...] = _rms(jnp.concatenate(tiles, axis=0), gattn).astype(BF16)


def _s_in_attn(x, rel_bias, sinks, g_mix, w_in, bucket, g_attn, cache_k, cache_v):
    nseq, steps, _ = x.shape
    rows = S_SEQ * steps
    xf = x.reshape(nseq * steps, D_MODEL)
    row = lambda width: pl.BlockSpec((rows, width), lambda i: (i, 0))
    cache = pl.BlockSpec((S_SEQ, WINDOW, KV_WIDTH), lambda i: (i, 0, 0))
    return pl.pallas_call(
        _s_in_attn_kernel,
        out_shape=(jax.ShapeDtypeStruct((nseq * steps, ATTN_WIDTH), BF16),
                   jax.ShapeDtypeStruct((nseq * steps, SSM_WIDTH), F32),
                   jax.ShapeDtypeStruct((nseq, WINDOW, KV_WIDTH), F32),
                   jax.ShapeDtypeStruct((nseq, WINDOW, KV_WIDTH), F32)),
        grid=(nseq // S_SEQ,),
        in_specs=[_smem_spec(), _smem_spec(), row(D_MODEL), _const_spec((1, D_MODEL)),
                  _const_spec((D_MODEL, D_IN_PROJ)), _const_spec((BLOCK, 2 * BLOCK)),
                  _const_spec((1, ATTN_WIDTH)), cache, cache],
        out_specs=(row(ATTN_WIDTH), row(SSM_WIDTH), cache, cache),
        scratch_shapes=[pltpu.VMEM((N_Q_HEADS, BLOCK, 2 * BLOCK), F32),
                        pltpu.VMEM((N_Q_HEADS * SUBLANES, 2 * WINDOW), F32)],
        compiler_params=pltpu.CompilerParams(
            dimension_semantics=("arbitrary",), vmem_limit_bytes=VMEM_LIMIT),
        name="s_in_attn",
    )(rel_bias, sinks, xf, g_mix, w_in, bucket, g_attn, cache_k, cache_v)


def _embed_blocks(bbr_ref, bbi_ref, cre_in, cim_in, bre, bim, cre, cim):
    for ref in (bre, bim, cre, cim):
        ref[...] = jnp.zeros_like(ref)
    for g in range(N_SSM_GROUPS):
        ch = slice(g * SSM_GROUP, (g + 1) * SSM_GROUP)
        st = slice(g * SSM_STATE, (g + 1) * SSM_STATE)
        bre[ch, st] = bbr_ref[g].astype(BF16)
        bim[ch, st] = bbi_ref[g].astype(BF16)
        cre[ch, st] = cre_in[g].astype(BF16)
        cim[ch, st] = cim_in[g].astype(BF16)


def _output_map(h_slab_re, h_slab_im, cre_ref, cim_ref, ch, cols):
    nt_dims = (((1,), (1,)), ((), ()))
    yr = lax.dot_general(h_slab_re.astype(BF16), cre_ref[ch, cols], nt_dims, preferred_element_type=F32)
    yi = lax.dot_general(h_slab_im.astype(BF16), cim_ref[ch, cols], nt_dims, preferred_element_type=F32)
    return yr - yi


def _glu_norm(y, wglu_ref, bglu_ref, gssm_ref):
    g = jax.nn.gelu(y)
    gate = jax.nn.sigmoid(jnp.dot(g.astype(BF16), wglu_ref[...], preferred_element_type=F32) + bglu_ref[...])
    return _rms(g * gate, gssm_ref[...])


SLAB = 4 * LANES
SLAB_CH = SLAB // SSM_STATE * SSM_GROUP


def _scan_kernel(u_ref, lr_ref, li_ref, lcr_ref, lci_ref, bbr_ref, bbi_ref, cre_in, cim_in,
                 d_ref, wglu_ref, bglu_ref, gssm_ref,
                 out_ref, hre_out, him_out,
                 bre_ref, bim_ref, cre_ref, cim_ref,
                 pad_scr, up_scr, hre, him, carry_re, carry_im, e_re, e_im, hc_re, hc_im):
    t = pl.program_id(1)
    nslab = SSM_WIDTH // LANES

    @pl.when(jnp.logical_and(pl.program_id(0) == 0, t == 0))
    def _():
        _embed_blocks(bbr_ref, bbi_ref, cre_in, cim_in, bre_ref, bim_ref, cre_ref, cim_ref)

    @pl.when(t == 0)
    def _():
        carry_re[...] = jnp.zeros_like(carry_re)
        carry_im[...] = jnp.zeros_like(carry_im)

    for c in range(SUBLANES):
        for s in range(nslab):
            pad_scr[s, c * PITCH:c * PITCH + CHUNK, :] = u_ref[c * CHUNK:(c + 1) * CHUNK, s * LANES:(s + 1) * LANES]
    for tau in range(CHUNK):
        for s in range(nslab):
            up_scr[tau * SUBLANES:(tau + 1) * SUBLANES, s * LANES:(s + 1) * LANES] = (
                pad_scr[s, pl.ds(tau, SUBLANES, stride=PITCH), :])
    up = up_scr[...]
    upb = up.astype(BF16)

    ys = []
    for sl in range(STATE_LANES // SLAB):
        cols = slice(sl * SLAB, (sl + 1) * SLAB)
        ch = slice(sl * SLAB_CH, (sl + 1) * SLAB_CH)
        hre[:, cols] = jnp.dot(upb[:, ch], bre_ref[ch, cols], preferred_element_type=F32)
        him[:, cols] = jnp.dot(upb[:, ch], bim_ref[ch, cols], preferred_element_type=F32)
        ar = jnp.broadcast_to(lr_ref[:, cols], (SUBLANES, SLAB))
        ai = jnp.broadcast_to(li_ref[:, cols], (SUBLANES, SLAB))

        hr = jnp.zeros((SUBLANES, SLAB), F32)
        hi = jnp.zeros((SUBLANES, SLAB), F32)
        for tau in range(CHUNK):
            rws = slice(tau * SUBLANES, (tau + 1) * SUBLANES)
            hr, hi = ar * hr - ai * hi + hre[rws, cols], ar * hi + ai * hr + him[rws, cols]
        e_re[:, cols] = hr
        e_im[:, cols] = hi

        cr = carry_re[:, cols]
        ci = carry_im[:, cols]
        lcr = lcr_ref[:, cols]
        lci = lci_ref[:, cols]
        for c in range(SUBLANES):
            hc_re[c:c + 1, cols] = cr
            hc_im[c:c + 1, cols] = ci
            er = e_re[c:c + 1, cols]
            ei = e_im[c:c + 1, cols]
            cr, ci = lcr * cr - lci * ci + er, lcr * ci + lci * cr + ei
        carry_re[:, cols] = cr
        carry_im[:, cols] = ci

        hr = hc_re[:, cols]
        hi = hc_im[:, cols]
        for tau in range(CHUNK):
            rws = slice(tau * SUBLANES, (tau + 1) * SUBLANES)
            hr, hi = ar * hr - ai * hi + hre[rws, cols], ar * hi + ai * hr + him[rws, cols]
            hre[rws, cols] = hr
            him[rws, cols] = hi

        ys.append(_output_map(hre[:, cols], him[:, cols], cre_ref, cim_ref, ch, cols))

    y = jnp.concatenate(ys, axis=-1) + d_ref[...] * up
    o = _glu_norm(y, wglu_ref, bglu_ref, gssm_ref)

    for tau in range(CHUNK):
        for s in range(nslab):
            pad_scr[s, pl.ds(tau, SUBLANES, stride=PITCH), :] = (
                o[tau * SUBLANES:(tau + 1) * SUBLANES, s * LANES:(s + 1) * LANES])
    for c in range(SUBLANES):
        for s in range(nslab):
            out_ref[c * CHUNK:(c + 1) * CHUNK, s * LANES:(s + 1) * LANES] = (
                pad_scr[s, c * PITCH:c * PITCH + CHUNK, :].astype(BF16))

    @pl.when(t == pl.num_programs(1) - 1)
    def _():
        hre_out[...] = carry_re[...]
        him_out[...] = carry_im[...]


def _block_operand_scratch():
    return [pltpu.VMEM((SSM_WIDTH, STATE_LANES), BF16) for _ in range(4)]


def _group_blocks():
    return _const_spec((N_SSM_GROUPS, SSM_GROUP, SSM_STATE))


def _scan(u, lam, lam_chunk, bbar, c_gcp, d, w_glu, b_glu, g_ssm):
    bsz, seq, _ = u.shape
    nt = seq // TM_B
    row = pl.BlockSpec((None, TM_B, SSM_WIDTH), lambda b, t: (b, t, 0))
    state = pl.BlockSpec((None, 1, STATE_LANES), lambda b, t: (b, 0, 0))
    vec = _const_spec((1, STATE_LANES))
    chan = _const_spec((1, SSM_WIDTH))
    return pl.pallas_call(
        _scan_kernel,
        out_shape=(jax.ShapeDtypeStruct((bsz, seq, SSM_WIDTH), BF16),
                   jax.ShapeDtypeStruct((bsz, 1, STATE_LANES), F32),
                   jax.ShapeDtypeStruct((bsz, 1, STATE_LANES), F32)),
        grid=(bsz, nt),
        in_specs=[row, vec, vec, vec, vec,
                  _group_blocks(), _group_blocks(), _group_blocks(), _group_blocks(),
                  chan, _const_spec((SSM_WIDTH, SSM_WIDTH)), chan, chan],
        out_specs=(row, state, state),
        scratch_shapes=_block_operand_scratch() + [
                        pltpu.VMEM((SSM_WIDTH // LANES, SUBLANES * PITCH, LANES), F32),
                        pltpu.VMEM((TM_B, SSM_WIDTH), F32),
                        pltpu.VMEM((TM_B, STATE_LANES), F32),
                        pltpu.VMEM((TM_B, STATE_LANES), F32),
                        pltpu.VMEM((1, STATE_LANES), F32),
                        pltpu.VMEM((1, STATE_LANES), F32),
                        pltpu.VMEM((SUBLANES, STATE_LANES), F32),
                        pltpu.VMEM((SUBLANES, STATE_LANES), F32),
                        pltpu.VMEM((SUBLANES, STATE_LANES), F32),
                        pltpu.VMEM((SUBLANES, STATE_LANES), F32)],
        compiler_params=pltpu.CompilerParams(
            dimension_semantics=("arbitrary", "arbitrary"), vmem_limit_bytes=VMEM_LIMIT),
        name="scan",
    )(u, lam[0], lam[1], lam_chunk[0], lam_chunk[1], bbar[0], bbar[1], c_gcp[0], c_gcp[1],
      d, w_glu, b_glu, g_ssm)


def _s_scan_kernel(u_ref, h0r_ref, h0i_ref, lr_ref, li_ref, bbr_ref, bbi_ref, cre_in, cim_in,
                   d_ref, wglu_ref, bglu_ref, gssm_ref,
                   out_ref, hre_out, him_out,
                   bre_ref, bim_ref, cre_ref, cim_ref,
                   pad_scr, up_scr, hre, him):
    nseq = h0r_ref.shape[0]
    steps = u_ref.shape[0] // nseq
    nslab = SSM_WIDTH // LANES
    _embed_blocks(bbr_ref, bbi_ref, cre_in, cim_in, bre_ref, bim_ref, cre_ref, cim_ref)

    for s in range(nslab):
        pad_scr[s] = u_ref[:, s * LANES:(s + 1) * LANES]
    for i in range(steps):
        for s in range(nslab):
            up_scr[i * nseq:(i + 1) * nseq, s * LANES:(s + 1) * LANES] = (
                pad_scr[s, pl.ds(i, nseq, stride=steps), :])
    up = up_scr[...]
    upb = up.astype(BF16)

    ys = []
    for sl in range(STATE_LANES // SLAB):
        cols = slice(sl * SLAB, (sl + 1) * SLAB)
        ch = slice(sl * SLAB_CH, (sl + 1) * SLAB_CH)
        hre[:, cols] = jnp.dot(upb[:, ch], bre_ref[ch, cols], preferred_element_type=F32)
        him[:, cols] = jnp.dot(upb[:, ch], bim_ref[ch, cols], preferred_element_type=F32)
        ar = lr_ref[:, cols]
        ai = li_ref[:, cols]
        hr = h0r_ref[:, cols]
        hi = h0i_ref[:, cols]
        for i in range(steps):
            rws = slice(i * nseq, (i + 1) * nseq)
            hr, hi = ar * hr - ai * hi + hre[rws, cols], ar * hi + ai * hr + him[rws, cols]
            hre[rws, cols] = hr
            him[rws, cols] = hi
        hre_out[:, cols] = hr
        him_out[:, cols] = hi
        ys.append(_output_map(hre[:, cols], him[:, cols], cre_ref, cim_ref, ch, cols))

    y = jnp.concatenate(ys, axis=-1) + d_ref[...] * up
    o = _glu_norm(y, wglu_ref, bglu_ref, gssm_ref)

    for i in range(steps):
        for s in range(nslab):
            pad_scr[s, pl.ds(i, nseq, stride=steps), :] = o[i * nseq:(i + 1) * nseq, s * LANES:(s + 1) * LANES]
    for s in range(nslab):
        out_ref[:, s * LANES:(s + 1) * LANES] = pad_scr[s].astype(BF16)


def _s_scan(u, h0_re, h0_im, lam, bbar, c_gcp, d, w_glu, b_glu, g_ssm):
    rows = u.shape[0]
    nseq = h0_re.shape[0]
    return pl.pallas_call(
        _s_scan_kernel,
        out_shape=(jax.ShapeDtypeStruct((rows, SSM_WIDTH), BF16),
                   jax.ShapeDtypeStruct((nseq, STATE_LANES), F32),
                   jax.ShapeDtypeStruct((nseq, STATE_LANES), F32)),
        scratch_shapes=_block_operand_scratch() + [
                        pltpu.VMEM((SSM_WIDTH // LANES, rows, LANES), F32),
                        pltpu.VMEM((rows, SSM_WIDTH), F32),
                        pltpu.VMEM((rows, STATE_LANES), F32),
                        pltpu.VMEM((rows, STATE_LANES), F32)],
        compiler_params=pltpu.CompilerParams(vmem_limit_bytes=VMEM_LIMIT),
        name="s_scan",
    )(u, h0_re, h0_im, lam[0], lam[1], bbar[0], bbar[1], c_gcp[0], c_gcp[1], d, w_glu, b_glu, g_ssm)


def _out_mlp_kernel(x_ref, a_ref, s_ref, wout_ref, gmlp_ref, wup_ref, wdown_ref, gfin_ref, y_ref):
    x1 = (x_ref[...]
          + jnp.dot(a_ref[...], wout_ref[0:ATTN_WIDTH, :], preferred_element_type=F32)
          + jnp.dot(s_ref[...], wout_ref[ATTN_WIDTH:, :], preferred_element_type=F32))
    hm = _rms(x1, gmlp_ref[...]).astype(BF16)
    y_ref[...] = x1
    ff_blk = 1024
    for c in range(D_FF // ff_blk):
        cols = slice(c * ff_blk, (c + 1) * ff_blk)
        up = jnp.dot(hm, wup_ref[:, cols], preferred_element_type=F32)
        act = jnp.square(jnp.maximum(up, 0.0)).astype(BF16)
        y_ref[...] += jnp.dot(act, wdown_ref[cols, :], preferred_element_type=F32)
    y_ref[...] = _rms(y_ref[...], gfin_ref[...])


def _out_mlp(x, attn_n, ssm_n, w_out, g_mlp, w_up, w_down, g_fin):
    rows = x.shape[0]
    row = lambda width: pl.BlockSpec((TM_C, width), lambda i: (i, 0))
    return pl.pallas_call(
        _out_mlp_kernel,
        out_shape=jax.ShapeDtypeStruct((rows, D_MODEL), F32),
        grid=(rows // TM_C,),
        in_specs=[row(D_MODEL), row(ATTN_WIDTH), row(SSM_WIDTH),
                  _const_spec((D_MODEL, D_MODEL)), _const_spec((1, D_MODEL)),
                  _const_spec((D_MODEL, D_FF)), _const_spec((D_FF, D_MODEL)),
                  _const_spec((1, D_MODEL))],
        out_specs=row(D_MODEL),
        compiler_params=pltpu.CompilerParams(
            dimension_semantics=("arbitrary",), vmem_limit_bytes=VMEM_LIMIT),
        name="out_mlp",
    )(x, attn_n, ssm_n, w_out, g_mlp, w_up, w_down, g_fin)


def _pair_heads(w, axis):
    shape = w.shape
    split = shape[:axis] + (N_KV_HEADS, Q_PER_KV, HEAD_DIM) + shape[axis + 1:]
    return jnp.swapaxes(w.reshape(split), axis, axis + 1).reshape(shape)


def kernel(x_prompt, x_sample, cache_k, cache_v, state_ssm_re, state_ssm_im, rel_bias, norm_mix, w_in, attn_sinks, ssm_a_re, ssm_a_im, ssm_log_step, ssm_b_re, ssm_b_im, ssm_c_re, ssm_c_im, ssm_d, w_glu, b_glu, norm_attn_out, norm_ssm_out, w_out, norm_mlp, w_up, w_down, norm_final):
    depth = w_in.shape[0]
    bsz, seq, _ = x_prompt.shape
    nseq, steps, _ = x_sample.shape
    bucket = jnp.asarray(_rel_bucket_table())
    g_fin = norm_final.reshape(1, D_MODEL)
    assert depth == 1, "the final norm is fused into the (single) layer's MLP kernel"

    xp, xs = x_prompt, x_sample
    outs = [[] for _ in range(8)]
    for l in range(depth):
        lr, li, lcr, lci, bbr, bbi = _ssm_prep(ssm_a_re[l], ssm_a_im[l], ssm_log_step[l], ssm_b_re[l], ssm_b_im[l])
        lam = (lr, li)
        lam_chunk = (lcr, lci)
        bbar = (bbr, bbi)
        c_gcp = (ssm_c_re[l], ssm_c_im[l])
        g_mix = norm_mix[l].reshape(1, D_MODEL)
        g_attn = _pair_heads(norm_attn_out[l].reshape(1, ATTN_WIDTH), axis=1)
        g_ssm = norm_ssm_out[l].reshape(1, SSM_WIDTH)
        g_mlp = norm_mlp[l].reshape(1, D_MODEL)
        sinks = attn_sinks[l].reshape(1, N_Q_HEADS)
        d = ssm_d[l].reshape(1, SSM_WIDTH)
        bg = b_glu[l].reshape(1, SSM_WIDTH)
        w_in_b = jnp.concatenate([_pair_heads(w_in[l][:, :ATTN_WIDTH], axis=1), w_in[l][:, ATTN_WIDTH:]],
                                 axis=1).astype(BF16)

        attn_n, u, k1, v1, w_out_b, w_glu_b, w_up_b, w_down_b = _in_attn(
            xp, rel_bias, sinks, g_mix, w_in_b, bucket.T, g_attn.reshape(ATTN_WIDTH, 1),
            w_out[l], w_glu[l], w_up[l], w_down[l])
        ssm_n, r1, i1 = _scan(u, lam, lam_chunk, bbar, c_gcp, d, w_glu_b, bg, g_ssm)
        xp = _out_mlp(xp.reshape(bsz * seq, D_MODEL), attn_n.reshape(bsz * seq, ATTN_WIDTH),
                      ssm_n.reshape(bsz * seq, SSM_WIDTH), w_out_b, g_mlp, w_up_b, w_down_b, g_fin
                      ).reshape(bsz, seq, D_MODEL)

        ck = cache_k[l].reshape(nseq, WINDOW, KV_WIDTH)
        cv = cache_v[l].reshape(nseq, WINDOW, KV_WIDTH)
        s_attn, s_u, k2, v2 = _s_in_attn(xs, rel_bias, sinks, g_mix, w_in_b, bucket, g_attn, ck, cv)
        s_ssm, r2, i2 = _s_scan(s_u, state_ssm_re[l].reshape(nseq, STATE_LANES),
                                state_ssm_im[l].reshape(nseq, STATE_LANES),
                                lam, bbar, c_gcp, d, w_glu_b, bg, g_ssm)
        xs = _out_mlp(xs.reshape(nseq * steps, D_MODEL), s_attn, s_ssm, w_out_b, g_mlp, w_up_b, w_down_b,
                      g_fin).reshape(nseq, steps, D_MODEL)

        kv_shape = (WINDOW, N_KV_HEADS, HEAD_DIM)
        st_shape = (N_SSM_GROUPS, SSM_STATE)
        for lst, val in zip(outs, (k1.reshape(bsz, *kv_shape), v1.reshape(bsz, *kv_shape),
                                   r1.reshape(bsz, *st_shape), i1.reshape(bsz, *st_shape),
                                   k2.reshape(nseq, *kv_shape), v2.reshape(nseq, *kv_shape),
                                   r2.reshape(nseq, *st_shape), i2.reshape(nseq, *st_shape))):
            lst.append(val)

    return (xp, xs) + tuple(jnp.stack(o) for o in outs)
```

```python
import math

import numpy as np
import jax
import jax.numpy as jnp
from jax import lax
from jax.experimental import pallas as pl
from jax.experimental.pallas import tpu as pltpu

D_MODEL = 1024
HEAD_DIM = 64
ATTN_WIDTH = 512
N_Q_HEADS = 8
N_KV_HEADS = 2
Q_PER_KV = 4
KV_WIDTH = 128
WINDOW = 128
BLOCK = 128
N_BUCKETS = 32
MAX_DISTANCE = 128
SSM_WIDTH = 512
SSM_GROUP = 16
N_SSM_GROUPS = 32
SSM_STATE = 64
STATE_LANES = N_SSM_GROUPS * SSM_STATE
D_FF = 4096
D_IN_PROJ = 1280
EPS = 1e-6
NEG = -1e30
LOG2E = math.log2(math.e)

LANES = 128
SUBLANES = 8
MXU_DIM = 256
VMEM_LIMIT = 56 * 1024 * 1024

F32 = jnp.float32
BF16 = jnp.bfloat16

TM_A = 512
TM_C = 512
TM_B = 512
CHUNK = TM_B // SUBLANES
PITCH = CHUNK + SUBLANES
S_SEQ = 16


def _rms(x, g):
    return x * lax.rsqrt(jnp.mean(x * x, axis=-1, keepdims=True) + EPS) * g


def _rel_bucket_table():
    i = np.arange(BLOCK)[:, None]
    j = np.arange(2 * BLOCK)[None, :]
    n = np.maximum(i + BLOCK - j, 0)
    max_exact = N_BUCKETS // 2
    nf = np.maximum(n, max_exact).astype(np.float32)
    large = max_exact + (np.log(nf / np.float32(max_exact)) / np.float32(math.log(MAX_DISTANCE / max_exact))
                         * np.float32(N_BUCKETS - max_exact)).astype(np.int32)
    large = np.minimum(large, N_BUCKETS - 1)
    return np.where(n < max_exact, n, large).astype(np.int32)


def _build_bias(relb_ref, bucket_ref, bias_scr):
    bucket = bucket_ref[...]
    for h in range(N_Q_HEADS):
        bias_scr[h] = jnp.zeros((BLOCK, 2 * BLOCK), F32)
    for b in range(N_BUCKETS):
        hit = bucket == b
        for h in range(N_Q_HEADS):
            bias_scr[h] = jnp.where(hit, relb_ref[b, h], bias_scr[h])


def _build_bias_t(relb_ref, bucket_t_ref, bias_t_scr):
    bucket_t = bucket_t_ref[...]
    bias_t_scr[...] = jnp.zeros_like(bias_t_scr)
    for b in range(N_BUCKETS):
        hit = bucket_t == b
        for h in range(N_Q_HEADS):
            cols = slice(h * BLOCK, (h + 1) * BLOCK)
            bias_t_scr[:, cols] = jnp.where(hit, relb_ref[b, h] * LOG2E, bias_t_scr[:, cols])
    kj = lax.broadcasted_iota(jnp.int32, (2 * BLOCK, BLOCK), 0)
    qi = lax.broadcasted_iota(jnp.int32, (2 * BLOCK, BLOCK), 1)
    dist = qi + BLOCK - kj
    in_band = jnp.logical_and(dist >= 0, dist < WINDOW)
    for h in range(N_Q_HEADS):
        cols = slice(h * BLOCK, (h + 1) * BLOCK)
        bias_t_scr[:, cols] = jnp.where(in_band, bias_t_scr[:, cols], NEG)


def _ssm_prep_kernel(are_ref, aim_ref, ls_ref, btr_ref, bti_ref,
                     lr_ref, li_ref, lcr_ref, lci_ref, bbr_ref, bbi_ref):
    ar = are_ref[...]
    ai = aim_ref[...]
    step = jnp.exp(ls_ref[...])
    zr = ar * step
    zi = ai * step
    mag = jnp.exp(zr)
    lr = mag * jnp.cos(zi)
    li = mag * jnp.sin(zi)
    pr, pi = lr, li
    for _ in range(int(math.log2(CHUNK))):
        pr, pi = pr * pr - pi * pi, 2.0 * pr * pi
    for g in range(N_SSM_GROUPS):
        st = slice(g * SSM_STATE, (g + 1) * SSM_STATE)
        lr_ref[:, st] = lr[g]
        li_ref[:, st] = li[g]
        lcr_ref[:, st] = pr[g]
        lci_ref[:, st] = pi[g]
    nr = lr - 1.0
    ni = li
    den = ar * ar + ai * ai
    cr = (nr * ar + ni * ai) / den
    ci = (ni * ar - nr * ai) / den
    btr = btr_ref[...]
    bti = bti_ref[...]
    bbr_ref[...] = cr * btr - ci * bti
    bbi_ref[...] = cr * bti + ci * btr


def _ssm_prep(a_re, a_im, log_step, b_re, b_im):
    g, p, c = N_SSM_GROUPS, SSM_STATE, SSM_GROUP
    vec = jax.ShapeDtypeStruct((1, STATE_LANES), F32)
    mat = jax.ShapeDtypeStruct((g, c, p), F32)
    return pl.pallas_call(
        _ssm_prep_kernel,
        out_shape=(vec, vec, vec, vec, mat, mat),
        name="ssm_prep",
    )(a_re.reshape(g, 1, p), a_im.reshape(g, 1, p), log_step.reshape(g, 1, 1),
      jnp.transpose(b_re, (0, 2, 1)), jnp.transpose(b_im, (0, 2, 1)))


def _stack_heads(q_blk, lane_lo):
    zero = jnp.zeros_like(q_blk[:, 0:LANES])
    lo, hi = [], []
    for c in range(ATTN_WIDTH // LANES):
        blk = q_blk[:, c * LANES:(c + 1) * LANES]
        lo.append(jnp.where(lane_lo, blk, zero))
        hi.append(jnp.where(lane_lo, zero, blk))
    return jnp.concatenate(lo + hi, axis=0)


def _unstack_heads(o, rows, lane_lo):
    nblk = ATTN_WIDTH // LANES
    return jnp.concatenate(
        [jnp.where(lane_lo, o[c * rows:(c + 1) * rows], o[(c + nblk) * rows:(c + nblk + 1) * rows])
         for c in range(nblk)], axis=-1)


Q_COLS = 256


def _proj_stages(x_ref, gmix_ref, win_ref, q_stage, k_stage, vt_stage, u_ref, newk_ref, newv_ref):
    xn = _rms(x_ref[...], gmix_ref[...]).astype(BF16)
    q_scale = HEAD_DIM ** -0.5 * LOG2E
    for c in range(ATTN_WIDTH // Q_COLS):
        cols = slice(c * Q_COLS, (c + 1) * Q_COLS)
        q = jnp.dot(xn, win_ref[:, cols], preferred_element_type=F32)
        q_stage[:, cols] = (q * q_scale).astype(BF16)
        yield
    kv = jnp.dot(xn, win_ref[:, ATTN_WIDTH:ATTN_WIDTH + 2 * KV_WIDTH], preferred_element_type=F32)
    k_stage[...] = kv[:, :KV_WIDTH].astype(BF16)
    vt_stage[...] = kv[:, KV_WIDTH:].T.astype(BF16)
    newk_ref[...] = kv[TM_A - WINDOW:, :KV_WIDTH]
    newv_ref[...] = kv[TM_A - WINDOW:, KV_WIDTH:]
    yield
    u0 = ATTN_WIDTH + 2 * KV_WIDTH
    for c in range(SSM_WIDTH // Q_COLS):
        cols = slice(u0 + c * Q_COLS, u0 + (c + 1) * Q_COLS)
        u_ref[:, c * Q_COLS:(c + 1) * Q_COLS] = jnp.dot(xn, win_ref[:, cols], preferred_element_type=F32)
        yield


def _attn_stages(sinks_ref, attn_ref, k_scr, vt_scr, q_scr, bias_t_scr, gt_scr, first_tile):
    nstk = N_Q_HEADS * BLOCK
    nblk = ATTN_WIDTH // LANES
    lane_lo = lax.broadcasted_iota(jnp.int32, (BLOCK, LANES), 1) < HEAD_DIM
    row_lo = lax.broadcasted_iota(jnp.int32, (KV_WIDTH, BLOCK), 0) < HEAD_DIM
    kj = lax.broadcasted_iota(jnp.int32, (2 * BLOCK, nstk), 0)
    has_prev = jnp.logical_or(kj >= BLOCK, jnp.logical_not(first_tile))
    head = lax.broadcasted_iota(jnp.int32, (1, nstk), 1) // BLOCK
    sink = jnp.zeros((1, nstk), F32)
    for h in range(N_Q_HEADS):
        sink = jnp.where(head == h, sinks_ref[0, h] * LOG2E, sink)

    scores = []
    for j in range(TM_A // BLOCK):
        r0 = j * BLOCK
        q2 = _stack_heads(q_scr[r0:r0 + BLOCK, :], lane_lo)
        keys = k_scr[r0:r0 + 2 * BLOCK, :]
        scores.append(lax.dot_general(keys, q2, (((1,), (1,)), ((), ())), preferred_element_type=F32))
        yield
    for j in range(TM_A // BLOCK):
        r0 = j * BLOCK
        s = scores[j] + bias_t_scr[...]
        if j == 0:
            s = jnp.where(has_prev, s, NEG)
        m = jnp.maximum(jnp.max(s, axis=0, keepdims=True), sink)
        p = jnp.exp2(s - m)
        denom = jnp.sum(p, axis=0, keepdims=True) + jnp.exp2(sink - m)
        o = jnp.dot(vt_scr[:, r0:r0 + 2 * BLOCK], p.astype(BF16), preferred_element_type=F32) / denom
        a = jnp.concatenate(
            [jnp.where(row_lo, o[:, c * BLOCK:(c + 1) * BLOCK], o[:, (c + nblk) * BLOCK:(c + nblk + 1) * BLOCK])
             for c in range(nblk)], axis=0)
        a = a * lax.rsqrt(jnp.mean(a * a, axis=0, keepdims=True) + EPS) * gt_scr[...]
        attn_ref[r0:r0 + BLOCK, :] = a.T.astype(BF16)
        yield


def _in_attn_kernel(relb_ref, sinks_ref, x_ref, gmix_ref, win_ref, bucket_t_ref, gattn_ref,
                    wout_f, wglu_f, wup_f, wdown_f,
                    attn_ref, u_ref, newk_ref, newv_ref, wout_b, wglu_b, wup_b, wdown_b,
                    k_scr, vt_scr, q_scr, k_stage, vt_stage, q_stage, bias_t_scr, gt_scr):
    b = pl.program_id(0)
    t = pl.program_id(1)

    @pl.when(jnp.logical_and(b == 0, t == 0))
    def _():
        _build_bias_t(relb_ref, bucket_t_ref, bias_t_scr)
        gt_scr[...] = jnp.broadcast_to(gattn_ref[...], (ATTN_WIDTH, BLOCK))
        k_scr[...] = jnp.zeros_like(k_scr)
        vt_scr[...] = jnp.zeros_like(vt_scr)
        k_stage[...] = jnp.zeros_like(k_stage)
        vt_stage[...] = jnp.zeros_like(vt_stage)
        q_stage[...] = jnp.zeros_like(q_stage)

    k_scr[0:BLOCK, :] = k_scr[TM_A:TM_A + BLOCK, :]
    vt_scr[:, 0:BLOCK] = vt_scr[:, TM_A:TM_A + BLOCK]
    k_scr[BLOCK:BLOCK + TM_A, :] = k_stage[...]
    vt_scr[:, BLOCK:BLOCK + TM_A] = vt_stage[...]
    q_scr[...] = q_stage[...]

    wout_b[...] = wout_f[...].astype(BF16)
    wglu_b[...] = wglu_f[...].astype(BF16)
    wup_b[...] = wup_f[...].astype(BF16)
    wdown_b[...] = wdown_f[...].astype(BF16)

    proj = _proj_stages(x_ref, gmix_ref, win_ref, q_stage, k_stage, vt_stage, u_ref, newk_ref, newv_ref)
    attn = _attn_stages(sinks_ref, attn_ref, k_scr, vt_scr, q_scr, bias_t_scr, gt_scr, t <= 1)
    for stage in "paapaapapaapa":
        next(proj if stage == "p" else attn)
    assert next(proj, None) is None and next(attn, None) is None


def _const_spec(shape):
    zeros = (0,) * len(shape)
    return pl.BlockSpec(shape, lambda *_: zeros, pipeline_mode=pl.Buffered(1))


def _smem_spec():
    return pl.BlockSpec(memory_space=pltpu.SMEM)


def _in_attn(x, rel_bias, sinks, g_mix, w_in, bucket_t, g_attn_col, w_out, w_glu, w_up, w_down):
    bsz, seq, _ = x.shape
    nt = seq // TM_A
    steps = bsz * nt
    cur = lambda width: pl.BlockSpec((None, TM_A, width), lambda b, t: (b, jnp.minimum(t, nt - 1), 0))
    prev = lambda width: pl.BlockSpec((None, TM_A, width), lambda b, t: (b, jnp.maximum(t - 1, 0), 0))
    win_spec = pl.BlockSpec((None, WINDOW, KV_WIDTH), lambda b, t: (b, 0, 0))

    def tile(b, t):
        return b * nt + jnp.minimum(t, nt - 1)

    def rows_of(w):
        blk = w.shape[0] // steps
        return pl.BlockSpec((blk, w.shape[1]), lambda b, t: (tile(b, t), 0))

    nhead_blk = D_MODEL // HEAD_DIM

    def wout_src(b, t):
        return (jnp.minimum(tile(b, t), nhead_blk - 1), 0)

    def wout_dst(b, t):
        j = jnp.minimum(tile(b, t), nhead_blk - 1)
        return (jnp.where(j < N_Q_HEADS, (j % Q_PER_KV) * N_KV_HEADS + j // Q_PER_KV, j), 0)

    assert steps >= nhead_blk and w_up.shape[0] % steps == 0 and w_down.shape[0] % steps == 0
    wout_in = pl.BlockSpec((HEAD_DIM, D_MODEL), wout_src)
    wout_out = pl.BlockSpec((HEAD_DIM, D_MODEL), wout_dst)
    cast_shape = lambda w: jax.ShapeDtypeStruct(w.shape, BF16)
    return pl.pallas_call(
        _in_attn_kernel,
        out_shape=(jax.ShapeDtypeStruct((bsz, seq, ATTN_WIDTH), BF16),
                   jax.ShapeDtypeStruct((bsz, seq, SSM_WIDTH), F32),
                   jax.ShapeDtypeStruct((bsz, WINDOW, KV_WIDTH), F32),
                   jax.ShapeDtypeStruct((bsz, WINDOW, KV_WIDTH), F32),
                   cast_shape(w_out), cast_shape(w_glu), cast_shape(w_up), cast_shape(w_down)),
        grid=(bsz, nt + 1),
        in_specs=[_smem_spec(), _smem_spec(), cur(D_MODEL), _const_spec((1, D_MODEL)),
                  _const_spec((D_MODEL, D_IN_PROJ)), _const_spec((2 * BLOCK, BLOCK)),
                  _const_spec((ATTN_WIDTH, 1)),
                  wout_in, rows_of(w_glu), rows_of(w_up), rows_of(w_down)],
        out_specs=(prev(ATTN_WIDTH), cur(SSM_WIDTH), win_spec, win_spec,
                   wout_out, rows_of(w_glu), rows_of(w_up), rows_of(w_down)),
        scratch_shapes=[pltpu.VMEM((TM_A + BLOCK, KV_WIDTH), BF16),
                        pltpu.VMEM((KV_WIDTH, TM_A + BLOCK), BF16),
                        pltpu.VMEM((TM_A, ATTN_WIDTH), BF16),
                        pltpu.VMEM((TM_A, KV_WIDTH), BF16),
                        pltpu.VMEM((KV_WIDTH, TM_A), BF16),
                        pltpu.VMEM((TM_A, ATTN_WIDTH), BF16),
                        pltpu.VMEM((2 * BLOCK, N_Q_HEADS * BLOCK), F32),
                        pltpu.VMEM((ATTN_WIDTH, BLOCK), F32)],
        compiler_params=pltpu.CompilerParams(
            dimension_semantics=("arbitrary", "arbitrary"), vmem_limit_bytes=VMEM_LIMIT),
        name="in_attn",
    )(rel_bias, sinks, x, g_mix, w_in, bucket_t, g_attn_col, w_out, w_glu, w_up, w_down)


def _s_in_attn_kernel(relb_ref, sinks_ref, x_ref, gmix_ref, win_ref, bucket_ref, gattn_ref,
                      ck_ref, cv_ref,
                      attn_ref, u_ref, newk_ref, newv_ref,
                      bias_scr, biasd_scr):
    steps = x_ref.shape[0] // S_SEQ
    pair = SUBLANES // steps
    prow = N_Q_HEADS * SUBLANES

    @pl.when(pl.program_id(0) == 0)
    def _():
        _build_bias(relb_ref, bucket_ref, bias_scr)
        for h in range(N_Q_HEADS):
            for k in range(pair):
                biasd_scr[h * SUBLANES + k * steps:h * SUBLANES + (k + 1) * steps, :] = bias_scr[h, 0:steps, :]

    xn = _rms(x_ref[...], gmix_ref[...]).astype(BF16)
    proj = jnp.dot(xn, win_ref[...], preferred_element_type=F32)
    u_ref[...] = proj[:, ATTN_WIDTH + 2 * KV_WIDTH:]
    qs = proj[:, :ATTN_WIDTH] * (HEAD_DIM ** -0.5)

    nkeys = 2 * WINDOW
    ri = lax.broadcasted_iota(jnp.int32, (prow, nkeys), 0)
    kj = lax.broadcasted_iota(jnp.int32, (prow, nkeys), 1)
    dist = ri % steps + WINDOW - kj
    valid = jnp.logical_and(dist >= 0, dist < WINDOW)
    head = lax.broadcasted_iota(jnp.int32, (prow, 1), 0) // SUBLANES
    sink = jnp.zeros((prow, 1), F32)
    for h in range(N_Q_HEADS):
        sink = jnp.where(head == h, sinks_ref[0, h], sink)
    seq_in_tile = (lax.broadcasted_iota(jnp.int32, (prow, LANES), 0) % SUBLANES) // steps
    lane_lo = lax.broadcasted_iota(jnp.int32, (SUBLANES, LANES), 1) < HEAD_DIM
    row8 = lax.broadcasted_iota(jnp.int32, (SUBLANES, KV_WIDTH), 0)
    gattn = gattn_ref[...]
    tail = jnp.zeros((nkeys - WINDOW - SUBLANES, KV_WIDTH), F32)
    bias = biasd_scr[...]

    tiles = []
    for tl in range(S_SEQ // pair):
        rws = slice(tl * SUBLANES, (tl + 1) * SUBLANES)
        q2 = _stack_heads(qs[rws], lane_lo).astype(BF16)
        kn8 = proj[rws, ATTN_WIDTH:ATTN_WIDTH + KV_WIDTH]
        vn8 = proj[rws, ATTN_WIDTH + KV_WIDTH:ATTN_WIDTH + 2 * KV_WIDTH]
        o_tile = jnp.zeros((prow, LANES), F32)
        for k in range(pair):
            sq = tl * pair + k
            shift = (SUBLANES - k * steps) % SUBLANES
            kn = jnp.where(row8 < steps, pltpu.roll(kn8, shift, axis=0) if shift else kn8, 0.0)
            vn = jnp.where(row8 < steps, pltpu.roll(vn8, shift, axis=0) if shift else vn8, 0.0)
            ck = ck_ref[sq]
            cv = cv_ref[sq]
            newk_ref[sq, 0:WINDOW - steps, :] = ck[steps:, :]
            newk_ref[sq, WINDOW - steps:, :] = kn[0:steps]
            newv_ref[sq, 0:WINDOW - steps, :] = cv[steps:, :]
            newv_ref[sq, WINDOW - steps:, :] = vn[0:steps]
            keys = jnp.concatenate([ck, kn, tail], axis=0).astype(BF16)
            vals = jnp.concatenate([cv, vn, tail], axis=0).astype(BF16)
            s = lax.dot_general(q2, keys, (((1,), (1,)), ((), ())), preferred_element_type=F32)
            s = jnp.where(valid, s + bias, NEG)
            m = jnp.maximum(jnp.max(s, axis=-1, keepdims=True), sink)
            p = jnp.exp(s - m)
            denom = jnp.sum(p, axis=-1, keepdims=True) + jnp.exp(sink - m)
            o = jnp.dot(p.astype(BF16), vals, preferred_element_type=F32) / denom
            o_tile = jnp.where(seq_in_tile == k, o, o_tile)
        tiles.append(_unstack_heads(o_tile, SUBLANES, lane_lo))
    attn_ref[...] = _rms(jnp.concatenate(tiles, axis=0), gattn).astype(BF16)


def _s_in_attn(x, rel_bias, sinks, g_mix, w_in, bucket, g_attn, cache_k, cache_v):
    nseq, steps, _ = x.shape
    rows = S_SEQ * steps
    xf = x.reshape(nseq * steps, D_MODEL)
    row = lambda width: pl.BlockSpec((rows, width), lambda i: (i, 0))
    cache = pl.BlockSpec((S_SEQ, WINDOW, KV_WIDTH), lambda i: (i, 0, 0))
    return pl.pallas_call(
        _s_in_attn_kernel,
        out_shape=(jax.ShapeDtypeStruct((nseq * steps, ATTN_WIDTH), BF16),
                   jax.ShapeDtypeStruct((nseq * steps, SSM_WIDTH), F32),
                   jax.ShapeDtypeStruct((nseq, WINDOW, KV_WIDTH), F32),
                   jax.ShapeDtypeStruct((nseq, WINDOW, KV_WIDTH), F32)),
        grid=(nseq // S_SEQ,),
        in_specs=[_smem_spec(), _smem_spec(), row(D_MODEL), _const_spec((1, D_MODEL)),
                  _const_spec((D_MODEL, D_IN_PROJ)), _const_spec((BLOCK, 2 * BLOCK)),
                  _const_spec((1, ATTN_WIDTH)), cache, cache],
        out_specs=(row(ATTN_WIDTH), row(SSM_WIDTH), cache, cache),
        scratch_shapes=[pltpu.VMEM((N_Q_HEADS, BLOCK, 2 * BLOCK), F32),
                        pltpu.VMEM((N_Q_HEADS * SUBLANES, 2 * WINDOW), F32)],
        compiler_params=pltpu.CompilerParams(
            dimension_semantics=("arbitrary",), vmem_limit_bytes=VMEM_LIMIT),
        name="s_in_attn",
    )(rel_bias, sinks, xf, g_mix, w_in, bucket, g_attn, cache_k, cache_v)


def _embed_blocks(bbr_ref, bbi_ref, cre_in, cim_in, bre, bim, cre, cim):
    for ref in (bre, bim, cre, cim):
        ref[...] = jnp.zeros_like(ref)
    for g in range(N_SSM_GROUPS):
        ch = slice(g * SSM_GROUP, (g + 1) * SSM_GROUP)
        st = slice(g * SSM_STATE, (g + 1) * SSM_STATE)
        bre[ch, st] = bbr_ref[g].astype(BF16)
        bim[ch, st] = bbi_ref[g].astype(BF16)
        cre[ch, st] = cre_in[g].astype(BF16)
        cim[ch, st] = cim_in[g].astype(BF16)


def _output_map(h_slab_re, h_slab_im, cre_ref, cim_ref, ch, cols):
    nt_dims = (((1,), (1,)), ((), ()))
    yr = lax.dot_general(h_slab_re.astype(BF16), cre_ref[ch, cols], nt_dims, preferred_element_type=F32)
    yi = lax.dot_general(h_slab_im.astype(BF16), cim_ref[ch, cols], nt_dims, preferred_element_type=F32)
    return yr - yi


def _glu_norm(y, wglu_ref, bglu_ref, gssm_ref):
    g = jax.nn.gelu(y)
    gate = jax.nn.sigmoid(jnp.dot(g.astype(BF16), wglu_ref[...], preferred_element_type=F32) + bglu_ref[...])
    return _rms(g * gate, gssm_ref[...])


SLAB = 4 * LANES
SLAB_CH = SLAB // SSM_STATE * SSM_GROUP


def _scan_kernel(u_ref, lr_ref, li_ref, lcr_ref, lci_ref, bbr_ref, bbi_ref, cre_in, cim_in,
                 d_ref, wglu_ref, bglu_ref, gssm_ref,
                 out_ref, hre_out, him_out,
                 bre_ref, bim_ref, cre_ref, cim_ref,
                 pad_scr, up_scr, hre, him, carry_re, carry_im, e_re, e_im, hc_re, hc_im):
    t = pl.program_id(1)
    nslab = SSM_WIDTH // LANES

    @pl.when(jnp.logical_and(pl.program_id(0) == 0, t == 0))
    def _():
        _embed_blocks(bbr_ref, bbi_ref, cre_in, cim_in, bre_ref, bim_ref, cre_ref, cim_ref)

    @pl.when(t == 0)
    def _():
        carry_re[...] = jnp.zeros_like(carry_re)
        carry_im[...] = jnp.zeros_like(carry_im)

    for c in range(SUBLANES):
        for s in range(nslab):
            pad_scr[s, c * PITCH:c * PITCH + CHUNK, :] = u_ref[c * CHUNK:(c + 1) * CHUNK, s * LANES:(s + 1) * LANES]
    for tau in range(CHUNK):
        for s in range(nslab):
            up_scr[tau * SUBLANES:(tau + 1) * SUBLANES, s * LANES:(s + 1) * LANES] = (
                pad_scr[s, pl.ds(tau, SUBLANES, stride=PITCH), :])
    up = up_scr[...]
    upb = up.astype(BF16)

    nslabs = STATE_LANES // SLAB

    def drive(sl):
        cols = slice(sl * SLAB, (sl + 1) * SLAB)
        ch = slice(sl * SLAB_CH, (sl + 1) * SLAB_CH)
        hre[:, cols] = jnp.dot(upb[:, ch], bre_ref[ch, cols], preferred_element_type=F32)
        him[:, cols] = jnp.dot(upb[:, ch], bim_ref[ch, cols], preferred_element_type=F32)

    for sl in range(nslabs):
        drive(sl)
    ys = []
    for sl in range(nslabs):
        cols = slice(sl * SLAB, (sl + 1) * SLAB)
        ch = slice(sl * SLAB_CH, (sl + 1) * SLAB_CH)
        ar = jnp.broadcast_to(lr_ref[:, cols], (SUBLANES, SLAB))
        ai = jnp.broadcast_to(li_ref[:, cols], (SUBLANES, SLAB))

        hr = jnp.zeros((SUBLANES, SLAB), F32)
        hi = jnp.zeros((SUBLANES, SLAB), F32)
        for tau in range(CHUNK):
            rws = slice(tau * SUBLANES, (tau + 1) * SUBLANES)
            hr, hi = ar * hr - ai * hi + hre[rws, cols], ar * hi + ai * hr + him[rws, cols]
        e_re[:, cols] = hr
        e_im[:, cols] = hi

        cr = carry_re[:, cols]
        ci = carry_im[:, cols]
        lcr = lcr_ref[:, cols]
        lci = lci_ref[:, cols]
        for c in range(SUBLANES):
            hc_re[c:c + 1, cols] = cr
            hc_im[c:c + 1, cols] = ci
            er = e_re[c:c + 1, cols]
            ei = e_im[c:c + 1, cols]
            cr, ci = lcr * cr - lci * ci + er, lcr * ci + lci * cr + ei
        carry_re[:, cols] = cr
        carry_im[:, cols] = ci

        hr = hc_re[:, cols]
        hi = hc_im[:, cols]
        for tau in range(CHUNK):
            rws = slice(tau * SUBLANES, (tau + 1) * SUBLANES)
            hr, hi = ar * hr - ai * hi + hre[rws, cols], ar * hi + ai * hr + him[rws, cols]
            hre[rws, cols] = hr
            him[rws, cols] = hi

        ys.append(_output_map(hre[:, cols], him[:, cols], cre_ref, cim_ref, ch, cols))

    y = jnp.concatenate(ys, axis=-1) + d_ref[...] * up
    o = _glu_norm(y, wglu_ref, bglu_ref, gssm_ref)

    for tau in range(CHUNK):
        for s in range(nslab):
            pad_scr[s, pl.ds(tau, SUBLANES, stride=PITCH), :] = (
                o[tau * SUBLANES:(tau + 1) * SUBLANES, s * LANES:(s + 1) * LANES])
    for c in range(SUBLANES):
        for s in range(nslab):
            out_ref[c * CHUNK:(c + 1) * CHUNK, s * LANES:(s + 1) * LANES] = (
                pad_scr[s, c * PITCH:c * PITCH + CHUNK, :].astype(BF16))

    @pl.when(t == pl.num_programs(1) - 1)
    def _():
        hre_out[...] = carry_re[...]
        him_out[...] = carry_im[...]


def _block_operand_scratch():
    return [pltpu.VMEM((SSM_WIDTH, STATE_LANES), BF16) for _ in range(4)]


def _group_blocks():
    return _const_spec((N_SSM_GROUPS, SSM_GROUP, SSM_STATE))


def _scan(u, lam, lam_chunk, bbar, c_gcp, d, w_glu, b_glu, g_ssm):
    bsz, seq, _ = u.shape
    nt = seq // TM_B
    row = pl.BlockSpec((None, TM_B, SSM_WIDTH), lambda b, t: (b, t, 0))
    state = pl.BlockSpec((None, 1, STATE_LANES), lambda b, t: (b, 0, 0))
    vec = _const_spec((1, STATE_LANES))
    chan = _const_spec((1, SSM_WIDTH))
    return pl.pallas_call(
        _scan_kernel,
        out_shape=(jax.ShapeDtypeStruct((bsz, seq, SSM_WIDTH), BF16),
                   jax.ShapeDtypeStruct((bsz, 1, STATE_LANES), F32),
                   jax.ShapeDtypeStruct((bsz, 1, STATE_LANES), F32)),
        grid=(bsz, nt),
        in_specs=[row, vec, vec, vec, vec,
                  _group_blocks(), _group_blocks(), _group_blocks(), _group_blocks(),
                  chan, _const_spec((SSM_WIDTH, SSM_WIDTH)), chan, chan],
        out_specs=(row, state, state),
        scratch_shapes=_block_operand_scratch() + [
                        pltpu.VMEM((SSM_WIDTH // LANES, SUBLANES * PITCH, LANES), F32),
                        pltpu.VMEM((TM_B, SSM_WIDTH), F32),
                        pltpu.VMEM((TM_B, STATE_LANES), F32),
                        pltpu.VMEM((TM_B, STATE_LANES), F32),
                        pltpu.VMEM((1, STATE_LANES), F32),
                        pltpu.VMEM((1, STATE_LANES), F32),
                        pltpu.VMEM((SUBLANES, STATE_LANES), F32),
                        pltpu.VMEM((SUBLANES, STATE_LANES), F32),
                        pltpu.VMEM((SUBLANES, STATE_LANES), F32),
                        pltpu.VMEM((SUBLANES, STATE_LANES), F32)],
        compiler_params=pltpu.CompilerParams(
            dimension_semantics=("arbitrary", "arbitrary"), vmem_limit_bytes=VMEM_LIMIT),
        name="scan",
    )(u, lam[0], lam[1], lam_chunk[0], lam_chunk[1], bbar[0], bbar[1], c_gcp[0], c_gcp[1],
      d, w_glu, b_glu, g_ssm)


def _s_scan_kernel(u_ref, h0r_ref, h0i_ref, lr_ref, li_ref, bbr_ref, bbi_ref, cre_in, cim_in,
                   d_ref, wglu_ref, bglu_ref, gssm_ref,
                   out_ref, hre_out, him_out,
                   bre_ref, bim_ref, cre_ref, cim_ref,
                   pad_scr, up_scr, hre, him):
    nseq = h0r_ref.shape[0]
    steps = u_ref.shape[0] // nseq
    nslab = SSM_WIDTH // LANES
    _embed_blocks(bbr_ref, bbi_ref, cre_in, cim_in, bre_ref, bim_ref, cre_ref, cim_ref)

    for s in range(nslab):
        pad_scr[s] = u_ref[:, s * LANES:(s + 1) * LANES]
    for i in range(steps):
        for s in range(nslab):
            up_scr[i * nseq:(i + 1) * nseq, s * LANES:(s + 1) * LANES] = (
                pad_scr[s, pl.ds(i, nseq, stride=steps), :])
    up = up_scr[...]
    upb = up.astype(BF16)

    ys = []
    for sl in range(STATE_LANES // SLAB):
        cols = slice(sl * SLAB, (sl + 1) * SLAB)
        ch = slice(sl * SLAB_CH, (sl + 1) * SLAB_CH)
        hre[:, cols] = jnp.dot(upb[:, ch], bre_ref[ch, cols], preferred_element_type=F32)
        him[:, cols] = jnp.dot(upb[:, ch], bim_ref[ch, cols], preferred_element_type=F32)
        ar = lr_ref[:, cols]
        ai = li_ref[:, cols]
        hr = h0r_ref[:, cols]
        hi = h0i_ref[:, cols]
        for i in range(steps):
            rws = slice(i * nseq, (i + 1) * nseq)
            hr, hi = ar * hr - ai * hi + hre[rws, cols], ar * hi + ai * hr + him[rws, cols]
            hre[rws, cols] = hr
            him[rws, cols] = hi
        hre_out[:, cols] = hr
        him_out[:, cols] = hi
        ys.append(_output_map(hre[:, cols], him[:, cols], cre_ref, cim_ref, ch, cols))

    y = jnp.concatenate(ys, axis=-1) + d_ref[...] * up
    o = _glu_norm(y, wglu_ref, bglu_ref, gssm_ref)

    for i in range(steps):
        for s in range(nslab):
            pad_scr[s, pl.ds(i, nseq, stride=steps), :] = o[i * nseq:(i + 1) * nseq, s * LANES:(s + 1) * LANES]
    for s in range(nslab):
        out_ref[:, s * LANES:(s + 1) * LANES] = pad_scr[s].astype(BF16)


def _s_scan(u, h0_re, h0_im, lam, bbar, c_gcp, d, w_glu, b_glu, g_ssm):
    rows = u.shape[0]
    nseq = h0_re.shape[0]
    return pl.pallas_call(
        _s_scan_kernel,
        out_shape=(jax.ShapeDtypeStruct((rows, SSM_WIDTH), BF16),
                   jax.ShapeDtypeStruct((nseq, STATE_LANES), F32),
                   jax.ShapeDtypeStruct((nseq, STATE_LANES), F32)),
        scratch_shapes=_block_operand_scratch() + [
                        pltpu.VMEM((SSM_WIDTH // LANES, rows, LANES), F32),
                        pltpu.VMEM((rows, SSM_WIDTH), F32),
                        pltpu.VMEM((rows, STATE_LANES), F32),
                        pltpu.VMEM((rows, STATE_LANES), F32)],
        compiler_params=pltpu.CompilerParams(vmem_limit_bytes=VMEM_LIMIT),
        name="s_scan",
    )(u, h0_re, h0_im, lam[0], lam[1], bbar[0], bbar[1], c_gcp[0], c_gcp[1], d, w_glu, b_glu, g_ssm)


def _out_mlp_kernel(x_ref, a_ref, s_ref, wout_ref, gmlp_ref, wup_ref, wdown_ref, gfin_ref, y_ref):
    x1 = (x_ref[...]
          + jnp.dot(a_ref[...], wout_ref[0:ATTN_WIDTH, :], preferred_element_type=F32)
          + jnp.dot(s_ref[...], wout_ref[ATTN_WIDTH:, :], preferred_element_type=F32))
    hm = _rms(x1, gmlp_ref[...]).astype(BF16)
    y_ref[...] = x1
    ff_blk = 1024
    for c in range(D_FF // ff_blk):
        cols = slice(c * ff_blk, (c + 1) * ff_blk)
        up = jnp.dot(hm, wup_ref[:, cols], preferred_element_type=F32)
        act = jnp.square(jnp.maximum(up, 0.0)).astype(BF16)
        y_ref[...] += jnp.dot(act, wdown_ref[cols, :], preferred_element_type=F32)
    y_ref[...] = _rms(y_ref[...], gfin_ref[...])


def _out_mlp(x, attn_n, ssm_n, w_out, g_mlp, w_up, w_down, g_fin):
    rows = x.shape[0]
    row = lambda width: pl.BlockSpec((TM_C, width), lambda i: (i, 0))
    return pl.pallas_call(
        _out_mlp_kernel,
        out_shape=jax.ShapeDtypeStruct((rows, D_MODEL), F32),
        grid=(rows // TM_C,),
        in_specs=[row(D_MODEL), row(ATTN_WIDTH), row(SSM_WIDTH),
                  _const_spec((D_MODEL, D_MODEL)), _const_spec((1, D_MODEL)),
                  _const_spec((D_MODEL, D_FF)), _const_spec((D_FF, D_MODEL)),
                  _const_spec((1, D_MODEL))],
        out_specs=row(D_MODEL),
        compiler_params=pltpu.CompilerParams(
            dimension_semantics=("arbitrary",), vmem_limit_bytes=VMEM_LIMIT),
        name="out_mlp",
    )(x, attn_n, ssm_n, w_out, g_mlp, w_up, w_down, g_fin)


def _pair_heads(w, axis):
    shape = w.shape
    split = shape[:axis] + (N_KV_HEADS, Q_PER_KV, HEAD_DIM) + shape[axis + 1:]
    return jnp.swapaxes(w.reshape(split), axis, axis + 1).reshape(shape)


def kernel(x_prompt, x_sample, cache_k, cache_v, state_ssm_re, state_ssm_im, rel_bias, norm_mix, w_in, attn_sinks, ssm_a_re, ssm_a_im, ssm_log_step, ssm_b_re, ssm_b_im, ssm_c_re, ssm_c_im, ssm_d, w_glu, b_glu, norm_attn_out, norm_ssm_out, w_out, norm_mlp, w_up, w_down, norm_final):
    depth = w_in.shape[0]
    bsz, seq, _ = x_prompt.shape
    nseq, steps, _ = x_sample.shape
    bucket = jnp.asarray(_rel_bucket_table())
    g_fin = norm_final.reshape(1, D_MODEL)
    assert depth == 1, "the final norm is fused into the (single) layer's MLP kernel"

    xp, xs = x_prompt, x_sample
    outs = [[] for _ in range(8)]
    for l in range(depth):
        lr, li, lcr, lci, bbr, bbi = _ssm_prep(ssm_a_re[l], ssm_a_im[l], ssm_log_step[l], ssm_b_re[l], ssm_b_im[l])
        lam = (lr, li)
        lam_chunk = (lcr, lci)
        bbar = (bbr, bbi)
        c_gcp = (ssm_c_re[l], ssm_c_im[l])
        g_mix = norm_mix[l].reshape(1, D_MODEL)
        g_attn = _pair_heads(norm_attn_out[l].reshape(1, ATTN_WIDTH), axis=1)
        g_ssm = norm_ssm_out[l].reshape(1, SSM_WIDTH)
        g_mlp = norm_mlp[l].reshape(1, D_MODEL)
        sinks = attn_sinks[l].reshape(1, N_Q_HEADS)
        d = ssm_d[l].reshape(1, SSM_WIDTH)
        bg = b_glu[l].reshape(1, SSM_WIDTH)
        w_in_b = jnp.concatenate([_pair_heads(w_in[l][:, :ATTN_WIDTH], axis=1), w_in[l][:, ATTN_WIDTH:]],
                                 axis=1).astype(BF16)

        attn_n, u, k1, v1, w_out_b, w_glu_b, w_up_b, w_down_b = _in_attn(
            xp, rel_bias, sinks, g_mix, w_in_b, bucket.T, g_attn.reshape(ATTN_WIDTH, 1),
            w_out[l], w_glu[l], w_up[l], w_down[l])
        ssm_n, r1, i1 = _scan(u, lam, lam_chunk, bbar, c_gcp, d, w_glu_b, bg, g_ssm)
        xp = _out_mlp(xp.reshape(bsz * seq, D_MODEL), attn_n.reshape(bsz * seq, ATTN_WIDTH),
                      ssm_n.reshape(bsz * seq, SSM_WIDTH), w_out_b, g_mlp, w_up_b, w_down_b, g_fin
                      ).reshape(bsz, seq, D_MODEL)

        ck = cache_k[l].reshape(nseq, WINDOW, KV_WIDTH)
        cv = cache_v[l].reshape(nseq, WINDOW, KV_WIDTH)
        s_attn, s_u, k2, v2 = _s_in_attn(xs, rel_bias, sinks, g_mix, w_in_b, bucket, g_attn, ck, cv)
        s_ssm, r2, i2 = _s_scan(s_u, state_ssm_re[l].reshape(nseq, STATE_LANES),
                                state_ssm_im[l].reshape(nseq, STATE_LANES),
                                lam, bbar, c_gcp, d, w_glu_b, bg, g_ssm)
        xs = _out_mlp(xs.reshape(nseq * steps, D_MODEL), s_attn, s_ssm, w_out_b, g_mlp, w_up_b, w_down_b,
                      g_fin).reshape(nseq, steps, D_MODEL)

        kv_shape = (WINDOW, N_KV_HEADS, HEAD_DIM)
        st_shape = (N_SSM_GROUPS, SSM_STATE)
        for lst, val in zip(outs, (k1.reshape(bsz, *kv_shape), v1.reshape(bsz, *kv_shape),
                                   r1.reshape(bsz, *st_shape), i1.reshape(bsz, *st_shape),
                                   k2.reshape(nseq, *kv_shape), v2.reshape(nseq, *kv_shape),
                                   r2.reshape(nseq, *st_shape), i2.reshape(nseq, *st_shape))):
            lst.append(val)

    return (xp, xs) + tuple(jnp.stack(o) for o in outs)
```

```python
import math

import numpy as np
import jax
import jax.numpy as jnp
from jax import lax
from jax.experimental import pallas as pl
from jax.experimental.pallas import tpu as pltpu

D_MODEL = 1024
HEAD_DIM = 64
ATTN_WIDTH = 512
N_Q_HEADS = 8
N_KV_HEADS = 2
Q_PER_KV = 4
KV_WIDTH = 128
WINDOW = 128
BLOCK = 128
N_BUCKETS = 32
MAX_DISTANCE = 128
SSM_WIDTH = 512
SSM_GROUP = 16
N_SSM_GROUPS = 32
SSM_STATE = 64
STATE_LANES = N_SSM_GROUPS * SSM_STATE
D_FF = 4096
D_IN_PROJ = 1280
EPS = 1e-6
NEG = -1e30
LOG2E = math.log2(math.e)

LANES = 128
SUBLANES = 8
MXU_DIM = 256
VMEM_LIMIT = 56 * 1024 * 1024

F32 = jnp.float32
BF16 = jnp.bfloat16

TM_A = 512
TM_C = 1024
TM_B = 512
CHUNK = TM_B // SUBLANES
PITCH = CHUNK + SUBLANES
S_SEQ = 16


def _rms(x, g):
    return x * lax.rsqrt(jnp.mean(x * x, axis=-1, keepdims=True) + EPS) * g


def _rel_bucket_table():
    i = np.arange(BLOCK)[:, None]
    j = np.arange(2 * BLOCK)[None, :]
    n = np.maximum(i + BLOCK - j, 0)
    max_exact = N_BUCKETS // 2
    nf = np.maximum(n, max_exact).astype(np.float32)
    large = max_exact + (np.log(nf / np.float32(max_exact)) / np.float32(math.log(MAX_DISTANCE / max_exact))
                         * np.float32(N_BUCKETS - max_exact)).astype(np.int32)
    large = np.minimum(large, N_BUCKETS - 1)
    return np.where(n < max_exact, n, large).astype(np.int32)


def _build_bias(relb_ref, bucket_ref, bias_scr):
    bucket = bucket_ref[...]
    for h in range(N_Q_HEADS):
        bias_scr[h] = jnp.zeros(bucket.shape, F32)
    for b in range(N_BUCKETS):
        hit = bucket == b
        for h in range(N_Q_HEADS):
            bias_scr[h] = jnp.where(hit, relb_ref[b, h], bias_scr[h])


def _build_bias_t(relb_ref, bucket_t_ref, bias_t_scr):
    bucket_t = bucket_t_ref[...]
    bias_t_scr[...] = jnp.zeros_like(bias_t_scr)
    for b in range(N_BUCKETS):
        hit = bucket_t == b
        for h in range(N_Q_HEADS):
            cols = slice(h * BLOCK, (h + 1) * BLOCK)
            bias_t_scr[:, cols] = jnp.where(hit, relb_ref[b, h] * LOG2E, bias_t_scr[:, cols])
    kj = lax.broadcasted_iota(jnp.int32, (2 * BLOCK, BLOCK), 0)
    qi = lax.broadcasted_iota(jnp.int32, (2 * BLOCK, BLOCK), 1)
    dist = qi + BLOCK - kj
    in_band = jnp.logical_and(dist >= 0, dist < WINDOW)
    for h in range(N_Q_HEADS):
        cols = slice(h * BLOCK, (h + 1) * BLOCK)
        bias_t_scr[:, cols] = jnp.where(in_band, bias_t_scr[:, cols], NEG)


def _ssm_prep_kernel(are_ref, aim_ref, ls_ref, btr_ref, bti_ref,
                     lr_ref, li_ref, lcr_ref, lci_ref, bbr_ref, bbi_ref):
    ar = are_ref[...]
    ai = aim_ref[...]
    step = jnp.exp(ls_ref[...])
    zr = ar * step
    zi = ai * step
    mag = jnp.exp(zr)
    lr = mag * jnp.cos(zi)
    li = mag * jnp.sin(zi)
    pr, pi = lr, li
    for _ in range(int(math.log2(CHUNK))):
        pr, pi = pr * pr - pi * pi, 2.0 * pr * pi
    for g in range(N_SSM_GROUPS):
        st = slice(g * SSM_STATE, (g + 1) * SSM_STATE)
        lr_ref[:, st] = lr[g]
        li_ref[:, st] = li[g]
        lcr_ref[:, st] = pr[g]
        lci_ref[:, st] = pi[g]
    nr = lr - 1.0
    ni = li
    den = ar * ar + ai * ai
    cr = (nr * ar + ni * ai) / den
    ci = (ni * ar - nr * ai) / den
    btr = btr_ref[...]
    bti = bti_ref[...]
    bbr_ref[...] = cr * btr - ci * bti
    bbi_ref[...] = cr * bti + ci * btr


def _ssm_prep(a_re, a_im, log_step, b_re, b_im):
    g, p, c = N_SSM_GROUPS, SSM_STATE, SSM_GROUP
    vec = jax.ShapeDtypeStruct((1, STATE_LANES), F32)
    mat = jax.ShapeDtypeStruct((g, c, p), F32)
    return pl.pallas_call(
        _ssm_prep_kernel,
        out_shape=(vec, vec, vec, vec, mat, mat),
        name="ssm_prep",
    )(a_re.reshape(g, 1, p), a_im.reshape(g, 1, p), log_step.reshape(g, 1, 1),
      jnp.transpose(b_re, (0, 2, 1)), jnp.transpose(b_im, (0, 2, 1)))


def _stack_heads(q_blk, lane_lo):
    zero = jnp.zeros_like(q_blk[:, 0:LANES])
    lo, hi = [], []
    for c in range(ATTN_WIDTH // LANES):
        blk = q_blk[:, c * LANES:(c + 1) * LANES]
        lo.append(jnp.where(lane_lo, blk, zero))
        hi.append(jnp.where(lane_lo, zero, blk))
    return jnp.concatenate(lo + hi, axis=0)


def _unstack_heads(o, rows, lane_lo):
    nblk = ATTN_WIDTH // LANES
    return jnp.concatenate(
        [jnp.where(lane_lo, o[c * rows:(c + 1) * rows], o[(c + nblk) * rows:(c + nblk + 1) * rows])
         for c in range(nblk)], axis=-1)


Q_COLS = 256


def _proj_stages(x_ref, gmix_ref, win_ref, q_stage, k_stage, vt_stage, u_ref, newk_ref, newv_ref):
    xn = _rms(x_ref[...], gmix_ref[...]).astype(BF16)
    q_scale = HEAD_DIM ** -0.5 * LOG2E
    for c in range(ATTN_WIDTH // Q_COLS):
        cols = slice(c * Q_COLS, (c + 1) * Q_COLS)
        q = jnp.dot(xn, win_ref[:, cols], preferred_element_type=F32)
        q_stage[:, cols] = (q * q_scale).astype(BF16)
        yield
    kv = jnp.dot(xn, win_ref[:, ATTN_WIDTH:ATTN_WIDTH + 2 * KV_WIDTH], preferred_element_type=F32)
    k_stage[...] = kv[:, :KV_WIDTH].astype(BF16)
    vt_stage[...] = kv[:, KV_WIDTH:].T.astype(BF16)
    newk_ref[...] = kv[TM_A - WINDOW:, :KV_WIDTH]
    newv_ref[...] = kv[TM_A - WINDOW:, KV_WIDTH:]
    yield
    u0 = ATTN_WIDTH + 2 * KV_WIDTH
    for c in range(SSM_WIDTH // Q_COLS):
        cols = slice(u0 + c * Q_COLS, u0 + (c + 1) * Q_COLS)
        u_ref[:, c * Q_COLS:(c + 1) * Q_COLS] = jnp.dot(xn, win_ref[:, cols], preferred_element_type=F32)
        yield


def _attn_stages(sinks_ref, attn_ref, k_scr, vt_scr, q_scr, bias_t_scr, gt_scr, first_tile):
    nstk = N_Q_HEADS * BLOCK
    nblk = ATTN_WIDTH // LANES
    lane_lo = lax.broadcasted_iota(jnp.int32, (BLOCK, LANES), 1) < HEAD_DIM
    row_lo = lax.broadcasted_iota(jnp.int32, (KV_WIDTH, BLOCK), 0) < HEAD_DIM
    kj = lax.broadcasted_iota(jnp.int32, (2 * BLOCK, nstk), 0)
    has_prev = jnp.logical_or(kj >= BLOCK, jnp.logical_not(first_tile))
    head = lax.broadcasted_iota(jnp.int32, (1, nstk), 1) // BLOCK
    sink = jnp.zeros((1, nstk), F32)
    for h in range(N_Q_HEADS):
        sink = jnp.where(head == h, sinks_ref[0, h] * LOG2E, sink)

    scores = []
    for j in range(TM_A // BLOCK):
        r0 = j * BLOCK
        q2 = _stack_heads(q_scr[r0:r0 + BLOCK, :], lane_lo)
        keys = k_scr[r0:r0 + 2 * BLOCK, :]
        scores.append(lax.dot_general(keys, q2, (((1,), (1,)), ((), ())), preferred_element_type=F32))
        yield
    for j in range(TM_A // BLOCK):
        r0 = j * BLOCK
        s = scores[j] + bias_t_scr[...]
        if j == 0:
            s = jnp.where(has_prev, s, NEG)
        m = jnp.maximum(jnp.max(s, axis=0, keepdims=True), sink)
        p = jnp.exp2(s - m)
        denom = jnp.sum(p, axis=0, keepdims=True) + jnp.exp2(sink - m)
        o = jnp.dot(vt_scr[:, r0:r0 + 2 * BLOCK], p.astype(BF16), preferred_element_type=F32) / denom
        a = jnp.concatenate(
            [jnp.where(row_lo, o[:, c * BLOCK:(c + 1) * BLOCK], o[:, (c + nblk) * BLOCK:(c + nblk + 1) * BLOCK])
             for c in range(nblk)], axis=0)
        a = a * lax.rsqrt(jnp.mean(a * a, axis=0, keepdims=True) + EPS) * gt_scr[...]
        attn_ref[r0:r0 + BLOCK, :] = a.T.astype(BF16)
        yield


def _in_attn_kernel(relb_ref, sinks_ref, x_ref, gmix_ref, win_ref, bucket_t_ref, gattn_ref,
                    wout_f, wglu_f, wup_f, wdown_f,
                    attn_ref, u_ref, newk_ref, newv_ref, wout_b, wglu_b, wup_b, wdown_b,
                    k_scr, vt_scr, q_scr, k_stage, vt_stage, q_stage, bias_t_scr, gt_scr):
    b = pl.program_id(0)
    t = pl.program_id(1)

    @pl.when(jnp.logical_and(b == 0, t == 0))
    def _():
        _build_bias_t(relb_ref, bucket_t_ref, bias_t_scr)
        gt_scr[...] = jnp.broadcast_to(gattn_ref[...], (ATTN_WIDTH, BLOCK))
        k_scr[...] = jnp.zeros_like(k_scr)
        vt_scr[...] = jnp.zeros_like(vt_scr)
        k_stage[...] = jnp.zeros_like(k_stage)
        vt_stage[...] = jnp.zeros_like(vt_stage)
        q_stage[...] = jnp.zeros_like(q_stage)

    k_scr[0:BLOCK, :] = k_scr[TM_A:TM_A + BLOCK, :]
    vt_scr[:, 0:BLOCK] = vt_scr[:, TM_A:TM_A + BLOCK]
    k_scr[BLOCK:BLOCK + TM_A, :] = k_stage[...]
    vt_scr[:, BLOCK:BLOCK + TM_A] = vt_stage[...]
    q_scr[...] = q_stage[...]

    wout_b[...] = wout_f[...].astype(BF16)
    wglu_b[...] = wglu_f[...].astype(BF16)
    wup_b[...] = wup_f[...].astype(BF16)
    wdown_b[...] = wdown_f[...].astype(BF16)

    proj = _proj_stages(x_ref, gmix_ref, win_ref, q_stage, k_stage, vt_stage, u_ref, newk_ref, newv_ref)
    attn = _attn_stages(sinks_ref, attn_ref, k_scr, vt_scr, q_scr, bias_t_scr, gt_scr, t <= 1)
    for stage in "paapaapapaapa":
        next(proj if stage == "p" else attn)
    assert next(proj, None) is None and next(attn, None) is None


def _const_spec(shape):
    zeros = (0,) * len(shape)
    return pl.BlockSpec(shape, lambda *_: zeros, pipeline_mode=pl.Buffered(1))


def _smem_spec():
    return pl.BlockSpec(memory_space=pltpu.SMEM)


def _in_attn(x, rel_bias, sinks, g_mix, w_in, bucket_t, g_attn_col, w_out, w_glu, w_up, w_down):
    bsz, seq, _ = x.shape
    nt = seq // TM_A
    steps = bsz * nt
    cur = lambda width: pl.BlockSpec((None, TM_A, width), lambda b, t: (b, jnp.minimum(t, nt - 1), 0))
    prev = lambda width: pl.BlockSpec((None, TM_A, width), lambda b, t: (b, jnp.maximum(t - 1, 0), 0))
    win_spec = pl.BlockSpec((None, WINDOW, KV_WIDTH), lambda b, t: (b, 0, 0))

    def tile(b, t):
        return b * nt + jnp.minimum(t, nt - 1)

    def rows_of(w):
        blk = w.shape[0] // steps
        return pl.BlockSpec((blk, w.shape[1]), lambda b, t: (tile(b, t), 0))

    nhead_blk = D_MODEL // HEAD_DIM

    def wout_src(b, t):
        return (jnp.minimum(tile(b, t), nhead_blk - 1), 0)

    def wout_dst(b, t):
        j = jnp.minimum(tile(b, t), nhead_blk - 1)
        return (jnp.where(j < N_Q_HEADS, (j % Q_PER_KV) * N_KV_HEADS + j // Q_PER_KV, j), 0)

    assert steps >= nhead_blk and w_up.shape[0] % steps == 0 and w_down.shape[0] % steps == 0
    wout_in = pl.BlockSpec((HEAD_DIM, D_MODEL), wout_src)
    wout_out = pl.BlockSpec((HEAD_DIM, D_MODEL), wout_dst)
    cast_shape = lambda w: jax.ShapeDtypeStruct(w.shape, BF16)
    return pl.pallas_call(
        _in_attn_kernel,
        out_shape=(jax.ShapeDtypeStruct((bsz, seq, ATTN_WIDTH), BF16),
                   jax.ShapeDtypeStruct((bsz, seq, SSM_WIDTH), F32),
                   jax.ShapeDtypeStruct((bsz, WINDOW, KV_WIDTH), F32),
                   jax.ShapeDtypeStruct((bsz, WINDOW, KV_WIDTH), F32),
                   cast_shape(w_out), cast_shape(w_glu), cast_shape(w_up), cast_shape(w_down)),
        grid=(bsz, nt + 1),
        in_specs=[_smem_spec(), _smem_spec(), cur(D_MODEL), _const_spec((1, D_MODEL)),
                  _const_spec((D_MODEL, D_IN_PROJ)), _const_spec((2 * BLOCK, BLOCK)),
                  _const_spec((ATTN_WIDTH, 1)),
                  wout_in, rows_of(w_glu), rows_of(w_up), rows_of(w_down)],
        out_specs=(prev(ATTN_WIDTH), cur(SSM_WIDTH), win_spec, win_spec,
                   wout_out, rows_of(w_glu), rows_of(w_up), rows_of(w_down)),
        scratch_shapes=[pltpu.VMEM((TM_A + BLOCK, KV_WIDTH), BF16),
                        pltpu.VMEM((KV_WIDTH, TM_A + BLOCK), BF16),
                        pltpu.VMEM((TM_A, ATTN_WIDTH), BF16),
                        pltpu.VMEM((TM_A, KV_WIDTH), BF16),
                        pltpu.VMEM((KV_WIDTH, TM_A), BF16),
                        pltpu.VMEM((TM_A, ATTN_WIDTH), BF16),
                        pltpu.VMEM((2 * BLOCK, N_Q_HEADS * BLOCK), F32),
                        pltpu.VMEM((ATTN_WIDTH, BLOCK), F32)],
        compiler_params=pltpu.CompilerParams(
            dimension_semantics=("arbitrary", "arbitrary"), vmem_limit_bytes=VMEM_LIMIT),
        name="in_attn",
    )(rel_bias, sinks, x, g_mix, w_in, bucket_t, g_attn_col, w_out, w_glu, w_up, w_down)


def _s_in_attn_kernel(relb_ref, sinks_ref, x_ref, gmix_ref, win_ref, bucket_ref, gattn_ref,
                      ck_ref, cv_ref,
                      attn_ref, u_ref, newk_ref, newv_ref,
                      bias_scr, biasd_scr, proj_scr):
    rows = attn_ref.shape[0]
    steps = rows // S_SEQ
    pair = SUBLANES // steps
    prow = N_Q_HEADS * SUBLANES

    @pl.when(pl.program_id(0) == 0)
    def _():
        _build_bias(relb_ref, bucket_ref, bias_scr)
        for h in range(N_Q_HEADS):
            for k in range(pair):
                biasd_scr[h * SUBLANES + k * steps:h * SUBLANES + (k + 1) * steps, :] = bias_scr[h, 0:steps, :]
        xn = _rms(x_ref[...], gmix_ref[...]).astype(BF16)
        proj_all = jnp.dot(xn, win_ref[...], preferred_element_type=F32)
        proj_scr[...] = proj_all
        u_ref[...] = proj_all[:, ATTN_WIDTH + 2 * KV_WIDTH:]

    r0 = pl.multiple_of(pl.program_id(0) * rows, rows)
    proj = proj_scr[pl.ds(r0, rows), :]
    qs = proj[:, :ATTN_WIDTH] * (HEAD_DIM ** -0.5)

    nkeys = 2 * WINDOW
    ri = lax.broadcasted_iota(jnp.int32, (prow, nkeys), 0)
    kj = lax.broadcasted_iota(jnp.int32, (prow, nkeys), 1)
    dist = ri % steps + WINDOW - kj
    valid = jnp.logical_and(dist >= 0, dist < WINDOW)
    head = lax.broadcasted_iota(jnp.int32, (prow, 1), 0) // SUBLANES
    sink = jnp.zeros((prow, 1), F32)
    for h in range(N_Q_HEADS):
        sink = jnp.where(head == h, sinks_ref[0, h], sink)
    seq_in_tile = (lax.broadcasted_iota(jnp.int32, (prow, LANES), 0) % SUBLANES) // steps
    lane_lo = lax.broadcasted_iota(jnp.int32, (SUBLANES, LANES), 1) < HEAD_DIM
    row8 = lax.broadcasted_iota(jnp.int32, (SUBLANES, KV_WIDTH), 0)
    gattn = gattn_ref[...]
    tail = jnp.zeros((nkeys - WINDOW - SUBLANES, KV_WIDTH), F32)
    bias = biasd_scr[...]

    scores, values = [], []
    for tl in range(S_SEQ // pair):
        rws = slice(tl * SUBLANES, (tl + 1) * SUBLANES)
        q2 = _stack_heads(qs[rws], lane_lo).astype(BF16)
        kn8 = proj[rws, ATTN_WIDTH:ATTN_WIDTH + KV_WIDTH]
        vn8 = proj[rws, ATTN_WIDTH + KV_WIDTH:ATTN_WIDTH + 2 * KV_WIDTH]
        for k in range(pair):
            sq = tl * pair + k
            shift = (SUBLANES - k * steps) % SUBLANES
            kn = jnp.where(row8 < steps, pltpu.roll(kn8, shift, axis=0) if shift else kn8, 0.0)
            vn = jnp.where(row8 < steps, pltpu.roll(vn8, shift, axis=0) if shift else vn8, 0.0)
            ck = ck_ref[sq]
            cv = cv_ref[sq]
            newk_ref[sq, 0:WINDOW - steps, :] = ck[steps:, :]
            newk_ref[sq, WINDOW - steps:, :] = kn[0:steps]
            newv_ref[sq, 0:WINDOW - steps, :] = cv[steps:, :]
            newv_ref[sq, WINDOW - steps:, :] = vn[0:steps]
            keys = jnp.concatenate([ck, kn, tail], axis=0).astype(BF16)
            values.append(jnp.concatenate([cv, vn, tail], axis=0).astype(BF16))
            scores.append(lax.dot_general(q2, keys, (((1,), (1,)), ((), ())), preferred_element_type=F32))

    tiles = []
    for tl in range(S_SEQ // pair):
        o_tile = jnp.zeros((prow, LANES), F32)
        for k in range(pair):
            sq = tl * pair + k
            s = jnp.where(valid, scores[sq] + bias, NEG)
            m = jnp.maximum(jnp.max(s, axis=-1, keepdims=True), sink)
            p = jnp.exp(s - m)
            denom = jnp.sum(p, axis=-1, keepdims=True) + jnp.exp(sink - m)
            o = jnp.dot(p.astype(BF16), values[sq], preferred_element_type=F32) / denom
            o_tile = jnp.where(seq_in_tile == k, o, o_tile)
        tiles.append(_unstack_heads(o_tile, SUBLANES, lane_lo))
    attn_ref[...] = _rms(jnp.concatenate(tiles, axis=0), gattn).astype(BF16)


def _s_in_attn(x, rel_bias, sinks, g_mix, w_in, bucket, g_attn, cache_k, cache_v):
    nseq, steps, _ = x.shape
    rows = S_SEQ * steps
    xf = x.reshape(nseq * steps, D_MODEL)
    row = lambda width: pl.BlockSpec((rows, width), lambda i: (i, 0))
    cache = pl.BlockSpec((S_SEQ, WINDOW, KV_WIDTH), lambda i: (i, 0, 0))
    return pl.pallas_call(
        _s_in_attn_kernel,
        out_shape=(jax.ShapeDtypeStruct((nseq * steps, ATTN_WIDTH), BF16),
                   jax.ShapeDtypeStruct((nseq * steps, SSM_WIDTH), F32),
                   jax.ShapeDtypeStruct((nseq, WINDOW, KV_WIDTH), F32),
                   jax.ShapeDtypeStruct((nseq, WINDOW, KV_WIDTH), F32)),
        grid=(nseq // S_SEQ,),
        in_specs=[_smem_spec(), _smem_spec(), _const_spec((nseq * steps, D_MODEL)), _const_spec((1, D_MODEL)),
                  _const_spec((D_MODEL, D_IN_PROJ)), _const_spec((SUBLANES, 2 * BLOCK)),
                  _const_spec((1, ATTN_WIDTH)), cache, cache],
        out_specs=(row(ATTN_WIDTH), pl.BlockSpec((nseq * steps, SSM_WIDTH), lambda i: (0, 0)), cache, cache),
        scratch_shapes=[pltpu.VMEM((N_Q_HEADS, SUBLANES, 2 * BLOCK), F32),
                        pltpu.VMEM((N_Q_HEADS * SUBLANES, 2 * WINDOW), F32),
                        pltpu.VMEM((nseq * steps, D_IN_PROJ), F32)],
        compiler_params=pltpu.CompilerParams(
            dimension_semantics=("arbitrary",), vmem_limit_bytes=VMEM_LIMIT),
        name="s_in_attn",
    )(rel_bias, sinks, xf, g_mix, w_in, bucket[0:SUBLANES], g_attn, cache_k, cache_v)


def _embed_blocks(bbr_ref, bbi_ref, cre_in, cim_in, bre, bim, cre, cim):
    for ref in (bre, bim, cre, cim):
        ref[...] = jnp.zeros_like(ref)
    for g in range(N_SSM_GROUPS):
        ch = slice(g * SSM_GROUP, (g + 1) * SSM_GROUP)
        st = slice(g * SSM_STATE, (g + 1) * SSM_STATE)
        bre[ch, st] = bbr_ref[g].astype(BF16)
        bim[ch, st] = bbi_ref[g].astype(BF16)
        cre[ch, st] = cre_in[g].astype(BF16)
        cim[ch, st] = cim_in[g].astype(BF16)


def _output_map(h_slab_re, h_slab_im, cre_ref, cim_ref, ch, cols):
    nt_dims = (((1,), (1,)), ((), ()))
    yr = lax.dot_general(h_slab_re.astype(BF16), cre_ref[ch, cols], nt_dims, preferred_element_type=F32)
    yi = lax.dot_general(h_slab_im.astype(BF16), cim_ref[ch, cols], nt_dims, preferred_element_type=F32)
    return yr - yi


def _glu_norm(y, wglu_ref, bglu_ref, gssm_ref):
    g = jax.nn.gelu(y)
    gate = jax.nn.sigmoid(jnp.dot(g.astype(BF16), wglu_ref[...], preferred_element_type=F32) + bglu_ref[...])
    return _rms(g * gate, gssm_ref[...])


SLAB = 4 * LANES
SLAB_CH = SLAB // SSM_STATE * SSM_GROUP


def _scan_kernel(u_ref, lr_ref, li_ref, lcr_ref, lci_ref, bbr_ref, bbi_ref, cre_in, cim_in,
                 d_ref, wglu_ref, bglu_ref, gssm_ref,
                 out_ref, hre_out, him_out,
                 bre_ref, bim_ref, cre_ref, cim_ref,
                 pad_scr, up_scr, hre, him, carry_re, carry_im, e_re, e_im, hc_re, hc_im):
    t = pl.program_id(1)
    nslab = SSM_WIDTH // LANES

    @pl.when(jnp.logical_and(pl.program_id(0) == 0, t == 0))
    def _():
        _embed_blocks(bbr_ref, bbi_ref, cre_in, cim_in, bre_ref, bim_ref, cre_ref, cim_ref)

    @pl.when(t == 0)
    def _():
        carry_re[...] = jnp.zeros_like(carry_re)
        carry_im[...] = jnp.zeros_like(carry_im)

    for c in range(SUBLANES):
        for s in range(nslab):
            pad_scr[s, c * PITCH:c * PITCH + CHUNK, :] = u_ref[c * CHUNK:(c + 1) * CHUNK, s * LANES:(s + 1) * LANES]
    for tau in range(CHUNK):
        for s in range(nslab):
            up_scr[tau * SUBLANES:(tau + 1) * SUBLANES, s * LANES:(s + 1) * LANES] = (
                pad_scr[s, pl.ds(tau, SUBLANES, stride=PITCH), :])
    up = up_scr[...]
    upb = up.astype(BF16)

    nslabs = STATE_LANES // SLAB

    def drive(sl):
        cols = slice(sl * SLAB, (sl + 1) * SLAB)
        ch = slice(sl * SLAB_CH, (sl + 1) * SLAB_CH)
        hre[:, cols] = jnp.dot(upb[:, ch], bre_ref[ch, cols], preferred_element_type=F32)
        him[:, cols] = jnp.dot(upb[:, ch], bim_ref[ch, cols], preferred_element_type=F32)

    for sl in range(nslabs):
        drive(sl)
    ys = []
    for sl in range(nslabs):
        cols = slice(sl * SLAB, (sl + 1) * SLAB)
        ch = slice(sl * SLAB_CH, (sl + 1) * SLAB_CH)
        ar = jnp.broadcast_to(lr_ref[:, cols], (SUBLANES, SLAB))
        ai = jnp.broadcast_to(li_ref[:, cols], (SUBLANES, SLAB))

        hr = jnp.zeros((SUBLANES, SLAB), F32)
        hi = jnp.zeros((SUBLANES, SLAB), F32)
        for tau in range(CHUNK):
            rws = slice(tau * SUBLANES, (tau + 1) * SUBLANES)
            hr, hi = ar * hr - ai * hi + hre[rws, cols], ar * hi + ai * hr + him[rws, cols]
        e_re[:, cols] = hr
        e_im[:, cols] = hi

        cr = carry_re[:, cols]
        ci = carry_im[:, cols]
        lcr = lcr_ref[:, cols]
        lci = lci_ref[:, cols]
        for c in range(SUBLANES):
            hc_re[c:c + 1, cols] = cr
            hc_im[c:c + 1, cols] = ci
            er = e_re[c:c + 1, cols]
            ei = e_im[c:c + 1, cols]
            cr, ci = lcr * cr - lci * ci + er, lcr * ci + lci * cr + ei
        carry_re[:, cols] = cr
        carry_im[:, cols] = ci

        hr = hc_re[:, cols]
        hi = hc_im[:, cols]
        for tau in range(CHUNK):
            rws = slice(tau * SUBLANES, (tau + 1) * SUBLANES)
            hr, hi = ar * hr - ai * hi + hre[rws, cols], ar * hi + ai * hr + him[rws, cols]
            hre[rws, cols] = hr
            him[rws, cols] = hi

        ys.append(_output_map(hre[:, cols], him[:, cols], cre_ref, cim_ref, ch, cols))

    y = jnp.concatenate(ys, axis=-1) + d_ref[...] * up
    o = _glu_norm(y, wglu_ref, bglu_ref, gssm_ref)

    for tau in range(CHUNK):
        for s in range(nslab):
            pad_scr[s, pl.ds(tau, SUBLANES, stride=PITCH), :] = (
                o[tau * SUBLANES:(tau + 1) * SUBLANES, s * LANES:(s + 1) * LANES])
    for c in range(SUBLANES):
        for s in range(nslab):
            out_ref[c * CHUNK:(c + 1) * CHUNK, s * LANES:(s + 1) * LANES] = (
                pad_scr[s, c * PITCH:c * PITCH + CHUNK, :].astype(BF16))

    @pl.when(t == pl.num_programs(1) - 1)
    def _():
        hre_out[...] = carry_re[...]
        him_out[...] = carry_im[...]


def _block_operand_scratch():
    return [pltpu.VMEM((SSM_WIDTH, STATE_LANES), BF16) for _ in range(4)]


def _group_blocks():
    return _const_spec((N_SSM_GROUPS, SSM_GROUP, SSM_STATE))


def _scan(u, lam, lam_chunk, bbar, c_gcp, d, w_glu, b_glu, g_ssm):
    bsz, seq, _ = u.shape
    nt = seq // TM_B
    row = pl.BlockSpec((None, TM_B, SSM_WIDTH), lambda b, t: (b, t, 0))
    state = pl.BlockSpec((None, 1, STATE_LANES), lambda b, t: (b, 0, 0))
    vec = _const_spec((1, STATE_LANES))
    chan = _const_spec((1, SSM_WIDTH))
    return pl.pallas_call(
        _scan_kernel,
        out_shape=(jax.ShapeDtypeStruct((bsz, seq, SSM_WIDTH), BF16),
                   jax.ShapeDtypeStruct((bsz, 1, STATE_LANES), F32),
                   jax.ShapeDtypeStruct((bsz, 1, STATE_LANES), F32)),
        grid=(bsz, nt),
        in_specs=[row, vec, vec, vec, vec,
                  _group_blocks(), _group_blocks(), _group_blocks(), _group_blocks(),
                  chan, _const_spec((SSM_WIDTH, SSM_WIDTH)), chan, chan],
        out_specs=(row, state, state),
        scratch_shapes=_block_operand_scratch() + [
                        pltpu.VMEM((SSM_WIDTH // LANES, SUBLANES * PITCH, LANES), F32),
                        pltpu.VMEM((TM_B, SSM_WIDTH), F32),
                        pltpu.VMEM((TM_B, STATE_LANES), F32),
                        pltpu.VMEM((TM_B, STATE_LANES), F32),
                        pltpu.VMEM((1, STATE_LANES), F32),
                        pltpu.VMEM((1, STATE_LANES), F32),
                        pltpu.VMEM((SUBLANES, STATE_LANES), F32),
                        pltpu.VMEM((SUBLANES, STATE_LANES), F32),
                        pltpu.VMEM((SUBLANES, STATE_LANES), F32),
                        pltpu.VMEM((SUBLANES, STATE_LANES), F32)],
        compiler_params=pltpu.CompilerParams(
            dimension_semantics=("arbitrary", "arbitrary"), vmem_limit_bytes=VMEM_LIMIT),
        name="scan",
    )(u, lam[0], lam[1], lam_chunk[0], lam_chunk[1], bbar[0], bbar[1], c_gcp[0], c_gcp[1],
      d, w_glu, b_glu, g_ssm)


def _s_scan_kernel(u_ref, h0r_ref, h0i_ref, lr_ref, li_ref, bbr_ref, bbi_ref, cre_in, cim_in,
                   d_ref, wglu_ref, bglu_ref, gssm_ref,
                   out_ref, hre_out, him_out,
                   bre_ref, bim_ref, cre_ref, cim_ref,
                   pad_scr, up_scr, hre, him):
    nseq = h0r_ref.shape[0]
    steps = u_ref.shape[0] // nseq
    nslab = SSM_WIDTH // LANES
    _embed_blocks(bbr_ref, bbi_ref, cre_in, cim_in, bre_ref, bim_ref, cre_ref, cim_ref)

    for s in range(nslab):
        pad_scr[s] = u_ref[:, s * LANES:(s + 1) * LANES]
    for i in range(steps):
        for s in range(nslab):
            up_scr[i * nseq:(i + 1) * nseq, s * LANES:(s + 1) * LANES] = (
                pad_scr[s, pl.ds(i, nseq, stride=steps), :])
    up = up_scr[...]
    upb = up.astype(BF16)

    ys = []
    for sl in range(STATE_LANES // SLAB):
        cols = slice(sl * SLAB, (sl + 1) * SLAB)
        ch = slice(sl * SLAB_CH, (sl + 1) * SLAB_CH)
        hre[:, cols] = jnp.dot(upb[:, ch], bre_ref[ch, cols], preferred_element_type=F32)
        him[:, cols] = jnp.dot(upb[:, ch], bim_ref[ch, cols], preferred_element_type=F32)
        ar = lr_ref[:, cols]
        ai = li_ref[:, cols]
        hr = h0r_ref[:, cols]
        hi = h0i_ref[:, cols]
        for i in range(steps):
            rws = slice(i * nseq, (i + 1) * nseq)
            hr, hi = ar * hr - ai * hi + hre[rws, cols], ar * hi + ai * hr + him[rws, cols]
            hre[rws, cols] = hr
            him[rws, cols] = hi
        hre_out[:, cols] = hr
        him_out[:, cols] = hi
        ys.append(_output_map(hre[:, cols], him[:, cols], cre_ref, cim_ref, ch, cols))

    y = jnp.concatenate(ys, axis=-1) + d_ref[...] * up
    o = _glu_norm(y, wglu_ref, bglu_ref, gssm_ref)

    for i in range(steps):
        for s in range(nslab):
            pad_scr[s, pl.ds(i, nseq, stride=steps), :] = o[i * nseq:(i + 1) * nseq, s * LANES:(s + 1) * LANES]
    for s in range(nslab):
        out_ref[:, s * LANES:(s + 1) * LANES] = pad_scr[s].astype(BF16)


def _s_scan(u, h0_re, h0_im, lam, bbar, c_gcp, d, w_glu, b_glu, g_ssm):
    rows = u.shape[0]
    nseq = h0_re.shape[0]
    return pl.pallas_call(
        _s_scan_kernel,
        out_shape=(jax.ShapeDtypeStruct((rows, SSM_WIDTH), BF16),
                   jax.ShapeDtypeStruct((nseq, STATE_LANES), F32),
                   jax.ShapeDtypeStruct((nseq, STATE_LANES), F32)),
        scratch_shapes=_block_operand_scratch() + [
                        pltpu.VMEM((SSM_WIDTH // LANES, rows, LANES), F32),
                        pltpu.VMEM((rows, SSM_WIDTH), F32),
                        pltpu.VMEM((rows, STATE_LANES), F32),
                        pltpu.VMEM((rows, STATE_LANES), F32)],
        compiler_params=pltpu.CompilerParams(vmem_limit_bytes=VMEM_LIMIT),
        name="s_scan",
    )(u, h0_re, h0_im, lam[0], lam[1], bbar[0], bbar[1], c_gcp[0], c_gcp[1], d, w_glu, b_glu, g_ssm)


def _out_mlp_kernel(x_ref, a_ref, s_ref, wout_ref, gmlp_ref, wup_ref, wdown_ref, gfin_ref, y_ref):
    x1 = (x_ref[...]
          + jnp.dot(a_ref[...], wout_ref[0:ATTN_WIDTH, :], preferred_element_type=F32)
          + jnp.dot(s_ref[...], wout_ref[ATTN_WIDTH:, :], preferred_element_type=F32))
    hm = _rms(x1, gmlp_ref[...]).astype(BF16)
    y_ref[...] = x1
    ff_blk = 1024
    for c in range(D_FF // ff_blk):
        cols = slice(c * ff_blk, (c + 1) * ff_blk)
        up = jnp.dot(hm, wup_ref[:, cols], preferred_element_type=F32)
        act = jnp.square(jnp.maximum(up, 0.0)).astype(BF16)
        y_ref[...] += jnp.dot(act, wdown_ref[cols, :], preferred_element_type=F32)
    y_ref[...] = _rms(y_ref[...], gfin_ref[...])


def _out_mlp(x, attn_n, ssm_n, w_out, g_mlp, w_up, w_down, g_fin):
    rows = x.shape[0]
    tm = min(TM_C, rows)
    row = lambda width: pl.BlockSpec((tm, width), lambda i: (i, 0))
    return pl.pallas_call(
        _out_mlp_kernel,
        out_shape=jax.ShapeDtypeStruct((rows, D_MODEL), F32),
        grid=(rows // tm,),
        in_specs=[row(D_MODEL), row(ATTN_WIDTH), row(SSM_WIDTH),
                  _const_spec((D_MODEL, D_MODEL)), _const_spec((1, D_MODEL)),
                  _const_spec((D_MODEL, D_FF)), _const_spec((D_FF, D_MODEL)),
                  _const_spec((1, D_MODEL))],
        out_specs=row(D_MODEL),
        compiler_params=pltpu.CompilerParams(
            dimension_semantics=("arbitrary",), vmem_limit_bytes=VMEM_LIMIT),
        name="out_mlp",
    )(x, attn_n, ssm_n, w_out, g_mlp, w_up, w_down, g_fin)


def _pair_heads(w, axis):
    shape = w.shape
    split = shape[:axis] + (N_KV_HEADS, Q_PER_KV, HEAD_DIM) + shape[axis + 1:]
    return jnp.swapaxes(w.reshape(split), axis, axis + 1).reshape(shape)


def kernel(x_prompt, x_sample, cache_k, cache_v, state_ssm_re, state_ssm_im, rel_bias, norm_mix, w_in, attn_sinks, ssm_a_re, ssm_a_im, ssm_log_step, ssm_b_re, ssm_b_im, ssm_c_re, ssm_c_im, ssm_d, w_glu, b_glu, norm_attn_out, norm_ssm_out, w_out, norm_mlp, w_up, w_down, norm_final):
    depth = w_in.shape[0]
    bsz, seq, _ = x_prompt.shape
    nseq, steps, _ = x_sample.shape
    bucket = jnp.asarray(_rel_bucket_table())
    g_fin = norm_final.reshape(1, D_MODEL)
    assert depth == 1, "the final norm is fused into the (single) layer's MLP kernel"

    xp, xs = x_prompt, x_sample
    outs = [[] for _ in range(8)]
    for l in range(depth):
        lr, li, lcr, lci, bbr, bbi = _ssm_prep(ssm_a_re[l], ssm_a_im[l], ssm_log_step[l], ssm_b_re[l], ssm_b_im[l])
        lam = (lr, li)
        lam_chunk = (lcr, lci)
        bbar = (bbr, bbi)
        c_gcp = (ssm_c_re[l], ssm_c_im[l])
        g_mix = norm_mix[l].reshape(1, D_MODEL)
        g_attn = _pair_heads(norm_attn_out[l].reshape(1, ATTN_WIDTH), axis=1)
        g_ssm = norm_ssm_out[l].reshape(1, SSM_WIDTH)
        g_mlp = norm_mlp[l].reshape(1, D_MODEL)
        sinks = attn_sinks[l].reshape(1, N_Q_HEADS)
        d = ssm_d[l].reshape(1, SSM_WIDTH)
        bg = b_glu[l].reshape(1, SSM_WIDTH)
        w_in_b = jnp.concatenate([_pair_heads(w_in[l][:, :ATTN_WIDTH], axis=1), w_in[l][:, ATTN_WIDTH:]],
                                 axis=1).astype(BF16)

        attn_n, u, k1, v1, w_out_b, w_glu_b, w_up_b, w_down_b = _in_attn(
            xp, rel_bias, sinks, g_mix, w_in_b, bucket.T, g_attn.reshape(ATTN_WIDTH, 1),
            w_out[l], w_glu[l], w_up[l], w_down[l])
        ssm_n, r1, i1 = _scan(u, lam, lam_chunk, bbar, c_gcp, d, w_glu_b, bg, g_ssm)
        xp = _out_mlp(xp.reshape(bsz * seq, D_MODEL), attn_n.reshape(bsz * seq, ATTN_WIDTH),
                      ssm_n.reshape(bsz * seq, SSM_WIDTH), w_out_b, g_mlp, w_up_b, w_down_b, g_fin
                      ).reshape(bsz, seq, D_MODEL)

        ck = cache_k[l].reshape(nseq, WINDOW, KV_WIDTH)
        cv = cache_v[l].reshape(nseq, WINDOW, KV_WIDTH)
        s_attn, s_u, k2, v2 = _s_in_attn(xs, rel_bias, sinks, g_mix, w_in_b, bucket, g_attn, ck, cv)
        s_ssm, r2, i2 = _s_scan(s_u, state_ssm_re[l].reshape(nseq, STATE_LANES),
                                state_ssm_im[l].reshape(nseq, STATE_LANES),
                                lam, bbar, c_gcp, d, w_glu_b, bg, g_ssm)
        xs = _out_mlp(xs.reshape(nseq * steps, D_MODEL), s_attn, s_ssm, w_out_b, g_mlp, w_up_b, w_down_b,
                      g_fin).reshape(nseq, steps, D_MODEL)

        kv_shape = (WINDOW, N_KV_HEADS, HEAD_DIM)
        st_shape = (N_SSM_GROUPS, SSM_STATE)
        for lst, val in zip(outs, (k1.reshape(bsz, *kv_shape), v1.reshape(bsz, *kv_shape),
                                   r1.reshape(bsz, *st_shape), i1.reshape(bsz, *st_shape),
                                   k2.reshape(nseq, *kv_shape), v2.reshape(nseq, *kv_shape),
                                   r2.reshape(nseq, *st_shape), i2.reshape(nseq, *st_shape))):
            lst.append(val)

    return (xp, xs) + tuple(jnp.stack(o) for o in outs)
```

```python
import math

import numpy as np
import jax
import jax.numpy as jnp
from jax import lax
from jax.experimental import pallas as pl
from jax.experimental.pallas import tpu as pltpu

D_MODEL = 1024
HEAD_DIM = 64
ATTN_WIDTH = 512
N_Q_HEADS = 8
N_KV_HEADS = 2
Q_PER_KV = 4
KV_WIDTH = 128
WINDOW = 128
BLOCK = 128
N_BUCKETS = 32
MAX_DISTANCE = 128
SSM_WIDTH = 512
SSM_GROUP = 16
N_SSM_GROUPS = 32
SSM_STATE = 64
STATE_LANES = N_SSM_GROUPS * SSM_STATE
D_FF = 4096
D_IN_PROJ = 1280
EPS = 1e-6
NEG = -1e30
LOG2E = math.log2(math.e)

LANES = 128
SUBLANES = 8
MXU_DIM = 256
VMEM_LIMIT = 56 * 1024 * 1024

F32 = jnp.float32
BF16 = jnp.bfloat16

TM_A = 512
TM_C = 1024
TM_B = 1024
CHUNK = TM_B // SUBLANES
PITCH = CHUNK + SUBLANES
S_SEQ = 16
NEW_ROWS = 16


def _rms(x, g):
    return x * lax.rsqrt(jnp.mean(x * x, axis=-1, keepdims=True) + EPS) * g


def _rel_bucket_table():
    i = np.arange(BLOCK)[:, None]
    j = np.arange(2 * BLOCK)[None, :]
    n = np.maximum(i + BLOCK - j, 0)
    max_exact = N_BUCKETS // 2
    nf = np.maximum(n, max_exact).astype(np.float32)
    large = max_exact + (np.log(nf / np.float32(max_exact)) / np.float32(math.log(MAX_DISTANCE / max_exact))
                         * np.float32(N_BUCKETS - max_exact)).astype(np.int32)
    large = np.minimum(large, N_BUCKETS - 1)
    return np.where(n < max_exact, n, large).astype(np.int32)


def _build_bias(relb_ref, bucket_ref, bias_scr):
    bucket = bucket_ref[...]
    for h in range(N_Q_HEADS):
        bias_scr[h] = jnp.zeros(bucket.shape, F32)
    for b in range(N_BUCKETS):
        hit = bucket == b
        for h in range(N_Q_HEADS):
            bias_scr[h] = jnp.where(hit, relb_ref[b, h], bias_scr[h])


def _build_bias_t(relb_ref, bucket_t_ref, bias_t_scr):
    bucket_t = bucket_t_ref[...]
    bias_t_scr[...] = jnp.zeros_like(bias_t_scr)
    for b in range(N_BUCKETS):
        hit = bucket_t == b
        for h in range(N_Q_HEADS):
            cols = slice(h * BLOCK, (h + 1) * BLOCK)
            bias_t_scr[:, cols] = jnp.where(hit, relb_ref[b, h] * LOG2E, bias_t_scr[:, cols])
    kj = lax.broadcasted_iota(jnp.int32, (2 * BLOCK, BLOCK), 0)
    qi = lax.broadcasted_iota(jnp.int32, (2 * BLOCK, BLOCK), 1)
    dist = qi + BLOCK - kj
    in_band = jnp.logical_and(dist >= 0, dist < WINDOW)
    for h in range(N_Q_HEADS):
        cols = slice(h * BLOCK, (h + 1) * BLOCK)
        bias_t_scr[:, cols] = jnp.where(in_band, bias_t_scr[:, cols], NEG)


def _ssm_prep_kernel(are_ref, aim_ref, ls_ref, btr_ref, bti_ref,
                     lr_ref, li_ref, lcr_ref, lci_ref, bbr_ref, bbi_ref):
    ar = are_ref[...]
    ai = aim_ref[...]
    step = jnp.exp(ls_ref[...])
    zr = ar * step
    zi = ai * step
    mag = jnp.exp(zr)
    lr = mag * jnp.cos(zi)
    li = mag * jnp.sin(zi)
    pr, pi = lr, li
    for _ in range(int(math.log2(CHUNK))):
        pr, pi = pr * pr - pi * pi, 2.0 * pr * pi
    for g in range(N_SSM_GROUPS):
        st = slice(g * SSM_STATE, (g + 1) * SSM_STATE)
        lr_ref[:, st] = lr[g]
        li_ref[:, st] = li[g]
        lcr_ref[:, st] = pr[g]
        lci_ref[:, st] = pi[g]
    nr = lr - 1.0
    ni = li
    den = ar * ar + ai * ai
    cr = (nr * ar + ni * ai) / den
    ci = (ni * ar - nr * ai) / den
    btr = btr_ref[...]
    bti = bti_ref[...]
    bbr_ref[...] = cr * btr - ci * bti
    bbi_ref[...] = cr * bti + ci * btr


def _ssm_prep(a_re, a_im, log_step, b_re, b_im):
    g, p, c = N_SSM_GROUPS, SSM_STATE, SSM_GROUP
    vec = jax.ShapeDtypeStruct((1, STATE_LANES), F32)
    mat = jax.ShapeDtypeStruct((g, c, p), F32)
    return pl.pallas_call(
        _ssm_prep_kernel,
        out_shape=(vec, vec, vec, vec, mat, mat),
        name="ssm_prep",
    )(a_re.reshape(g, 1, p), a_im.reshape(g, 1, p), log_step.reshape(g, 1, 1),
      jnp.transpose(b_re, (0, 2, 1)), jnp.transpose(b_im, (0, 2, 1)))


def _stack_heads(q_blk, lane_lo):
    zero = jnp.zeros_like(q_blk[:, 0:LANES])
    lo, hi = [], []
    for c in range(ATTN_WIDTH // LANES):
        blk = q_blk[:, c * LANES:(c + 1) * LANES]
        lo.append(jnp.where(lane_lo, blk, zero))
        hi.append(jnp.where(lane_lo, zero, blk))
    return jnp.concatenate(lo + hi, axis=0)


def _unstack_heads(o, rows, lane_lo):
    nblk = ATTN_WIDTH // LANES
    return jnp.concatenate(
        [jnp.where(lane_lo, o[c * rows:(c + 1) * rows], o[(c + nblk) * rows:(c + nblk + 1) * rows])
         for c in range(nblk)], axis=-1)


Q_COLS = 256


def _proj_stages(x_ref, gmix_ref, win_ref, q_stage, k_stage, vt_stage, u_ref, newk_ref, newv_ref):
    xn = _rms(x_ref[...], gmix_ref[...]).astype(BF16)
    q_scale = HEAD_DIM ** -0.5 * LOG2E
    for c in range(ATTN_WIDTH // Q_COLS):
        cols = slice(c * Q_COLS, (c + 1) * Q_COLS)
        q = jnp.dot(xn, win_ref[:, cols], preferred_element_type=F32)
        q_stage[:, cols] = (q * q_scale).astype(BF16)
        yield
    kv = jnp.dot(xn, win_ref[:, ATTN_WIDTH:ATTN_WIDTH + 2 * KV_WIDTH], preferred_element_type=F32)
    k_stage[...] = kv[:, :KV_WIDTH].astype(BF16)
    vt_stage[...] = kv[:, KV_WIDTH:].T.astype(BF16)
    newk_ref[...] = kv[TM_A - WINDOW:, :KV_WIDTH]
    newv_ref[...] = kv[TM_A - WINDOW:, KV_WIDTH:]
    yield
    u0 = ATTN_WIDTH + 2 * KV_WIDTH
    for c in range(SSM_WIDTH // Q_COLS):
        cols = slice(u0 + c * Q_COLS, u0 + (c + 1) * Q_COLS)
        u_ref[:, c * Q_COLS:(c + 1) * Q_COLS] = jnp.dot(xn, win_ref[:, cols], preferred_element_type=F32)
        yield


def _attn_stages(sinks_ref, attn_ref, k_scr, vt_scr, q_scr, bias_t_scr, gt_scr, first_tile):
    nstk = N_Q_HEADS * BLOCK
    nblk = ATTN_WIDTH // LANES
    lane_lo = lax.broadcasted_iota(jnp.int32, (BLOCK, LANES), 1) < HEAD_DIM
    row_lo = lax.broadcasted_iota(jnp.int32, (KV_WIDTH, BLOCK), 0) < HEAD_DIM
    kj = lax.broadcasted_iota(jnp.int32, (2 * BLOCK, nstk), 0)
    has_prev = jnp.logical_or(kj >= BLOCK, jnp.logical_not(first_tile))
    head = lax.broadcasted_iota(jnp.int32, (1, nstk), 1) // BLOCK
    sink = jnp.zeros((1, nstk), F32)
    for h in range(N_Q_HEADS):
        sink = jnp.where(head == h, sinks_ref[0, h] * LOG2E, sink)

    scores = []
    for j in range(TM_A // BLOCK):
        r0 = j * BLOCK
        q2 = _stack_heads(q_scr[r0:r0 + BLOCK, :], lane_lo)
        keys = k_scr[r0:r0 + 2 * BLOCK, :]
        scores.append(lax.dot_general(keys, q2, (((1,), (1,)), ((), ())), preferred_element_type=F32))
        yield
    for j in range(TM_A // BLOCK):
        r0 = j * BLOCK
        s = scores[j] + bias_t_scr[...]
        if j == 0:
            s = jnp.where(has_prev, s, NEG)
        m = jnp.maximum(jnp.max(s, axis=0, keepdims=True), sink)
        p = jnp.exp2(s - m)
        denom = jnp.sum(p, axis=0, keepdims=True) + jnp.exp2(sink - m)
        o = jnp.dot(vt_scr[:, r0:r0 + 2 * BLOCK], p.astype(BF16), preferred_element_type=F32) / denom
        a = jnp.concatenate(
            [jnp.where(row_lo, o[:, c * BLOCK:(c + 1) * BLOCK], o[:, (c + nblk) * BLOCK:(c + nblk + 1) * BLOCK])
             for c in range(nblk)], axis=0)
        a = a * lax.rsqrt(jnp.mean(a * a, axis=0, keepdims=True) + EPS) * gt_scr[...]
        attn_ref[r0:r0 + BLOCK, :] = a.T.astype(BF16)
        yield


def _in_attn_kernel(relb_ref, sinks_ref, x_ref, gmix_ref, win_ref, bucket_t_ref, gattn_ref,
                    wout_f, wglu_f, wup_f, wdown_f,
                    attn_ref, u_ref, newk_ref, newv_ref, wout_b, wglu_b, wup_b, wdown_b,
                    k_scr, vt_scr, q_scr, k_stage, vt_stage, q_stage, bias_t_scr, gt_scr):
    b = pl.program_id(0)
    t = pl.program_id(1)

    @pl.when(jnp.logical_and(b == 0, t == 0))
    def _():
        _build_bias_t(relb_ref, bucket_t_ref, bias_t_scr)
        gt_scr[...] = jnp.broadcast_to(gattn_ref[...], (ATTN_WIDTH, BLOCK))
        k_scr[...] = jnp.zeros_like(k_scr)
        vt_scr[...] = jnp.zeros_like(vt_scr)
        k_stage[...] = jnp.zeros_like(k_stage)
        vt_stage[...] = jnp.zeros_like(vt_stage)
        q_stage[...] = jnp.zeros_like(q_stage)

    k_scr[0:BLOCK, :] = k_scr[TM_A:TM_A + BLOCK, :]
    vt_scr[:, 0:BLOCK] = vt_scr[:, TM_A:TM_A + BLOCK]
    k_scr[BLOCK:BLOCK + TM_A, :] = k_stage[...]
    vt_scr[:, BLOCK:BLOCK + TM_A] = vt_stage[...]
    q_scr[...] = q_stage[...]

    wout_b[...] = wout_f[...].astype(BF16)
    wglu_b[...] = wglu_f[...].astype(BF16)
    wup_b[...] = wup_f[...].astype(BF16)
    wdown_b[...] = wdown_f[...].astype(BF16)

    proj = _proj_stages(x_ref, gmix_ref, win_ref, q_stage, k_stage, vt_stage, u_ref, newk_ref, newv_ref)
    attn = _attn_stages(sinks_ref, attn_ref, k_scr, vt_scr, q_scr, bias_t_scr, gt_scr, t <= 1)
    nb = TM_A // BLOCK
    order = ("p" + "a" * (nb // 2)) * 2 + "p" + "a" * (nb // 4) + "p" + "a" * (nb // 2) + "p" + "a" * (nb - nb // 4 - nb // 2)
    for stage in order:
        next(proj if stage == "p" else attn)
    assert next(proj, None) is None and next(attn, None) is None


def _const_spec(shape):
    zeros = (0,) * len(shape)
    return pl.BlockSpec(shape, lambda *_: zeros, pipeline_mode=pl.Buffered(1))


def _smem_spec():
    return pl.BlockSpec(memory_space=pltpu.SMEM)


def _in_attn(x, rel_bias, sinks, g_mix, w_in, bucket_t, g_attn_col, w_out, w_glu, w_up, w_down):
    bsz, seq, _ = x.shape
    nt = seq // TM_A
    steps = bsz * nt
    cur = lambda width: pl.BlockSpec((None, TM_A, width), lambda b, t: (b, jnp.minimum(t, nt - 1), 0))
    prev = lambda width: pl.BlockSpec((None, TM_A, width), lambda b, t: (b, jnp.maximum(t - 1, 0), 0))
    win_spec = pl.BlockSpec((None, WINDOW, KV_WIDTH), lambda b, t: (b, 0, 0))

    def tile(b, t):
        return b * nt + jnp.minimum(t, nt - 1)

    def rows_of(w):
        blk = w.shape[0] // steps
        return pl.BlockSpec((blk, w.shape[1]), lambda b, t: (tile(b, t), 0))

    nhead_blk = D_MODEL // HEAD_DIM

    def wout_src(b, t):
        return (jnp.minimum(tile(b, t), nhead_blk - 1), 0)

    def wout_dst(b, t):
        j = jnp.minimum(tile(b, t), nhead_blk - 1)
        return (jnp.where(j < N_Q_HEADS, (j % Q_PER_KV) * N_KV_HEADS + j // Q_PER_KV, j), 0)

    assert steps >= nhead_blk and w_up.shape[0] % steps == 0 and w_down.shape[0] % steps == 0
    wout_in = pl.BlockSpec((HEAD_DIM, D_MODEL), wout_src)
    wout_out = pl.BlockSpec((HEAD_DIM, D_MODEL), wout_dst)
    cast_shape = lambda w: jax.ShapeDtypeStruct(w.shape, BF16)
    return pl.pallas_call(
        _in_attn_kernel,
        out_shape=(jax.ShapeDtypeStruct((bsz, seq, ATTN_WIDTH), BF16),
                   jax.ShapeDtypeStruct((bsz, seq, SSM_WIDTH), F32),
                   jax.ShapeDtypeStruct((bsz, WINDOW, KV_WIDTH), F32),
                   jax.ShapeDtypeStruct((bsz, WINDOW, KV_WIDTH), F32),
                   cast_shape(w_out), cast_shape(w_glu), cast_shape(w_up), cast_shape(w_down)),
        grid=(bsz, nt + 1),
        in_specs=[_smem_spec(), _smem_spec(), cur(D_MODEL), _const_spec((1, D_MODEL)),
                  _const_spec((D_MODEL, D_IN_PROJ)), _const_spec((2 * BLOCK, BLOCK)),
                  _const_spec((ATTN_WIDTH, 1)),
                  wout_in, rows_of(w_glu), rows_of(w_up), rows_of(w_down)],
        out_specs=(prev(ATTN_WIDTH), cur(SSM_WIDTH), win_spec, win_spec,
                   wout_out, rows_of(w_glu), rows_of(w_up), rows_of(w_down)),
        scratch_shapes=[pltpu.VMEM((TM_A + BLOCK, KV_WIDTH), BF16),
                        pltpu.VMEM((KV_WIDTH, TM_A + BLOCK), BF16),
                        pltpu.VMEM((TM_A, ATTN_WIDTH), BF16),
                        pltpu.VMEM((TM_A, KV_WIDTH), BF16),
                        pltpu.VMEM((KV_WIDTH, TM_A), BF16),
                        pltpu.VMEM((TM_A, ATTN_WIDTH), BF16),
                        pltpu.VMEM((2 * BLOCK, N_Q_HEADS * BLOCK), F32),
                        pltpu.VMEM((ATTN_WIDTH, BLOCK), F32)],
        compiler_params=pltpu.CompilerParams(
            dimension_semantics=("arbitrary", "arbitrary"), vmem_limit_bytes=VMEM_LIMIT),
        name="in_attn",
    )(rel_bias, sinks, x, g_mix, w_in, bucket_t, g_attn_col, w_out, w_glu, w_up, w_down)


def _s_in_attn_kernel(relb_ref, sinks_ref, x_ref, gmix_ref, win_ref, bucket_ref, gattn_ref,
                      ck_ref, cv_ref,
                      attn_ref, u_ref, newk_ref, newv_ref,
                      bias_scr, biasd_scr, proj_scr):
    rows = attn_ref.shape[0]
    steps = rows // S_SEQ
    pair = SUBLANES // steps
    prow = N_Q_HEADS * SUBLANES

    @pl.when(pl.program_id(0) == 0)
    def _():
        _build_bias(relb_ref, bucket_ref, bias_scr)
        for h in range(N_Q_HEADS):
            for k in range(pair):
                biasd_scr[h * SUBLANES + k * steps:h * SUBLANES + (k + 1) * steps, :] = bias_scr[h, 0:steps, :]
        xn = _rms(x_ref[...], gmix_ref[...]).astype(BF16)
        proj_all = jnp.dot(xn, win_ref[...], preferred_element_type=F32)
        proj_scr[...] = proj_all
        u_ref[...] = proj_all[:, ATTN_WIDTH + 2 * KV_WIDTH:]

    r0 = pl.multiple_of(pl.program_id(0) * rows, rows)
    proj = proj_scr[pl.ds(r0, rows), :]
    qs = proj[:, :ATTN_WIDTH] * (HEAD_DIM ** -0.5)

    qrow = lax.broadcasted_iota(jnp.int32, (prow, WINDOW), 0) % steps
    kcol = lax.broadcasted_iota(jnp.int32, (prow, WINDOW), 1)
    valid_c = kcol > qrow
    qrow_n = lax.broadcasted_iota(jnp.int32, (prow, NEW_ROWS), 0) % steps
    kcol_n = lax.broadcasted_iota(jnp.int32, (prow, NEW_ROWS), 1)
    valid_n = kcol_n <= qrow_n
    head = lax.broadcasted_iota(jnp.int32, (prow, 1), 0) // SUBLANES
    sink = jnp.zeros((prow, 1), F32)
    for h in range(N_Q_HEADS):
        sink = jnp.where(head == h, sinks_ref[0, h], sink)
    seq_in_tile = (lax.broadcasted_iota(jnp.int32, (prow, LANES), 0) % SUBLANES) // steps
    lane_lo = lax.broadcasted_iota(jnp.int32, (SUBLANES, LANES), 1) < HEAD_DIM
    row8 = lax.broadcasted_iota(jnp.int32, (SUBLANES, KV_WIDTH), 0)
    gattn = gattn_ref[...]
    pad = jnp.zeros((NEW_ROWS - SUBLANES, KV_WIDTH), F32)
    bias_c = biasd_scr[:, 0:WINDOW]
    bias_n = biasd_scr[:, WINDOW:WINDOW + NEW_ROWS]
    nt_dims = (((1,), (1,)), ((), ()))

    tiles = []
    for tl in range(S_SEQ // pair):
        rws = slice(tl * SUBLANES, (tl + 1) * SUBLANES)
        q2 = _stack_heads(qs[rws], lane_lo).astype(BF16)
        kn8 = proj[rws, ATTN_WIDTH:ATTN_WIDTH + KV_WIDTH]
        vn8 = proj[rws, ATTN_WIDTH + KV_WIDTH:ATTN_WIDTH + 2 * KV_WIDTH]
        o_tile = jnp.zeros((prow, LANES), F32)
        for k in range(pair):
            sq = tl * pair + k
            shift = (SUBLANES - k * steps) % SUBLANES
            kn = jnp.where(row8 < steps, pltpu.roll(kn8, shift, axis=0) if shift else kn8, 0.0)
            vn = jnp.where(row8 < steps, pltpu.roll(vn8, shift, axis=0) if shift else vn8, 0.0)
            ck = ck_ref[sq]
            cv = cv_ref[sq]
            newk_ref[sq, 0:WINDOW - steps, :] = ck[steps:, :]
            newk_ref[sq, WINDOW - steps:, :] = kn[0:steps]
            newv_ref[sq, 0:WINDOW - steps, :] = cv[steps:, :]
            newv_ref[sq, WINDOW - steps:, :] = vn[0:steps]
            kn_b = jnp.concatenate([kn, pad], axis=0).astype(BF16)
            vn_b = jnp.concatenate([vn, pad], axis=0).astype(BF16)
            s_c = lax.dot_general(q2, ck.astype(BF16), nt_dims, preferred_element_type=F32)
            s_n = lax.dot_general(q2, kn_b, nt_dims, preferred_element_type=F32)
            s_c = jnp.where(valid_c, s_c + bias_c, NEG)
            s_n = jnp.where(valid_n, s_n + bias_n, NEG)
            m = jnp.maximum(jnp.maximum(jnp.max(s_c, axis=-1, keepdims=True),
                                        jnp.max(s_n, axis=-1, keepdims=True)), sink)
            p_c = jnp.exp(s_c - m)
            p_n = jnp.exp(s_n - m)
            denom = (jnp.sum(p_c, axis=-1, keepdims=True) + jnp.sum(p_n, axis=-1, keepdims=True)
                     + jnp.exp(sink - m))
            o = (jnp.dot(p_c.astype(BF16), cv.astype(BF16), preferred_element_type=F32)
                 + jnp.dot(p_n.astype(BF16), vn_b, preferred_element_type=F32)) / denom
            o_tile = jnp.where(seq_in_tile == k, o, o_tile)
        tiles.append(_unstack_heads(o_tile, SUBLANES, lane_lo))
    attn_ref[...] = _rms(jnp.concatenate(tiles, axis=0), gattn).astype(BF16)


def _s_in_attn(x, rel_bias, sinks, g_mix, w_in, bucket, g_attn, cache_k, cache_v):
    nseq, steps, _ = x.shape
    rows = S_SEQ * steps
    xf = x.reshape(nseq * steps, D_MODEL)
    row = lambda width: pl.BlockSpec((rows, width), lambda i: (i, 0))
    cache = pl.BlockSpec((S_SEQ, WINDOW, KV_WIDTH), lambda i: (i, 0, 0))
    return pl.pallas_call(
        _s_in_attn_kernel,
        out_shape=(jax.ShapeDtypeStruct((nseq * steps, ATTN_WIDTH), BF16),
                   jax.ShapeDtypeStruct((nseq * steps, SSM_WIDTH), F32),
                   jax.ShapeDtypeStruct((nseq, WINDOW, KV_WIDTH), F32),
                   jax.ShapeDtypeStruct((nseq, WINDOW, KV_WIDTH), F32)),
        grid=(nseq // S_SEQ,),
        in_specs=[_smem_spec(), _smem_spec(), _const_spec((nseq * steps, D_MODEL)), _const_spec((1, D_MODEL)),
                  _const_spec((D_MODEL, D_IN_PROJ)), _const_spec((SUBLANES, 2 * BLOCK)),
                  _const_spec((1, ATTN_WIDTH)), cache, cache],
        out_specs=(row(ATTN_WIDTH), pl.BlockSpec((nseq * steps, SSM_WIDTH), lambda i: (0, 0)), cache, cache),
        scratch_shapes=[pltpu.VMEM((N_Q_HEADS, SUBLANES, 2 * BLOCK), F32),
                        pltpu.VMEM((N_Q_HEADS * SUBLANES, 2 * WINDOW), F32),
                        pltpu.VMEM((nseq * steps, D_IN_PROJ), F32)],
        compiler_params=pltpu.CompilerParams(
            dimension_semantics=("arbitrary",), vmem_limit_bytes=VMEM_LIMIT),
        name="s_in_attn",
    )(rel_bias, sinks, xf, g_mix, w_in, bucket[0:SUBLANES], g_attn, cache_k, cache_v)


def _embed_blocks(bbr_ref, bbi_ref, cre_in, cim_in, bre, bim, cre, cim):
    for ref in (bre, bim, cre, cim):
        ref[...] = jnp.zeros_like(ref)
    for g in range(N_SSM_GROUPS):
        ch = slice(g * SSM_GROUP, (g + 1) * SSM_GROUP)
        st = slice(g * SSM_STATE, (g + 1) * SSM_STATE)
        bre[ch, st] = bbr_ref[g].astype(BF16)
        bim[ch, st] = bbi_ref[g].astype(BF16)
        cre[ch, st] = cre_in[g].astype(BF16)
        cim[ch, st] = cim_in[g].astype(BF16)


def _output_map(h_slab_re, h_slab_im, cre_ref, cim_ref, ch, cols):
    nt_dims = (((1,), (1,)), ((), ()))
    yr = lax.dot_general(h_slab_re.astype(BF16), cre_ref[ch, cols], nt_dims, preferred_element_type=F32)
    yi = lax.dot_general(h_slab_im.astype(BF16), cim_ref[ch, cols], nt_dims, preferred_element_type=F32)
    return yr - yi


def _glu_norm(y, wglu_ref, bglu_ref, gssm_ref):
    g = jax.nn.gelu(y)
    gate = jax.nn.sigmoid(jnp.dot(g.astype(BF16), wglu_ref[...], preferred_element_type=F32) + bglu_ref[...])
    return _rms(g * gate, gssm_ref[...])


SLAB = 4 * LANES
SLAB_CH = SLAB // SSM_STATE * SSM_GROUP


def _scan_kernel(u_ref, lr_ref, li_ref, lcr_ref, lci_ref, bbr_ref, bbi_ref, cre_in, cim_in,
                 d_ref, wglu_ref, bglu_ref, gssm_ref,
                 out_ref, hre_out, him_out,
                 bre_ref, bim_ref, cre_ref, cim_ref,
                 pad_scr, up_scr, hre, him, carry_re, carry_im, e_re, e_im, hc_re, hc_im):
    t = pl.program_id(1)
    nslab = SSM_WIDTH // LANES

    @pl.when(jnp.logical_and(pl.program_id(0) == 0, t == 0))
    def _():
        _embed_blocks(bbr_ref, bbi_ref, cre_in, cim_in, bre_ref, bim_ref, cre_ref, cim_ref)

    @pl.when(t == 0)
    def _():
        carry_re[...] = jnp.zeros_like(carry_re)
        carry_im[...] = jnp.zeros_like(carry_im)

    for c in range(SUBLANES):
        for s in range(nslab):
            pad_scr[s, c * PITCH:c * PITCH + CHUNK, :] = u_ref[c * CHUNK:(c + 1) * CHUNK, s * LANES:(s + 1) * LANES]
    for tau in range(CHUNK):
        for s in range(nslab):
            up_scr[tau * SUBLANES:(tau + 1) * SUBLANES, s * LANES:(s + 1) * LANES] = (
                pad_scr[s, pl.ds(tau, SUBLANES, stride=PITCH), :])
    up = up_scr[...]
    upb = up.astype(BF16)

    nslabs = STATE_LANES // SLAB

    def drive(sl):
        cols = slice(sl * SLAB, (sl + 1) * SLAB)
        ch = slice(sl * SLAB_CH, (sl + 1) * SLAB_CH)
        hre[:, cols] = jnp.dot(upb[:, ch], bre_ref[ch, cols], preferred_element_type=F32)
        him[:, cols] = jnp.dot(upb[:, ch], bim_ref[ch, cols], preferred_element_type=F32)

    for sl in range(nslabs):
        drive(sl)
    ys = []
    for sl in range(nslabs):
        cols = slice(sl * SLAB, (sl + 1) * SLAB)
        ch = slice(sl * SLAB_CH, (sl + 1) * SLAB_CH)
        ar = jnp.broadcast_to(lr_ref[:, cols], (SUBLANES, SLAB))
        ai = jnp.broadcast_to(li_ref[:, cols], (SUBLANES, SLAB))

        hr = jnp.zeros((SUBLANES, SLAB), F32)
        hi = jnp.zeros((SUBLANES, SLAB), F32)
        for tau in range(CHUNK):
            rws = slice(tau * SUBLANES, (tau + 1) * SUBLANES)
            hr, hi = ar * hr - ai * hi + hre[rws, cols], ar * hi + ai * hr + him[rws, cols]
        e_re[:, cols] = hr
        e_im[:, cols] = hi

        cr = carry_re[:, cols]
        ci = carry_im[:, cols]
        lcr = lcr_ref[:, cols]
        lci = lci_ref[:, cols]
        for c in range(SUBLANES):
            hc_re[c:c + 1, cols] = cr
            hc_im[c:c + 1, cols] = ci
            er = e_re[c:c + 1, cols]
            ei = e_im[c:c + 1, cols]
            cr, ci = lcr * cr - lci * ci + er, lcr * ci + lci * cr + ei
        carry_re[:, cols] = cr
        carry_im[:, cols] = ci

        hr = hc_re[:, cols]
        hi = hc_im[:, cols]
        for tau in range(CHUNK):
            rws = slice(tau * SUBLANES, (tau + 1) * SUBLANES)
            hr, hi = ar * hr - ai * hi + hre[rws, cols], ar * hi + ai * hr + him[rws, cols]
            hre[rws, cols] = hr
            him[rws, cols] = hi

        ys.append(_output_map(hre[:, cols], him[:, cols], cre_ref, cim_ref, ch, cols))

    y = jnp.concatenate(ys, axis=-1) + d_ref[...] * up
    o = _glu_norm(y, wglu_ref, bglu_ref, gssm_ref)

    for tau in range(CHUNK):
        for s in range(nslab):
            pad_scr[s, pl.ds(tau, SUBLANES, stride=PITCH), :] = (
                o[tau * SUBLANES:(tau + 1) * SUBLANES, s * LANES:(s + 1) * LANES])
    for c in range(SUBLANES):
        for s in range(nslab):
            out_ref[c * CHUNK:(c + 1) * CHUNK, s * LANES:(s + 1) * LANES] = (
                pad_scr[s, c * PITCH:c * PITCH + CHUNK, :].astype(BF16))

    @pl.when(t == pl.num_programs(1) - 1)
    def _():
        hre_out[...] = carry_re[...]
        him_out[...] = carry_im[...]


def _block_operand_scratch():
    return [pltpu.VMEM((SSM_WIDTH, STATE_LANES), BF16) for _ in range(4)]


def _group_blocks():
    return _const_spec((N_SSM_GROUPS, SSM_GROUP, SSM_STATE))


def _scan(u, lam, lam_chunk, bbar, c_gcp, d, w_glu, b_glu, g_ssm):
    bsz, seq, _ = u.shape
    nt = seq // TM_B
    row = pl.BlockSpec((None, TM_B, SSM_WIDTH), lambda b, t: (b, t, 0))
    state = pl.BlockSpec((None, 1, STATE_LANES), lambda b, t: (b, 0, 0))
    vec = _const_spec((1, STATE_LANES))
    chan = _const_spec((1, SSM_WIDTH))
    return pl.pallas_call(
        _scan_kernel,
        out_shape=(jax.ShapeDtypeStruct((bsz, seq, SSM_WIDTH), BF16),
                   jax.ShapeDtypeStruct((bsz, 1, STATE_LANES), F32),
                   jax.ShapeDtypeStruct((bsz, 1, STATE_LANES), F32)),
        grid=(bsz, nt),
        in_specs=[row, vec, vec, vec, vec,
                  _group_blocks(), _group_blocks(), _group_blocks(), _group_blocks(),
                  chan, _const_spec((SSM_WIDTH, SSM_WIDTH)), chan, chan],
        out_specs=(row, state, state),
        scratch_shapes=_block_operand_scratch() + [
                        pltpu.VMEM((SSM_WIDTH // LANES, SUBLANES * PITCH, LANES), F32),
                        pltpu.VMEM((TM_B, SSM_WIDTH), F32),
                        pltpu.VMEM((TM_B, STATE_LANES), F32),
                        pltpu.VMEM((TM_B, STATE_LANES), F32),
                        pltpu.VMEM((1, STATE_LANES), F32),
                        pltpu.VMEM((1, STATE_LANES), F32),
                        pltpu.VMEM((SUBLANES, STATE_LANES), F32),
                        pltpu.VMEM((SUBLANES, STATE_LANES), F32),
                        pltpu.VMEM((SUBLANES, STATE_LANES), F32),
                        pltpu.VMEM((SUBLANES, STATE_LANES), F32)],
        compiler_params=pltpu.CompilerParams(
            dimension_semantics=("arbitrary", "arbitrary"), vmem_limit_bytes=VMEM_LIMIT),
        name="scan",
    )(u, lam[0], lam[1], lam_chunk[0], lam_chunk[1], bbar[0], bbar[1], c_gcp[0], c_gcp[1],
      d, w_glu, b_glu, g_ssm)


def _s_scan_kernel(u_ref, h0r_ref, h0i_ref, lr_ref, li_ref, bbr_ref, bbi_ref, cre_in, cim_in,
                   d_ref, wglu_ref, bglu_ref, gssm_ref,
                   out_ref, hre_out, him_out,
                   bre_ref, bim_ref, cre_ref, cim_ref,
                   pad_scr, up_scr, hre, him):
    nseq = h0r_ref.shape[0]
    steps = u_ref.shape[0] // nseq
    nslab = SSM_WIDTH // LANES
    _embed_blocks(bbr_ref, bbi_ref, cre_in, cim_in, bre_ref, bim_ref, cre_ref, cim_ref)

    for s in range(nslab):
        pad_scr[s] = u_ref[:, s * LANES:(s + 1) * LANES]
    for i in range(steps):
        for s in range(nslab):
            up_scr[i * nseq:(i + 1) * nseq, s * LANES:(s + 1) * LANES] = (
                pad_scr[s, pl.ds(i, nseq, stride=steps), :])
    up = up_scr[...]
    upb = up.astype(BF16)

    ys = []
    for sl in range(STATE_LANES // SLAB):
        cols = slice(sl * SLAB, (sl + 1) * SLAB)
        ch = slice(sl * SLAB_CH, (sl + 1) * SLAB_CH)
        hre[:, cols] = jnp.dot(upb[:, ch], bre_ref[ch, cols], preferred_element_type=F32)
        him[:, cols] = jnp.dot(upb[:, ch], bim_ref[ch, cols], preferred_element_type=F32)
        ar = lr_ref[:, cols]
        ai = li_ref[:, cols]
        hr = h0r_ref[:, cols]
        hi = h0i_ref[:, cols]
        for i in range(steps):
            rws = slice(i * nseq, (i + 1) * nseq)
            hr, hi = ar * hr - ai * hi + hre[rws, cols], ar * hi + ai * hr + him[rws, cols]
            hre[rws, cols] = hr
            him[rws, cols] = hi
        hre_out[:, cols] = hr
        him_out[:, cols] = hi
        ys.append(_output_map(hre[:, cols], him[:, cols], cre_ref, cim_ref, ch, cols))

    y = jnp.concatenate(ys, axis=-1) + d_ref[...] * up
    o = _glu_norm(y, wglu_ref, bglu_ref, gssm_ref)

    for i in range(steps):
        for s in range(nslab):
            pad_scr[s, pl.ds(i, nseq, stride=steps), :] = o[i * nseq:(i + 1) * nseq, s * LANES:(s + 1) * LANES]
    for s in range(nslab):
        out_ref[:, s * LANES:(s + 1) * LANES] = pad_scr[s].astype(BF16)


def _s_scan(u, h0_re, h0_im, lam, bbar, c_gcp, d, w_glu, b_glu, g_ssm):
    rows = u.shape[0]
    nseq = h0_re.shape[0]
    return pl.pallas_call(
        _s_scan_kernel,
        out_shape=(jax.ShapeDtypeStruct((rows, SSM_WIDTH), BF16),
                   jax.ShapeDtypeStruct((nseq, STATE_LANES), F32),
                   jax.ShapeDtypeStruct((nseq, STATE_LANES), F32)),
        scratch_shapes=_block_operand_scratch() + [
                        pltpu.VMEM((SSM_WIDTH // LANES, rows, LANES), F32),
                        pltpu.VMEM((rows, SSM_WIDTH), F32),
                        pltpu.VMEM((rows, STATE_LANES), F32),
                        pltpu.VMEM((rows, STATE_LANES), F32)],
        compiler_params=pltpu.CompilerParams(vmem_limit_bytes=VMEM_LIMIT),
        name="s_scan",
    )(u, h0_re, h0_im, lam[0], lam[1], bbar[0], bbar[1], c_gcp[0], c_gcp[1], d, w_glu, b_glu, g_ssm)


def _out_mlp_kernel(x_ref, a_ref, s_ref, wout_ref, gmlp_ref, wup_ref, wdown_ref, gfin_ref, y_ref):
    x1 = (x_ref[...]
          + jnp.dot(a_ref[...], wout_ref[0:ATTN_WIDTH, :], preferred_element_type=F32)
          + jnp.dot(s_ref[...], wout_ref[ATTN_WIDTH:, :], preferred_element_type=F32))
    hm = _rms(x1, gmlp_ref[...]).astype(BF16)
    y_ref[...] = x1
    ff_blk = 1024
    for c in range(D_FF // ff_blk):
        cols = slice(c * ff_blk, (c + 1) * ff_blk)
        up = jnp.dot(hm, wup_ref[:, cols], preferred_element_type=F32)
        act = jnp.square(jnp.maximum(up, 0.0)).astype(BF16)
        y_ref[...] += jnp.dot(act, wdown_ref[cols, :], preferred_element_type=F32)
    y_ref[...] = _rms(y_ref[...], gfin_ref[...])


def _out_mlp(x, attn_n, ssm_n, w_out, g_mlp, w_up, w_down, g_fin):
    rows = x.shape[0]
    tm = min(TM_C, rows)
    row = lambda width: pl.BlockSpec((tm, width), lambda i: (i, 0))
    return pl.pallas_call(
        _out_mlp_kernel,
        out_shape=jax.ShapeDtypeStruct((rows, D_MODEL), F32),
        grid=(rows // tm,),
        in_specs=[row(D_MODEL), row(ATTN_WIDTH), row(SSM_WIDTH),
                  _const_spec((D_MODEL, D_MODEL)), _const_spec((1, D_MODEL)),
                  _const_spec((D_MODEL, D_FF)), _const_spec((D_FF, D_MODEL)),
                  _const_spec((1, D_MODEL))],
        out_specs=row(D_MODEL),
        compiler_params=pltpu.CompilerParams(
            dimension_semantics=("arbitrary",), vmem_limit_bytes=VMEM_LIMIT),
        name="out_mlp",
    )(x, attn_n, ssm_n, w_out, g_mlp, w_up, w_down, g_fin)


def _pair_heads(w, axis):
    shape = w.shape
    split = shape[:axis] + (N_KV_HEADS, Q_PER_KV, HEAD_DIM) + shape[axis + 1:]
    return jnp.swapaxes(w.reshape(split), axis, axis + 1).reshape(shape)


def kernel(x_prompt, x_sample, cache_k, cache_v, state_ssm_re, state_ssm_im, rel_bias, norm_mix, w_in, attn_sinks, ssm_a_re, ssm_a_im, ssm_log_step, ssm_b_re, ssm_b_im, ssm_c_re, ssm_c_im, ssm_d, w_glu, b_glu, norm_attn_out, norm_ssm_out, w_out, norm_mlp, w_up, w_down, norm_final):
    depth = w_in.shape[0]
    bsz, seq, _ = x_prompt.shape
    nseq, steps, _ = x_sample.shape
    bucket = jnp.asarray(_rel_bucket_table())
    g_fin = norm_final.reshape(1, D_MODEL)
    assert depth == 1, "the final norm is fused into the (single) layer's MLP kernel"

    xp, xs = x_prompt, x_sample
    outs = [[] for _ in range(8)]
    for l in range(depth):
        lr, li, lcr, lci, bbr, bbi = _ssm_prep(ssm_a_re[l], ssm_a_im[l], ssm_log_step[l], ssm_b_re[l], ssm_b_im[l])
        lam = (lr, li)
        lam_chunk = (lcr, lci)
        bbar = (bbr, bbi)
        c_gcp = (ssm_c_re[l], ssm_c_im[l])
        g_mix = norm_mix[l].reshape(1, D_MODEL)
        g_attn = _pair_heads(norm_attn_out[l].reshape(1, ATTN_WIDTH), axis=1)
        g_ssm = norm_ssm_out[l].reshape(1, SSM_WIDTH)
        g_mlp = norm_mlp[l].reshape(1, D_MODEL)
        sinks = attn_sinks[l].reshape(1, N_Q_HEADS)
        d = ssm_d[l].reshape(1, SSM_WIDTH)
        bg = b_glu[l].reshape(1, SSM_WIDTH)
        w_in_b = jnp.concatenate([_pair_heads(w_in[l][:, :ATTN_WIDTH], axis=1), w_in[l][:, ATTN_WIDTH:]],
                                 axis=1).astype(BF16)

        attn_n, u, k1, v1, w_out_b, w_glu_b, w_up_b, w_down_b = _in_attn(
            xp, rel_bias, sinks, g_mix, w_in_b, bucket.T, g_attn.reshape(ATTN_WIDTH, 1),
            w_out[l], w_glu[l], w_up[l], w_down[l])
        ssm_n, r1, i1 = _scan(u, lam, lam_chunk, bbar, c_gcp, d, w_glu_b, bg, g_ssm)
        xp = _out_mlp(xp.reshape(bsz * seq, D_MODEL), attn_n.reshape(bsz * seq, ATTN_WIDTH),
                      ssm_n.reshape(bsz * seq, SSM_WIDTH), w_out_b, g_mlp, w_up_b, w_down_b, g_fin
                      ).reshape(bsz, seq, D_MODEL)

        ck = cache_k[l].reshape(nseq, WINDOW, KV_WIDTH)
        cv = cache_v[l].reshape(nseq, WINDOW, KV_WIDTH)
        s_attn, s_u, k2, v2 = _s_in_attn(xs, rel_bias, sinks, g_mix, w_in_b, bucket, g_attn, ck, cv)
        s_ssm, r2, i2 = _s_scan(s_u, state_ssm_re[l].reshape(nseq, STATE_LANES),
                                state_ssm_im[l].reshape(nseq, STATE_LANES),
                                lam, bbar, c_gcp, d, w_glu_b, bg, g_ssm)
        xs = _out_mlp(xs.reshape(nseq * steps, D_MODEL), s_attn, s_ssm, w_out_b, g_mlp, w_up_b, w_down_b,
                      g_fin).reshape(nseq, steps, D_MODEL)

        kv_shape = (WINDOW, N_KV_HEADS, HEAD_DIM)
        st_shape = (N_SSM_GROUPS, SSM_STATE)
        for lst, val in zip(outs, (k1.reshape(bsz, *kv_shape), v1.reshape(bsz, *kv_shape),
                                   r1.reshape(bsz, *st_shape), i1.reshape(bsz, *st_shape),
                                   k2.reshape(nseq, *kv_shape), v2.reshape(nseq, *kv_shape),
                                   r2.reshape(nseq, *st_shape), i2.reshape(nseq, *st_shape))):
            lst.append(val)

    return (xp, xs) + tuple(jnp.stack(o) for o in outs)
```

```python
import functools
import math

import numpy as np
import jax
import jax.numpy as jnp
from jax import lax
from jax.experimental import pallas as pl
from jax.experimental.pallas import tpu as pltpu

D_MODEL = 1024
HEAD_DIM = 64
ATTN_WIDTH = 512
N_Q_HEADS = 8
N_KV_HEADS = 2
Q_PER_KV = 4
KV_WIDTH = 128
WINDOW = 128
BLOCK = 128
N_BUCKETS = 32
MAX_DISTANCE = 128
SSM_WIDTH = 512
SSM_GROUP = 16
N_SSM_GROUPS = 32
SSM_STATE = 64
STATE_LANES = N_SSM_GROUPS * SSM_STATE
D_FF = 4096
D_IN_PROJ = 1280
EPS = 1e-6
NEG = -1e30
LOG2E = math.log2(math.e)

LANES = 128
SUBLANES = 8
MXU_DIM = 256
VMEM_LIMIT = 56 * 1024 * 1024

F32 = jnp.float32
BF16 = jnp.bfloat16

TM_A = 512
TM_C = 1024
TM_B = 512
CHUNK = TM_B // SUBLANES
PITCH = CHUNK + SUBLANES
S_SEQ = 16
NEW_ROWS = 16


def _rms(x, g):
    return x * lax.rsqrt(jnp.mean(x * x, axis=-1, keepdims=True) + EPS) * g


def _rel_bucket_table():
    i = np.arange(BLOCK)[:, None]
    j = np.arange(2 * BLOCK)[None, :]
    n = np.maximum(i + BLOCK - j, 0)
    max_exact = N_BUCKETS // 2
    nf = np.maximum(n, max_exact).astype(np.float32)
    large = max_exact + (np.log(nf / np.float32(max_exact)) / np.float32(math.log(MAX_DISTANCE / max_exact))
                         * np.float32(N_BUCKETS - max_exact)).astype(np.int32)
    large = np.minimum(large, N_BUCKETS - 1)
    return np.where(n < max_exact, n, large).astype(np.int32)


def _build_bias(relb_ref, bucket_ref, bias_scr):
    bucket = bucket_ref[...]
    for h in range(N_Q_HEADS):
        bias_scr[h] = jnp.zeros(bucket.shape, F32)
    for b in range(N_BUCKETS):
        hit = bucket == b
        for h in range(N_Q_HEADS):
            bias_scr[h] = jnp.where(hit, relb_ref[b, h], bias_scr[h])


def _build_bias_t(relb_ref, bucket_t_ref, bias_t_scr):
    bucket_t = bucket_t_ref[...]
    bias_t_scr[...] = jnp.zeros_like(bias_t_scr)
    for b in range(N_BUCKETS):
        hit = bucket_t == b
        for h in range(N_Q_HEADS):
            cols = slice(h * BLOCK, (h + 1) * BLOCK)
            bias_t_scr[:, cols] = jnp.where(hit, relb_ref[b, h] * LOG2E, bias_t_scr[:, cols])
    kj = lax.broadcasted_iota(jnp.int32, (2 * BLOCK, BLOCK), 0)
    qi = lax.broadcasted_iota(jnp.int32, (2 * BLOCK, BLOCK), 1)
    dist = qi + BLOCK - kj
    in_band = jnp.logical_and(dist >= 0, dist < WINDOW)
    for h in range(N_Q_HEADS):
        cols = slice(h * BLOCK, (h + 1) * BLOCK)
        bias_t_scr[:, cols] = jnp.where(in_band, bias_t_scr[:, cols], NEG)


def _ssm_prep_kernel(are_ref, aim_ref, ls_ref, btr_ref, bti_ref,
                     lr_ref, li_ref, lcr_ref, lci_ref, bbr_ref, bbi_ref):
    ar = are_ref[...]
    ai = aim_ref[...]
    step = jnp.exp(ls_ref[...])
    zr = ar * step
    zi = ai * step
    mag = jnp.exp(zr)
    lr = mag * jnp.cos(zi)
    li = mag * jnp.sin(zi)
    pr, pi = lr, li
    for _ in range(int(math.log2(CHUNK))):
        pr, pi = pr * pr - pi * pi, 2.0 * pr * pi
    for g in range(N_SSM_GROUPS):
        st = slice(g * SSM_STATE, (g + 1) * SSM_STATE)
        lr_ref[:, st] = lr[g]
        li_ref[:, st] = li[g]
        lcr_ref[:, st] = pr[g]
        lci_ref[:, st] = pi[g]
    nr = lr - 1.0
    ni = li
    den = ar * ar + ai * ai
    cr = (nr * ar + ni * ai) / den
    ci = (ni * ar - nr * ai) / den
    btr = btr_ref[...]
    bti = bti_ref[...]
    bbr_ref[...] = cr * btr - ci * bti
    bbi_ref[...] = cr * bti + ci * btr


def _ssm_prep(a_re, a_im, log_step, b_re, b_im):
    g, p, c = N_SSM_GROUPS, SSM_STATE, SSM_GROUP
    vec = jax.ShapeDtypeStruct((1, STATE_LANES), F32)
    mat = jax.ShapeDtypeStruct((g, c, p), F32)
    return pl.pallas_call(
        _ssm_prep_kernel,
        out_shape=(vec, vec, vec, vec, mat, mat),
        name="ssm_prep",
    )(a_re.reshape(g, 1, p), a_im.reshape(g, 1, p), log_step.reshape(g, 1, 1),
      jnp.transpose(b_re, (0, 2, 1)), jnp.transpose(b_im, (0, 2, 1)))


def _stack_heads(q_blk, lane_lo):
    zero = jnp.zeros_like(q_blk[:, 0:LANES])
    lo, hi = [], []
    for c in range(ATTN_WIDTH // LANES):
        blk = q_blk[:, c * LANES:(c + 1) * LANES]
        lo.append(jnp.where(lane_lo, blk, zero))
        hi.append(jnp.where(lane_lo, zero, blk))
    return jnp.concatenate(lo + hi, axis=0)


def _unstack_heads(o, rows, lane_lo):
    nblk = ATTN_WIDTH // LANES
    return jnp.concatenate(
        [jnp.where(lane_lo, o[c * rows:(c + 1) * rows], o[(c + nblk) * rows:(c + nblk + 1) * rows])
         for c in range(nblk)], axis=-1)


Q_COLS = 256


def _proj_stages(x_ref, gmix_ref, win_ref, q_stage, k_stage, vt_stage, u_ref, newk_ref, newv_ref):
    xn = _rms(x_ref[...], gmix_ref[...]).astype(BF16)
    q_scale = HEAD_DIM ** -0.5 * LOG2E
    for c in range(ATTN_WIDTH // Q_COLS):
        cols = slice(c * Q_COLS, (c + 1) * Q_COLS)
        q = jnp.dot(xn, win_ref[:, cols], preferred_element_type=F32)
        q_stage[:, cols] = (q * q_scale).astype(BF16)
        yield
    kv = jnp.dot(xn, win_ref[:, ATTN_WIDTH:ATTN_WIDTH + 2 * KV_WIDTH], preferred_element_type=F32)
    k_stage[...] = kv[:, :KV_WIDTH].astype(BF16)
    vt_stage[...] = kv[:, KV_WIDTH:].T.astype(BF16)
    newk_ref[...] = kv[TM_A - WINDOW:, :KV_WIDTH]
    newv_ref[...] = kv[TM_A - WINDOW:, KV_WIDTH:]
    yield
    u0 = ATTN_WIDTH + 2 * KV_WIDTH
    for c in range(SSM_WIDTH // Q_COLS):
        cols = slice(u0 + c * Q_COLS, u0 + (c + 1) * Q_COLS)
        u_ref[:, c * Q_COLS:(c + 1) * Q_COLS] = jnp.dot(xn, win_ref[:, cols], preferred_element_type=F32)
        yield


def _attn_stages(sinks_ref, attn_ref, k_scr, vt_scr, q_scr, bias_t_scr, gt_scr, first_tile):
    nstk = N_Q_HEADS * BLOCK
    nblk = ATTN_WIDTH // LANES
    lane_lo = lax.broadcasted_iota(jnp.int32, (BLOCK, LANES), 1) < HEAD_DIM
    row_lo = lax.broadcasted_iota(jnp.int32, (KV_WIDTH, BLOCK), 0) < HEAD_DIM
    kj = lax.broadcasted_iota(jnp.int32, (2 * BLOCK, nstk), 0)
    has_prev = jnp.logical_or(kj >= BLOCK, jnp.logical_not(first_tile))
    head = lax.broadcasted_iota(jnp.int32, (1, nstk), 1) // BLOCK
    sink = jnp.zeros((1, nstk), F32)
    for h in range(N_Q_HEADS):
        sink = jnp.where(head == h, sinks_ref[0, h] * LOG2E, sink)

    scores = []
    for j in range(TM_A // BLOCK):
        r0 = j * BLOCK
        q2 = _stack_heads(q_scr[r0:r0 + BLOCK, :], lane_lo)
        keys = k_scr[r0:r0 + 2 * BLOCK, :]
        scores.append(lax.dot_general(keys, q2, (((1,), (1,)), ((), ())), preferred_element_type=F32))
        yield
    for j in range(TM_A // BLOCK):
        r0 = j * BLOCK
        s = scores[j] + bias_t_scr[...]
        if j == 0:
            s = jnp.where(has_prev, s, NEG)
        m = jnp.maximum(jnp.max(s, axis=0, keepdims=True), sink)
        p = jnp.exp2(s - m)
        denom = jnp.sum(p, axis=0, keepdims=True) + jnp.exp2(sink - m)
        o = jnp.dot(vt_scr[:, r0:r0 + 2 * BLOCK], p.astype(BF16), preferred_element_type=F32) / denom
        a = jnp.concatenate(
            [jnp.where(row_lo, o[:, c * BLOCK:(c + 1) * BLOCK], o[:, (c + nblk) * BLOCK:(c + nblk + 1) * BLOCK])
             for c in range(nblk)], axis=0)
        a = a * lax.rsqrt(jnp.mean(a * a, axis=0, keepdims=True) + EPS) * gt_scr[...]
        attn_ref[r0:r0 + BLOCK, :] = a.T.astype(BF16)
        yield


def _in_attn_kernel(relb_ref, sinks_ref, x_ref, gmix_ref, win_ref, bucket_t_ref, gattn_ref,
                    wout_f, wglu_f, wup_f, wdown_f,
                    attn_ref, u_ref, newk_ref, newv_ref, wout_b, wglu_b, wup_b, wdown_b,
                    k_scr, vt_scr, q_scr, k_stage, vt_stage, q_stage, bias_t_scr, gt_scr):
    b = pl.program_id(0)
    t = pl.program_id(1)

    @pl.when(jnp.logical_and(b == 0, t == 0))
    def _():
        _build_bias_t(relb_ref, bucket_t_ref, bias_t_scr)
        gt_scr[...] = jnp.broadcast_to(gattn_ref[...], (ATTN_WIDTH, BLOCK))
        k_scr[...] = jnp.zeros_like(k_scr)
        vt_scr[...] = jnp.zeros_like(vt_scr)
        k_stage[...] = jnp.zeros_like(k_stage)
        vt_stage[...] = jnp.zeros_like(vt_stage)
        q_stage[...] = jnp.zeros_like(q_stage)

    k_scr[0:BLOCK, :] = k_scr[TM_A:TM_A + BLOCK, :]
    vt_scr[:, 0:BLOCK] = vt_scr[:, TM_A:TM_A + BLOCK]
    k_scr[BLOCK:BLOCK + TM_A, :] = k_stage[...]
    vt_scr[:, BLOCK:BLOCK + TM_A] = vt_stage[...]
    q_scr[...] = q_stage[...]

    wout_b[...] = wout_f[...].astype(BF16)
    wglu_b[...] = wglu_f[...].astype(BF16)
    wup_b[...] = wup_f[...].astype(BF16)
    wdown_b[...] = wdown_f[...].astype(BF16)

    proj = _proj_stages(x_ref, gmix_ref, win_ref, q_stage, k_stage, vt_stage, u_ref, newk_ref, newv_ref)
    attn = _attn_stages(sinks_ref, attn_ref, k_scr, vt_scr, q_scr, bias_t_scr, gt_scr, t <= 1)
    nb = TM_A // BLOCK
    order = ("p" + "a" * (nb // 2)) * 2 + "p" + "a" * (nb // 4) + "p" + "a" * (nb // 2) + "p" + "a" * (nb - nb // 4 - nb // 2)
    for stage in order:
        next(proj if stage == "p" else attn)
    assert next(proj, None) is None and next(attn, None) is None


def _const_spec(shape):
    zeros = (0,) * len(shape)
    return pl.BlockSpec(shape, lambda *_: zeros, pipeline_mode=pl.Buffered(1))


def _smem_spec():
    return pl.BlockSpec(memory_space=pltpu.SMEM)


def _in_attn(x, rel_bias, sinks, g_mix, w_in, bucket_t, g_attn_col, w_out, w_glu, w_up, w_down):
    bsz, seq, _ = x.shape
    nt = seq // TM_A
    steps = bsz * nt
    cur = lambda width: pl.BlockSpec((None, TM_A, width), lambda b, t: (b, jnp.minimum(t, nt - 1), 0))
    prev = lambda width: pl.BlockSpec((None, TM_A, width), lambda b, t: (b, jnp.maximum(t - 1, 0), 0))
    win_spec = pl.BlockSpec((None, WINDOW, KV_WIDTH), lambda b, t: (b, 0, 0))

    def tile(b, t):
        return b * nt + jnp.minimum(t, nt - 1)

    def rows_of(w):
        blk = w.shape[0] // steps
        return pl.BlockSpec((blk, w.shape[1]), lambda b, t: (tile(b, t), 0))

    nhead_blk = D_MODEL // HEAD_DIM

    def wout_src(b, t):
        return (jnp.minimum(tile(b, t), nhead_blk - 1), 0)

    def wout_dst(b, t):
        j = jnp.minimum(tile(b, t), nhead_blk - 1)
        return (jnp.where(j < N_Q_HEADS, (j % Q_PER_KV) * N_KV_HEADS + j // Q_PER_KV, j), 0)

    assert steps >= nhead_blk and w_up.shape[0] % steps == 0 and w_down.shape[0] % steps == 0
    wout_in = pl.BlockSpec((HEAD_DIM, D_MODEL), wout_src)
    wout_out = pl.BlockSpec((HEAD_DIM, D_MODEL), wout_dst)
    cast_shape = lambda w: jax.ShapeDtypeStruct(w.shape, BF16)
    return pl.pallas_call(
        _in_attn_kernel,
        out_shape=(jax.ShapeDtypeStruct((bsz, seq, ATTN_WIDTH), BF16),
                   jax.ShapeDtypeStruct((bsz, seq, SSM_WIDTH), F32),
                   jax.ShapeDtypeStruct((bsz, WINDOW, KV_WIDTH), F32),
                   jax.ShapeDtypeStruct((bsz, WINDOW, KV_WIDTH), F32),
                   cast_shape(w_out), cast_shape(w_glu), cast_shape(w_up), cast_shape(w_down)),
        grid=(bsz, nt + 1),
        in_specs=[_smem_spec(), _smem_spec(), cur(D_MODEL), _const_spec((1, D_MODEL)),
                  _const_spec((D_MODEL, D_IN_PROJ)), _const_spec((2 * BLOCK, BLOCK)),
                  _const_spec((ATTN_WIDTH, 1)),
                  wout_in, rows_of(w_glu), rows_of(w_up), rows_of(w_down)],
        out_specs=(prev(ATTN_WIDTH), cur(SSM_WIDTH), win_spec, win_spec,
                   wout_out, rows_of(w_glu), rows_of(w_up), rows_of(w_down)),
        scratch_shapes=[pltpu.VMEM((TM_A + BLOCK, KV_WIDTH), BF16),
                        pltpu.VMEM((KV_WIDTH, TM_A + BLOCK), BF16),
                        pltpu.VMEM((TM_A, ATTN_WIDTH), BF16),
                        pltpu.VMEM((TM_A, KV_WIDTH), BF16),
                        pltpu.VMEM((KV_WIDTH, TM_A), BF16),
                        pltpu.VMEM((TM_A, ATTN_WIDTH), BF16),
                        pltpu.VMEM((2 * BLOCK, N_Q_HEADS * BLOCK), F32),
                        pltpu.VMEM((ATTN_WIDTH, BLOCK), F32)],
        compiler_params=pltpu.CompilerParams(
            dimension_semantics=("arbitrary", "arbitrary"), vmem_limit_bytes=VMEM_LIMIT),
        name="in_attn",
    )(rel_bias, sinks, x, g_mix, w_in, bucket_t, g_attn_col, w_out, w_glu, w_up, w_down)


def _s_in_attn_kernel(relb_ref, sinks_ref, x_ref, gmix_ref, win_ref, bucket_ref, gattn_ref,
                      ck_ref, cv_ref,
                      attn_ref, u_ref, newk_ref, newv_ref,
                      bias_scr, biasd_scr, proj_scr):
    rows = attn_ref.shape[0]
    steps = rows // S_SEQ
    pair = SUBLANES // steps
    prow = N_Q_HEADS * SUBLANES

    @pl.when(pl.program_id(0) == 0)
    def _():
        _build_bias(relb_ref, bucket_ref, bias_scr)
        for h in range(N_Q_HEADS):
            for k in range(pair):
                biasd_scr[h * SUBLANES + k * steps:h * SUBLANES + (k + 1) * steps, :] = bias_scr[h, 0:steps, :]
        xn = _rms(x_ref[...], gmix_ref[...]).astype(BF16)
        proj_all = jnp.dot(xn, win_ref[...], preferred_element_type=F32)
        proj_scr[...] = proj_all
        u_ref[...] = proj_all[:, ATTN_WIDTH + 2 * KV_WIDTH:]

    r0 = pl.multiple_of(pl.program_id(0) * rows, rows)
    proj = proj_scr[pl.ds(r0, rows), :]
    qs = proj[:, :ATTN_WIDTH] * (HEAD_DIM ** -0.5)

    qrow = lax.broadcasted_iota(jnp.int32, (prow, WINDOW), 0) % steps
    kcol = lax.broadcasted_iota(jnp.int32, (prow, WINDOW), 1)
    valid_c = kcol > qrow
    qrow_n = lax.broadcasted_iota(jnp.int32, (prow, NEW_ROWS), 0) % steps
    kcol_n = lax.broadcasted_iota(jnp.int32, (prow, NEW_ROWS), 1)
    valid_n = kcol_n <= qrow_n
    head = lax.broadcasted_iota(jnp.int32, (prow, 1), 0) // SUBLANES
    sink = jnp.zeros((prow, 1), F32)
    for h in range(N_Q_HEADS):
        sink = jnp.where(head == h, sinks_ref[0, h], sink)
    seq_in_tile = (lax.broadcasted_iota(jnp.int32, (prow, LANES), 0) % SUBLANES) // steps
    lane_lo = lax.broadcasted_iota(jnp.int32, (SUBLANES, LANES), 1) < HEAD_DIM
    row8 = lax.broadcasted_iota(jnp.int32, (SUBLANES, KV_WIDTH), 0)
    gattn = gattn_ref[...]
    pad = jnp.zeros((NEW_ROWS - SUBLANES, KV_WIDTH), F32)
    bias_c = biasd_scr[:, 0:WINDOW]
    bias_n = biasd_scr[:, WINDOW:WINDOW + NEW_ROWS]
    nt_dims = (((1,), (1,)), ((), ()))

    tiles = []
    for tl in range(S_SEQ // pair):
        rws = slice(tl * SUBLANES, (tl + 1) * SUBLANES)
        q2 = _stack_heads(qs[rws], lane_lo).astype(BF16)
        kn8 = proj[rws, ATTN_WIDTH:ATTN_WIDTH + KV_WIDTH]
        vn8 = proj[rws, ATTN_WIDTH + KV_WIDTH:ATTN_WIDTH + 2 * KV_WIDTH]
        o_tile = jnp.zeros((prow, LANES), F32)
        for k in range(pair):
            sq = tl * pair + k
            shift = (SUBLANES - k * steps) % SUBLANES
            kn = jnp.where(row8 < steps, pltpu.roll(kn8, shift, axis=0) if shift else kn8, 0.0)
            vn = jnp.where(row8 < steps, pltpu.roll(vn8, shift, axis=0) if shift else vn8, 0.0)
            ck = ck_ref[sq]
            cv = cv_ref[sq]
            newk_ref[sq, 0:WINDOW - steps, :] = ck[steps:, :]
            newk_ref[sq, WINDOW - steps:, :] = kn[0:steps]
            newv_ref[sq, 0:WINDOW - steps, :] = cv[steps:, :]
            newv_ref[sq, WINDOW - steps:, :] = vn[0:steps]
            kn_b = jnp.concatenate([kn, pad], axis=0).astype(BF16)
            vn_b = jnp.concatenate([vn, pad], axis=0).astype(BF16)
            s_c = lax.dot_general(q2, ck.astype(BF16), nt_dims, preferred_element_type=F32)
            s_n = lax.dot_general(q2, kn_b, nt_dims, preferred_element_type=F32)
            s_c = jnp.where(valid_c, s_c + bias_c, NEG)
            s_n = jnp.where(valid_n, s_n + bias_n, NEG)
            m = jnp.maximum(jnp.maximum(jnp.max(s_c, axis=-1, keepdims=True),
                                        jnp.max(s_n, axis=-1, keepdims=True)), sink)
            p_c = jnp.exp(s_c - m)
            p_n = jnp.exp(s_n - m)
            denom = (jnp.sum(p_c, axis=-1, keepdims=True) + jnp.sum(p_n, axis=-1, keepdims=True)
                     + jnp.exp(sink - m))
            o = (jnp.dot(p_c.astype(BF16), cv.astype(BF16), preferred_element_type=F32)
                 + jnp.dot(p_n.astype(BF16), vn_b, preferred_element_type=F32)) / denom
            o_tile = jnp.where(seq_in_tile == k, o, o_tile)
        tiles.append(_unstack_heads(o_tile, SUBLANES, lane_lo))
    attn_ref[...] = _rms(jnp.concatenate(tiles, axis=0), gattn).astype(BF16)


def _s_in_attn(x, rel_bias, sinks, g_mix, w_in, bucket, g_attn, cache_k, cache_v):
    nseq, steps, _ = x.shape
    rows = S_SEQ * steps
    xf = x.reshape(nseq * steps, D_MODEL)
    row = lambda width: pl.BlockSpec((rows, width), lambda i: (i, 0))
    cache = pl.BlockSpec((S_SEQ, WINDOW, KV_WIDTH), lambda i: (i, 0, 0))
    return pl.pallas_call(
        _s_in_attn_kernel,
        out_shape=(jax.ShapeDtypeStruct((nseq * steps, ATTN_WIDTH), BF16),
                   jax.ShapeDtypeStruct((nseq * steps, SSM_WIDTH), F32),
                   jax.ShapeDtypeStruct((nseq, WINDOW, KV_WIDTH), F32),
                   jax.ShapeDtypeStruct((nseq, WINDOW, KV_WIDTH), F32)),
        grid=(nseq // S_SEQ,),
        in_specs=[_smem_spec(), _smem_spec(), _const_spec((nseq * steps, D_MODEL)), _const_spec((1, D_MODEL)),
                  _const_spec((D_MODEL, D_IN_PROJ)), _const_spec((SUBLANES, 2 * BLOCK)),
                  _const_spec((1, ATTN_WIDTH)), cache, cache],
        out_specs=(row(ATTN_WIDTH), pl.BlockSpec((nseq * steps, SSM_WIDTH), lambda i: (0, 0)), cache, cache),
        scratch_shapes=[pltpu.VMEM((N_Q_HEADS, SUBLANES, 2 * BLOCK), F32),
                        pltpu.VMEM((N_Q_HEADS * SUBLANES, 2 * WINDOW), F32),
                        pltpu.VMEM((nseq * steps, D_IN_PROJ), F32)],
        compiler_params=pltpu.CompilerParams(
            dimension_semantics=("arbitrary",), vmem_limit_bytes=VMEM_LIMIT),
        name="s_in_attn",
    )(rel_bias, sinks, xf, g_mix, w_in, bucket[0:SUBLANES], g_attn, cache_k, cache_v)


def _embed_blocks(bbr_ref, bbi_ref, cre_in, cim_in, bre, bim, cre, cim):
    for ref in (bre, bim, cre, cim):
        ref[...] = jnp.zeros_like(ref)
    for g in range(N_SSM_GROUPS):
        ch = slice(g * SSM_GROUP, (g + 1) * SSM_GROUP)
        st = slice(g * SSM_STATE, (g + 1) * SSM_STATE)
        bre[ch, st] = bbr_ref[g].astype(BF16)
        bim[ch, st] = bbi_ref[g].astype(BF16)
        cre[ch, st] = cre_in[g].astype(BF16)
        cim[ch, st] = cim_in[g].astype(BF16)


def _output_map(h_slab_re, h_slab_im, cre_ref, cim_ref, ch, cols):
    nt_dims = (((1,), (1,)), ((), ()))
    yr = lax.dot_general(h_slab_re.astype(BF16), cre_ref[ch, cols], nt_dims, preferred_element_type=F32)
    yi = lax.dot_general(h_slab_im.astype(BF16), cim_ref[ch, cols], nt_dims, preferred_element_type=F32)
    return yr - yi


def _glu_norm(y, wglu_ref, bglu_ref, gssm_ref):
    g = jax.nn.gelu(y)
    gate = jax.nn.sigmoid(jnp.dot(g.astype(BF16), wglu_ref[...], preferred_element_type=F32) + bglu_ref[...])
    return _rms(g * gate, gssm_ref[...])


SLAB = 4 * LANES
SLAB_CH = SLAB // SSM_STATE * SSM_GROUP


def _scan_kernel(u_ref, lr_ref, li_ref, lcr_ref, lci_ref, bbr_ref, bbi_ref, cre_in, cim_in,
                 d_ref, wglu_ref, bglu_ref, gssm_ref,
                 out_ref, hre_out, him_out,
                 bre_ref, bim_ref, cre_ref, cim_ref,
                 pad_scr, up_scr, hre, him, carry_re, carry_im, e_re, e_im, hc_re, hc_im):
    t = pl.program_id(1)
    nslab = SSM_WIDTH // LANES

    @pl.when(jnp.logical_and(pl.program_id(0) == 0, t == 0))
    def _():
        _embed_blocks(bbr_ref, bbi_ref, cre_in, cim_in, bre_ref, bim_ref, cre_ref, cim_ref)

    @pl.when(t == 0)
    def _():
        carry_re[...] = jnp.zeros_like(carry_re)
        carry_im[...] = jnp.zeros_like(carry_im)

    for c in range(SUBLANES):
        for s in range(nslab):
            pad_scr[s, c * PITCH:c * PITCH + CHUNK, :] = u_ref[c * CHUNK:(c + 1) * CHUNK, s * LANES:(s + 1) * LANES]
    for tau in range(CHUNK):
        for s in range(nslab):
            up_scr[tau * SUBLANES:(tau + 1) * SUBLANES, s * LANES:(s + 1) * LANES] = (
                pad_scr[s, pl.ds(tau, SUBLANES, stride=PITCH), :])
    up = up_scr[...]
    upb = up.astype(BF16)

    nslabs = STATE_LANES // SLAB

    def drive(sl):
        cols = slice(sl * SLAB, (sl + 1) * SLAB)
        ch = slice(sl * SLAB_CH, (sl + 1) * SLAB_CH)
        hre[:, cols] = jnp.dot(upb[:, ch], bre_ref[ch, cols], preferred_element_type=F32)
        him[:, cols] = jnp.dot(upb[:, ch], bim_ref[ch, cols], preferred_element_type=F32)

    for sl in range(nslabs):
        drive(sl)
    ys = []
    for sl in range(nslabs):
        cols = slice(sl * SLAB, (sl + 1) * SLAB)
        ch = slice(sl * SLAB_CH, (sl + 1) * SLAB_CH)
        ar = jnp.broadcast_to(lr_ref[:, cols], (SUBLANES, SLAB))
        ai = jnp.broadcast_to(li_ref[:, cols], (SUBLANES, SLAB))

        hr = jnp.zeros((SUBLANES, SLAB), F32)
        hi = jnp.zeros((SUBLANES, SLAB), F32)
        for tau in range(CHUNK):
            rws = slice(tau * SUBLANES, (tau + 1) * SUBLANES)
            hr, hi = ar * hr - ai * hi + hre[rws, cols], ar * hi + ai * hr + him[rws, cols]
        e_re[:, cols] = hr
        e_im[:, cols] = hi

        cr = carry_re[:, cols]
        ci = carry_im[:, cols]
        lcr = lcr_ref[:, cols]
        lci = lci_ref[:, cols]
        for c in range(SUBLANES):
            hc_re[c:c + 1, cols] = cr
            hc_im[c:c + 1, cols] = ci
            er = e_re[c:c + 1, cols]
            ei = e_im[c:c + 1, cols]
            cr, ci = lcr * cr - lci * ci + er, lcr * ci + lci * cr + ei
        carry_re[:, cols] = cr
        carry_im[:, cols] = ci

        hr = hc_re[:, cols]
        hi = hc_im[:, cols]
        for tau in range(CHUNK):
            rws = slice(tau * SUBLANES, (tau + 1) * SUBLANES)
            hr, hi = ar * hr - ai * hi + hre[rws, cols], ar * hi + ai * hr + him[rws, cols]
            hre[rws, cols] = hr
            him[rws, cols] = hi

        ys.append(_output_map(hre[:, cols], him[:, cols], cre_ref, cim_ref, ch, cols))

    y = jnp.concatenate(ys, axis=-1) + d_ref[...] * up
    o = _glu_norm(y, wglu_ref, bglu_ref, gssm_ref)

    for tau in range(CHUNK):
        for s in range(nslab):
            pad_scr[s, pl.ds(tau, SUBLANES, stride=PITCH), :] = (
                o[tau * SUBLANES:(tau + 1) * SUBLANES, s * LANES:(s + 1) * LANES])
    for c in range(SUBLANES):
        for s in range(nslab):
            out_ref[c * CHUNK:(c + 1) * CHUNK, s * LANES:(s + 1) * LANES] = (
                pad_scr[s, c * PITCH:c * PITCH + CHUNK, :].astype(BF16))

    @pl.when(t == pl.num_programs(1) - 1)
    def _():
        hre_out[...] = carry_re[...]
        him_out[...] = carry_im[...]


REC_PIECE = 16
EPI_ROWS = 128
FF_BLK = 1024


def _scan_stages(u_ref, lr_ref, li_ref, lcr_ref, lci_ref, d_ref, wglu_ref, bglu_ref, gssm_ref, out_ref,
                 bre_ref, bim_ref, cre_ref, cim_ref,
                 pad_scr, up_scr, y_scr, hre, him, carry_re, carry_im, e_re, e_im, hc_re, hc_im):
    nslab = SSM_WIDTH // LANES
    nslabs = STATE_LANES // SLAB

    for c in range(SUBLANES):
        for s in range(nslab):
            pad_scr[s, c * PITCH:c * PITCH + CHUNK, :] = u_ref[c * CHUNK:(c + 1) * CHUNK, s * LANES:(s + 1) * LANES]
    yield
    for tau in range(CHUNK):
        for s in range(nslab):
            up_scr[tau * SUBLANES:(tau + 1) * SUBLANES, s * LANES:(s + 1) * LANES] = (
                pad_scr[s, pl.ds(tau, SUBLANES, stride=PITCH), :])
        if tau % REC_PIECE == REC_PIECE - 1:
            yield

    for sl in range(nslabs):
        cols = slice(sl * SLAB, (sl + 1) * SLAB)
        ch = slice(sl * SLAB_CH, (sl + 1) * SLAB_CH)
        upb = up_scr[:, ch].astype(BF16)
        hre[:, cols] = jnp.dot(upb, bre_ref[ch, cols], preferred_element_type=F32)
        yield
        him[:, cols] = jnp.dot(upb, bim_ref[ch, cols], preferred_element_type=F32)
        yield

    for sl in range(nslabs):
        cols = slice(sl * SLAB, (sl + 1) * SLAB)
        ch = slice(sl * SLAB_CH, (sl + 1) * SLAB_CH)
        ar = jnp.broadcast_to(lr_ref[:, cols], (SUBLANES, SLAB))
        ai = jnp.broadcast_to(li_ref[:, cols], (SUBLANES, SLAB))

        hr = jnp.zeros((SUBLANES, SLAB), F32)
        hi = jnp.zeros((SUBLANES, SLAB), F32)
        for tau in range(CHUNK):
            rws = slice(tau * SUBLANES, (tau + 1) * SUBLANES)
            hr, hi = ar * hr - ai * hi + hre[rws, cols], ar * hi + ai * hr + him[rws, cols]
            if tau % REC_PIECE == REC_PIECE - 1:
                yield
        e_re[:, cols] = hr
        e_im[:, cols] = hi

        cr = carry_re[:, cols]
        ci = carry_im[:, cols]
        lcr = lcr_ref[:, cols]
        lci = lci_ref[:, cols]
        for c in range(SUBLANES):
            hc_re[c:c + 1, cols] = cr
            hc_im[c:c + 1, cols] = ci
            er = e_re[c:c + 1, cols]
            ei = e_im[c:c + 1, cols]
            cr, ci = lcr * cr - lci * ci + er, lcr * ci + lci * cr + ei
        carry_re[:, cols] = cr
        carry_im[:, cols] = ci
        yield

        hr = hc_re[:, cols]
        hi = hc_im[:, cols]
        for tau in range(CHUNK):
            rws = slice(tau * SUBLANES, (tau + 1) * SUBLANES)
            hr, hi = ar * hr - ai * hi + hre[rws, cols], ar * hi + ai * hr + him[rws, cols]
            hre[rws, cols] = hr
            him[rws, cols] = hi
            if tau % REC_PIECE == REC_PIECE - 1:
                yield

        y_scr[:, ch] = _output_map(hre[:, cols], him[:, cols], cre_ref, cim_ref, ch, cols)
        yield

    taus = EPI_ROWS // SUBLANES
    for r in range(TM_B // EPI_ROWS):
        rows = slice(r * EPI_ROWS, (r + 1) * EPI_ROWS)
        o = _glu_norm(y_scr[rows, :] + d_ref[...] * up_scr[rows, :], wglu_ref, bglu_ref, gssm_ref)
        for i in range(taus):
            for s in range(nslab):
                pad_scr[s, pl.ds(r * taus + i, SUBLANES, stride=PITCH), :] = (
                    o[i * SUBLANES:(i + 1) * SUBLANES, s * LANES:(s + 1) * LANES])
        yield
    for c in range(SUBLANES):
        for s in range(nslab):
            out_ref[c * CHUNK:(c + 1) * CHUNK, s * LANES:(s + 1) * LANES] = (
                pad_scr[s, c * PITCH:c * PITCH + CHUNK, :].astype(BF16))
    yield


N_SCAN_STAGES = (1 + CHUNK // REC_PIECE + 2 * (STATE_LANES // SLAB)
                 + (STATE_LANES // SLAB) * (2 * (CHUNK // REC_PIECE) + 2) + TM_B // EPI_ROWS + 1)


def _mlp_stages(x_ref, a_ref, s_ref, wout_ref, gmlp_ref, wup_ref, wdown_ref, gfin_ref, y_ref, hm_scr, act_scr):
    ncol = D_MODEL // Q_COLS
    for n in range(ncol):
        cols = slice(n * Q_COLS, (n + 1) * Q_COLS)
        y_ref[:, cols] = (x_ref[:, cols]
                          + jnp.dot(a_ref[...], wout_ref[0:ATTN_WIDTH, cols], preferred_element_type=F32)
                          + jnp.dot(s_ref[...], wout_ref[ATTN_WIDTH:, cols], preferred_element_type=F32))
        yield
    hm_scr[...] = _rms(y_ref[...], gmlp_ref[...]).astype(BF16)
    for c in range(D_FF // FF_BLK):
        act = act_scr.at[c % 2]
        for n in range(FF_BLK // Q_COLS):
            cols = slice(c * FF_BLK + n * Q_COLS, c * FF_BLK + (n + 1) * Q_COLS)
            up = jnp.dot(hm_scr[...], wup_ref[:, cols], preferred_element_type=F32)
            act[:, n * Q_COLS:(n + 1) * Q_COLS] = jnp.square(jnp.maximum(up, 0.0)).astype(BF16)
            yield
        for n in range(ncol):
            cols = slice(n * Q_COLS, (n + 1) * Q_COLS)
            y_ref[:, cols] += jnp.dot(act[...], wdown_ref[c * FF_BLK:(c + 1) * FF_BLK, cols],
                                      preferred_element_type=F32)
            yield
    y_ref[...] = _rms(y_ref[...], gfin_ref[...])
    yield


N_MLP_STAGES = D_MODEL // Q_COLS + (D_FF // FF_BLK) * (FF_BLK // Q_COLS + D_MODEL // Q_COLS) + 1


def _even_merge(n_a, n_b):
    keys = [((k + 0.5) / n_a, "a") for k in range(n_a)] + [((k + 0.5) / n_b, "b") for k in range(n_b)]
    return [name for _, name in sorted(keys)]


def _scan_mlp_kernel(x_ref, a_ref, u_ref, lr_ref, li_ref, lcr_ref, lci_ref, bbr_ref, bbi_ref, cre_in, cim_in,
                     d_ref, wglu_ref, bglu_ref, gssm_ref, wout_ref, gmlp_ref, wup_ref, wdown_ref, gfin_ref,
                     y_ref, hre_out, him_out,
                     ssm_scr, hm_scr, act_scr, bre_ref, bim_ref, cre_ref, cim_ref,
                     pad_scr, up_scr, y_scr, hre, him, carry_re, carry_im, e_re, e_im, hc_re, hc_im, *, nt):
    i = pl.program_id(0)

    @pl.when(i == 0)
    def _():
        _embed_blocks(bbr_ref, bbi_ref, cre_in, cim_in, bre_ref, bim_ref, cre_ref, cim_ref)
        ssm_scr[...] = jnp.zeros_like(ssm_scr)

    @pl.when(i % nt == 0)
    def _():
        carry_re[...] = jnp.zeros_like(carry_re)
        carry_im[...] = jnp.zeros_like(carry_im)

    mlp = _mlp_stages(x_ref, a_ref, ssm_scr, wout_ref, gmlp_ref, wup_ref, wdown_ref, gfin_ref, y_ref,
                      hm_scr, act_scr)
    scan = _scan_stages(u_ref, lr_ref, li_ref, lcr_ref, lci_ref, d_ref, wglu_ref, bglu_ref, gssm_ref, ssm_scr,
                        bre_ref, bim_ref, cre_ref, cim_ref,
                        pad_scr, up_scr, y_scr, hre, him, carry_re, carry_im, e_re, e_im, hc_re, hc_im)
    for stage in _even_merge(N_MLP_STAGES, N_SCAN_STAGES):
        next(mlp if stage == "a" else scan)
    assert next(scan, None) is None and next(mlp, None) is None

    @pl.when(i % nt == nt - 1)
    def _():
        hre_out[...] = carry_re[...]
        him_out[...] = carry_im[...]


def _scan_mlp(x, attn_n, u, lam, lam_chunk, bbar, c_gcp, d, w_glu, b_glu, g_ssm, w_out, g_mlp, w_up, w_down, g_fin):
    bsz, seq, _ = u.shape
    nt = seq // TM_B
    n = bsz * nt
    flat = lambda v: v.reshape(n * TM_B, v.shape[-1])
    cur = lambda width: pl.BlockSpec((TM_B, width), lambda i: (jnp.minimum(i, n - 1), 0))
    prev = lambda width: pl.BlockSpec((TM_B, width), lambda i: (jnp.maximum(i - 1, 0), 0))
    state = pl.BlockSpec((None, 1, STATE_LANES), lambda i: (jnp.minimum(i, n - 1) // nt, 0, 0))
    vec = _const_spec((1, STATE_LANES))
    chan = _const_spec((1, SSM_WIDTH))
    model = _const_spec((1, D_MODEL))
    y, h_re, h_im = pl.pallas_call(
        functools.partial(_scan_mlp_kernel, nt=nt),
        out_shape=(jax.ShapeDtypeStruct((n * TM_B, D_MODEL), F32),
                   jax.ShapeDtypeStruct((bsz, 1, STATE_LANES), F32),
                   jax.ShapeDtypeStruct((bsz, 1, STATE_LANES), F32)),
        grid=(n + 1,),
        in_specs=[prev(D_MODEL), prev(ATTN_WIDTH), cur(SSM_WIDTH), vec, vec, vec, vec,
                  _group_blocks(), _group_blocks(), _group_blocks(), _group_blocks(),
                  chan, _const_spec((SSM_WIDTH, SSM_WIDTH)), chan, chan,
                  _const_spec((D_MODEL, D_MODEL)), model, _const_spec((D_MODEL, D_FF)),
                  _const_spec((D_FF, D_MODEL)), model],
        out_specs=(prev(D_MODEL), state, state),
        scratch_shapes=[pltpu.VMEM((TM_B, SSM_WIDTH), BF16),
                        pltpu.VMEM((TM_B, D_MODEL), BF16),
                        pltpu.VMEM((2, TM_B, FF_BLK), BF16)] + _block_operand_scratch() + [
                        pltpu.VMEM((SSM_WIDTH // LANES, SUBLANES * PITCH, LANES), F32),
                        pltpu.VMEM((TM_B, SSM_WIDTH), F32),
                        pltpu.VMEM((TM_B, SSM_WIDTH), F32),
                        pltpu.VMEM((TM_B, STATE_LANES), F32),
                        pltpu.VMEM((TM_B, STATE_LANES), F32),
                        pltpu.VMEM((1, STATE_LANES), F32),
                        pltpu.VMEM((1, STATE_LANES), F32),
                        pltpu.VMEM((SUBLANES, STATE_LANES), F32),
                        pltpu.VMEM((SUBLANES, STATE_LANES), F32),
                        pltpu.VMEM((SUBLANES, STATE_LANES), F32),
                        pltpu.VMEM((SUBLANES, STATE_LANES), F32)],
        compiler_params=pltpu.CompilerParams(
            dimension_semantics=("arbitrary",), vmem_limit_bytes=VMEM_LIMIT),
        name="scan_mlp",
    )(flat(x), flat(attn_n), flat(u), lam[0], lam[1], lam_chunk[0], lam_chunk[1],
      bbar[0], bbar[1], c_gcp[0], c_gcp[1], d, w_glu, b_glu, g_ssm, w_out, g_mlp, w_up, w_down, g_fin)
    return y.reshape(bsz, seq, D_MODEL), h_re, h_im


def _block_operand_scratch():
    return [pltpu.VMEM((SSM_WIDTH, STATE_LANES), BF16) for _ in range(4)]


def _group_blocks():
    return _const_spec((N_SSM_GROUPS, SSM_GROUP, SSM_STATE))


def _scan(u, lam, lam_chunk, bbar, c_gcp, d, w_glu, b_glu, g_ssm):
    bsz, seq, _ = u.shape
    nt = seq // TM_B
    row = pl.BlockSpec((None, TM_B, SSM_WIDTH), lambda b, t: (b, t, 0))
    state = pl.BlockSpec((None, 1, STATE_LANES), lambda b, t: (b, 0, 0))
    vec = _const_spec((1, STATE_LANES))
    chan = _const_spec((1, SSM_WIDTH))
    return pl.pallas_call(
        _scan_kernel,
        out_shape=(jax.ShapeDtypeStruct((bsz, seq, SSM_WIDTH), BF16),
                   jax.ShapeDtypeStruct((bsz, 1, STATE_LANES), F32),
                   jax.ShapeDtypeStruct((bsz, 1, STATE_LANES), F32)),
        grid=(bsz, nt),
        in_specs=[row, vec, vec, vec, vec,
                  _group_blocks(), _group_blocks(), _group_blocks(), _group_blocks(),
                  chan, _const_spec((SSM_WIDTH, SSM_WIDTH)), chan, chan],
        out_specs=(row, state, state),
        scratch_shapes=_block_operand_scratch() + [
                        pltpu.VMEM((SSM_WIDTH // LANES, SUBLANES * PITCH, LANES), F32),
                        pltpu.VMEM((TM_B, SSM_WIDTH), F32),
                        pltpu.VMEM((TM_B, STATE_LANES), F32),
                        pltpu.VMEM((TM_B, STATE_LANES), F32),
                        pltpu.VMEM((1, STATE_LANES), F32),
                        pltpu.VMEM((1, STATE_LANES), F32),
                        pltpu.VMEM((SUBLANES, STATE_LANES), F32),
                        pltpu.VMEM((SUBLANES, STATE_LANES), F32),
                        pltpu.VMEM((SUBLANES, STATE_LANES), F32),
                        pltpu.VMEM((SUBLANES, STATE_LANES), F32)],
        compiler_params=pltpu.CompilerParams(
            dimension_semantics=("arbitrary", "arbitrary"), vmem_limit_bytes=VMEM_LIMIT),
        name="scan",
    )(u, lam[0], lam[1], lam_chunk[0], lam_chunk[1], bbar[0], bbar[1], c_gcp[0], c_gcp[1],
      d, w_glu, b_glu, g_ssm)


def _s_scan_kernel(u_ref, h0r_ref, h0i_ref, lr_ref, li_ref, bbr_ref, bbi_ref, cre_in, cim_in,
                   d_ref, wglu_ref, bglu_ref, gssm_ref,
                   out_ref, hre_out, him_out,
                   bre_ref, bim_ref, cre_ref, cim_ref,
                   pad_scr, up_scr, hre, him):
    nseq = h0r_ref.shape[0]
    steps = u_ref.shape[0] // nseq
    nslab = SSM_WIDTH // LANES
    _embed_blocks(bbr_ref, bbi_ref, cre_in, cim_in, bre_ref, bim_ref, cre_ref, cim_ref)

    for s in range(nslab):
        pad_scr[s] = u_ref[:, s * LANES:(s + 1) * LANES]
    for i in range(steps):
        for s in range(nslab):
            up_scr[i * nseq:(i + 1) * nseq, s * LANES:(s + 1) * LANES] = (
                pad_scr[s, pl.ds(i, nseq, stride=steps), :])
    up = up_scr[...]
    upb = up.astype(BF16)

    ys = []
    for sl in range(STATE_LANES // SLAB):
        cols = slice(sl * SLAB, (sl + 1) * SLAB)
        ch = slice(sl * SLAB_CH, (sl + 1) * SLAB_CH)
        hre[:, cols] = jnp.dot(upb[:, ch], bre_ref[ch, cols], preferred_element_type=F32)
        him[:, cols] = jnp.dot(upb[:, ch], bim_ref[ch, cols], preferred_element_type=F32)
        ar = lr_ref[:, cols]
        ai = li_ref[:, cols]
        hr = h0r_ref[:, cols]
        hi = h0i_ref[:, cols]
        for i in range(steps):
            rws = slice(i * nseq, (i + 1) * nseq)
            hr, hi = ar * hr - ai * hi + hre[rws, cols], ar * hi + ai * hr + him[rws, cols]
            hre[rws, cols] = hr
            him[rws, cols] = hi
        hre_out[:, cols] = hr
        him_out[:, cols] = hi
        ys.append(_output_map(hre[:, cols], him[:, cols], cre_ref, cim_ref, ch, cols))

    y = jnp.concatenate(ys, axis=-1) + d_ref[...] * up
    o = _glu_norm(y, wglu_ref, bglu_ref, gssm_ref)

    for i in range(steps):
        for s in range(nslab):
            pad_scr[s, pl.ds(i, nseq, stride=steps), :] = o[i * nseq:(i + 1) * nseq, s * LANES:(s + 1) * LANES]
    for s in range(nslab):
        out_ref[:, s * LANES:(s + 1) * LANES] = pad_scr[s].astype(BF16)


def _s_scan(u, h0_re, h0_im, lam, bbar, c_gcp, d, w_glu, b_glu, g_ssm):
    rows = u.shape[0]
    nseq = h0_re.shape[0]
    return pl.pallas_call(
        _s_scan_kernel,
        out_shape=(jax.ShapeDtypeStruct((rows, SSM_WIDTH), BF16),
                   jax.ShapeDtypeStruct((nseq, STATE_LANES), F32),
                   jax.ShapeDtypeStruct((nseq, STATE_LANES), F32)),
        scratch_shapes=_block_operand_scratch() + [
                        pltpu.VMEM((SSM_WIDTH // LANES, rows, LANES), F32),
                        pltpu.VMEM((rows, SSM_WIDTH), F32),
                        pltpu.VMEM((rows, STATE_LANES), F32),
                        pltpu.VMEM((rows, STATE_LANES), F32)],
        compiler_params=pltpu.CompilerParams(vmem_limit_bytes=VMEM_LIMIT),
        name="s_scan",
    )(u, h0_re, h0_im, lam[0], lam[1], bbar[0], bbar[1], c_gcp[0], c_gcp[1], d, w_glu, b_glu, g_ssm)


def _out_mlp_kernel(x_ref, a_ref, s_ref, wout_ref, gmlp_ref, wup_ref, wdown_ref, gfin_ref, y_ref):
    x1 = (x_ref[...]
          + jnp.dot(a_ref[...], wout_ref[0:ATTN_WIDTH, :], preferred_element_type=F32)
          + jnp.dot(s_ref[...], wout_ref[ATTN_WIDTH:, :], preferred_element_type=F32))
    hm = _rms(x1, gmlp_ref[...]).astype(BF16)
    y_ref[...] = x1
    ff_blk = 1024
    for c in range(D_FF // ff_blk):
        cols = slice(c * ff_blk, (c + 1) * ff_blk)
        up = jnp.dot(hm, wup_ref[:, cols], preferred_element_type=F32)
        act = jnp.square(jnp.maximum(up, 0.0)).astype(BF16)
        y_ref[...] += jnp.dot(act, wdown_ref[cols, :], preferred_element_type=F32)
    y_ref[...] = _rms(y_ref[...], gfin_ref[...])


def _out_mlp(x, attn_n, ssm_n, w_out, g_mlp, w_up, w_down, g_fin):
    rows = x.shape[0]
    tm = min(TM_C, rows)
    row = lambda width: pl.BlockSpec((tm, width), lambda i: (i, 0))
    return pl.pallas_call(
        _out_mlp_kernel,
        out_shape=jax.ShapeDtypeStruct((rows, D_MODEL), F32),
        grid=(rows // tm,),
        in_specs=[row(D_MODEL), row(ATTN_WIDTH), row(SSM_WIDTH),
                  _const_spec((D_MODEL, D_MODEL)), _const_spec((1, D_MODEL)),
                  _const_spec((D_MODEL, D_FF)), _const_spec((D_FF, D_MODEL)),
                  _const_spec((1, D_MODEL))],
        out_specs=row(D_MODEL),
        compiler_params=pltpu.CompilerParams(
            dimension_semantics=("arbitrary",), vmem_limit_bytes=VMEM_LIMIT),
        name="out_mlp",
    )(x, attn_n, ssm_n, w_out, g_mlp, w_up, w_down, g_fin)


def _pair_heads(w, axis):
    shape = w.shape
    split = shape[:axis] + (N_KV_HEADS, Q_PER_KV, HEAD_DIM) + shape[axis + 1:]
    return jnp.swapaxes(w.reshape(split), axis, axis + 1).reshape(shape)


def kernel(x_prompt, x_sample, cache_k, cache_v, state_ssm_re, state_ssm_im, rel_bias, norm_mix, w_in, attn_sinks, ssm_a_re, ssm_a_im, ssm_log_step, ssm_b_re, ssm_b_im, ssm_c_re, ssm_c_im, ssm_d, w_glu, b_glu, norm_attn_out, norm_ssm_out, w_out, norm_mlp, w_up, w_down, norm_final):
    depth = w_in.shape[0]
    bsz, seq, _ = x_prompt.shape
    nseq, steps, _ = x_sample.shape
    bucket = jnp.asarray(_rel_bucket_table())
    g_fin = norm_final.reshape(1, D_MODEL)
    assert depth == 1, "the final norm is fused into the (single) layer's MLP kernel"

    xp, xs = x_prompt, x_sample
    outs = [[] for _ in range(8)]
    for l in range(depth):
        lr, li, lcr, lci, bbr, bbi = _ssm_prep(ssm_a_re[l], ssm_a_im[l], ssm_log_step[l], ssm_b_re[l], ssm_b_im[l])
        lam = (lr, li)
        lam_chunk = (lcr, lci)
        bbar = (bbr, bbi)
        c_gcp = (ssm_c_re[l], ssm_c_im[l])
        g_mix = norm_mix[l].reshape(1, D_MODEL)
        g_attn = _pair_heads(norm_attn_out[l].reshape(1, ATTN_WIDTH), axis=1)
        g_ssm = norm_ssm_out[l].reshape(1, SSM_WIDTH)
        g_mlp = norm_mlp[l].reshape(1, D_MODEL)
        sinks = attn_sinks[l].reshape(1, N_Q_HEADS)
        d = ssm_d[l].reshape(1, SSM_WIDTH)
        bg = b_glu[l].reshape(1, SSM_WIDTH)
        w_in_b = jnp.concatenate([_pair_heads(w_in[l][:, :ATTN_WIDTH], axis=1), w_in[l][:, ATTN_WIDTH:]],
                                 axis=1).astype(BF16)

        attn_n, u, k1, v1, w_out_b, w_glu_b, w_up_b, w_down_b = _in_attn(
            xp, rel_bias, sinks, g_mix, w_in_b, bucket.T, g_attn.reshape(ATTN_WIDTH, 1),
            w_out[l], w_glu[l], w_up[l], w_down[l])
        xp, r1, i1 = _scan_mlp(xp, attn_n, u, lam, lam_chunk, bbar, c_gcp, d, w_glu_b, bg, g_ssm,
                               w_out_b, g_mlp, w_up_b, w_down_b, g_fin)

        ck = cache_k[l].reshape(nseq, WINDOW, KV_WIDTH)
        cv = cache_v[l].reshape(nseq, WINDOW, KV_WIDTH)
        s_attn, s_u, k2, v2 = _s_in_attn(xs, rel_bias, sinks, g_mix, w_in_b, bucket, g_attn, ck, cv)
        s_ssm, r2, i2 = _s_scan(s_u, state_ssm_re[l].reshape(nseq, STATE_LANES),
                                state_ssm_im[l].reshape(nseq, STATE_LANES),
                                lam, bbar, c_gcp, d, w_glu_b, bg, g_ssm)
        xs = _out_mlp(xs.reshape(nseq * steps, D_MODEL), s_attn, s_ssm, w_out_b, g_mlp, w_up_b, w_down_b,
                      g_fin).reshape(nseq, steps, D_MODEL)

        kv_shape = (WINDOW, N_KV_HEADS, HEAD_DIM)
        st_shape = (N_SSM_GROUPS, SSM_STATE)
        for lst, val in zip(outs, (k1.reshape(bsz, *kv_shape), v1.reshape(bsz, *kv_shape),
                                   r1.reshape(bsz, *st_shape), i1.reshape(bsz, *st_shape),
                                   k2.reshape(nseq, *kv_shape), v2.reshape(nseq, *kv_shape),
                                   r2.reshape(nseq, *st_shape), i2.reshape(nseq, *st_shape))):
            lst.append(val)

    return (xp, xs) + tuple(jnp.stack(o) for o in outs)
```

```python
import functools
import math

import numpy as np
import jax
import jax.numpy as jnp
from jax import lax
from jax.experimental import pallas as pl
from jax.experimental.pallas import tpu as pltpu

D_MODEL = 1024
HEAD_DIM = 64
ATTN_WIDTH = 512
N_Q_HEADS = 8
N_KV_HEADS = 2
Q_PER_KV = 4
KV_WIDTH = 128
WINDOW = 128
BLOCK = 128
N_BUCKETS = 32
MAX_DISTANCE = 128
SSM_WIDTH = 512
SSM_GROUP = 16
N_SSM_GROUPS = 32
SSM_STATE = 64
STATE_LANES = N_SSM_GROUPS * SSM_STATE
D_FF = 4096
D_IN_PROJ = 1280
EPS = 1e-6
NEG = -1e30
LOG2E = math.log2(math.e)

LANES = 128
SUBLANES = 8
MXU_DIM = 256
VMEM_LIMIT = 56 * 1024 * 1024

F32 = jnp.float32
BF16 = jnp.bfloat16

TM_A = 512
TM_B = 512
CHUNK = TM_B // SUBLANES
PITCH = CHUNK + SUBLANES
S_SEQ = 16
NEW_ROWS = 16


def _rms(x, g):
    return x * lax.rsqrt(jnp.mean(x * x, axis=-1, keepdims=True) + EPS) * g


def _rel_bucket_table():
    i = np.arange(BLOCK)[:, None]
    j = np.arange(2 * BLOCK)[None, :]
    n = np.maximum(i + BLOCK - j, 0)
    max_exact = N_BUCKETS // 2
    nf = np.maximum(n, max_exact).astype(np.float32)
    large = max_exact + (np.log(nf / np.float32(max_exact)) / np.float32(math.log(MAX_DISTANCE / max_exact))
                         * np.float32(N_BUCKETS - max_exact)).astype(np.int32)
    large = np.minimum(large, N_BUCKETS - 1)
    return np.where(n < max_exact, n, large).astype(np.int32)


def _build_bias(relb_ref, bucket_ref, bias_scr):
    bucket = bucket_ref[...]
    for h in range(N_Q_HEADS):
        bias_scr[h] = jnp.zeros(bucket.shape, F32)
    for b in range(N_BUCKETS):
        hit = bucket == b
        for h in range(N_Q_HEADS):
            bias_scr[h] = jnp.where(hit, relb_ref[b, h], bias_scr[h])


def _build_bias_t(relb_ref, bucket_t_ref, bias_t_scr):
    bucket_t = bucket_t_ref[...]
    bias_t_scr[...] = jnp.zeros_like(bias_t_scr)
    for b in range(N_BUCKETS):
        hit = bucket_t == b
        for h in range(N_Q_HEADS):
            cols = slice(h * BLOCK, (h + 1) * BLOCK)
            bias_t_scr[:, cols] = jnp.where(hit, relb_ref[b, h] * LOG2E, bias_t_scr[:, cols])
    kj = lax.broadcasted_iota(jnp.int32, (2 * BLOCK, BLOCK), 0)
    qi = lax.broadcasted_iota(jnp.int32, (2 * BLOCK, BLOCK), 1)
    dist = qi + BLOCK - kj
    in_band = jnp.logical_and(dist >= 0, dist < WINDOW)
    for h in range(N_Q_HEADS):
        cols = slice(h * BLOCK, (h + 1) * BLOCK)
        bias_t_scr[:, cols] = jnp.where(in_band, bias_t_scr[:, cols], NEG)


def _ssm_prep_kernel(are_ref, aim_ref, ls_ref, btr_ref, bti_ref,
                     lr_ref, li_ref, lcr_ref, lci_ref, bbr_ref, bbi_ref):
    ar = are_ref[...]
    ai = aim_ref[...]
    step = jnp.exp(ls_ref[...])
    zr = ar * step
    zi = ai * step
    mag = jnp.exp(zr)
    lr = mag * jnp.cos(zi)
    li = mag * jnp.sin(zi)
    pr, pi = lr, li
    for _ in range(int(math.log2(CHUNK))):
        pr, pi = pr * pr - pi * pi, 2.0 * pr * pi
    for g in range(N_SSM_GROUPS):
        st = slice(g * SSM_STATE, (g + 1) * SSM_STATE)
        lr_ref[:, st] = lr[g]
        li_ref[:, st] = li[g]
        lcr_ref[:, st] = pr[g]
        lci_ref[:, st] = pi[g]
    nr = lr - 1.0
    ni = li
    den = ar * ar + ai * ai
    cr = (nr * ar + ni * ai) / den
    ci = (ni * ar - nr * ai) / den
    btr = btr_ref[...]
    bti = bti_ref[...]
    bbr_ref[...] = cr * btr - ci * bti
    bbi_ref[...] = cr * bti + ci * btr


def _ssm_prep(a_re, a_im, log_step, b_re, b_im):
    g, p, c = N_SSM_GROUPS, SSM_STATE, SSM_GROUP
    vec = jax.ShapeDtypeStruct((1, STATE_LANES), F32)
    mat = jax.ShapeDtypeStruct((g, c, p), F32)
    return pl.pallas_call(
        _ssm_prep_kernel,
        out_shape=(vec, vec, vec, vec, mat, mat),
        name="ssm_prep",
    )(a_re.reshape(g, 1, p), a_im.reshape(g, 1, p), log_step.reshape(g, 1, 1),
      jnp.transpose(b_re, (0, 2, 1)), jnp.transpose(b_im, (0, 2, 1)))


def _stack_heads(q_blk, lane_lo):
    zero = jnp.zeros_like(q_blk[:, 0:LANES])
    lo, hi = [], []
    for c in range(ATTN_WIDTH // LANES):
        blk = q_blk[:, c * LANES:(c + 1) * LANES]
        lo.append(jnp.where(lane_lo, blk, zero))
        hi.append(jnp.where(lane_lo, zero, blk))
    return jnp.concatenate(lo + hi, axis=0)


def _unstack_heads(o, rows, lane_lo):
    nblk = ATTN_WIDTH // LANES
    return jnp.concatenate(
        [jnp.where(lane_lo, o[c * rows:(c + 1) * rows], o[(c + nblk) * rows:(c + nblk + 1) * rows])
         for c in range(nblk)], axis=-1)


Q_COLS = 256


def _proj_stages(x_ref, gmix_ref, win_ref, q_stage, k_stage, vt_stage, u_ref, newk_ref, newv_ref):
    xn = _rms(x_ref[...], gmix_ref[...]).astype(BF16)
    q_scale = HEAD_DIM ** -0.5 * LOG2E
    for c in range(ATTN_WIDTH // Q_COLS):
        cols = slice(c * Q_COLS, (c + 1) * Q_COLS)
        q = jnp.dot(xn, win_ref[:, cols], preferred_element_type=F32)
        q_stage[:, cols] = (q * q_scale).astype(BF16)
        yield
    kv = jnp.dot(xn, win_ref[:, ATTN_WIDTH:ATTN_WIDTH + 2 * KV_WIDTH], preferred_element_type=F32)
    k_stage[...] = kv[:, :KV_WIDTH].astype(BF16)
    vt_stage[...] = kv[:, KV_WIDTH:].T.astype(BF16)
    newk_ref[...] = kv[TM_A - WINDOW:, :KV_WIDTH]
    newv_ref[...] = kv[TM_A - WINDOW:, KV_WIDTH:]
    yield
    u0 = ATTN_WIDTH + 2 * KV_WIDTH
    for c in range(SSM_WIDTH // Q_COLS):
        cols = slice(u0 + c * Q_COLS, u0 + (c + 1) * Q_COLS)
        u_ref[:, c * Q_COLS:(c + 1) * Q_COLS] = jnp.dot(xn, win_ref[:, cols], preferred_element_type=F32)
        yield


def _attn_stages(sinks_ref, attn_ref, k_scr, vt_scr, q_scr, bias_t_scr, gt_scr, first_tile):
    nstk = N_Q_HEADS * BLOCK
    nblk = ATTN_WIDTH // LANES
    lane_lo = lax.broadcasted_iota(jnp.int32, (BLOCK, LANES), 1) < HEAD_DIM
    row_lo = lax.broadcasted_iota(jnp.int32, (KV_WIDTH, BLOCK), 0) < HEAD_DIM
    kj = lax.broadcasted_iota(jnp.int32, (2 * BLOCK, nstk), 0)
    has_prev = jnp.logical_or(kj >= BLOCK, jnp.logical_not(first_tile))
    head = lax.broadcasted_iota(jnp.int32, (1, nstk), 1) // BLOCK
    sink = jnp.zeros((1, nstk), F32)
    for h in range(N_Q_HEADS):
        sink = jnp.where(head == h, sinks_ref[0, h] * LOG2E, sink)

    scores = []
    for j in range(TM_A // BLOCK):
        r0 = j * BLOCK
        q2 = _stack_heads(q_scr[r0:r0 + BLOCK, :], lane_lo)
        keys = k_scr[r0:r0 + 2 * BLOCK, :]
        scores.append(lax.dot_general(keys, q2, (((1,), (1,)), ((), ())), preferred_element_type=F32))
        yield
    for j in range(TM_A // BLOCK):
        r0 = j * BLOCK
        s = scores[j] + bias_t_scr[...]
        if j == 0:
            s = jnp.where(has_prev, s, NEG)
        m = jnp.maximum(jnp.max(s, axis=0, keepdims=True), sink)
        p = jnp.exp2(s - m)
        denom = jnp.sum(p, axis=0, keepdims=True) + jnp.exp2(sink - m)
        o = jnp.dot(vt_scr[:, r0:r0 + 2 * BLOCK], p.astype(BF16), preferred_element_type=F32) / denom
        a = jnp.concatenate(
            [jnp.where(row_lo, o[:, c * BLOCK:(c + 1) * BLOCK], o[:, (c + nblk) * BLOCK:(c + nblk + 1) * BLOCK])
             for c in range(nblk)], axis=0)
        a = a * lax.rsqrt(jnp.mean(a * a, axis=0, keepdims=True) + EPS) * gt_scr[...]
        attn_ref[r0:r0 + BLOCK, :] = a.T.astype(BF16)
        yield


def _in_attn_kernel(relb_ref, sinks_ref, x_ref, gmix_ref, win_ref, bucket_t_ref, gattn_ref,
                    wout_f, wglu_f, wup_f, wdown_f,
                    attn_ref, u_ref, newk_ref, newv_ref, wout_b, wglu_b, wup_b, wdown_b,
                    k_scr, vt_scr, q_scr, k_stage, vt_stage, q_stage, bias_t_scr, gt_scr):
    b = pl.program_id(0)
    t = pl.program_id(1)

    @pl.when(jnp.logical_and(b == 0, t == 0))
    def _():
        _build_bias_t(relb_ref, bucket_t_ref, bias_t_scr)
        gt_scr[...] = jnp.broadcast_to(gattn_ref[...], (ATTN_WIDTH, BLOCK))
        k_scr[...] = jnp.zeros_like(k_scr)
        vt_scr[...] = jnp.zeros_like(vt_scr)
        k_stage[...] = jnp.zeros_like(k_stage)
        vt_stage[...] = jnp.zeros_like(vt_stage)
        q_stage[...] = jnp.zeros_like(q_stage)

    k_scr[0:BLOCK, :] = k_scr[TM_A:TM_A + BLOCK, :]
    vt_scr[:, 0:BLOCK] = vt_scr[:, TM_A:TM_A + BLOCK]
    k_scr[BLOCK:BLOCK + TM_A, :] = k_stage[...]
    vt_scr[:, BLOCK:BLOCK + TM_A] = vt_stage[...]
    q_scr[...] = q_stage[...]

    wout_b[...] = wout_f[...].astype(BF16)
    wglu_b[...] = wglu_f[...].astype(BF16)
    wup_b[...] = wup_f[...].astype(BF16)
    wdown_b[...] = wdown_f[...].astype(BF16)

    proj = _proj_stages(x_ref, gmix_ref, win_ref, q_stage, k_stage, vt_stage, u_ref, newk_ref, newv_ref)
    attn = _attn_stages(sinks_ref, attn_ref, k_scr, vt_scr, q_scr, bias_t_scr, gt_scr, t <= 1)
    nb = TM_A // BLOCK
    order = ("p" + "a" * (nb // 2)) * 2 + "p" + "a" * (nb // 4) + "p" + "a" * (nb // 2) + "p" + "a" * (nb - nb // 4 - nb // 2)
    for stage in order:
        next(proj if stage == "p" else attn)
    assert next(proj, None) is None and next(attn, None) is None


def _const_spec(shape):
    zeros = (0,) * len(shape)
    return pl.BlockSpec(shape, lambda *_: zeros, pipeline_mode=pl.Buffered(1))


def _smem_spec():
    return pl.BlockSpec(memory_space=pltpu.SMEM)


def _in_attn(x, rel_bias, sinks, g_mix, w_in, bucket_t, g_attn_col, w_out, w_glu, w_up, w_down):
    bsz, seq, _ = x.shape
    nt = seq // TM_A
    steps = bsz * nt
    cur = lambda width: pl.BlockSpec((None, TM_A, width), lambda b, t: (b, jnp.minimum(t, nt - 1), 0))
    prev = lambda width: pl.BlockSpec((None, TM_A, width), lambda b, t: (b, jnp.maximum(t - 1, 0), 0))
    win_spec = pl.BlockSpec((None, WINDOW, KV_WIDTH), lambda b, t: (b, 0, 0))

    def tile(b, t):
        return b * nt + jnp.minimum(t, nt - 1)

    def rows_of(w):
        blk = w.shape[0] // steps
        return pl.BlockSpec((blk, w.shape[1]), lambda b, t: (tile(b, t), 0))

    nhead_blk = D_MODEL // HEAD_DIM

    def wout_src(b, t):
        return (jnp.minimum(tile(b, t), nhead_blk - 1), 0)

    def wout_dst(b, t):
        j = jnp.minimum(tile(b, t), nhead_blk - 1)
        return (jnp.where(j < N_Q_HEADS, (j % Q_PER_KV) * N_KV_HEADS + j // Q_PER_KV, j), 0)

    assert steps >= nhead_blk and w_up.shape[0] % steps == 0 and w_down.shape[0] % steps == 0
    wout_in = pl.BlockSpec((HEAD_DIM, D_MODEL), wout_src)
    wout_out = pl.BlockSpec((HEAD_DIM, D_MODEL), wout_dst)
    cast_shape = lambda w: jax.ShapeDtypeStruct(w.shape, BF16)
    return pl.pallas_call(
        _in_attn_kernel,
        out_shape=(jax.ShapeDtypeStruct((bsz, seq, ATTN_WIDTH), BF16),
                   jax.ShapeDtypeStruct((bsz, seq, SSM_WIDTH), F32),
                   jax.ShapeDtypeStruct((bsz, WINDOW, KV_WIDTH), F32),
                   jax.ShapeDtypeStruct((bsz, WINDOW, KV_WIDTH), F32),
                   cast_shape(w_out), cast_shape(w_glu), cast_shape(w_up), cast_shape(w_down)),
        grid=(bsz, nt + 1),
        in_specs=[_smem_spec(), _smem_spec(), cur(D_MODEL), _const_spec((1, D_MODEL)),
                  _const_spec((D_MODEL, D_IN_PROJ)), _const_spec((2 * BLOCK, BLOCK)),
                  _const_spec((ATTN_WIDTH, 1)),
                  wout_in, rows_of(w_glu), rows_of(w_up), rows_of(w_down)],
        out_specs=(prev(ATTN_WIDTH), cur(SSM_WIDTH), win_spec, win_spec,
                   wout_out, rows_of(w_glu), rows_of(w_up), rows_of(w_down)),
        scratch_shapes=[pltpu.VMEM((TM_A + BLOCK, KV_WIDTH), BF16),
                        pltpu.VMEM((KV_WIDTH, TM_A + BLOCK), BF16),
                        pltpu.VMEM((TM_A, ATTN_WIDTH), BF16),
                        pltpu.VMEM((TM_A, KV_WIDTH), BF16),
                        pltpu.VMEM((KV_WIDTH, TM_A), BF16),
                        pltpu.VMEM((TM_A, ATTN_WIDTH), BF16),
                        pltpu.VMEM((2 * BLOCK, N_Q_HEADS * BLOCK), F32),
                        pltpu.VMEM((ATTN_WIDTH, BLOCK), F32)],
        compiler_params=pltpu.CompilerParams(
            dimension_semantics=("arbitrary", "arbitrary"), vmem_limit_bytes=VMEM_LIMIT),
        name="in_attn",
    )(rel_bias, sinks, x, g_mix, w_in, bucket_t, g_attn_col, w_out, w_glu, w_up, w_down)


def _s_in_attn_kernel(relb_ref, sinks_ref, x_ref, gmix_ref, win_ref, bucket_ref, gattn_ref,
                      ck_ref, cv_ref,
                      attn_ref, u_ref, newk_ref, newv_ref,
                      bias_scr, biasd_scr, proj_scr):
    rows = attn_ref.shape[0]
    steps = rows // S_SEQ
    pair = SUBLANES // steps
    prow = N_Q_HEADS * SUBLANES

    @pl.when(pl.program_id(0) == 0)
    def _():
        _build_bias(relb_ref, bucket_ref, bias_scr)
        for h in range(N_Q_HEADS):
            for k in range(pair):
                biasd_scr[h * SUBLANES + k * steps:h * SUBLANES + (k + 1) * steps, :] = bias_scr[h, 0:steps, :]
        xn = _rms(x_ref[...], gmix_ref[...]).astype(BF16)
        proj_all = jnp.dot(xn, win_ref[...], preferred_element_type=F32)
        proj_scr[...] = proj_all
        u_ref[...] = proj_all[:, ATTN_WIDTH + 2 * KV_WIDTH:]

    r0 = pl.multiple_of(pl.program_id(0) * rows, rows)
    proj = proj_scr[pl.ds(r0, rows), :]
    qs = proj[:, :ATTN_WIDTH] * (HEAD_DIM ** -0.5)

    qrow = lax.broadcasted_iota(jnp.int32, (prow, WINDOW), 0) % steps
    kcol = lax.broadcasted_iota(jnp.int32, (prow, WINDOW), 1)
    valid_c = kcol > qrow
    qrow_n = lax.broadcasted_iota(jnp.int32, (prow, NEW_ROWS), 0) % steps
    kcol_n = lax.broadcasted_iota(jnp.int32, (prow, NEW_ROWS), 1)
    valid_n = kcol_n <= qrow_n
    head = lax.broadcasted_iota(jnp.int32, (prow, 1), 0) // SUBLANES
    sink = jnp.zeros((prow, 1), F32)
    for h in range(N_Q_HEADS):
        sink = jnp.where(head == h, sinks_ref[0, h], sink)
    seq_in_tile = (lax.broadcasted_iota(jnp.int32, (prow, LANES), 0) % SUBLANES) // steps
    lane_lo = lax.broadcasted_iota(jnp.int32, (SUBLANES, LANES), 1) < HEAD_DIM
    row8 = lax.broadcasted_iota(jnp.int32, (SUBLANES, KV_WIDTH), 0)
    gattn = gattn_ref[...]
    pad = jnp.zeros((NEW_ROWS - SUBLANES, KV_WIDTH), F32)
    bias_c = biasd_scr[:, 0:WINDOW]
    bias_n = biasd_scr[:, WINDOW:WINDOW + NEW_ROWS]
    nt_dims = (((1,), (1,)), ((), ()))

    tiles = []
    for tl in range(S_SEQ // pair):
        rws = slice(tl * SUBLANES, (tl + 1) * SUBLANES)
        q2 = _stack_heads(qs[rws], lane_lo).astype(BF16)
        kn8 = proj[rws, ATTN_WIDTH:ATTN_WIDTH + KV_WIDTH]
        vn8 = proj[rws, ATTN_WIDTH + KV_WIDTH:ATTN_WIDTH + 2 * KV_WIDTH]
        o_tile = jnp.zeros((prow, LANES), F32)
        for k in range(pair):
            sq = tl * pair + k
            shift = (SUBLANES - k * steps) % SUBLANES
            kn = jnp.where(row8 < steps, pltpu.roll(kn8, shift, axis=0) if shift else kn8, 0.0)
            vn = jnp.where(row8 < steps, pltpu.roll(vn8, shift, axis=0) if shift else vn8, 0.0)
            ck = ck_ref[sq]
            cv = cv_ref[sq]
            newk_ref[sq, 0:WINDOW - steps, :] = ck[steps:, :]
            newk_ref[sq, WINDOW - steps:, :] = kn[0:steps]
            newv_ref[sq, 0:WINDOW - steps, :] = cv[steps:, :]
            newv_ref[sq, WINDOW - steps:, :] = vn[0:steps]
            kn_b = jnp.concatenate([kn, pad], axis=0).astype(BF16)
            vn_b = jnp.concatenate([vn, pad], axis=0).astype(BF16)
            s_c = lax.dot_general(q2, ck.astype(BF16), nt_dims, preferred_element_type=F32)
            s_n = lax.dot_general(q2, kn_b, nt_dims, preferred_element_type=F32)
            s_c = jnp.where(valid_c, s_c + bias_c, NEG)
            s_n = jnp.where(valid_n, s_n + bias_n, NEG)
            m = jnp.maximum(jnp.maximum(jnp.max(s_c, axis=-1, keepdims=True),
                                        jnp.max(s_n, axis=-1, keepdims=True)), sink)
            p_c = jnp.exp(s_c - m)
            p_n = jnp.exp(s_n - m)
            denom = (jnp.sum(p_c, axis=-1, keepdims=True) + jnp.sum(p_n, axis=-1, keepdims=True)
                     + jnp.exp(sink - m))
            o = (jnp.dot(p_c.astype(BF16), cv.astype(BF16), preferred_element_type=F32)
                 + jnp.dot(p_n.astype(BF16), vn_b, preferred_element_type=F32)) / denom
            o_tile = jnp.where(seq_in_tile == k, o, o_tile)
        tiles.append(_unstack_heads(o_tile, SUBLANES, lane_lo))
    attn_ref[...] = _rms(jnp.concatenate(tiles, axis=0), gattn).astype(BF16)


def _s_in_attn(x, rel_bias, sinks, g_mix, w_in, bucket, g_attn, cache_k, cache_v):
    nseq, steps, _ = x.shape
    rows = S_SEQ * steps
    xf = x.reshape(nseq * steps, D_MODEL)
    row = lambda width: pl.BlockSpec((rows, width), lambda i: (i, 0))
    cache = pl.BlockSpec((S_SEQ, WINDOW, KV_WIDTH), lambda i: (i, 0, 0))
    return pl.pallas_call(
        _s_in_attn_kernel,
        out_shape=(jax.ShapeDtypeStruct((nseq * steps, ATTN_WIDTH), BF16),
                   jax.ShapeDtypeStruct((nseq * steps, SSM_WIDTH), F32),
                   jax.ShapeDtypeStruct((nseq, WINDOW, KV_WIDTH), F32),
                   jax.ShapeDtypeStruct((nseq, WINDOW, KV_WIDTH), F32)),
        grid=(nseq // S_SEQ,),
        in_specs=[_smem_spec(), _smem_spec(), _const_spec((nseq * steps, D_MODEL)), _const_spec((1, D_MODEL)),
                  _const_spec((D_MODEL, D_IN_PROJ)), _const_spec((SUBLANES, 2 * BLOCK)),
                  _const_spec((1, ATTN_WIDTH)), cache, cache],
        out_specs=(row(ATTN_WIDTH), pl.BlockSpec((nseq * steps, SSM_WIDTH), lambda i: (0, 0)), cache, cache),
        scratch_shapes=[pltpu.VMEM((N_Q_HEADS, SUBLANES, 2 * BLOCK), F32),
                        pltpu.VMEM((N_Q_HEADS * SUBLANES, 2 * WINDOW), F32),
                        pltpu.VMEM((nseq * steps, D_IN_PROJ), F32)],
        compiler_params=pltpu.CompilerParams(
            dimension_semantics=("arbitrary",), vmem_limit_bytes=VMEM_LIMIT),
        name="s_in_attn",
    )(rel_bias, sinks, xf, g_mix, w_in, bucket[0:SUBLANES], g_attn, cache_k, cache_v)


def _embed_blocks(bbr_ref, bbi_ref, cre_in, cim_in, bre, bim, cre, cim):
    for ref in (bre, bim, cre, cim):
        ref[...] = jnp.zeros_like(ref)
    for g in range(N_SSM_GROUPS):
        sl, j = divmod(g, SLAB // SSM_STATE)
        ch = slice(j * SSM_GROUP, (j + 1) * SSM_GROUP)
        st = slice(j * SSM_STATE, (j + 1) * SSM_STATE)
        bre[sl, ch, st] = bbr_ref[g].astype(BF16)
        bim[sl, ch, st] = bbi_ref[g].astype(BF16)
        cre[sl, ch, st] = cre_in[g].astype(BF16)
        cim[sl, ch, st] = cim_in[g].astype(BF16)


def _output_map(h_slab_re, h_slab_im, c_slab_re, c_slab_im):
    nt_dims = (((1,), (1,)), ((), ()))
    yr = lax.dot_general(h_slab_re.astype(BF16), c_slab_re, nt_dims, preferred_element_type=F32)
    yi = lax.dot_general(h_slab_im.astype(BF16), c_slab_im, nt_dims, preferred_element_type=F32)
    return yr - yi


def _glu_norm(y, wglu_ref, bglu_ref, gssm_ref):
    g = jax.nn.gelu(y)
    gate = jax.nn.sigmoid(jnp.dot(g.astype(BF16), wglu_ref[...], preferred_element_type=F32) + bglu_ref[...])
    return _rms(g * gate, gssm_ref[...])


SLAB = 4 * LANES
SLAB_CH = SLAB // SSM_STATE * SSM_GROUP


REC_PIECE = 16
EPI_ROWS = 128
FF_BLK = 1024


def _scan_stages(u_ref, lr_ref, li_ref, lcr_ref, lci_ref, d_ref, wglu_ref, bglu_ref, gssm_ref, out_ref,
                 bre_ref, bim_ref, cre_ref, cim_ref,
                 pad_scr, up_scr, y_scr, hre, him, carry_re, carry_im, e_re, e_im, hc_re, hc_im):
    nslab = SSM_WIDTH // LANES
    nslabs = STATE_LANES // SLAB

    for c in range(SUBLANES):
        for s in range(nslab):
            pad_scr[s, c * PITCH:c * PITCH + CHUNK, :] = u_ref[c * CHUNK:(c + 1) * CHUNK, s * LANES:(s + 1) * LANES]
    yield
    for tau in range(CHUNK):
        for s in range(nslab):
            up_scr[tau * SUBLANES:(tau + 1) * SUBLANES, s * LANES:(s + 1) * LANES] = (
                pad_scr[s, pl.ds(tau, SUBLANES, stride=PITCH), :])
        if tau % REC_PIECE == REC_PIECE - 1:
            yield

    for sl in range(nslabs):
        cols = slice(sl * SLAB, (sl + 1) * SLAB)
        ch = slice(sl * SLAB_CH, (sl + 1) * SLAB_CH)
        upb = up_scr[:, ch].astype(BF16)
        hre[:, cols] = jnp.dot(upb, bre_ref[sl], preferred_element_type=F32)
        yield
        him[:, cols] = jnp.dot(upb, bim_ref[sl], preferred_element_type=F32)
        yield

    for sl in range(nslabs):
        cols = slice(sl * SLAB, (sl + 1) * SLAB)
        ch = slice(sl * SLAB_CH, (sl + 1) * SLAB_CH)
        ar = jnp.broadcast_to(lr_ref[:, cols], (SUBLANES, SLAB))
        ai = jnp.broadcast_to(li_ref[:, cols], (SUBLANES, SLAB))

        hr = jnp.zeros((SUBLANES, SLAB), F32)
        hi = jnp.zeros((SUBLANES, SLAB), F32)
        for tau in range(CHUNK):
            rws = slice(tau * SUBLANES, (tau + 1) * SUBLANES)
            hr, hi = ar * hr - ai * hi + hre[rws, cols], ar * hi + ai * hr + him[rws, cols]
            if tau % REC_PIECE == REC_PIECE - 1:
                yield
        e_re[:, cols] = hr
        e_im[:, cols] = hi

        cr = carry_re[:, cols]
        ci = carry_im[:, cols]
        lcr = lcr_ref[:, cols]
        lci = lci_ref[:, cols]
        for c in range(SUBLANES):
            hc_re[c:c + 1, cols] = cr
            hc_im[c:c + 1, cols] = ci
            er = e_re[c:c + 1, cols]
            ei = e_im[c:c + 1, cols]
            cr, ci = lcr * cr - lci * ci + er, lcr * ci + lci * cr + ei
        carry_re[:, cols] = cr
        carry_im[:, cols] = ci
        yield

        hr = hc_re[:, cols]
        hi = hc_im[:, cols]
        for tau in range(CHUNK):
            rws = slice(tau * SUBLANES, (tau + 1) * SUBLANES)
            hr, hi = ar * hr - ai * hi + hre[rws, cols], ar * hi + ai * hr + him[rws, cols]
            hre[rws, cols] = hr
            him[rws, cols] = hi
            if tau % REC_PIECE == REC_PIECE - 1:
                yield

        y_scr[:, ch] = _output_map(hre[:, cols], him[:, cols], cre_ref[sl], cim_ref[sl])
        yield

    taus = EPI_ROWS // SUBLANES
    for r in range(TM_B // EPI_ROWS):
        rows = slice(r * EPI_ROWS, (r + 1) * EPI_ROWS)
        o = _glu_norm(y_scr[rows, :] + d_ref[...] * up_scr[rows, :], wglu_ref, bglu_ref, gssm_ref)
        for i in range(taus):
            for s in range(nslab):
                pad_scr[s, pl.ds(r * taus + i, SUBLANES, stride=PITCH), :] = (
                    o[i * SUBLANES:(i + 1) * SUBLANES, s * LANES:(s + 1) * LANES])
        yield
    for c in range(SUBLANES):
        for s in range(nslab):
            out_ref[c * CHUNK:(c + 1) * CHUNK, s * LANES:(s + 1) * LANES] = (
                pad_scr[s, c * PITCH:c * PITCH + CHUNK, :].astype(BF16))
    yield


N_SCAN_STAGES = (1 + CHUNK // REC_PIECE + 2 * (STATE_LANES // SLAB)
                 + (STATE_LANES // SLAB) * (2 * (CHUNK // REC_PIECE) + 2) + TM_B // EPI_ROWS + 1)


def _mlp_stages(x_cols, attn, s_ref, wout_ref, gmlp_ref, wup_ref, wdown_ref, gfin_ref, y_ref, hm_scr, act_scr):
    ncol = D_MODEL // Q_COLS
    for n in range(ncol):
        cols = slice(n * Q_COLS, (n + 1) * Q_COLS)
        y_ref[:, cols] = (x_cols(cols)
                          + jnp.dot(attn(), wout_ref[0:ATTN_WIDTH, cols], preferred_element_type=F32)
                          + jnp.dot(s_ref[...], wout_ref[ATTN_WIDTH:, cols], preferred_element_type=F32))
        yield
    hm_scr[...] = _rms(y_ref[...], gmlp_ref[...]).astype(BF16)
    for c in range(D_FF // FF_BLK):
        act = act_scr.at[c % 2]
        for n in range(FF_BLK // Q_COLS):
            cols = slice(c * FF_BLK + n * Q_COLS, c * FF_BLK + (n + 1) * Q_COLS)
            up = jnp.dot(hm_scr[...], wup_ref[:, cols], preferred_element_type=F32)
            act[:, n * Q_COLS:(n + 1) * Q_COLS] = jnp.square(jnp.maximum(up, 0.0)).astype(BF16)
            yield
        for n in range(ncol):
            cols = slice(n * Q_COLS, (n + 1) * Q_COLS)
            y_ref[:, cols] += jnp.dot(act[...], wdown_ref[c * FF_BLK:(c + 1) * FF_BLK, cols],
                                      preferred_element_type=F32)
            yield
    y_ref[...] = _rms(y_ref[...], gfin_ref[...])
    yield


N_MLP_STAGES = D_MODEL // Q_COLS + (D_FF // FF_BLK) * (FF_BLK // Q_COLS + D_MODEL // Q_COLS) + 1


def _even_merge(n_a, n_b):
    keys = [((k + 0.5) / n_a, "a") for k in range(n_a)] + [((k + 0.5) / n_b, "b") for k in range(n_b)]
    return [name for _, name in sorted(keys)]


def _scan_mlp_kernel(x_ref, a_ref, u_ref, xdec_ref, adec_ref, sdec_ref,
                     lr_ref, li_ref, lcr_ref, lci_ref, bbr_ref, bbi_ref, cre_in, cim_in,
                     d_ref, wglu_ref, bglu_ref, gssm_ref, wout_ref, gmlp_ref, wup_ref, wdown_ref, gfin_ref,
                     y_ref, ydec_ref, hre_out, him_out,
                     ssm_scr, hm_scr, act_scr, bre_ref, bim_ref, cre_ref, cim_ref,
                     pad_scr, up_scr, y_scr, hre, him, carry_re, carry_im, e_re, e_im, hc_re, hc_im, *, nt):
    i = pl.program_id(0)
    dec = i == 0

    @pl.when(dec)
    def _():
        _embed_blocks(bbr_ref, bbi_ref, cre_in, cim_in, bre_ref, bim_ref, cre_ref, cim_ref)
        ssm_scr[...] = sdec_ref[...]

    @pl.when(i % nt == 0)
    def _():
        carry_re[...] = jnp.zeros_like(carry_re)
        carry_im[...] = jnp.zeros_like(carry_im)

    mlp = _mlp_stages(lambda cols: jnp.where(dec, xdec_ref[:, cols], x_ref[:, cols]),
                      lambda: jnp.where(dec, adec_ref[...], a_ref[...]),
                      ssm_scr, wout_ref, gmlp_ref, wup_ref, wdown_ref, gfin_ref, y_ref, hm_scr, act_scr)
    scan = _scan_stages(u_ref, lr_ref, li_ref, lcr_ref, lci_ref, d_ref, wglu_ref, bglu_ref, gssm_ref, ssm_scr,
                        bre_ref, bim_ref, cre_ref, cim_ref,
                        pad_scr, up_scr, y_scr, hre, him, carry_re, carry_im, e_re, e_im, hc_re, hc_im)
    for stage in _even_merge(N_MLP_STAGES, N_SCAN_STAGES):
        next(mlp if stage == "a" else scan)
    assert next(scan, None) is None and next(mlp, None) is None

    @pl.when(dec)
    def _():
        ydec_ref[...] = y_ref[...]

    @pl.when(i % nt == nt - 1)
    def _():
        hre_out[...] = carry_re[...]
        him_out[...] = carry_im[...]


def _scan_mlp(x, attn_n, u, x_dec, attn_dec, ssm_dec, lam, lam_chunk, bbar, c_gcp, d, w_glu, b_glu, g_ssm,
              w_out, g_mlp, w_up, w_down, g_fin):
    bsz, seq, _ = u.shape
    nt = seq // TM_B
    n = bsz * nt
    assert x_dec.shape[0] == TM_B, "the decode rows ride through the MLP as one tile"
    flat = lambda v: v.reshape(n * TM_B, v.shape[-1])
    cur = lambda width: pl.BlockSpec((TM_B, width), lambda i: (jnp.minimum(i, n - 1), 0))
    prev = lambda width: pl.BlockSpec((TM_B, width), lambda i: (jnp.maximum(i - 1, 0), 0))
    state = pl.BlockSpec((None, 1, STATE_LANES), lambda i: (jnp.minimum(i, n - 1) // nt, 0, 0))
    vec = _const_spec((1, STATE_LANES))
    chan = _const_spec((1, SSM_WIDTH))
    model = _const_spec((1, D_MODEL))
    dec = lambda width: _const_spec((TM_B, width))
    y, y_dec, h_re, h_im = pl.pallas_call(
        functools.partial(_scan_mlp_kernel, nt=nt),
        out_shape=(jax.ShapeDtypeStruct((n * TM_B, D_MODEL), F32),
                   jax.ShapeDtypeStruct((TM_B, D_MODEL), F32),
                   jax.ShapeDtypeStruct((bsz, 1, STATE_LANES), F32),
                   jax.ShapeDtypeStruct((bsz, 1, STATE_LANES), F32)),
        grid=(n + 1,),
        in_specs=[prev(D_MODEL), prev(ATTN_WIDTH), cur(SSM_WIDTH), dec(D_MODEL), dec(ATTN_WIDTH), dec(SSM_WIDTH),
                  vec, vec, vec, vec,
                  _group_blocks(), _group_blocks(), _group_blocks(), _group_blocks(),
                  chan, _const_spec((SSM_WIDTH, SSM_WIDTH)), chan, chan,
                  _const_spec((D_MODEL, D_MODEL)), model, _const_spec((D_MODEL, D_FF)),
                  _const_spec((D_FF, D_MODEL)), model],
        out_specs=(prev(D_MODEL), pl.BlockSpec((TM_B, D_MODEL), lambda i: (0, 0)), state, state),
        scratch_shapes=[pltpu.VMEM((TM_B, SSM_WIDTH), BF16),
                        pltpu.VMEM((TM_B, D_MODEL), BF16),
                        pltpu.VMEM((2, TM_B, FF_BLK), BF16)] + _block_operand_scratch() + [
                        pltpu.VMEM((SSM_WIDTH // LANES, SUBLANES * PITCH, LANES), F32),
                        pltpu.VMEM((TM_B, SSM_WIDTH), F32),
                        pltpu.VMEM((TM_B, SSM_WIDTH), F32),
                        pltpu.VMEM((TM_B, STATE_LANES), F32),
                        pltpu.VMEM((TM_B, STATE_LANES), F32),
                        pltpu.VMEM((1, STATE_LANES), F32),
                        pltpu.VMEM((1, STATE_LANES), F32),
                        pltpu.VMEM((SUBLANES, STATE_LANES), F32),
                        pltpu.VMEM((SUBLANES, STATE_LANES), F32),
                        pltpu.VMEM((SUBLANES, STATE_LANES), F32),
                        pltpu.VMEM((SUBLANES, STATE_LANES), F32)],
        compiler_params=pltpu.CompilerParams(
            dimension_semantics=("arbitrary",), vmem_limit_bytes=VMEM_LIMIT),
        name="scan_mlp",
    )(flat(x), flat(attn_n), flat(u), x_dec, attn_dec, ssm_dec, lam[0], lam[1], lam_chunk[0], lam_chunk[1],
      bbar[0], bbar[1], c_gcp[0], c_gcp[1], d, w_glu, b_glu, g_ssm, w_out, g_mlp, w_up, w_down, g_fin)
    return y.reshape(bsz, seq, D_MODEL), y_dec, h_re, h_im


def _block_operand_scratch():
    return [pltpu.VMEM((STATE_LANES // SLAB, SLAB_CH, SLAB), BF16) for _ in range(4)]


def _group_blocks():
    return _const_spec((N_SSM_GROUPS, SSM_GROUP, SSM_STATE))


def _s_scan_kernel(u_ref, h0r_ref, h0i_ref, lr_ref, li_ref, bbr_ref, bbi_ref, cre_in, cim_in,
                   d_ref, wglu_ref, bglu_ref, gssm_ref,
                   out_ref, hre_out, him_out,
                   bre_ref, bim_ref, cre_ref, cim_ref,
                   pad_scr, up_scr, hre, him):
    nseq = h0r_ref.shape[0]
    steps = u_ref.shape[0] // nseq
    nslab = SSM_WIDTH // LANES
    _embed_blocks(bbr_ref, bbi_ref, cre_in, cim_in, bre_ref, bim_ref, cre_ref, cim_ref)

    for s in range(nslab):
        pad_scr[s] = u_ref[:, s * LANES:(s + 1) * LANES]
    for i in range(steps):
        for s in range(nslab):
            up_scr[i * nseq:(i + 1) * nseq, s * LANES:(s + 1) * LANES] = (
                pad_scr[s, pl.ds(i, nseq, stride=steps), :])
    up = up_scr[...]
    upb = up.astype(BF16)

    ys = []
    for sl in range(STATE_LANES // SLAB):
        cols = slice(sl * SLAB, (sl + 1) * SLAB)
        ch = slice(sl * SLAB_CH, (sl + 1) * SLAB_CH)
        hre[:, cols] = jnp.dot(upb[:, ch], bre_ref[sl], preferred_element_type=F32)
        him[:, cols] = jnp.dot(upb[:, ch], bim_ref[sl], preferred_element_type=F32)
        ar = lr_ref[:, cols]
        ai = li_ref[:, cols]
        hr = h0r_ref[:, cols]
        hi = h0i_ref[:, cols]
        for i in range(steps):
            rws = slice(i * nseq, (i + 1) * nseq)
            hr, hi = ar * hr - ai * hi + hre[rws, cols], ar * hi + ai * hr + him[rws, cols]
            hre[rws, cols] = hr
            him[rws, cols] = hi
        hre_out[:, cols] = hr
        him_out[:, cols] = hi
        ys.append(_output_map(hre[:, cols], him[:, cols], cre_ref[sl], cim_ref[sl]))

    y = jnp.concatenate(ys, axis=-1) + d_ref[...] * up
    o = _glu_norm(y, wglu_ref, bglu_ref, gssm_ref)

    for i in range(steps):
        for s in range(nslab):
            pad_scr[s, pl.ds(i, nseq, stride=steps), :] = o[i * nseq:(i + 1) * nseq, s * LANES:(s + 1) * LANES]
    for s in range(nslab):
        out_ref[:, s * LANES:(s + 1) * LANES] = pad_scr[s].astype(BF16)


def _s_scan(u, h0_re, h0_im, lam, bbar, c_gcp, d, w_glu, b_glu, g_ssm):
    rows = u.shape[0]
    nseq = h0_re.shape[0]
    return pl.pallas_call(
        _s_scan_kernel,
        out_shape=(jax.ShapeDtypeStruct((rows, SSM_WIDTH), BF16),
                   jax.ShapeDtypeStruct((nseq, STATE_LANES), F32),
                   jax.ShapeDtypeStruct((nseq, STATE_LANES), F32)),
        scratch_shapes=_block_operand_scratch() + [
                        pltpu.VMEM((SSM_WIDTH // LANES, rows, LANES), F32),
                        pltpu.VMEM((rows, SSM_WIDTH), F32),
                        pltpu.VMEM((rows, STATE_LANES), F32),
                        pltpu.VMEM((rows, STATE_LANES), F32)],
        compiler_params=pltpu.CompilerParams(vmem_limit_bytes=VMEM_LIMIT),
        name="s_scan",
    )(u, h0_re, h0_im, lam[0], lam[1], bbar[0], bbar[1], c_gcp[0], c_gcp[1], d, w_glu, b_glu, g_ssm)


def _pair_heads(w, axis):
    shape = w.shape
    split = shape[:axis] + (N_KV_HEADS, Q_PER_KV, HEAD_DIM) + shape[axis + 1:]
    return jnp.swapaxes(w.reshape(split), axis, axis + 1).reshape(shape)


def kernel(x_prompt, x_sample, cache_k, cache_v, state_ssm_re, state_ssm_im, rel_bias, norm_mix, w_in, attn_sinks, ssm_a_re, ssm_a_im, ssm_log_step, ssm_b_re, ssm_b_im, ssm_c_re, ssm_c_im, ssm_d, w_glu, b_glu, norm_attn_out, norm_ssm_out, w_out, norm_mlp, w_up, w_down, norm_final):
    depth = w_in.shape[0]
    bsz, seq, _ = x_prompt.shape
    nseq, steps, _ = x_sample.shape
    bucket = jnp.asarray(_rel_bucket_table())
    g_fin = norm_final.reshape(1, D_MODEL)
    assert depth == 1, "the final norm is fused into the (single) layer's MLP kernel"

    xp, xs = x_prompt, x_sample
    outs = [[] for _ in range(8)]
    for l in range(depth):
        lr, li, lcr, lci, bbr, bbi = _ssm_prep(ssm_a_re[l], ssm_a_im[l], ssm_log_step[l], ssm_b_re[l], ssm_b_im[l])
        lam = (lr, li)
        lam_chunk = (lcr, lci)
        bbar = (bbr, bbi)
        c_gcp = (ssm_c_re[l], ssm_c_im[l])
        g_mix = norm_mix[l].reshape(1, D_MODEL)
        g_attn = _pair_heads(norm_attn_out[l].reshape(1, ATTN_WIDTH), axis=1)
        g_ssm = norm_ssm_out[l].reshape(1, SSM_WIDTH)
        g_mlp = norm_mlp[l].reshape(1, D_MODEL)
        sinks = attn_sinks[l].reshape(1, N_Q_HEADS)
        d = ssm_d[l].reshape(1, SSM_WIDTH)
        bg = b_glu[l].reshape(1, SSM_WIDTH)
        w_in_b = jnp.concatenate([_pair_heads(w_in[l][:, :ATTN_WIDTH], axis=1), w_in[l][:, ATTN_WIDTH:]],
                                 axis=1).astype(BF16)

        attn_n, u, k1, v1, w_out_b, w_glu_b, w_up_b, w_down_b = _in_attn(
            xp, rel_bias, sinks, g_mix, w_in_b, bucket.T, g_attn.reshape(ATTN_WIDTH, 1),
            w_out[l], w_glu[l], w_up[l], w_down[l])

        ck = cache_k[l].reshape(nseq, WINDOW, KV_WIDTH)
        cv = cache_v[l].reshape(nseq, WINDOW, KV_WIDTH)
        s_attn, s_u, k2, v2 = _s_in_attn(xs, rel_bias, sinks, g_mix, w_in_b, bucket, g_attn, ck, cv)
        s_ssm, r2, i2 = _s_scan(s_u, state_ssm_re[l].reshape(nseq, STATE_LANES),
                                state_ssm_im[l].reshape(nseq, STATE_LANES),
                                lam, bbar, c_gcp, d, w_glu_b, bg, g_ssm)

        xp, xs, r1, i1 = _scan_mlp(xp, attn_n, u, xs.reshape(nseq * steps, D_MODEL), s_attn, s_ssm,
                                   lam, lam_chunk, bbar, c_gcp, d, w_glu_b, bg, g_ssm,
                                   w_out_b, g_mlp, w_up_b, w_down_b, g_fin)
        xs = xs.reshape(nseq, steps, D_MODEL)

        kv_shape = (WINDOW, N_KV_HEADS, HEAD_DIM)
        st_shape = (N_SSM_GROUPS, SSM_STATE)
        for lst, val in zip(outs, (k1.reshape(bsz, *kv_shape), v1.reshape(bsz, *kv_shape),
                                   r1.reshape(bsz, *st_shape), i1.reshape(bsz, *st_shape),
                                   k2.reshape(nseq, *kv_shape), v2.reshape(nseq, *kv_shape),
                                   r2.reshape(nseq, *st_shape), i2.reshape(nseq, *st_shape))):
            lst.append(val)

    return (xp, xs) + tuple(jnp.stack(o) for o in outs)
```

```python
import functools
import math

import numpy as np
import jax
import jax.numpy as jnp
from jax import lax
from jax.experimental import pallas as pl
from jax.experimental.pallas import tpu as pltpu

D_MODEL = 1024
HEAD_DIM = 64
ATTN_WIDTH = 512
N_Q_HEADS = 8
N_KV_HEADS = 2
Q_PER_KV = 4
KV_WIDTH = 128
WINDOW = 128
BLOCK = 128
N_BUCKETS = 32
MAX_DISTANCE = 128
SSM_WIDTH = 512
SSM_GROUP = 16
N_SSM_GROUPS = 32
SSM_STATE = 64
STATE_LANES = N_SSM_GROUPS * SSM_STATE
D_FF = 4096
D_IN_PROJ = 1280
EPS = 1e-6
NEG = -1e30
LOG2E = math.log2(math.e)

LANES = 128
SUBLANES = 8
MXU_DIM = 256
VMEM_LIMIT = 56 * 1024 * 1024

F32 = jnp.float32
BF16 = jnp.bfloat16

TM_A = 512
TM_B = 512
CHUNK = TM_B // SUBLANES
PITCH = CHUNK + SUBLANES
S_SEQ = 16
NEW_ROWS = 16


def _rms(x, g):
    return x * lax.rsqrt(jnp.mean(x * x, axis=-1, keepdims=True) + EPS) * g


def _rel_bucket_table():
    i = np.arange(BLOCK)[:, None]
    j = np.arange(2 * BLOCK)[None, :]
    n = np.maximum(i + BLOCK - j, 0)
    max_exact = N_BUCKETS // 2
    nf = np.maximum(n, max_exact).astype(np.float32)
    large = max_exact + (np.log(nf / np.float32(max_exact)) / np.float32(math.log(MAX_DISTANCE / max_exact))
                         * np.float32(N_BUCKETS - max_exact)).astype(np.int32)
    large = np.minimum(large, N_BUCKETS - 1)
    return np.where(n < max_exact, n, large).astype(np.int32)


def _build_bias(relb_ref, bucket_ref, bias_scr):
    bucket = bucket_ref[...]
    for h in range(N_Q_HEADS):
        bias_scr[h] = jnp.zeros(bucket.shape, F32)
    for b in range(N_BUCKETS):
        hit = bucket == b
        for h in range(N_Q_HEADS):
            bias_scr[h] = jnp.where(hit, relb_ref[b, h], bias_scr[h])


def _build_bias_t(relb_ref, bucket_t_ref, bias_t_scr):
    bucket_t = bucket_t_ref[...]
    bias_t_scr[...] = jnp.zeros_like(bias_t_scr)
    for b in range(N_BUCKETS):
        hit = bucket_t == b
        for h in range(N_Q_HEADS):
            cols = slice(h * BLOCK, (h + 1) * BLOCK)
            bias_t_scr[:, cols] = jnp.where(hit, relb_ref[b, h] * LOG2E, bias_t_scr[:, cols])
    kj = lax.broadcasted_iota(jnp.int32, (2 * BLOCK, BLOCK), 0)
    qi = lax.broadcasted_iota(jnp.int32, (2 * BLOCK, BLOCK), 1)
    dist = qi + BLOCK - kj
    in_band = jnp.logical_and(dist >= 0, dist < WINDOW)
    for h in range(N_Q_HEADS):
        cols = slice(h * BLOCK, (h + 1) * BLOCK)
        bias_t_scr[:, cols] = jnp.where(in_band, bias_t_scr[:, cols], NEG)


def _ssm_prep_kernel(are_ref, aim_ref, ls_ref, btr_ref, bti_ref,
                     lr_ref, li_ref, lcr_ref, lci_ref, bbr_ref, bbi_ref):
    ar = are_ref[...]
    ai = aim_ref[...]
    step = jnp.exp(ls_ref[...])
    zr = ar * step
    zi = ai * step
    mag = jnp.exp(zr)
    lr = mag * jnp.cos(zi)
    li = mag * jnp.sin(zi)
    pr, pi = lr, li
    for _ in range(int(math.log2(CHUNK))):
        pr, pi = pr * pr - pi * pi, 2.0 * pr * pi
    for g in range(N_SSM_GROUPS):
        st = slice(g * SSM_STATE, (g + 1) * SSM_STATE)
        lr_ref[:, st] = lr[g]
        li_ref[:, st] = li[g]
        lcr_ref[:, st] = pr[g]
        lci_ref[:, st] = pi[g]
    nr = lr - 1.0
    ni = li
    den = ar * ar + ai * ai
    cr = (nr * ar + ni * ai) / den
    ci = (ni * ar - nr * ai) / den
    btr = btr_ref[...]
    bti = bti_ref[...]
    bbr_ref[...] = cr * btr - ci * bti
    bbi_ref[...] = cr * bti + ci * btr


def _ssm_prep(a_re, a_im, log_step, b_re, b_im):
    g, p, c = N_SSM_GROUPS, SSM_STATE, SSM_GROUP
    vec = jax.ShapeDtypeStruct((1, STATE_LANES), F32)
    mat = jax.ShapeDtypeStruct((g, c, p), F32)
    return pl.pallas_call(
        _ssm_prep_kernel,
        out_shape=(vec, vec, vec, vec, mat, mat),
        name="ssm_prep",
    )(a_re.reshape(g, 1, p), a_im.reshape(g, 1, p), log_step.reshape(g, 1, 1),
      jnp.transpose(b_re, (0, 2, 1)), jnp.transpose(b_im, (0, 2, 1)))


def _stack_heads(q_blk, lane_lo):
    zero = jnp.zeros_like(q_blk[:, 0:LANES])
    lo, hi = [], []
    for c in range(ATTN_WIDTH // LANES):
        blk = q_blk[:, c * LANES:(c + 1) * LANES]
        lo.append(jnp.where(lane_lo, blk, zero))
        hi.append(jnp.where(lane_lo, zero, blk))
    return jnp.concatenate(lo + hi, axis=0)


def _unstack_heads(o, rows, lane_lo):
    nblk = ATTN_WIDTH // LANES
    return jnp.concatenate(
        [jnp.where(lane_lo, o[c * rows:(c + 1) * rows], o[(c + nblk) * rows:(c + nblk + 1) * rows])
         for c in range(nblk)], axis=-1)


Q_COLS = 256


def _proj_stages(x_ref, gmix_ref, win_ref, q_stage, k_stage, vt_stage, u_ref, newk_ref, newv_ref):
    xn = _rms(x_ref[...], gmix_ref[...]).astype(BF16)
    q_scale = HEAD_DIM ** -0.5 * LOG2E
    for c in range(ATTN_WIDTH // Q_COLS):
        cols = slice(c * Q_COLS, (c + 1) * Q_COLS)
        q = jnp.dot(xn, win_ref[:, cols], preferred_element_type=F32)
        q_stage[:, cols] = (q * q_scale).astype(BF16)
        yield
    kv = jnp.dot(xn, win_ref[:, ATTN_WIDTH:ATTN_WIDTH + 2 * KV_WIDTH], preferred_element_type=F32)
    k_stage[...] = kv[:, :KV_WIDTH].astype(BF16)
    vt_stage[...] = kv[:, KV_WIDTH:].T.astype(BF16)
    newk_ref[...] = kv[TM_A - WINDOW:, :KV_WIDTH]
    newv_ref[...] = kv[TM_A - WINDOW:, KV_WIDTH:]
    yield
    u0 = ATTN_WIDTH + 2 * KV_WIDTH
    for c in range(SSM_WIDTH // Q_COLS):
        cols = slice(u0 + c * Q_COLS, u0 + (c + 1) * Q_COLS)
        u_ref[:, c * Q_COLS:(c + 1) * Q_COLS] = jnp.dot(xn, win_ref[:, cols], preferred_element_type=F32)
        yield


def _attn_stages(sinks_ref, attn_ref, k_scr, vt_scr, q_scr, bias_t_scr, gt_scr, first_tile):
    nstk = N_Q_HEADS * BLOCK
    nblk = ATTN_WIDTH // LANES
    lane_lo = lax.broadcasted_iota(jnp.int32, (BLOCK, LANES), 1) < HEAD_DIM
    row_lo = lax.broadcasted_iota(jnp.int32, (KV_WIDTH, BLOCK), 0) < HEAD_DIM
    kj = lax.broadcasted_iota(jnp.int32, (2 * BLOCK, nstk), 0)
    has_prev = jnp.logical_or(kj >= BLOCK, jnp.logical_not(first_tile))
    head = lax.broadcasted_iota(jnp.int32, (1, nstk), 1) // BLOCK
    sink = jnp.zeros((1, nstk), F32)
    for h in range(N_Q_HEADS):
        sink = jnp.where(head == h, sinks_ref[0, h] * LOG2E, sink)

    scores = []
    for j in range(TM_A // BLOCK):
        r0 = j * BLOCK
        q2 = _stack_heads(q_scr[r0:r0 + BLOCK, :], lane_lo)
        keys = k_scr[r0:r0 + 2 * BLOCK, :]
        scores.append(lax.dot_general(keys, q2, (((1,), (1,)), ((), ())), preferred_element_type=F32))
        yield
    for j in range(TM_A // BLOCK):
        r0 = j * BLOCK
        s = scores[j] + bias_t_scr[...]
        if j == 0:
            s = jnp.where(has_prev, s, NEG)
        m = jnp.maximum(jnp.max(s, axis=0, keepdims=True), sink)
        p = jnp.exp2(s - m)
        denom = jnp.sum(p, axis=0, keepdims=True) + jnp.exp2(sink - m)
        o = jnp.dot(vt_scr[:, r0:r0 + 2 * BLOCK], p.astype(BF16), preferred_element_type=F32) / denom
        a = jnp.concatenate(
            [jnp.where(row_lo, o[:, c * BLOCK:(c + 1) * BLOCK], o[:, (c + nblk) * BLOCK:(c + nblk + 1) * BLOCK])
             for c in range(nblk)], axis=0)
        a = a * lax.rsqrt(jnp.mean(a * a, axis=0, keepdims=True) + EPS) * gt_scr[...]
        attn_ref[r0:r0 + BLOCK, :] = a.T.astype(BF16)
        yield


def _in_attn_kernel(relb_ref, sinks_ref, x_ref, gmix_ref, win_ref, bucket_t_ref, gattn_ref,
                    wout_f, wglu_f, wup_f, wdown_f,
                    attn_ref, u_ref, newk_ref, newv_ref, wout_b, wglu_b, wup_b, wdown_b,
                    k_scr, vt_scr, q_scr, k_stage, vt_stage, q_stage, bias_t_scr, gt_scr):
    b = pl.program_id(0)
    t = pl.program_id(1)

    @pl.when(jnp.logical_and(b == 0, t == 0))
    def _():
        _build_bias_t(relb_ref, bucket_t_ref, bias_t_scr)
        gt_scr[...] = jnp.broadcast_to(gattn_ref[...], (ATTN_WIDTH, BLOCK))
        k_scr[...] = jnp.zeros_like(k_scr)
        vt_scr[...] = jnp.zeros_like(vt_scr)
        k_stage[...] = jnp.zeros_like(k_stage)
        vt_stage[...] = jnp.zeros_like(vt_stage)
        q_stage[...] = jnp.zeros_like(q_stage)

    k_scr[0:BLOCK, :] = k_scr[TM_A:TM_A + BLOCK, :]
    vt_scr[:, 0:BLOCK] = vt_scr[:, TM_A:TM_A + BLOCK]
    k_scr[BLOCK:BLOCK + TM_A, :] = k_stage[...]
    vt_scr[:, BLOCK:BLOCK + TM_A] = vt_stage[...]
    q_scr[...] = q_stage[...]

    wout_b[...] = wout_f[...].astype(BF16)
    wglu_b[...] = wglu_f[...].astype(BF16)
    wup_b[...] = wup_f[...].astype(BF16)
    wdown_b[...] = wdown_f[...].astype(BF16)

    proj = _proj_stages(x_ref, gmix_ref, win_ref, q_stage, k_stage, vt_stage, u_ref, newk_ref, newv_ref)
    attn = _attn_stages(sinks_ref, attn_ref, k_scr, vt_scr, q_scr, bias_t_scr, gt_scr, t <= 1)
    nb = TM_A // BLOCK
    order = ("p" + "a" * (nb // 2)) * 2 + "p" + "a" * (nb // 4) + "p" + "a" * (nb // 2) + "p" + "a" * (nb - nb // 4 - nb // 2)
    for stage in order:
        next(proj if stage == "p" else attn)
    assert next(proj, None) is None and next(attn, None) is None


def _const_spec(shape):
    zeros = (0,) * len(shape)
    return pl.BlockSpec(shape, lambda *_: zeros, pipeline_mode=pl.Buffered(1))


def _smem_spec():
    return pl.BlockSpec(memory_space=pltpu.SMEM)


def _in_attn(x, rel_bias, sinks, g_mix, w_in, bucket_t, g_attn_col, w_out, w_glu, w_up, w_down):
    bsz, seq, _ = x.shape
    nt = seq // TM_A
    steps = bsz * nt
    cur = lambda width: pl.BlockSpec((None, TM_A, width), lambda b, t: (b, jnp.minimum(t, nt - 1), 0))
    prev = lambda width: pl.BlockSpec((None, TM_A, width), lambda b, t: (b, jnp.maximum(t - 1, 0), 0))
    win_spec = pl.BlockSpec((None, WINDOW, KV_WIDTH), lambda b, t: (b, 0, 0))

    def tile(b, t):
        return b * nt + jnp.minimum(t, nt - 1)

    def rows_of(w):
        blk = w.shape[0] // steps
        return pl.BlockSpec((blk, w.shape[1]), lambda b, t: (tile(b, t), 0))

    nhead_blk = D_MODEL // HEAD_DIM

    def wout_src(b, t):
        return (jnp.minimum(tile(b, t), nhead_blk - 1), 0)

    def wout_dst(b, t):
        j = jnp.minimum(tile(b, t), nhead_blk - 1)
        return (jnp.where(j < N_Q_HEADS, (j % Q_PER_KV) * N_KV_HEADS + j // Q_PER_KV, j), 0)

    assert steps >= nhead_blk and w_up.shape[0] % steps == 0 and w_down.shape[0] % steps == 0
    wout_in = pl.BlockSpec((HEAD_DIM, D_MODEL), wout_src)
    wout_out = pl.BlockSpec((HEAD_DIM, D_MODEL), wout_dst)
    cast_shape = lambda w: jax.ShapeDtypeStruct(w.shape, BF16)
    return pl.pallas_call(
        _in_attn_kernel,
        out_shape=(jax.ShapeDtypeStruct((bsz, seq, ATTN_WIDTH), BF16),
                   jax.ShapeDtypeStruct((bsz, seq, SSM_WIDTH), F32),
                   jax.ShapeDtypeStruct((bsz, WINDOW, KV_WIDTH), F32),
                   jax.ShapeDtypeStruct((bsz, WINDOW, KV_WIDTH), F32),
                   cast_shape(w_out), cast_shape(w_glu), cast_shape(w_up), cast_shape(w_down)),
        grid=(bsz, nt + 1),
        in_specs=[_smem_spec(), _smem_spec(), cur(D_MODEL), _const_spec((1, D_MODEL)),
                  _const_spec((D_MODEL, D_IN_PROJ)), _const_spec((2 * BLOCK, BLOCK)),
                  _const_spec((ATTN_WIDTH, 1)),
                  wout_in, rows_of(w_glu), rows_of(w_up), rows_of(w_down)],
        out_specs=(prev(ATTN_WIDTH), cur(SSM_WIDTH), win_spec, win_spec,
                   wout_out, rows_of(w_glu), rows_of(w_up), rows_of(w_down)),
        scratch_shapes=[pltpu.VMEM((TM_A + BLOCK, KV_WIDTH), BF16),
                        pltpu.VMEM((KV_WIDTH, TM_A + BLOCK), BF16),
                        pltpu.VMEM((TM_A, ATTN_WIDTH), BF16),
                        pltpu.VMEM((TM_A, KV_WIDTH), BF16),
                        pltpu.VMEM((KV_WIDTH, TM_A), BF16),
                        pltpu.VMEM((TM_A, ATTN_WIDTH), BF16),
                        pltpu.VMEM((2 * BLOCK, N_Q_HEADS * BLOCK), F32),
                        pltpu.VMEM((ATTN_WIDTH, BLOCK), F32)],
        compiler_params=pltpu.CompilerParams(
            dimension_semantics=("arbitrary", "arbitrary"), vmem_limit_bytes=VMEM_LIMIT),
        name="in_attn",
    )(rel_bias, sinks, x, g_mix, w_in, bucket_t, g_attn_col, w_out, w_glu, w_up, w_down)


def _s_in_attn_kernel(relb_ref, sinks_ref, x_ref, gmix_ref, win_ref, bucket_ref, gattn_ref,
                      ck_ref, cv_ref,
                      attn_ref, u_ref, newk_ref, newv_ref,
                      bias_scr, biasd_scr, proj_scr):
    rows = attn_ref.shape[0]
    steps = rows // S_SEQ
    pair = SUBLANES // steps
    prow = N_Q_HEADS * SUBLANES

    @pl.when(pl.program_id(0) == 0)
    def _():
        _build_bias(relb_ref, bucket_ref, bias_scr)
        for h in range(N_Q_HEADS):
            for k in range(pair):
                biasd_scr[h * SUBLANES + k * steps:h * SUBLANES + (k + 1) * steps, :] = bias_scr[h, 0:steps, :]
        xn = _rms(x_ref[...], gmix_ref[...]).astype(BF16)
        proj_all = jnp.dot(xn, win_ref[...], preferred_element_type=F32)
        proj_scr[...] = proj_all
        u_ref[...] = proj_all[:, ATTN_WIDTH + 2 * KV_WIDTH:]

    r0 = pl.multiple_of(pl.program_id(0) * rows, rows)
    proj = proj_scr[pl.ds(r0, rows), :]
    qs = proj[:, :ATTN_WIDTH] * (HEAD_DIM ** -0.5)

    row = lax.broadcasted_iota(jnp.int32, (prow, 1), 0)
    qseq = (row % SUBLANES) // steps
    qrow = row % steps
    kcol = lax.broadcasted_iota(jnp.int32, (prow, pair * WINDOW), 1)
    valid_c = jnp.logical_and(kcol // WINDOW == qseq, kcol % WINDOW > qrow)
    kcol_n = lax.broadcasted_iota(jnp.int32, (prow, NEW_ROWS), 1)
    valid_n = jnp.logical_and(jnp.logical_and(kcol_n < SUBLANES, kcol_n // steps == qseq),
                              kcol_n % steps <= qrow)
    sink = jnp.zeros((prow, 1), F32)
    for h in range(N_Q_HEADS):
        sink = jnp.where(row // SUBLANES == h, sinks_ref[0, h], sink)
    lane_lo = lax.broadcasted_iota(jnp.int32, (SUBLANES, LANES), 1) < HEAD_DIM
    gattn = gattn_ref[...]
    pad = jnp.zeros((NEW_ROWS - SUBLANES, KV_WIDTH), F32)
    bias_c = jnp.concatenate([biasd_scr[:, 0:WINDOW]] * pair, axis=1)
    bias_new = biasd_scr[:, WINDOW:2 * WINDOW]
    bias_n = bias_new
    lane = lax.broadcasted_iota(jnp.int32, (prow, WINDOW), 1)
    for k in range(1, pair):
        bias_n = jnp.where(lane >= k * steps, pltpu.roll(bias_new, k * steps, axis=1), bias_n)
    bias_n = bias_n[:, 0:NEW_ROWS]
    nt_dims = (((1,), (1,)), ((), ()))

    tiles = []
    for tl in range(S_SEQ // pair):
        rws = slice(tl * SUBLANES, (tl + 1) * SUBLANES)
        q2 = _stack_heads(qs[rws], lane_lo).astype(BF16)
        kn = proj[rws, ATTN_WIDTH:ATTN_WIDTH + KV_WIDTH]
        vn = proj[rws, ATTN_WIDTH + KV_WIDTH:ATTN_WIDTH + 2 * KV_WIDTH]
        cks, cvs = [], []
        for k in range(pair):
            sq = tl * pair + k
            ck = ck_ref[sq]
            cv = cv_ref[sq]
            newk_ref[sq, 0:WINDOW - steps, :] = ck[steps:, :]
            newk_ref[sq, WINDOW - steps:, :] = kn[k * steps:(k + 1) * steps]
            newv_ref[sq, 0:WINDOW - steps, :] = cv[steps:, :]
            newv_ref[sq, WINDOW - steps:, :] = vn[k * steps:(k + 1) * steps]
            cks.append(ck.astype(BF16))
            cvs.append(cv.astype(BF16))
        kn_b = jnp.concatenate([kn, pad], axis=0).astype(BF16)
        vn_b = jnp.concatenate([vn, pad], axis=0).astype(BF16)
        s_c = lax.dot_general(q2, jnp.concatenate(cks, axis=0), nt_dims, preferred_element_type=F32)
        s_n = lax.dot_general(q2, kn_b, nt_dims, preferred_element_type=F32)
        s_c = jnp.where(valid_c, s_c + bias_c, NEG)
        s_n = jnp.where(valid_n, s_n + bias_n, NEG)
        m = jnp.maximum(jnp.maximum(jnp.max(s_c, axis=-1, keepdims=True),
                                    jnp.max(s_n, axis=-1, keepdims=True)), sink)
        p_c = jnp.exp(s_c - m)
        p_n = jnp.exp(s_n - m)
        denom = (jnp.sum(p_c, axis=-1, keepdims=True) + jnp.sum(p_n, axis=-1, keepdims=True)
                 + jnp.exp(sink - m))
        o = (jnp.dot(p_c.astype(BF16), jnp.concatenate(cvs, axis=0), preferred_element_type=F32)
             + jnp.dot(p_n.astype(BF16), vn_b, preferred_element_type=F32)) / denom
        tiles.append(_unstack_heads(o, SUBLANES, lane_lo))
    attn_ref[...] = _rms(jnp.concatenate(tiles, axis=0), gattn).astype(BF16)


def _s_in_attn(x, rel_bias, sinks, g_mix, w_in, bucket, g_attn, cache_k, cache_v):
    nseq, steps, _ = x.shape
    rows = S_SEQ * steps
    xf = x.reshape(nseq * steps, D_MODEL)
    row = lambda width: pl.BlockSpec((rows, width), lambda i: (i, 0))
    cache = pl.BlockSpec((S_SEQ, WINDOW, KV_WIDTH), lambda i: (i, 0, 0))
    return pl.pallas_call(
        _s_in_attn_kernel,
        out_shape=(jax.ShapeDtypeStruct((nseq * steps, ATTN_WIDTH), BF16),
                   jax.ShapeDtypeStruct((nseq * steps, SSM_WIDTH), F32),
                   jax.ShapeDtypeStruct((nseq, WINDOW, KV_WIDTH), F32),
                   jax.ShapeDtypeStruct((nseq, WINDOW, KV_WIDTH), F32)),
        grid=(nseq // S_SEQ,),
        in_specs=[_smem_spec(), _smem_spec(), _const_spec((nseq * steps, D_MODEL)), _const_spec((1, D_MODEL)),
                  _const_spec((D_MODEL, D_IN_PROJ)), _const_spec((SUBLANES, 2 * BLOCK)),
                  _const_spec((1, ATTN_WIDTH)), cache, cache],
        out_specs=(row(ATTN_WIDTH), pl.BlockSpec((nseq * steps, SSM_WIDTH), lambda i: (0, 0)), cache, cache),
        scratch_shapes=[pltpu.VMEM((N_Q_HEADS, SUBLANES, 2 * BLOCK), F32),
                        pltpu.VMEM((N_Q_HEADS * SUBLANES, 2 * WINDOW), F32),
                        pltpu.VMEM((nseq * steps, D_IN_PROJ), F32)],
        compiler_params=pltpu.CompilerParams(
            dimension_semantics=("arbitrary",), vmem_limit_bytes=VMEM_LIMIT),
        name="s_in_attn",
    )(rel_bias, sinks, xf, g_mix, w_in, bucket[0:SUBLANES], g_attn, cache_k, cache_v)


def _embed_blocks(bbr_ref, bbi_ref, cre_in, cim_in, bre, bim, cre, cim):
    for ref in (bre, bim, cre, cim):
        ref[...] = jnp.zeros_like(ref)
    for g in range(N_SSM_GROUPS):
        sl, j = divmod(g, SLAB // SSM_STATE)
        ch = slice(j * SSM_GROUP, (j + 1) * SSM_GROUP)
        st = slice(j * SSM_STATE, (j + 1) * SSM_STATE)
        bre[sl, ch, st] = bbr_ref[g].astype(BF16)
        bim[sl, ch, st] = bbi_ref[g].astype(BF16)
        cre[sl, ch, st] = cre_in[g].astype(BF16)
        cim[sl, ch, st] = cim_in[g].astype(BF16)


def _output_map(h_slab_re, h_slab_im, c_slab_re, c_slab_im):
    nt_dims = (((1,), (1,)), ((), ()))
    yr = lax.dot_general(h_slab_re.astype(BF16), c_slab_re, nt_dims, preferred_element_type=F32)
    yi = lax.dot_general(h_slab_im.astype(BF16), c_slab_im, nt_dims, preferred_element_type=F32)
    return yr - yi


def _glu_norm(y, wglu_ref, bglu_ref, gssm_ref):
    g = jax.nn.gelu(y)
    gate = jax.nn.sigmoid(jnp.dot(g.astype(BF16), wglu_ref[...], preferred_element_type=F32) + bglu_ref[...])
    return _rms(g * gate, gssm_ref[...])


SLAB = 4 * LANES
SLAB_CH = SLAB // SSM_STATE * SSM_GROUP


REC_PIECE = 16
EPI_ROWS = 128
FF_BLK = 1024


def _scan_stages(u_ref, lr_ref, li_ref, lcr_ref, lci_ref, d_ref, wglu_ref, bglu_ref, gssm_ref, out_ref,
                 bre_ref, bim_ref, cre_ref, cim_ref,
                 pad_scr, up_scr, y_scr, hre, him, carry_re, carry_im, e_re, e_im, hc_re, hc_im):
    nslab = SSM_WIDTH // LANES
    nslabs = STATE_LANES // SLAB

    for c in range(SUBLANES):
        for s in range(nslab):
            pad_scr[s, c * PITCH:c * PITCH + CHUNK, :] = u_ref[c * CHUNK:(c + 1) * CHUNK, s * LANES:(s + 1) * LANES]
    yield
    for tau in range(CHUNK):
        for s in range(nslab):
            up_scr[tau * SUBLANES:(tau + 1) * SUBLANES, s * LANES:(s + 1) * LANES] = (
                pad_scr[s, pl.ds(tau, SUBLANES, stride=PITCH), :])
        if tau % REC_PIECE == REC_PIECE - 1:
            yield

    for sl in range(nslabs):
        cols = slice(sl * SLAB, (sl + 1) * SLAB)
        ch = slice(sl * SLAB_CH, (sl + 1) * SLAB_CH)
        upb = up_scr[:, ch].astype(BF16)
        hre[:, cols] = jnp.dot(upb, bre_ref[sl], preferred_element_type=F32)
        yield
        him[:, cols] = jnp.dot(upb, bim_ref[sl], preferred_element_type=F32)
        yield

    for sl in range(nslabs):
        cols = slice(sl * SLAB, (sl + 1) * SLAB)
        ch = slice(sl * SLAB_CH, (sl + 1) * SLAB_CH)
        ar = jnp.broadcast_to(lr_ref[:, cols], (SUBLANES, SLAB))
        ai = jnp.broadcast_to(li_ref[:, cols], (SUBLANES, SLAB))

        hr = jnp.zeros((SUBLANES, SLAB), F32)
        hi = jnp.zeros((SUBLANES, SLAB), F32)
        for tau in range(CHUNK):
            rws = slice(tau * SUBLANES, (tau + 1) * SUBLANES)
            hr, hi = ar * hr - ai * hi + hre[rws, cols], ar * hi + ai * hr + him[rws, cols]
            if tau % REC_PIECE == REC_PIECE - 1:
                yield
        e_re[:, cols] = hr
        e_im[:, cols] = hi

        cr = carry_re[:, cols]
        ci = carry_im[:, cols]
        lcr = lcr_ref[:, cols]
        lci = lci_ref[:, cols]
        for c in range(SUBLANES):
            hc_re[c:c + 1, cols] = cr
            hc_im[c:c + 1, cols] = ci
            er = e_re[c:c + 1, cols]
            ei = e_im[c:c + 1, cols]
            cr, ci = lcr * cr - lci * ci + er, lcr * ci + lci * cr + ei
        carry_re[:, cols] = cr
        carry_im[:, cols] = ci
        yield

        hr = hc_re[:, cols]
        hi = hc_im[:, cols]
        for tau in range(CHUNK):
            rws = slice(tau * SUBLANES, (tau + 1) * SUBLANES)
            hr, hi = ar * hr - ai * hi + hre[rws, cols], ar * hi + ai * hr + him[rws, cols]
            hre[rws, cols] = hr
            him[rws, cols] = hi
            if tau % REC_PIECE == REC_PIECE - 1:
                yield

        y_scr[:, ch] = _output_map(hre[:, cols], him[:, cols], cre_ref[sl], cim_ref[sl])
        yield

    taus = EPI_ROWS // SUBLANES
    for r in range(TM_B // EPI_ROWS):
        rows = slice(r * EPI_ROWS, (r + 1) * EPI_ROWS)
        o = _glu_norm(y_scr[rows, :] + d_ref[...] * up_scr[rows, :], wglu_ref, bglu_ref, gssm_ref)
        for i in range(taus):
            for s in range(nslab):
                pad_scr[s, pl.ds(r * taus + i, SUBLANES, stride=PITCH), :] = (
                    o[i * SUBLANES:(i + 1) * SUBLANES, s * LANES:(s + 1) * LANES])
        yield
    for c in range(SUBLANES):
        for s in range(nslab):
            out_ref[c * CHUNK:(c + 1) * CHUNK, s * LANES:(s + 1) * LANES] = (
                pad_scr[s, c * PITCH:c * PITCH + CHUNK, :].astype(BF16))
    yield


N_SCAN_STAGES = (1 + CHUNK // REC_PIECE + 2 * (STATE_LANES // SLAB)
                 + (STATE_LANES // SLAB) * (2 * (CHUNK // REC_PIECE) + 2) + TM_B // EPI_ROWS + 1)


def _mlp_stages(x_cols, attn, s_ref, wout_ref, gmlp_ref, wup_ref, wdown_ref, gfin_ref, y_ref, hm_scr, act_scr):
    ncol = D_MODEL // Q_COLS
    for n in range(ncol):
        cols = slice(n * Q_COLS, (n + 1) * Q_COLS)
        y_ref[:, cols] = (x_cols(cols)
                          + jnp.dot(attn(), wout_ref[0:ATTN_WIDTH, cols], preferred_element_type=F32)
                          + jnp.dot(s_ref[...], wout_ref[ATTN_WIDTH:, cols], preferred_element_type=F32))
        yield
    hm_scr[...] = _rms(y_ref[...], gmlp_ref[...]).astype(BF16)
    for c in range(D_FF // FF_BLK):
        act = act_scr.at[c % 2]
        for n in range(FF_BLK // Q_COLS):
            cols = slice(c * FF_BLK + n * Q_COLS, c * FF_BLK + (n + 1) * Q_COLS)
            up = jnp.dot(hm_scr[...], wup_ref[:, cols], preferred_element_type=F32)
            act[:, n * Q_COLS:(n + 1) * Q_COLS] = jnp.square(jnp.maximum(up, 0.0)).astype(BF16)
            yield
        for n in range(ncol):
            cols = slice(n * Q_COLS, (n + 1) * Q_COLS)
            y_ref[:, cols] += jnp.dot(act[...], wdown_ref[c * FF_BLK:(c + 1) * FF_BLK, cols],
                                      preferred_element_type=F32)
            yield
    y_ref[...] = _rms(y_ref[...], gfin_ref[...])
    yield


N_MLP_STAGES = D_MODEL // Q_COLS + (D_FF // FF_BLK) * (FF_BLK // Q_COLS + D_MODEL // Q_COLS) + 1


def _even_merge(n_a, n_b):
    keys = [((k + 0.5) / n_a, "a") for k in range(n_a)] + [((k + 0.5) / n_b, "b") for k in range(n_b)]
    return [name for _, name in sorted(keys)]


def _scan_mlp_kernel(x_ref, a_ref, u_ref, xdec_ref, adec_ref, sdec_ref,
                     lr_ref, li_ref, lcr_ref, lci_ref, bbr_ref, bbi_ref, cre_in, cim_in,
                     d_ref, wglu_ref, bglu_ref, gssm_ref, wout_ref, gmlp_ref, wup_ref, wdown_ref, gfin_ref,
                     y_ref, ydec_ref, hre_out, him_out,
                     ssm_scr, hm_scr, act_scr, bre_ref, bim_ref, cre_ref, cim_ref,
                     pad_scr, up_scr, y_scr, hre, him, carry_re, carry_im, e_re, e_im, hc_re, hc_im, *, nt):
    i = pl.program_id(0)
    dec = i == 0

    @pl.when(dec)
    def _():
        _embed_blocks(bbr_ref, bbi_ref, cre_in, cim_in, bre_ref, bim_ref, cre_ref, cim_ref)
        ssm_scr[...] = sdec_ref[...]

    @pl.when(i % nt == 0)
    def _():
        carry_re[...] = jnp.zeros_like(carry_re)
        carry_im[...] = jnp.zeros_like(carry_im)

    mlp = _mlp_stages(lambda cols: jnp.where(dec, xdec_ref[:, cols], x_ref[:, cols]),
                      lambda: jnp.where(dec, adec_ref[...], a_ref[...]),
                      ssm_scr, wout_ref, gmlp_ref, wup_ref, wdown_ref, gfin_ref, y_ref, hm_scr, act_scr)
    scan = _scan_stages(u_ref, lr_ref, li_ref, lcr_ref, lci_ref, d_ref, wglu_ref, bglu_ref, gssm_ref, ssm_scr,
                        bre_ref, bim_ref, cre_ref, cim_ref,
                        pad_scr, up_scr, y_scr, hre, him, carry_re, carry_im, e_re, e_im, hc_re, hc_im)
    for stage in _even_merge(N_MLP_STAGES, N_SCAN_STAGES):
        next(mlp if stage == "a" else scan)
    assert next(scan, None) is None and next(mlp, None) is None

    @pl.when(dec)
    def _():
        ydec_ref[...] = y_ref[...]

    @pl.when(i % nt == nt - 1)
    def _():
        hre_out[...] = carry_re[...]
        him_out[...] = carry_im[...]


def _scan_mlp(x, attn_n, u, x_dec, attn_dec, ssm_dec, lam, lam_chunk, bbar, c_gcp, d, w_glu, b_glu, g_ssm,
              w_out, g_mlp, w_up, w_down, g_fin):
    bsz, seq, _ = u.shape
    nt = seq // TM_B
    n = bsz * nt
    assert x_dec.shape[0] == TM_B, "the decode rows ride through the MLP as one tile"
    flat = lambda v: v.reshape(n * TM_B, v.shape[-1])
    cur = lambda width: pl.BlockSpec((TM_B, width), lambda i: (jnp.minimum(i, n - 1), 0))
    prev = lambda width: pl.BlockSpec((TM_B, width), lambda i: (jnp.maximum(i - 1, 0), 0))
    state = pl.BlockSpec((None, 1, STATE_LANES), lambda i: (jnp.minimum(i, n - 1) // nt, 0, 0))
    vec = _const_spec((1, STATE_LANES))
    chan = _const_spec((1, SSM_WIDTH))
    model = _const_spec((1, D_MODEL))
    dec = lambda width: _const_spec((TM_B, width))
    y, y_dec, h_re, h_im = pl.pallas_call(
        functools.partial(_scan_mlp_kernel, nt=nt),
        out_shape=(jax.ShapeDtypeStruct((n * TM_B, D_MODEL), F32),
                   jax.ShapeDtypeStruct((TM_B, D_MODEL), F32),
                   jax.ShapeDtypeStruct((bsz, 1, STATE_LANES), F32),
                   jax.ShapeDtypeStruct((bsz, 1, STATE_LANES), F32)),
        grid=(n + 1,),
        in_specs=[prev(D_MODEL), prev(ATTN_WIDTH), cur(SSM_WIDTH), dec(D_MODEL), dec(ATTN_WIDTH), dec(SSM_WIDTH),
                  vec, vec, vec, vec,
                  _group_blocks(), _group_blocks(), _group_blocks(), _group_blocks(),
                  chan, _const_spec((SSM_WIDTH, SSM_WIDTH)), chan, chan,
                  _const_spec((D_MODEL, D_MODEL)), model, _const_spec((D_MODEL, D_FF)),
                  _const_spec((D_FF, D_MODEL)), model],
        out_specs=(prev(D_MODEL), pl.BlockSpec((TM_B, D_MODEL), lambda i: (0, 0)), state, state),
        scratch_shapes=[pltpu.VMEM((TM_B, SSM_WIDTH), BF16),
                        pltpu.VMEM((TM_B, D_MODEL), BF16),
                        pltpu.VMEM((2, TM_B, FF_BLK), BF16)] + _block_operand_scratch() + [
                        pltpu.VMEM((SSM_WIDTH // LANES, SUBLANES * PITCH, LANES), F32),
                        pltpu.VMEM((TM_B, SSM_WIDTH), F32),
                        pltpu.VMEM((TM_B, SSM_WIDTH), F32),
                        pltpu.VMEM((TM_B, STATE_LANES), F32),
                        pltpu.VMEM((TM_B, STATE_LANES), F32),
                        pltpu.VMEM((1, STATE_LANES), F32),
                        pltpu.VMEM((1, STATE_LANES), F32),
                        pltpu.VMEM((SUBLANES, STATE_LANES), F32),
                        pltpu.VMEM((SUBLANES, STATE_LANES), F32),
                        pltpu.VMEM((SUBLANES, STATE_LANES), F32),
                        pltpu.VMEM((SUBLANES, STATE_LANES), F32)],
        compiler_params=pltpu.CompilerParams(
            dimension_semantics=("arbitrary",), vmem_limit_bytes=VMEM_LIMIT),
        name="scan_mlp",
    )(flat(x), flat(attn_n), flat(u), x_dec, attn_dec, ssm_dec, lam[0], lam[1], lam_chunk[0], lam_chunk[1],
      bbar[0], bbar[1], c_gcp[0], c_gcp[1], d, w_glu, b_glu, g_ssm, w_out, g_mlp, w_up, w_down, g_fin)
    return y.reshape(bsz, seq, D_MODEL), y_dec, h_re, h_im


def _block_operand_scratch():
    return [pltpu.VMEM((STATE_LANES // SLAB, SLAB_CH, SLAB), BF16) for _ in range(4)]


def _group_blocks():
    return _const_spec((N_SSM_GROUPS, SSM_GROUP, SSM_STATE))


def _s_scan_kernel(u_ref, h0r_ref, h0i_ref, lr_ref, li_ref, bbr_ref, bbi_ref, cre_in, cim_in,
                   d_ref, wglu_ref, bglu_ref, gssm_ref,
                   out_ref, hre_out, him_out,
                   bre_ref, bim_ref, cre_ref, cim_ref,
                   pad_scr, up_scr, hre, him):
    nseq = h0r_ref.shape[0]
    steps = u_ref.shape[0] // nseq
    nslab = SSM_WIDTH // LANES
    _embed_blocks(bbr_ref, bbi_ref, cre_in, cim_in, bre_ref, bim_ref, cre_ref, cim_ref)

    for s in range(nslab):
        pad_scr[s] = u_ref[:, s * LANES:(s + 1) * LANES]
    for i in range(steps):
        for s in range(nslab):
            up_scr[i * nseq:(i + 1) * nseq, s * LANES:(s + 1) * LANES] = (
                pad_scr[s, pl.ds(i, nseq, stride=steps), :])
    up = up_scr[...]
    upb = up.astype(BF16)

    ys = []
    for sl in range(STATE_LANES // SLAB):
        cols = slice(sl * SLAB, (sl + 1) * SLAB)
        ch = slice(sl * SLAB_CH, (sl + 1) * SLAB_CH)
        hre[:, cols] = jnp.dot(upb[:, ch], bre_ref[sl], preferred_element_type=F32)
        him[:, cols] = jnp.dot(upb[:, ch], bim_ref[sl], preferred_element_type=F32)
        ar = lr_ref[:, cols]
        ai = li_ref[:, cols]
        hr = h0r_ref[:, cols]
        hi = h0i_ref[:, cols]
        for i in range(steps):
            rws = slice(i * nseq, (i + 1) * nseq)
            hr, hi = ar * hr - ai * hi + hre[rws, cols], ar * hi + ai * hr + him[rws, cols]
            hre[rws, cols] = hr
            him[rws, cols] = hi
        hre_out[:, cols] = hr
        him_out[:, cols] = hi
        ys.append(_output_map(hre[:, cols], him[:, cols], cre_ref[sl], cim_ref[sl]))

    y = jnp.concatenate(ys, axis=-1) + d_ref[...] * up
    o = _glu_norm(y, wglu_ref, bglu_ref, gssm_ref)

    for i in range(steps):
        for s in range(nslab):
            pad_scr[s, pl.ds(i, nseq, stride=steps), :] = o[i * nseq:(i + 1) * nseq, s * LANES:(s + 1) * LANES]
    for s in range(nslab):
        out_ref[:, s * LANES:(s + 1) * LANES] = pad_scr[s].astype(BF16)


def _s_scan(u, h0_re, h0_im, lam, bbar, c_gcp, d, w_glu, b_glu, g_ssm):
    rows = u.shape[0]
    nseq = h0_re.shape[0]
    return pl.pallas_call(
        _s_scan_kernel,
        out_shape=(jax.ShapeDtypeStruct((rows, SSM_WIDTH), BF16),
                   jax.ShapeDtypeStruct((nseq, STATE_LANES), F32),
                   jax.ShapeDtypeStruct((nseq, STATE_LANES), F32)),
        scratch_shapes=_block_operand_scratch() + [
                        pltpu.VMEM((SSM_WIDTH // LANES, rows, LANES), F32),
                        pltpu.VMEM((rows, SSM_WIDTH), F32),
                        pltpu.VMEM((rows, STATE_LANES), F32),
                        pltpu.VMEM((rows, STATE_LANES), F32)],
        compiler_params=pltpu.CompilerParams(vmem_limit_bytes=VMEM_LIMIT),
        name="s_scan",
    )(u, h0_re, h0_im, lam[0], lam[1], bbar[0], bbar[1], c_gcp[0], c_gcp[1], d, w_glu, b_glu, g_ssm)


def _pair_heads(w, axis):
    shape = w.shape
    split = shape[:axis] + (N_KV_HEADS, Q_PER_KV, HEAD_DIM) + shape[axis + 1:]
    return jnp.swapaxes(w.reshape(split), axis, axis + 1).reshape(shape)


def kernel(x_prompt, x_sample, cache_k, cache_v, state_ssm_re, state_ssm_im, rel_bias, norm_mix, w_in, attn_sinks, ssm_a_re, ssm_a_im, ssm_log_step, ssm_b_re, ssm_b_im, ssm_c_re, ssm_c_im, ssm_d, w_glu, b_glu, norm_attn_out, norm_ssm_out, w_out, norm_mlp, w_up, w_down, norm_final):
    depth = w_in.shape[0]
    bsz, seq, _ = x_prompt.shape
    nseq, steps, _ = x_sample.shape
    bucket = jnp.asarray(_rel_bucket_table())
    g_fin = norm_final.reshape(1, D_MODEL)
    assert depth == 1, "the final norm is fused into the (single) layer's MLP kernel"

    xp, xs = x_prompt, x_sample
    outs = [[] for _ in range(8)]
    for l in range(depth):
        lr, li, lcr, lci, bbr, bbi = _ssm_prep(ssm_a_re[l], ssm_a_im[l], ssm_log_step[l], ssm_b_re[l], ssm_b_im[l])
        lam = (lr, li)
        lam_chunk = (lcr, lci)
        bbar = (bbr, bbi)
        c_gcp = (ssm_c_re[l], ssm_c_im[l])
        g_mix = norm_mix[l].reshape(1, D_MODEL)
        g_attn = _pair_heads(norm_attn_out[l].reshape(1, ATTN_WIDTH), axis=1)
        g_ssm = norm_ssm_out[l].reshape(1, SSM_WIDTH)
        g_mlp = norm_mlp[l].reshape(1, D_MODEL)
        sinks = attn_sinks[l].reshape(1, N_Q_HEADS)
        d = ssm_d[l].reshape(1, SSM_WIDTH)
        bg = b_glu[l].reshape(1, SSM_WIDTH)
        w_in_b = jnp.concatenate([_pair_heads(w_in[l][:, :ATTN_WIDTH], axis=1), w_in[l][:, ATTN_WIDTH:]],
                                 axis=1).astype(BF16)

        attn_n, u, k1, v1, w_out_b, w_glu_b, w_up_b, w_down_b = _in_attn(
            xp, rel_bias, sinks, g_mix, w_in_b, bucket.T, g_attn.reshape(ATTN_WIDTH, 1),
            w_out[l], w_glu[l], w_up[l], w_down[l])

        ck = cache_k[l].reshape(nseq, WINDOW, KV_WIDTH)
        cv = cache_v[l].reshape(nseq, WINDOW, KV_WIDTH)
        s_attn, s_u, k2, v2 = _s_in_attn(xs, rel_bias, sinks, g_mix, w_in_b, bucket, g_attn, ck, cv)
        s_ssm, r2, i2 = _s_scan(s_u, state_ssm_re[l].reshape(nseq, STATE_LANES),
                                state_ssm_im[l].reshape(nseq, STATE_LANES),
                                lam, bbar, c_gcp, d, w_glu_b, bg, g_ssm)

        xp, xs, r1, i1 = _scan_mlp(xp, attn_n, u, xs.reshape(nseq * steps, D_MODEL), s_attn, s_ssm,
                                   lam, lam_chunk, bbar, c_gcp, d, w_glu_b, bg, g_ssm,
                                   w_out_b, g_mlp, w_up_b, w_down_b, g_fin)
        xs = xs.reshape(nseq, steps, D_MODEL)

        kv_shape = (WINDOW, N_KV_HEADS, HEAD_DIM)
        st_shape = (N_SSM_GROUPS, SSM_STATE)
        for lst, val in zip(outs, (k1.reshape(bsz, *kv_shape), v1.reshape(bsz, *kv_shape),
                                   r1.reshape(bsz, *st_shape), i1.reshape(bsz, *st_shape),
                                   k2.reshape(nseq, *kv_shape), v2.reshape(nseq, *kv_shape),
                                   r2.reshape(nseq, *st_shape), i2.reshape(nseq, *st_shape))):
            lst.append(val)

    return (xp, xs) + tuple(jnp.stack(o) for o in outs)
```

```python
import functools
import math

import numpy as np
import jax
import jax.numpy as jnp
from jax import lax
from jax.experimental import pallas as pl
from jax.experimental.pallas import tpu as pltpu

D_MODEL = 1024
HEAD_DIM = 64
ATTN_WIDTH = 512
N_Q_HEADS = 8
N_KV_HEADS = 2
Q_PER_KV = 4
KV_WIDTH = 128
WINDOW = 128
BLOCK = 128
N_BUCKETS = 32
MAX_DISTANCE = 128
SSM_WIDTH = 512
SSM_GROUP = 16
N_SSM_GROUPS = 32
SSM_STATE = 64
STATE_LANES = N_SSM_GROUPS * SSM_STATE
D_FF = 4096
D_IN_PROJ = 1280
EPS = 1e-6
NEG = -1e30
LOG2E = math.log2(math.e)

LANES = 128
SUBLANES = 8
MXU_DIM = 256
VMEM_LIMIT = 56 * 1024 * 1024

F32 = jnp.float32
BF16 = jnp.bfloat16

TM_A = 512
TM_B = 512
CHUNK = TM_B // SUBLANES
PITCH = CHUNK + SUBLANES
S_SEQ = 16
NEW_ROWS = 16


def _rms(x, g):
    return x * lax.rsqrt(jnp.mean(x * x, axis=-1, keepdims=True) + EPS) * g


def _rel_bucket_table():
    i = np.arange(BLOCK)[:, None]
    j = np.arange(2 * BLOCK)[None, :]
    n = np.maximum(i + BLOCK - j, 0)
    max_exact = N_BUCKETS // 2
    nf = np.maximum(n, max_exact).astype(np.float32)
    large = max_exact + (np.log(nf / np.float32(max_exact)) / np.float32(math.log(MAX_DISTANCE / max_exact))
                         * np.float32(N_BUCKETS - max_exact)).astype(np.int32)
    large = np.minimum(large, N_BUCKETS - 1)
    return np.where(n < max_exact, n, large).astype(np.int32)


def _build_bias(relb_ref, bucket_ref, bias_scr):
    bucket = bucket_ref[...]
    for h in range(N_Q_HEADS):
        bias_scr[h] = jnp.zeros(bucket.shape, F32)
    for b in range(N_BUCKETS):
        hit = bucket == b
        for h in range(N_Q_HEADS):
            bias_scr[h] = jnp.where(hit, relb_ref[b, h], bias_scr[h])


def _build_bias_t(relb_ref, bucket_t_ref, bias_t_scr):
    bucket_t = bucket_t_ref[...]
    bias_t_scr[...] = jnp.zeros_like(bias_t_scr)
    for b in range(N_BUCKETS):
        hit = bucket_t == b
        for h in range(N_Q_HEADS):
            cols = slice(h * BLOCK, (h + 1) * BLOCK)
            bias_t_scr[:, cols] = jnp.where(hit, relb_ref[b, h] * LOG2E, bias_t_scr[:, cols])
    kj = lax.broadcasted_iota(jnp.int32, (2 * BLOCK, BLOCK), 0)
    qi = lax.broadcasted_iota(jnp.int32, (2 * BLOCK, BLOCK), 1)
    dist = qi + BLOCK - kj
    in_band = jnp.logical_and(dist >= 0, dist < WINDOW)
    for h in range(N_Q_HEADS):
        cols = slice(h * BLOCK, (h + 1) * BLOCK)
        bias_t_scr[:, cols] = jnp.where(in_band, bias_t_scr[:, cols], NEG)


def _ssm_prep_kernel(are_ref, aim_ref, ls_ref, btr_ref, bti_ref,
                     lr_ref, li_ref, lcr_ref, lci_ref, bbr_ref, bbi_ref):
    ar = are_ref[...]
    ai = aim_ref[...]
    step = jnp.exp(ls_ref[...])
    zr = ar * step
    zi = ai * step
    mag = jnp.exp(zr)
    lr = mag * jnp.cos(zi)
    li = mag * jnp.sin(zi)
    pr, pi = lr, li
    for _ in range(int(math.log2(CHUNK))):
        pr, pi = pr * pr - pi * pi, 2.0 * pr * pi
    for g in range(N_SSM_GROUPS):
        st = slice(g * SSM_STATE, (g + 1) * SSM_STATE)
        lr_ref[:, st] = lr[g]
        li_ref[:, st] = li[g]
        lcr_ref[:, st] = pr[g]
        lci_ref[:, st] = pi[g]
    nr = lr - 1.0
    ni = li
    den = ar * ar + ai * ai
    cr = (nr * ar + ni * ai) / den
    ci = (ni * ar - nr * ai) / den
    btr = btr_ref[...]
    bti = bti_ref[...]
    bbr_ref[...] = cr * btr - ci * bti
    bbi_ref[...] = cr * bti + ci * btr


def _ssm_prep(a_re, a_im, log_step, b_re, b_im):
    g, p, c = N_SSM_GROUPS, SSM_STATE, SSM_GROUP
    vec = jax.ShapeDtypeStruct((1, STATE_LANES), F32)
    mat = jax.ShapeDtypeStruct((g, c, p), F32)
    return pl.pallas_call(
        _ssm_prep_kernel,
        out_shape=(vec, vec, vec, vec, mat, mat),
        name="ssm_prep",
    )(a_re.reshape(g, 1, p), a_im.reshape(g, 1, p), log_step.reshape(g, 1, 1),
      jnp.transpose(b_re, (0, 2, 1)), jnp.transpose(b_im, (0, 2, 1)))


def _stack_heads(q_blk, lane_lo):
    zero = jnp.zeros_like(q_blk[:, 0:LANES])
    lo, hi = [], []
    for c in range(ATTN_WIDTH // LANES):
        blk = q_blk[:, c * LANES:(c + 1) * LANES]
        lo.append(jnp.where(lane_lo, blk, zero))
        hi.append(jnp.where(lane_lo, zero, blk))
    return jnp.concatenate(lo + hi, axis=0)


def _unstack_heads(o, rows, lane_lo):
    nblk = ATTN_WIDTH // LANES
    return jnp.concatenate(
        [jnp.where(lane_lo, o[c * rows:(c + 1) * rows], o[(c + nblk) * rows:(c + nblk + 1) * rows])
         for c in range(nblk)], axis=-1)


Q_COLS = 256


def _proj_stages(x_ref, gmix_ref, win_ref, q_stage, k_stage, vt_stage, u_ref, newk_ref, newv_ref):
    xn = _rms(x_ref[...], gmix_ref[...]).astype(BF16)
    q_scale = HEAD_DIM ** -0.5 * LOG2E
    for c in range(ATTN_WIDTH // Q_COLS):
        cols = slice(c * Q_COLS, (c + 1) * Q_COLS)
        q = jnp.dot(xn, win_ref[:, cols], preferred_element_type=F32)
        q_stage[:, cols] = (q * q_scale).astype(BF16)
        yield
    kv = jnp.dot(xn, win_ref[:, ATTN_WIDTH:ATTN_WIDTH + 2 * KV_WIDTH], preferred_element_type=F32)
    k_stage[...] = kv[:, :KV_WIDTH].astype(BF16)
    vt_stage[...] = kv[:, KV_WIDTH:].T.astype(BF16)
    newk_ref[...] = kv[TM_A - WINDOW:, :KV_WIDTH]
    newv_ref[...] = kv[TM_A - WINDOW:, KV_WIDTH:]
    yield
    u0 = ATTN_WIDTH + 2 * KV_WIDTH
    for c in range(SSM_WIDTH // Q_COLS):
        cols = slice(u0 + c * Q_COLS, u0 + (c + 1) * Q_COLS)
        u_ref[:, c * Q_COLS:(c + 1) * Q_COLS] = jnp.dot(xn, win_ref[:, cols], preferred_element_type=F32)
        yield


def _attn_stages(sinks_ref, attn_ref, k_scr, vt_scr, q_scr, bias_t_scr, gt_scr, first_tile):
    nstk = N_Q_HEADS * BLOCK
    nblk = ATTN_WIDTH // LANES
    lane_lo = lax.broadcasted_iota(jnp.int32, (BLOCK, LANES), 1) < HEAD_DIM
    row_lo = lax.broadcasted_iota(jnp.int32, (KV_WIDTH, BLOCK), 0) < HEAD_DIM
    kj = lax.broadcasted_iota(jnp.int32, (2 * BLOCK, nstk), 0)
    has_prev = jnp.logical_or(kj >= BLOCK, jnp.logical_not(first_tile))
    head = lax.broadcasted_iota(jnp.int32, (1, nstk), 1) // BLOCK
    sink = jnp.zeros((1, nstk), F32)
    for h in range(N_Q_HEADS):
        sink = jnp.where(head == h, sinks_ref[0, h] * LOG2E, sink)

    scores = []
    for j in range(TM_A // BLOCK):
        r0 = j * BLOCK
        q2 = _stack_heads(q_scr[r0:r0 + BLOCK, :], lane_lo)
        keys = k_scr[r0:r0 + 2 * BLOCK, :]
        scores.append(lax.dot_general(keys, q2, (((1,), (1,)), ((), ())), preferred_element_type=F32))
        yield
    for j in range(TM_A // BLOCK):
        r0 = j * BLOCK
        s = scores[j] + bias_t_scr[...]
        if j == 0:
            s = jnp.where(has_prev, s, NEG)
        m = jnp.maximum(jnp.max(s, axis=0, keepdims=True), sink)
        p = jnp.exp2(s - m)
        denom = jnp.sum(p, axis=0, keepdims=True) + jnp.exp2(sink - m)
        o = jnp.dot(vt_scr[:, r0:r0 + 2 * BLOCK], p.astype(BF16), preferred_element_type=F32) / denom
        a = jnp.concatenate(
            [jnp.where(row_lo, o[:, c * BLOCK:(c + 1) * BLOCK], o[:, (c + nblk) * BLOCK:(c + nblk + 1) * BLOCK])
             for c in range(nblk)], axis=0)
        a = a * lax.rsqrt(jnp.mean(a * a, axis=0, keepdims=True) + EPS) * gt_scr[...]
        attn_ref[r0:r0 + BLOCK, :] = a.T.astype(BF16)
        yield


def _in_attn_kernel(relb_ref, sinks_ref, x_ref, gmix_ref, win_ref, bucket_t_ref, gattn_ref,
                    wout_f, wglu_f, wup_f, wdown_f,
                    attn_ref, u_ref, newk_ref, newv_ref, wout_b, wglu_b, wup_b, wdown_b,
                    k_scr, vt_scr, q_scr, k_stage, vt_stage, q_stage, bias_t_scr, gt_scr, *, nt):
    i = pl.program_id(0)

    @pl.when(i == 0)
    def _():
        _build_bias_t(relb_ref, bucket_t_ref, bias_t_scr)
        gt_scr[...] = jnp.broadcast_to(gattn_ref[...], (ATTN_WIDTH, BLOCK))
        k_scr[...] = jnp.zeros_like(k_scr)
        vt_scr[...] = jnp.zeros_like(vt_scr)
        k_stage[...] = jnp.zeros_like(k_stage)
        vt_stage[...] = jnp.zeros_like(vt_stage)
        q_stage[...] = jnp.zeros_like(q_stage)

    k_scr[0:BLOCK, :] = k_scr[TM_A:TM_A + BLOCK, :]
    vt_scr[:, 0:BLOCK] = vt_scr[:, TM_A:TM_A + BLOCK]
    k_scr[BLOCK:BLOCK + TM_A, :] = k_stage[...]
    vt_scr[:, BLOCK:BLOCK + TM_A] = vt_stage[...]
    q_scr[...] = q_stage[...]

    wout_b[...] = wout_f[...].astype(BF16)
    wglu_b[...] = wglu_f[...].astype(BF16)
    wup_b[...] = wup_f[...].astype(BF16)
    wdown_b[...] = wdown_f[...].astype(BF16)

    proj = _proj_stages(x_ref, gmix_ref, win_ref, q_stage, k_stage, vt_stage, u_ref, newk_ref, newv_ref)
    attn = _attn_stages(sinks_ref, attn_ref, k_scr, vt_scr, q_scr, bias_t_scr, gt_scr, i % nt == 1 % nt)
    nb = TM_A // BLOCK
    order = ("p" + "a" * (nb // 2)) * 2 + "p" + "a" * (nb // 4) + "p" + "a" * (nb // 2) + "p" + "a" * (nb - nb // 4 - nb // 2)
    for stage in order:
        next(proj if stage == "p" else attn)
    assert next(proj, None) is None and next(attn, None) is None


def _const_spec(shape):
    zeros = (0,) * len(shape)
    return pl.BlockSpec(shape, lambda *_: zeros, pipeline_mode=pl.Buffered(1))


def _smem_spec():
    return pl.BlockSpec(memory_space=pltpu.SMEM)


def _in_attn(x, rel_bias, sinks, g_mix, w_in, bucket_t, g_attn_col, w_out, w_glu, w_up, w_down):
    bsz, seq, _ = x.shape
    nt = seq // TM_A
    steps = bsz * nt

    def tile(i):
        return jnp.minimum(i, steps - 1)

    cur = lambda width: pl.BlockSpec((TM_A, width), lambda i: (tile(i), 0))
    prev = lambda width: pl.BlockSpec((TM_A, width), lambda i: (jnp.maximum(i - 1, 0), 0))
    win_spec = pl.BlockSpec((None, WINDOW, KV_WIDTH), lambda i: (tile(i) // nt, 0, 0))

    def rows_of(w):
        blk = w.shape[0] // steps
        return pl.BlockSpec((blk, w.shape[1]), lambda i: (tile(i), 0))

    nhead_blk = D_MODEL // HEAD_DIM

    def wout_src(i):
        return (jnp.minimum(tile(i), nhead_blk - 1), 0)

    def wout_dst(i):
        j = jnp.minimum(tile(i), nhead_blk - 1)
        return (jnp.where(j < N_Q_HEADS, (j % Q_PER_KV) * N_KV_HEADS + j // Q_PER_KV, j), 0)

    assert steps >= nhead_blk and w_up.shape[0] % steps == 0 and w_down.shape[0] % steps == 0
    wout_in = pl.BlockSpec((HEAD_DIM, D_MODEL), wout_src)
    wout_out = pl.BlockSpec((HEAD_DIM, D_MODEL), wout_dst)
    cast_shape = lambda w: jax.ShapeDtypeStruct(w.shape, BF16)
    return pl.pallas_call(
        functools.partial(_in_attn_kernel, nt=nt),
        out_shape=(jax.ShapeDtypeStruct((bsz * seq, ATTN_WIDTH), BF16),
                   jax.ShapeDtypeStruct((bsz * seq, SSM_WIDTH), F32),
                   jax.ShapeDtypeStruct((bsz, WINDOW, KV_WIDTH), F32),
                   jax.ShapeDtypeStruct((bsz, WINDOW, KV_WIDTH), F32),
                   cast_shape(w_out), cast_shape(w_glu), cast_shape(w_up), cast_shape(w_down)),
        grid=(steps + 1,),
        in_specs=[_smem_spec(), _smem_spec(), cur(D_MODEL), _const_spec((1, D_MODEL)),
                  _const_spec((D_MODEL, D_IN_PROJ)), _const_spec((2 * BLOCK, BLOCK)),
                  _const_spec((ATTN_WIDTH, 1)),
                  wout_in, rows_of(w_glu), rows_of(w_up), rows_of(w_down)],
        out_specs=(prev(ATTN_WIDTH), cur(SSM_WIDTH), win_spec, win_spec,
                   wout_out, rows_of(w_glu), rows_of(w_up), rows_of(w_down)),
        scratch_shapes=[pltpu.VMEM((TM_A + BLOCK, KV_WIDTH), BF16),
                        pltpu.VMEM((KV_WIDTH, TM_A + BLOCK), BF16),
                        pltpu.VMEM((TM_A, ATTN_WIDTH), BF16),
                        pltpu.VMEM((TM_A, KV_WIDTH), BF16),
                        pltpu.VMEM((KV_WIDTH, TM_A), BF16),
                        pltpu.VMEM((TM_A, ATTN_WIDTH), BF16),
                        pltpu.VMEM((2 * BLOCK, N_Q_HEADS * BLOCK), F32),
                        pltpu.VMEM((ATTN_WIDTH, BLOCK), F32)],
        compiler_params=pltpu.CompilerParams(
            dimension_semantics=("arbitrary",), vmem_limit_bytes=VMEM_LIMIT),
        name="in_attn",
    )(rel_bias, sinks, x.reshape(bsz * seq, D_MODEL), g_mix, w_in, bucket_t, g_attn_col, w_out, w_glu, w_up, w_down)


def _s_in_attn_kernel(relb_ref, sinks_ref, x_ref, gmix_ref, win_ref, bucket_ref, gattn_ref,
                      ck_ref, cv_ref,
                      attn_ref, u_ref, newk_ref, newv_ref,
                      bias_scr, biasd_scr, proj_scr):
    rows = attn_ref.shape[0]
    steps = rows // S_SEQ
    pair = SUBLANES // steps
    prow = N_Q_HEADS * SUBLANES

    @pl.when(pl.program_id(0) == 0)
    def _():
        _build_bias(relb_ref, bucket_ref, bias_scr)
        for h in range(N_Q_HEADS):
            for k in range(pair):
                biasd_scr[h * SUBLANES + k * steps:h * SUBLANES + (k + 1) * steps, :] = bias_scr[h, 0:steps, :]
        xn = _rms(x_ref[...], gmix_ref[...]).astype(BF16)
        proj_all = jnp.dot(xn, win_ref[...], preferred_element_type=F32)
        proj_scr[...] = proj_all
        u_ref[...] = proj_all[:, ATTN_WIDTH + 2 * KV_WIDTH:]

    r0 = pl.multiple_of(pl.program_id(0) * rows, rows)
    proj = proj_scr[pl.ds(r0, rows), :]
    qs = proj[:, :ATTN_WIDTH] * (HEAD_DIM ** -0.5)

    row = lax.broadcasted_iota(jnp.int32, (prow, 1), 0)
    qseq = (row % SUBLANES) // steps
    qrow = row % steps
    kcol = lax.broadcasted_iota(jnp.int32, (prow, pair * WINDOW), 1)
    valid_c = jnp.logical_and(kcol // WINDOW == qseq, kcol % WINDOW > qrow)
    kcol_n = lax.broadcasted_iota(jnp.int32, (prow, NEW_ROWS), 1)
    valid_n = jnp.logical_and(jnp.logical_and(kcol_n < SUBLANES, kcol_n // steps == qseq),
                              kcol_n % steps <= qrow)
    sink = jnp.zeros((prow, 1), F32)
    for h in range(N_Q_HEADS):
        sink = jnp.where(row // SUBLANES == h, sinks_ref[0, h], sink)
    lane_lo = lax.broadcasted_iota(jnp.int32, (SUBLANES, LANES), 1) < HEAD_DIM
    gattn = gattn_ref[...]
    pad = jnp.zeros((NEW_ROWS - SUBLANES, KV_WIDTH), F32)
    bias_c = jnp.concatenate([biasd_scr[:, 0:WINDOW]] * pair, axis=1)
    bias_new = biasd_scr[:, WINDOW:2 * WINDOW]
    bias_n = bias_new
    lane = lax.broadcasted_iota(jnp.int32, (prow, WINDOW), 1)
    for k in range(1, pair):
        bias_n = jnp.where(lane >= k * steps, pltpu.roll(bias_new, k * steps, axis=1), bias_n)
    bias_n = bias_n[:, 0:NEW_ROWS]
    nt_dims = (((1,), (1,)), ((), ()))

    tiles = []
    for tl in range(S_SEQ // pair):
        rws = slice(tl * SUBLANES, (tl + 1) * SUBLANES)
        q2 = _stack_heads(qs[rws], lane_lo).astype(BF16)
        kn = proj[rws, ATTN_WIDTH:ATTN_WIDTH + KV_WIDTH]
        vn = proj[rws, ATTN_WIDTH + KV_WIDTH:ATTN_WIDTH + 2 * KV_WIDTH]
        cks, cvs = [], []
        for k in range(pair):
            sq = tl * pair + k
            ck = ck_ref[sq]
            cv = cv_ref[sq]
            newk_ref[sq, 0:WINDOW - steps, :] = ck[steps:, :]
            newk_ref[sq, WINDOW - steps:, :] = kn[k * steps:(k + 1) * steps]
            newv_ref[sq, 0:WINDOW - steps, :] = cv[steps:, :]
            newv_ref[sq, WINDOW - steps:, :] = vn[k * steps:(k + 1) * steps]
            cks.append(ck.astype(BF16))
            cvs.append(cv.astype(BF16))
        kn_b = jnp.concatenate([kn, pad], axis=0).astype(BF16)
        vn_b = jnp.concatenate([vn, pad], axis=0).astype(BF16)
        s_c = lax.dot_general(q2, jnp.concatenate(cks, axis=0), nt_dims, preferred_element_type=F32)
        s_n = lax.dot_general(q2, kn_b, nt_dims, preferred_element_type=F32)
        s_c = jnp.where(valid_c, s_c + bias_c, NEG)
        s_n = jnp.where(valid_n, s_n + bias_n, NEG)
        m = jnp.maximum(jnp.maximum(jnp.max(s_c, axis=-1, keepdims=True),
                                    jnp.max(s_n, axis=-1, keepdims=True)), sink)
        p_c = jnp.exp(s_c - m)
        p_n = jnp.exp(s_n - m)
        denom = (jnp.sum(p_c, axis=-1, keepdims=True) + jnp.sum(p_n, axis=-1, keepdims=True)
                 + jnp.exp(sink - m))
        o = (jnp.dot(p_c.astype(BF16), jnp.concatenate(cvs, axis=0), preferred_element_type=F32)
             + jnp.dot(p_n.astype(BF16), vn_b, preferred_element_type=F32)) / denom
        tiles.append(_unstack_heads(o, SUBLANES, lane_lo))
    attn_ref[...] = _rms(jnp.concatenate(tiles, axis=0), gattn).astype(BF16)


def _s_in_attn(x, rel_bias, sinks, g_mix, w_in, bucket, g_attn, cache_k, cache_v):
    nseq, steps, _ = x.shape
    rows = S_SEQ * steps
    xf = x.reshape(nseq * steps, D_MODEL)
    row = lambda width: pl.BlockSpec((rows, width), lambda i: (i, 0))
    cache = pl.BlockSpec((S_SEQ, WINDOW, KV_WIDTH), lambda i: (i, 0, 0))
    return pl.pallas_call(
        _s_in_attn_kernel,
        out_shape=(jax.ShapeDtypeStruct((nseq * steps, ATTN_WIDTH), BF16),
                   jax.ShapeDtypeStruct((nseq * steps, SSM_WIDTH), F32),
                   jax.ShapeDtypeStruct((nseq, WINDOW, KV_WIDTH), F32),
                   jax.ShapeDtypeStruct((nseq, WINDOW, KV_WIDTH), F32)),
        grid=(nseq // S_SEQ,),
        in_specs=[_smem_spec(), _smem_spec(), _const_spec((nseq * steps, D_MODEL)), _const_spec((1, D_MODEL)),
                  _const_spec((D_MODEL, D_IN_PROJ)), _const_spec((SUBLANES, 2 * BLOCK)),
                  _const_spec((1, ATTN_WIDTH)), cache, cache],
        out_specs=(row(ATTN_WIDTH), pl.BlockSpec((nseq * steps, SSM_WIDTH), lambda i: (0, 0)), cache, cache),
        scratch_shapes=[pltpu.VMEM((N_Q_HEADS, SUBLANES, 2 * BLOCK), F32),
                        pltpu.VMEM((N_Q_HEADS * SUBLANES, 2 * WINDOW), F32),
                        pltpu.VMEM((nseq * steps, D_IN_PROJ), F32)],
        compiler_params=pltpu.CompilerParams(
            dimension_semantics=("arbitrary",), vmem_limit_bytes=VMEM_LIMIT),
        name="s_in_attn",
    )(rel_bias, sinks, xf, g_mix, w_in, bucket[0:SUBLANES], g_attn, cache_k, cache_v)


def _embed_blocks(bbr_ref, bbi_ref, cre_in, cim_in, bre, bim, cre, cim):
    for ref in (bre, bim, cre, cim):
        ref[...] = jnp.zeros_like(ref)
    for g in range(N_SSM_GROUPS):
        sl, j = divmod(g, SLAB // SSM_STATE)
        ch = slice(j * SSM_GROUP, (j + 1) * SSM_GROUP)
        st = slice(j * SSM_STATE, (j + 1) * SSM_STATE)
        bre[sl, ch, st] = bbr_ref[g].astype(BF16)
        bim[sl, ch, st] = bbi_ref[g].astype(BF16)
        cre[sl, ch, st] = cre_in[g].astype(BF16)
        cim[sl, ch, st] = cim_in[g].astype(BF16)


def _output_map(h_slab_re, h_slab_im, c_slab_re, c_slab_im):
    nt_dims = (((1,), (1,)), ((), ()))
    yr = lax.dot_general(h_slab_re.astype(BF16), c_slab_re, nt_dims, preferred_element_type=F32)
    yi = lax.dot_general(h_slab_im.astype(BF16), c_slab_im, nt_dims, preferred_element_type=F32)
    return yr - yi


def _glu_norm(y, wglu_ref, bglu_ref, gssm_ref):
    g = jax.nn.gelu(y)
    gate = jax.nn.sigmoid(jnp.dot(g.astype(BF16), wglu_ref[...], preferred_element_type=F32) + bglu_ref[...])
    return _rms(g * gate, gssm_ref[...])


SLAB = 4 * LANES
SLAB_CH = SLAB // SSM_STATE * SSM_GROUP


REC_PIECE = 16
EPI_ROWS = 128
FF_BLK = 1024


def _scan_stages(u_ref, lr_ref, li_ref, lcr_ref, lci_ref, d_ref, wglu_ref, bglu_ref, gssm_ref, out_ref,
                 bre_ref, bim_ref, cre_ref, cim_ref,
                 pad_scr, up_scr, y_scr, hre, him, carry_re, carry_im, e_re, e_im, hc_re, hc_im):
    nslab = SSM_WIDTH // LANES
    nslabs = STATE_LANES // SLAB

    for c in range(SUBLANES):
        for s in range(nslab):
            pad_scr[s, c * PITCH:c * PITCH + CHUNK, :] = u_ref[c * CHUNK:(c + 1) * CHUNK, s * LANES:(s + 1) * LANES]
    yield
    for tau in range(CHUNK):
        for s in range(nslab):
            up_scr[tau * SUBLANES:(tau + 1) * SUBLANES, s * LANES:(s + 1) * LANES] = (
                pad_scr[s, pl.ds(tau, SUBLANES, stride=PITCH), :])
        if tau % REC_PIECE == REC_PIECE - 1:
            yield

    for sl in range(nslabs):
        cols = slice(sl * SLAB, (sl + 1) * SLAB)
        ch = slice(sl * SLAB_CH, (sl + 1) * SLAB_CH)
        upb = up_scr[:, ch].astype(BF16)
        hre[:, cols] = jnp.dot(upb, bre_ref[sl], preferred_element_type=F32)
        yield
        him[:, cols] = jnp.dot(upb, bim_ref[sl], preferred_element_type=F32)
        yield

    for sl in range(nslabs):
        cols = slice(sl * SLAB, (sl + 1) * SLAB)
        ch = slice(sl * SLAB_CH, (sl + 1) * SLAB_CH)
        ar = jnp.broadcast_to(lr_ref[:, cols], (SUBLANES, SLAB))
        ai = jnp.broadcast_to(li_ref[:, cols], (SUBLANES, SLAB))

        hr = jnp.zeros((SUBLANES, SLAB), F32)
        hi = jnp.zeros((SUBLANES, SLAB), F32)
        for tau in range(CHUNK):
            rws = slice(tau * SUBLANES, (tau + 1) * SUBLANES)
            hr, hi = ar * hr - ai * hi + hre[rws, cols], ar * hi + ai * hr + him[rws, cols]
            if tau % REC_PIECE == REC_PIECE - 1:
                yield
        e_re[:, cols] = hr
        e_im[:, cols] = hi

        cr = carry_re[:, cols]
        ci = carry_im[:, cols]
        lcr = lcr_ref[:, cols]
        lci = lci_ref[:, cols]
        for c in range(SUBLANES):
            hc_re[c:c + 1, cols] = cr
            hc_im[c:c + 1, cols] = ci
            er = e_re[c:c + 1, cols]
            ei = e_im[c:c + 1, cols]
            cr, ci = lcr * cr - lci * ci + er, lcr * ci + lci * cr + ei
        carry_re[:, cols] = cr
        carry_im[:, cols] = ci
        yield

        hr = hc_re[:, cols]
        hi = hc_im[:, cols]
        for tau in range(CHUNK):
            rws = slice(tau * SUBLANES, (tau + 1) * SUBLANES)
            hr, hi = ar * hr - ai * hi + hre[rws, cols], ar * hi + ai * hr + him[rws, cols]
            hre[rws, cols] = hr
            him[rws, cols] = hi
            if tau % REC_PIECE == REC_PIECE - 1:
                yield

        y_scr[:, ch] = _output_map(hre[:, cols], him[:, cols], cre_ref[sl], cim_ref[sl])
        yield

    taus = EPI_ROWS // SUBLANES
    for r in range(TM_B // EPI_ROWS):
        rows = slice(r * EPI_ROWS, (r + 1) * EPI_ROWS)
        o = _glu_norm(y_scr[rows, :] + d_ref[...] * up_scr[rows, :], wglu_ref, bglu_ref, gssm_ref)
        for i in range(taus):
            for s in range(nslab):
                pad_scr[s, pl.ds(r * taus + i, SUBLANES, stride=PITCH), :] = (
                    o[i * SUBLANES:(i + 1) * SUBLANES, s * LANES:(s + 1) * LANES])
        yield
    for c in range(SUBLANES):
        for s in range(nslab):
            out_ref[c * CHUNK:(c + 1) * CHUNK, s * LANES:(s + 1) * LANES] = (
                pad_scr[s, c * PITCH:c * PITCH + CHUNK, :].astype(BF16))
    yield


N_SCAN_STAGES = (1 + CHUNK // REC_PIECE + 2 * (STATE_LANES // SLAB)
                 + (STATE_LANES // SLAB) * (2 * (CHUNK // REC_PIECE) + 2) + TM_B // EPI_ROWS + 1)


def _mlp_stages(x_cols, attn, s_ref, wout_ref, gmlp_ref, wup_ref, wdown_ref, gfin_ref, y_ref, hm_scr, act_scr):
    ncol = D_MODEL // Q_COLS
    for n in range(ncol):
        cols = slice(n * Q_COLS, (n + 1) * Q_COLS)
        y_ref[:, cols] = (x_cols(cols)
                          + jnp.dot(attn(), wout_ref[0:ATTN_WIDTH, cols], preferred_element_type=F32)
                          + jnp.dot(s_ref[...], wout_ref[ATTN_WIDTH:, cols], preferred_element_type=F32))
        yield
    hm_scr[...] = _rms(y_ref[...], gmlp_ref[...]).astype(BF16)
    for c in range(D_FF // FF_BLK):
        act = act_scr.at[c % 2]
        for n in range(FF_BLK // Q_COLS):
            cols = slice(c * FF_BLK + n * Q_COLS, c * FF_BLK + (n + 1) * Q_COLS)
            up = jnp.dot(hm_scr[...], wup_ref[:, cols], preferred_element_type=F32)
            act[:, n * Q_COLS:(n + 1) * Q_COLS] = jnp.square(jnp.maximum(up, 0.0)).astype(BF16)
            yield
        for n in range(ncol):
            cols = slice(n * Q_COLS, (n + 1) * Q_COLS)
            y_ref[:, cols] += jnp.dot(act[...], wdown_ref[c * FF_BLK:(c + 1) * FF_BLK, cols],
                                      preferred_element_type=F32)
            yield
    y_ref[...] = _rms(y_ref[...], gfin_ref[...])
    yield


N_MLP_STAGES = D_MODEL // Q_COLS + (D_FF // FF_BLK) * (FF_BLK // Q_COLS + D_MODEL // Q_COLS) + 1


def _even_merge(n_a, n_b):
    keys = [((k + 0.5) / n_a, "a") for k in range(n_a)] + [((k + 0.5) / n_b, "b") for k in range(n_b)]
    return [name for _, name in sorted(keys)]


def _scan_mlp_kernel(x_ref, a_ref, u_ref, xdec_ref, adec_ref, sdec_ref,
                     lr_ref, li_ref, lcr_ref, lci_ref, bbr_ref, bbi_ref, cre_in, cim_in,
                     d_ref, wglu_ref, bglu_ref, gssm_ref, wout_ref, gmlp_ref, wup_ref, wdown_ref, gfin_ref,
                     y_ref, ydec_ref, hre_out, him_out,
                     ssm_scr, hm_scr, act_scr, bre_ref, bim_ref, cre_ref, cim_ref,
                     pad_scr, up_scr, y_scr, hre, him, carry_re, carry_im, e_re, e_im, hc_re, hc_im, *, nt):
    i = pl.program_id(0)
    dec = i == 0

    @pl.when(dec)
    def _():
        _embed_blocks(bbr_ref, bbi_ref, cre_in, cim_in, bre_ref, bim_ref, cre_ref, cim_ref)
        ssm_scr[...] = sdec_ref[...]

    @pl.when(i % nt == 0)
    def _():
        carry_re[...] = jnp.zeros_like(carry_re)
        carry_im[...] = jnp.zeros_like(carry_im)

    mlp = _mlp_stages(lambda cols: jnp.where(dec, xdec_ref[:, cols], x_ref[:, cols]),
                      lambda: jnp.where(dec, adec_ref[...], a_ref[...]),
                      ssm_scr, wout_ref, gmlp_ref, wup_ref, wdown_ref, gfin_ref, y_ref, hm_scr, act_scr)
    scan = _scan_stages(u_ref, lr_ref, li_ref, lcr_ref, lci_ref, d_ref, wglu_ref, bglu_ref, gssm_ref, ssm_scr,
                        bre_ref, bim_ref, cre_ref, cim_ref,
                        pad_scr, up_scr, y_scr, hre, him, carry_re, carry_im, e_re, e_im, hc_re, hc_im)
    for stage in _even_merge(N_MLP_STAGES, N_SCAN_STAGES):
        next(mlp if stage == "a" else scan)
    assert next(scan, None) is None and next(mlp, None) is None

    @pl.when(dec)
    def _():
        ydec_ref[...] = y_ref[...]

    @pl.when(i % nt == nt - 1)
    def _():
        hre_out[...] = carry_re[...]
        him_out[...] = carry_im[...]


def _scan_mlp(x, attn_n, u, x_dec, attn_dec, ssm_dec, lam, lam_chunk, bbar, c_gcp, d, w_glu, b_glu, g_ssm,
              w_out, g_mlp, w_up, w_down, g_fin):
    bsz, seq, _ = x.shape
    nt = seq // TM_B
    n = bsz * nt
    assert x_dec.shape[0] == TM_B, "the decode rows ride through the MLP as one tile"
    flat = lambda v: v.reshape(n * TM_B, v.shape[-1])
    cur = lambda width: pl.BlockSpec((TM_B, width), lambda i: (jnp.minimum(i, n - 1), 0))
    prev = lambda width: pl.BlockSpec((TM_B, width), lambda i: (jnp.maximum(i - 1, 0), 0))
    state = pl.BlockSpec((None, 1, STATE_LANES), lambda i: (jnp.minimum(i, n - 1) // nt, 0, 0))
    vec = _const_spec((1, STATE_LANES))
    chan = _const_spec((1, SSM_WIDTH))
    model = _const_spec((1, D_MODEL))
    dec = lambda width: _const_spec((TM_B, width))
    y, y_dec, h_re, h_im = pl.pallas_call(
        functools.partial(_scan_mlp_kernel, nt=nt),
        out_shape=(jax.ShapeDtypeStruct((n * TM_B, D_MODEL), F32),
                   jax.ShapeDtypeStruct((TM_B, D_MODEL), F32),
                   jax.ShapeDtypeStruct((bsz, 1, STATE_LANES), F32),
                   jax.ShapeDtypeStruct((bsz, 1, STATE_LANES), F32)),
        grid=(n + 1,),
        in_specs=[prev(D_MODEL), prev(ATTN_WIDTH), cur(SSM_WIDTH), dec(D_MODEL), dec(ATTN_WIDTH), dec(SSM_WIDTH),
                  vec, vec, vec, vec,
                  _group_blocks(), _group_blocks(), _group_blocks(), _group_blocks(),
                  chan, _const_spec((SSM_WIDTH, SSM_WIDTH)), chan, chan,
                  _const_spec((D_MODEL, D_MODEL)), model, _const_spec((D_MODEL, D_FF)),
                  _const_spec((D_FF, D_MODEL)), model],
        out_specs=(prev(D_MODEL), pl.BlockSpec((TM_B, D_MODEL), lambda i: (0, 0)), state, state),
        scratch_shapes=[pltpu.VMEM((TM_B, SSM_WIDTH), BF16),
                        pltpu.VMEM((TM_B, D_MODEL), BF16),
                        pltpu.VMEM((2, TM_B, FF_BLK), BF16)] + _block_operand_scratch() + [
                        pltpu.VMEM((SSM_WIDTH // LANES, SUBLANES * PITCH, LANES), F32),
                        pltpu.VMEM((TM_B, SSM_WIDTH), F32),
                        pltpu.VMEM((TM_B, SSM_WIDTH), F32),
                        pltpu.VMEM((TM_B, STATE_LANES), F32),
                        pltpu.VMEM((TM_B, STATE_LANES), F32),
                        pltpu.VMEM((1, STATE_LANES), F32),
                        pltpu.VMEM((1, STATE_LANES), F32),
                        pltpu.VMEM((SUBLANES, STATE_LANES), F32),
                        pltpu.VMEM((SUBLANES, STATE_LANES), F32),
                        pltpu.VMEM((SUBLANES, STATE_LANES), F32),
                        pltpu.VMEM((SUBLANES, STATE_LANES), F32)],
        compiler_params=pltpu.CompilerParams(
            dimension_semantics=("arbitrary",), vmem_limit_bytes=VMEM_LIMIT),
        name="scan_mlp",
    )(flat(x), flat(attn_n), flat(u), x_dec, attn_dec, ssm_dec, lam[0], lam[1], lam_chunk[0], lam_chunk[1],
      bbar[0], bbar[1], c_gcp[0], c_gcp[1], d, w_glu, b_glu, g_ssm, w_out, g_mlp, w_up, w_down, g_fin)
    return y.reshape(bsz, seq, D_MODEL), y_dec, h_re, h_im


def _block_operand_scratch():
    return [pltpu.VMEM((STATE_LANES // SLAB, SLAB_CH, SLAB), BF16) for _ in range(4)]


def _group_blocks():
    return _const_spec((N_SSM_GROUPS, SSM_GROUP, SSM_STATE))


def _s_scan_kernel(u_ref, h0r_ref, h0i_ref, lr_ref, li_ref, bbr_ref, bbi_ref, cre_in, cim_in,
                   d_ref, wglu_ref, bglu_ref, gssm_ref,
                   out_ref, hre_out, him_out,
                   bre_ref, bim_ref, cre_ref, cim_ref,
                   pad_scr, up_scr, hre, him):
    nseq = h0r_ref.shape[0]
    steps = u_ref.shape[0] // nseq
    nslab = SSM_WIDTH // LANES
    _embed_blocks(bbr_ref, bbi_ref, cre_in, cim_in, bre_ref, bim_ref, cre_ref, cim_ref)

    for s in range(nslab):
        pad_scr[s] = u_ref[:, s * LANES:(s + 1) * LANES]
    for i in range(steps):
        for s in range(nslab):
            up_scr[i * nseq:(i + 1) * nseq, s * LANES:(s + 1) * LANES] = (
                pad_scr[s, pl.ds(i, nseq, stride=steps), :])
    up = up_scr[...]
    upb = up.astype(BF16)

    ys = []
    for sl in range(STATE_LANES // SLAB):
        cols = slice(sl * SLAB, (sl + 1) * SLAB)
        ch = slice(sl * SLAB_CH, (sl + 1) * SLAB_CH)
        hre[:, cols] = jnp.dot(upb[:, ch], bre_ref[sl], preferred_element_type=F32)
        him[:, cols] = jnp.dot(upb[:, ch], bim_ref[sl], preferred_element_type=F32)
        ar = lr_ref[:, cols]
        ai = li_ref[:, cols]
        hr = h0r_ref[:, cols]
        hi = h0i_ref[:, cols]
        for i in range(steps):
            rws = slice(i * nseq, (i + 1) * nseq)
            hr, hi = ar * hr - ai * hi + hre[rws, cols], ar * hi + ai * hr + him[rws, cols]
            hre[rws, cols] = hr
            him[rws, cols] = hi
        hre_out[:, cols] = hr
        him_out[:, cols] = hi
        ys.append(_output_map(hre[:, cols], him[:, cols], cre_ref[sl], cim_ref[sl]))

    y = jnp.concatenate(ys, axis=-1) + d_ref[...] * up
    o = _glu_norm(y, wglu_ref, bglu_ref, gssm_ref)

    for i in range(steps):
        for s in range(nslab):
            pad_scr[s, pl.ds(i, nseq, stride=steps), :] = o[i * nseq:(i + 1) * nseq, s * LANES:(s + 1) * LANES]
    for s in range(nslab):
        out_ref[:, s * LANES:(s + 1) * LANES] = pad_scr[s].astype(BF16)


def _s_scan(u, h0_re, h0_im, lam, bbar, c_gcp, d, w_glu, b_glu, g_ssm):
    rows = u.shape[0]
    nseq = h0_re.shape[0]
    return pl.pallas_call(
        _s_scan_kernel,
        out_shape=(jax.ShapeDtypeStruct((rows, SSM_WIDTH), BF16),
                   jax.ShapeDtypeStruct((nseq, STATE_LANES), F32),
                   jax.ShapeDtypeStruct((nseq, STATE_LANES), F32)),
        scratch_shapes=_block_operand_scratch() + [
                        pltpu.VMEM((SSM_WIDTH // LANES, rows, LANES), F32),
                        pltpu.VMEM((rows, SSM_WIDTH), F32),
                        pltpu.VMEM((rows, STATE_LANES), F32),
                        pltpu.VMEM((rows, STATE_LANES), F32)],
        compiler_params=pltpu.CompilerParams(vmem_limit_bytes=VMEM_LIMIT),
        name="s_scan",
    )(u, h0_re, h0_im, lam[0], lam[1], bbar[0], bbar[1], c_gcp[0], c_gcp[1], d, w_glu, b_glu, g_ssm)


def _pair_heads(w, axis):
    shape = w.shape
    split = shape[:axis] + (N_KV_HEADS, Q_PER_KV, HEAD_DIM) + shape[axis + 1:]
    return jnp.swapaxes(w.reshape(split), axis, axis + 1).reshape(shape)


def kernel(x_prompt, x_sample, cache_k, cache_v, state_ssm_re, state_ssm_im, rel_bias, norm_mix, w_in, attn_sinks, ssm_a_re, ssm_a_im, ssm_log_step, ssm_b_re, ssm_b_im, ssm_c_re, ssm_c_im, ssm_d, w_glu, b_glu, norm_attn_out, norm_ssm_out, w_out, norm_mlp, w_up, w_down, norm_final):
    depth = w_in.shape[0]
    bsz, seq, _ = x_prompt.shape
    nseq, steps, _ = x_sample.shape
    bucket = jnp.asarray(_rel_bucket_table())
    g_fin = norm_final.reshape(1, D_MODEL)
    assert depth == 1, "the final norm is fused into the (single) layer's MLP kernel"

    xp, xs = x_prompt, x_sample
    outs = [[] for _ in range(8)]
    for l in range(depth):
        lr, li, lcr, lci, bbr, bbi = _ssm_prep(ssm_a_re[l], ssm_a_im[l], ssm_log_step[l], ssm_b_re[l], ssm_b_im[l])
        lam = (lr, li)
        lam_chunk = (lcr, lci)
        bbar = (bbr, bbi)
        c_gcp = (ssm_c_re[l], ssm_c_im[l])
        g_mix = norm_mix[l].reshape(1, D_MODEL)
        g_attn = _pair_heads(norm_attn_out[l].reshape(1, ATTN_WIDTH), axis=1)
        g_ssm = norm_ssm_out[l].reshape(1, SSM_WIDTH)
        g_mlp = norm_mlp[l].reshape(1, D_MODEL)
        sinks = attn_sinks[l].reshape(1, N_Q_HEADS)
        d = ssm_d[l].reshape(1, SSM_WIDTH)
        bg = b_glu[l].reshape(1, SSM_WIDTH)
        w_in_b = jnp.concatenate([_pair_heads(w_in[l][:, :ATTN_WIDTH], axis=1), w_in[l][:, ATTN_WIDTH:]],
                                 axis=1).astype(BF16)

        attn_n, u, k1, v1, w_out_b, w_glu_b, w_up_b, w_down_b = _in_attn(
            xp, rel_bias, sinks, g_mix, w_in_b, bucket.T, g_attn.reshape(ATTN_WIDTH, 1),
            w_out[l], w_glu[l], w_up[l], w_down[l])

        ck = cache_k[l].reshape(nseq, WINDOW, KV_WIDTH)
        cv = cache_v[l].reshape(nseq, WINDOW, KV_WIDTH)
        s_attn, s_u, k2, v2 = _s_in_attn(xs, rel_bias, sinks, g_mix, w_in_b, bucket, g_attn, ck, cv)
        s_ssm, r2, i2 = _s_scan(s_u, state_ssm_re[l].reshape(nseq, STATE_LANES),
                                state_ssm_im[l].reshape(nseq, STATE_LANES),
                                lam, bbar, c_gcp, d, w_glu_b, bg, g_ssm)

        xp, xs, r1, i1 = _scan_mlp(xp, attn_n, u, xs.reshape(nseq * steps, D_MODEL), s_attn, s_ssm,
                                   lam, lam_chunk, bbar, c_gcp, d, w_glu_b, bg, g_ssm,
                                   w_out_b, g_mlp, w_up_b, w_down_b, g_fin)
        xs = xs.reshape(nseq, steps, D_MODEL)

        kv_shape = (WINDOW, N_KV_HEADS, HEAD_DIM)
        st_shape = (N_SSM_GROUPS, SSM_STATE)
        for lst, val in zip(outs, (k1.reshape(bsz, *kv_shape), v1.reshape(bsz, *kv_shape),
                                   r1.reshape(bsz, *st_shape), i1.reshape(bsz, *st_shape),
                                   k2.reshape(nseq, *kv_shape), v2.reshape(nseq, *kv_shape),
                                   r2.reshape(nseq, *st_shape), i2.reshape(nseq, *st_shape))):
            lst.append(val)

    return (xp, xs) + tuple(jnp.stack(o) for o in outs)
```

```python
import functools
import math

import numpy as np
import jax
import jax.numpy as jnp
from jax import lax
from jax.experimental import pallas as pl
from jax.experimental.pallas import tpu as pltpu

D_MODEL = 1024
HEAD_DIM = 64
ATTN_WIDTH = 512
N_Q_HEADS = 8
N_KV_HEADS = 2
Q_PER_KV = 4
KV_WIDTH = 128
WINDOW = 128
BLOCK = 128
N_BUCKETS = 32
MAX_DISTANCE = 128
SSM_WIDTH = 512
SSM_GROUP = 16
N_SSM_GROUPS = 32
SSM_STATE = 64
STATE_LANES = N_SSM_GROUPS * SSM_STATE
D_FF = 4096
D_IN_PROJ = 1280
EPS = 1e-6
NEG = -1e30
LOG2E = math.log2(math.e)

LANES = 128
SUBLANES = 8
MXU_DIM = 256
VMEM_LIMIT = 56 * 1024 * 1024

F32 = jnp.float32
BF16 = jnp.bfloat16

TM_A = 512
TM_B = 512
CHUNK = TM_B // SUBLANES
PITCH = CHUNK + SUBLANES
S_SEQ = 16
NEW_ROWS = 16


def _rms(x, g):
    return x * lax.rsqrt(jnp.mean(x * x, axis=-1, keepdims=True) + EPS) * g


def _rel_bucket_table():
    i = np.arange(BLOCK)[:, None]
    j = np.arange(2 * BLOCK)[None, :]
    n = np.maximum(i + BLOCK - j, 0)
    max_exact = N_BUCKETS // 2
    nf = np.maximum(n, max_exact).astype(np.float32)
    large = max_exact + (np.log(nf / np.float32(max_exact)) / np.float32(math.log(MAX_DISTANCE / max_exact))
                         * np.float32(N_BUCKETS - max_exact)).astype(np.int32)
    large = np.minimum(large, N_BUCKETS - 1)
    return np.where(n < max_exact, n, large).astype(np.int32)


def _build_bias(relb_ref, bucket_ref, bias_scr):
    bucket = bucket_ref[...]
    for h in range(N_Q_HEADS):
        bias_scr[h] = jnp.zeros(bucket.shape, F32)
    for b in range(N_BUCKETS):
        hit = bucket == b
        for h in range(N_Q_HEADS):
            bias_scr[h] = jnp.where(hit, relb_ref[b, h], bias_scr[h])


def _build_bias_t(relb_ref, bucket_t_ref, bias_t_scr):
    bucket_t = bucket_t_ref[...]
    bias_t_scr[...] = jnp.zeros_like(bias_t_scr)
    for b in range(N_BUCKETS):
        hit = bucket_t == b
        for h in range(N_Q_HEADS):
            cols = slice(h * BLOCK, (h + 1) * BLOCK)
            bias_t_scr[:, cols] = jnp.where(hit, relb_ref[b, h] * LOG2E, bias_t_scr[:, cols])
    kj = lax.broadcasted_iota(jnp.int32, (2 * BLOCK, BLOCK), 0)
    qi = lax.broadcasted_iota(jnp.int32, (2 * BLOCK, BLOCK), 1)
    dist = qi + BLOCK - kj
    in_band = jnp.logical_and(dist >= 0, dist < WINDOW)
    for h in range(N_Q_HEADS):
        cols = slice(h * BLOCK, (h + 1) * BLOCK)
        bias_t_scr[:, cols] = jnp.where(in_band, bias_t_scr[:, cols], NEG)


def _ssm_prep_kernel(are_ref, aim_ref, ls_ref, btr_ref, bti_ref,
                     lr_ref, li_ref, lcr_ref, lci_ref, bbr_ref, bbi_ref):
    ar = are_ref[...]
    ai = aim_ref[...]
    step = jnp.exp(ls_ref[...])
    zr = ar * step
    zi = ai * step
    mag = jnp.exp(zr)
    lr = mag * jnp.cos(zi)
    li = mag * jnp.sin(zi)
    pr, pi = lr, li
    for _ in range(int(math.log2(CHUNK))):
        pr, pi = pr * pr - pi * pi, 2.0 * pr * pi
    for g in range(N_SSM_GROUPS):
        st = slice(g * SSM_STATE, (g + 1) * SSM_STATE)
        lr_ref[:, st] = lr[g]
        li_ref[:, st] = li[g]
        lcr_ref[:, st] = pr[g]
        lci_ref[:, st] = pi[g]
    nr = lr - 1.0
    ni = li
    den = ar * ar + ai * ai
    cr = (nr * ar + ni * ai) / den
    ci = (ni * ar - nr * ai) / den
    btr = btr_ref[...]
    bti = bti_ref[...]
    bbr_ref[...] = cr * btr - ci * bti
    bbi_ref[...] = cr * bti + ci * btr


def _ssm_prep(a_re, a_im, log_step, b_re, b_im):
    g, p, c = N_SSM_GROUPS, SSM_STATE, SSM_GROUP
    vec = jax.ShapeDtypeStruct((1, STATE_LANES), F32)
    mat = jax.ShapeDtypeStruct((g, c, p), F32)
    return pl.pallas_call(
        _ssm_prep_kernel,
        out_shape=(vec, vec, vec, vec, mat, mat),
        name="ssm_prep",
    )(a_re.reshape(g, 1, p), a_im.reshape(g, 1, p), log_step.reshape(g, 1, 1),
      jnp.transpose(b_re, (0, 2, 1)), jnp.transpose(b_im, (0, 2, 1)))


def _stack_heads(q_blk, lane_lo):
    zero = jnp.zeros_like(q_blk[:, 0:LANES])
    lo, hi = [], []
    for c in range(ATTN_WIDTH // LANES):
        blk = q_blk[:, c * LANES:(c + 1) * LANES]
        lo.append(jnp.where(lane_lo, blk, zero))
        hi.append(jnp.where(lane_lo, zero, blk))
    return jnp.concatenate(lo + hi, axis=0)


def _unstack_heads(o, rows, lane_lo):
    nblk = ATTN_WIDTH // LANES
    return jnp.concatenate(
        [jnp.where(lane_lo, o[c * rows:(c + 1) * rows], o[(c + nblk) * rows:(c + nblk + 1) * rows])
         for c in range(nblk)], axis=-1)


Q_COLS = MXU_DIM


def _proj_stages(x_ref, gmix_ref, win_ref, q_stage, k_stage, vt_stage, u_ref, newk_ref, newv_ref):
    xn = _rms(x_ref[...], gmix_ref[...]).astype(BF16)
    q_scale = HEAD_DIM ** -0.5 * LOG2E
    for c in range(ATTN_WIDTH // Q_COLS):
        cols = slice(c * Q_COLS, (c + 1) * Q_COLS)
        q = jnp.dot(xn, win_ref[:, cols], preferred_element_type=F32)
        q_stage[:, cols] = (q * q_scale).astype(BF16)
        yield
    kv = jnp.dot(xn, win_ref[:, ATTN_WIDTH:ATTN_WIDTH + 2 * KV_WIDTH], preferred_element_type=F32)
    k_stage[...] = kv[:, :KV_WIDTH].astype(BF16)
    vt_stage[...] = kv[:, KV_WIDTH:].T.astype(BF16)
    newk_ref[...] = kv[TM_A - WINDOW:, :KV_WIDTH]
    newv_ref[...] = kv[TM_A - WINDOW:, KV_WIDTH:]
    yield
    u0 = ATTN_WIDTH + 2 * KV_WIDTH
    for c in range(SSM_WIDTH // Q_COLS):
        cols = slice(u0 + c * Q_COLS, u0 + (c + 1) * Q_COLS)
        u_ref[:, c * Q_COLS:(c + 1) * Q_COLS] = jnp.dot(xn, win_ref[:, cols], preferred_element_type=F32)
        yield


def _attn_stages(sinks_ref, attn_ref, k_scr, vt_scr, q_scr, bias_t_scr, gt_scr, first_tile):
    nstk = N_Q_HEADS * BLOCK
    nblk = ATTN_WIDTH // LANES
    lane_lo = lax.broadcasted_iota(jnp.int32, (BLOCK, LANES), 1) < HEAD_DIM
    row_lo = lax.broadcasted_iota(jnp.int32, (KV_WIDTH, BLOCK), 0) < HEAD_DIM
    kj = lax.broadcasted_iota(jnp.int32, (2 * BLOCK, nstk), 0)
    has_prev = jnp.logical_or(kj >= BLOCK, jnp.logical_not(first_tile))
    head = lax.broadcasted_iota(jnp.int32, (1, nstk), 1) // BLOCK
    sink = jnp.zeros((1, nstk), F32)
    for h in range(N_Q_HEADS):
        sink = jnp.where(head == h, sinks_ref[0, h] * LOG2E, sink)

    scores = []
    for j in range(TM_A // BLOCK):
        r0 = j * BLOCK
        q2 = _stack_heads(q_scr[r0:r0 + BLOCK, :], lane_lo)
        keys = k_scr[r0:r0 + 2 * BLOCK, :]
        scores.append(lax.dot_general(keys, q2, (((1,), (1,)), ((), ())), preferred_element_type=F32))
        yield
    for j in range(TM_A // BLOCK):
        r0 = j * BLOCK
        s = scores[j] + bias_t_scr[...]
        if j == 0:
            s = jnp.where(has_prev, s, NEG)
        m = jnp.maximum(jnp.max(s, axis=0, keepdims=True), sink)
        p = jnp.exp2(s - m)
        denom = jnp.sum(p, axis=0, keepdims=True) + jnp.exp2(sink - m)
        o = jnp.dot(vt_scr[:, r0:r0 + 2 * BLOCK], p.astype(BF16), preferred_element_type=F32) / denom
        a = jnp.concatenate(
            [jnp.where(row_lo, o[:, c * BLOCK:(c + 1) * BLOCK], o[:, (c + nblk) * BLOCK:(c + nblk + 1) * BLOCK])
             for c in range(nblk)], axis=0)
        a = a * lax.rsqrt(jnp.mean(a * a, axis=0, keepdims=True) + EPS) * gt_scr[...]
        attn_ref[r0:r0 + BLOCK, :] = a.T.astype(BF16)
        yield


def _in_attn_kernel(relb_ref, sinks_ref, x_ref, gmix_ref, win_ref, bucket_t_ref, gattn_ref,
                    wout_f, wglu_f, wup_f, wdown_f,
                    attn_ref, u_ref, newk_ref, newv_ref, wout_b, wglu_b, wup_b, wdown_b,
                    k_scr, vt_scr, q_scr, k_stage, vt_stage, q_stage, bias_t_scr, gt_scr, *, nt):
    i = pl.program_id(0)

    @pl.when(i == 0)
    def _():
        _build_bias_t(relb_ref, bucket_t_ref, bias_t_scr)
        gt_scr[...] = jnp.broadcast_to(gattn_ref[...], (ATTN_WIDTH, BLOCK))
        k_scr[...] = jnp.zeros_like(k_scr)
        vt_scr[...] = jnp.zeros_like(vt_scr)
        k_stage[...] = jnp.zeros_like(k_stage)
        vt_stage[...] = jnp.zeros_like(vt_stage)
        q_stage[...] = jnp.zeros_like(q_stage)

    k_scr[0:BLOCK, :] = k_scr[TM_A:TM_A + BLOCK, :]
    vt_scr[:, 0:BLOCK] = vt_scr[:, TM_A:TM_A + BLOCK]
    k_scr[BLOCK:BLOCK + TM_A, :] = k_stage[...]
    vt_scr[:, BLOCK:BLOCK + TM_A] = vt_stage[...]
    q_scr[...] = q_stage[...]

    wout_b[...] = wout_f[...].astype(BF16)
    wglu_b[...] = wglu_f[...].astype(BF16)
    wup_b[...] = wup_f[...].astype(BF16)
    wdown_b[...] = wdown_f[...].astype(BF16)

    proj = _proj_stages(x_ref, gmix_ref, win_ref, q_stage, k_stage, vt_stage, u_ref, newk_ref, newv_ref)
    attn = _attn_stages(sinks_ref, attn_ref, k_scr, vt_scr, q_scr, bias_t_scr, gt_scr, i % nt == 1 % nt)
    nb = TM_A // BLOCK
    order = ("p" + "a" * (nb // 2)) * 2 + "p" + "a" * (nb // 4) + "p" + "a" * (nb // 2) + "p" + "a" * (nb - nb // 4 - nb // 2)
    for stage in order:
        next(proj if stage == "p" else attn)
    assert next(proj, None) is None and next(attn, None) is None


def _const_spec(shape):
    zeros = (0,) * len(shape)
    return pl.BlockSpec(shape, lambda *_: zeros, pipeline_mode=pl.Buffered(1))


def _smem_spec():
    return pl.BlockSpec(memory_space=pltpu.SMEM)


def _in_attn(x, rel_bias, sinks, g_mix, w_in, bucket_t, g_attn_col, w_out, w_glu, w_up, w_down):
    bsz, seq, _ = x.shape
    nt = seq // TM_A
    steps = bsz * nt

    def tile(i):
        return jnp.minimum(i, steps - 1)

    cur = lambda width: pl.BlockSpec((TM_A, width), lambda i: (tile(i), 0))
    prev = lambda width: pl.BlockSpec((TM_A, width), lambda i: (jnp.maximum(i - 1, 0), 0))
    win_spec = pl.BlockSpec((None, WINDOW, KV_WIDTH), lambda i: (tile(i) // nt, 0, 0))

    def rows_of(w):
        blk = w.shape[0] // steps
        return pl.BlockSpec((blk, w.shape[1]), lambda i: (tile(i), 0))

    nhead_blk = D_MODEL // HEAD_DIM

    def wout_src(i):
        return (jnp.minimum(tile(i), nhead_blk - 1), 0)

    def wout_dst(i):
        j = jnp.minimum(tile(i), nhead_blk - 1)
        return (jnp.where(j < N_Q_HEADS, (j % Q_PER_KV) * N_KV_HEADS + j // Q_PER_KV, j), 0)

    assert steps >= nhead_blk and w_up.shape[0] % steps == 0 and w_down.shape[0] % steps == 0
    wout_in = pl.BlockSpec((HEAD_DIM, D_MODEL), wout_src)
    wout_out = pl.BlockSpec((HEAD_DIM, D_MODEL), wout_dst)
    cast_shape = lambda w: jax.ShapeDtypeStruct(w.shape, BF16)
    return pl.pallas_call(
        functools.partial(_in_attn_kernel, nt=nt),
        out_shape=(jax.ShapeDtypeStruct((bsz * seq, ATTN_WIDTH), BF16),
                   jax.ShapeDtypeStruct((bsz * seq, SSM_WIDTH), F32),
                   jax.ShapeDtypeStruct((bsz, WINDOW, KV_WIDTH), F32),
                   jax.ShapeDtypeStruct((bsz, WINDOW, KV_WIDTH), F32),
                   cast_shape(w_out), cast_shape(w_glu), cast_shape(w_up), cast_shape(w_down)),
        grid=(steps + 1,),
        in_specs=[_smem_spec(), _smem_spec(), cur(D_MODEL), _const_spec((1, D_MODEL)),
                  _const_spec((D_MODEL, D_IN_PROJ)), _const_spec((2 * BLOCK, BLOCK)),
                  _const_spec((ATTN_WIDTH, 1)),
                  wout_in, rows_of(w_glu), rows_of(w_up), rows_of(w_down)],
        out_specs=(prev(ATTN_WIDTH), cur(SSM_WIDTH), win_spec, win_spec,
                   wout_out, rows_of(w_glu), rows_of(w_up), rows_of(w_down)),
        scratch_shapes=[pltpu.VMEM((TM_A + BLOCK, KV_WIDTH), BF16),
                        pltpu.VMEM((KV_WIDTH, TM_A + BLOCK), BF16),
                        pltpu.VMEM((TM_A, ATTN_WIDTH), BF16),
                        pltpu.VMEM((TM_A, KV_WIDTH), BF16),
                        pltpu.VMEM((KV_WIDTH, TM_A), BF16),
                        pltpu.VMEM((TM_A, ATTN_WIDTH), BF16),
                        pltpu.VMEM((2 * BLOCK, N_Q_HEADS * BLOCK), F32),
                        pltpu.VMEM((ATTN_WIDTH, BLOCK), F32)],
        compiler_params=pltpu.CompilerParams(
            dimension_semantics=("arbitrary",), vmem_limit_bytes=VMEM_LIMIT),
        name="in_attn",
    )(rel_bias, sinks, x.reshape(bsz * seq, D_MODEL), g_mix, w_in, bucket_t, g_attn_col, w_out, w_glu, w_up, w_down)


def _s_in_attn_kernel(relb_ref, sinks_ref, x_ref, gmix_ref, win_ref, bucket_ref, gattn_ref,
                      ck_ref, cv_ref,
                      attn_ref, u_ref, newk_ref, newv_ref,
                      bias_scr, biasd_scr, proj_scr):
    rows = attn_ref.shape[0]
    steps = rows // S_SEQ
    pair = SUBLANES // steps
    prow = N_Q_HEADS * SUBLANES

    @pl.when(pl.program_id(0) == 0)
    def _():
        _build_bias(relb_ref, bucket_ref, bias_scr)
        for h in range(N_Q_HEADS):
            for k in range(pair):
                biasd_scr[h * SUBLANES + k * steps:h * SUBLANES + (k + 1) * steps, :] = bias_scr[h, 0:steps, :]
        xn = _rms(x_ref[...], gmix_ref[...]).astype(BF16)
        proj_all = jnp.dot(xn, win_ref[...], preferred_element_type=F32)
        proj_scr[...] = proj_all
        u_ref[...] = proj_all[:, ATTN_WIDTH + 2 * KV_WIDTH:]

    r0 = pl.multiple_of(pl.program_id(0) * rows, rows)
    proj = proj_scr[pl.ds(r0, rows), :]
    qs = proj[:, :ATTN_WIDTH] * (HEAD_DIM ** -0.5)

    grp = NEW_ROWS // SUBLANES
    nsq = grp * pair
    row = lax.broadcasted_iota(jnp.int32, (grp * prow, 1), 0)
    qseq = row // prow * pair + (row % SUBLANES) // steps
    qrow = row % steps
    kcol = lax.broadcasted_iota(jnp.int32, (grp * prow, nsq * WINDOW), 1)
    valid_c = jnp.logical_and(kcol // WINDOW == qseq, kcol % WINDOW > qrow)
    kcol_n = lax.broadcasted_iota(jnp.int32, (grp * prow, NEW_ROWS), 1)
    valid_n = jnp.logical_and(kcol_n // steps == qseq, kcol_n % steps <= qrow)
    sink = jnp.zeros((grp * prow, 1), F32)
    for h in range(N_Q_HEADS):
        sink = jnp.where(row % prow // SUBLANES == h, sinks_ref[0, h], sink)
    lane_lo = lax.broadcasted_iota(jnp.int32, (SUBLANES, LANES), 1) < HEAD_DIM
    gattn = gattn_ref[...]
    bias_rows = jnp.concatenate([biasd_scr[...]] * grp, axis=0)
    bias_c = jnp.concatenate([bias_rows[:, 0:WINDOW]] * nsq, axis=1)
    bias_new = bias_rows[:, WINDOW:2 * WINDOW]
    bias_n = bias_new
    lane = lax.broadcasted_iota(jnp.int32, (grp * prow, WINDOW), 1)
    for k in range(1, nsq):
        bias_n = jnp.where(lane >= k * steps, pltpu.roll(bias_new, k * steps, axis=1), bias_n)
    bias_n = bias_n[:, 0:NEW_ROWS]
    nt_dims = (((1,), (1,)), ((), ()))

    tiles = []
    for gi in range(S_SEQ // nsq):
        rws = slice(gi * NEW_ROWS, (gi + 1) * NEW_ROWS)
        q2 = jnp.concatenate(
            [_stack_heads(qs[gi * NEW_ROWS + t * SUBLANES:gi * NEW_ROWS + (t + 1) * SUBLANES], lane_lo)
             for t in range(grp)], axis=0).astype(BF16)
        kn = proj[rws, ATTN_WIDTH:ATTN_WIDTH + KV_WIDTH]
        vn = proj[rws, ATTN_WIDTH + KV_WIDTH:ATTN_WIDTH + 2 * KV_WIDTH]
        cks, cvs = [], []
        for k in range(nsq):
            sq = gi * nsq + k
            ck = ck_ref[sq]
            cv = cv_ref[sq]
            newk_ref[sq, 0:WINDOW - steps, :] = ck[steps:, :]
            newk_ref[sq, WINDOW - steps:, :] = kn[k * steps:(k + 1) * steps]
            newv_ref[sq, 0:WINDOW - steps, :] = cv[steps:, :]
            newv_ref[sq, WINDOW - steps:, :] = vn[k * steps:(k + 1) * steps]
            cks.append(ck.astype(BF16))
            cvs.append(cv.astype(BF16))
        s_c = lax.dot_general(q2, jnp.concatenate(cks, axis=0), nt_dims, preferred_element_type=F32)
        s_n = lax.dot_general(q2, kn.astype(BF16), nt_dims, preferred_element_type=F32)
        s_c = jnp.where(valid_c, s_c + bias_c, NEG)
        s_n = jnp.where(valid_n, s_n + bias_n, NEG)
        m = jnp.maximum(jnp.maximum(jnp.max(s_c, axis=-1, keepdims=True),
                                    jnp.max(s_n, axis=-1, keepdims=True)), sink)
        p_c = jnp.exp(s_c - m)
        p_n = jnp.exp(s_n - m)
        denom = (jnp.sum(p_c, axis=-1, keepdims=True) + jnp.sum(p_n, axis=-1, keepdims=True)
                 + jnp.exp(sink - m))
        o = (jnp.dot(p_c.astype(BF16), jnp.concatenate(cvs, axis=0), preferred_element_type=F32)
             + jnp.dot(p_n.astype(BF16), vn.astype(BF16), preferred_element_type=F32)) / denom
        for t in range(grp):
            tiles.append(_unstack_heads(o[t * prow:(t + 1) * prow], SUBLANES, lane_lo))
    attn_ref[...] = _rms(jnp.concatenate(tiles, axis=0), gattn).astype(BF16)


def _s_in_attn(x, rel_bias, sinks, g_mix, w_in, bucket, g_attn, cache_k, cache_v):
    nseq, steps, _ = x.shape
    rows = S_SEQ * steps
    xf = x.reshape(nseq * steps, D_MODEL)
    row = lambda width: pl.BlockSpec((rows, width), lambda i: (i, 0))
    cache = pl.BlockSpec((S_SEQ, WINDOW, KV_WIDTH), lambda i: (i, 0, 0))
    return pl.pallas_call(
        _s_in_attn_kernel,
        out_shape=(jax.ShapeDtypeStruct((nseq * steps, ATTN_WIDTH), BF16),
                   jax.ShapeDtypeStruct((nseq * steps, SSM_WIDTH), F32),
                   jax.ShapeDtypeStruct((nseq, WINDOW, KV_WIDTH), F32),
                   jax.ShapeDtypeStruct((nseq, WINDOW, KV_WIDTH), F32)),
        grid=(nseq // S_SEQ,),
        in_specs=[_smem_spec(), _smem_spec(), _const_spec((nseq * steps, D_MODEL)), _const_spec((1, D_MODEL)),
                  _const_spec((D_MODEL, D_IN_PROJ)), _const_spec((SUBLANES, 2 * BLOCK)),
                  _const_spec((1, ATTN_WIDTH)), cache, cache],
        out_specs=(row(ATTN_WIDTH), pl.BlockSpec((nseq * steps, SSM_WIDTH), lambda i: (0, 0)), cache, cache),
        scratch_shapes=[pltpu.VMEM((N_Q_HEADS, SUBLANES, 2 * BLOCK), F32),
                        pltpu.VMEM((N_Q_HEADS * SUBLANES, 2 * WINDOW), F32),
                        pltpu.VMEM((nseq * steps, D_IN_PROJ), F32)],
        compiler_params=pltpu.CompilerParams(
            dimension_semantics=("arbitrary",), vmem_limit_bytes=VMEM_LIMIT),
        name="s_in_attn",
    )(rel_bias, sinks, xf, g_mix, w_in, bucket[0:SUBLANES], g_attn, cache_k, cache_v)


def _embed_blocks(bbr_ref, bbi_ref, cre_in, cim_in, bre, bim, cre, cim):
    for ref in (bre, bim, cre, cim):
        ref[...] = jnp.zeros_like(ref)
    for g in range(N_SSM_GROUPS):
        sl, j = divmod(g, SLAB // SSM_STATE)
        ch = slice(j * SSM_GROUP, (j + 1) * SSM_GROUP)
        st = slice(j * SSM_STATE, (j + 1) * SSM_STATE)
        bre[sl, ch, st] = bbr_ref[g].astype(BF16)
        bim[sl, ch, st] = bbi_ref[g].astype(BF16)
        cre[sl, ch, st] = cre_in[g].astype(BF16)
        cim[sl, ch, st] = cim_in[g].astype(BF16)


def _output_map(h_slab_re, h_slab_im, c_slab_re, c_slab_im):
    nt_dims = (((1,), (1,)), ((), ()))
    yr = lax.dot_general(h_slab_re.astype(BF16), c_slab_re, nt_dims, preferred_element_type=F32)
    yi = lax.dot_general(h_slab_im.astype(BF16), c_slab_im, nt_dims, preferred_element_type=F32)
    return yr - yi


def _glu_norm(y, wglu_ref, bglu_ref, gssm_ref):
    g = jax.nn.gelu(y)
    gate = jax.nn.sigmoid(jnp.dot(g.astype(BF16), wglu_ref[...], preferred_element_type=F32) + bglu_ref[...])
    return _rms(g * gate, gssm_ref[...])


SLAB = 4 * LANES
SLAB_CH = SLAB // SSM_STATE * SSM_GROUP


REC_PIECE = 16
EPI_ROWS = 128
FF_BLK = 1024


def _scan_stages(u_ref, lr_ref, li_ref, lcr_ref, lci_ref, d_ref, wglu_ref, bglu_ref, gssm_ref, out_ref,
                 bre_ref, bim_ref, cre_ref, cim_ref,
                 pad_scr, up_scr, y_scr, hre, him, carry_re, carry_im, e_re, e_im, hc_re, hc_im):
    nslab = SSM_WIDTH // LANES
    nslabs = STATE_LANES // SLAB

    for c in range(SUBLANES):
        for s in range(nslab):
            pad_scr[s, c * PITCH:c * PITCH + CHUNK, :] = u_ref[c * CHUNK:(c + 1) * CHUNK, s * LANES:(s + 1) * LANES]
    yield
    for tau in range(CHUNK):
        for s in range(nslab):
            up_scr[tau * SUBLANES:(tau + 1) * SUBLANES, s * LANES:(s + 1) * LANES] = (
                pad_scr[s, pl.ds(tau, SUBLANES, stride=PITCH), :])
        if tau % REC_PIECE == REC_PIECE - 1:
            yield

    for sl in range(nslabs):
        cols = slice(sl * SLAB, (sl + 1) * SLAB)
        ch = slice(sl * SLAB_CH, (sl + 1) * SLAB_CH)
        upb = up_scr[:, ch].astype(BF16)
        hre[:, cols] = jnp.dot(upb, bre_ref[sl], preferred_element_type=F32)
        yield
        him[:, cols] = jnp.dot(upb, bim_ref[sl], preferred_element_type=F32)
        yield

    for sl in range(nslabs):
        cols = slice(sl * SLAB, (sl + 1) * SLAB)
        ch = slice(sl * SLAB_CH, (sl + 1) * SLAB_CH)
        ar = jnp.broadcast_to(lr_ref[:, cols], (SUBLANES, SLAB))
        ai = jnp.broadcast_to(li_ref[:, cols], (SUBLANES, SLAB))

        hr = jnp.zeros((SUBLANES, SLAB), F32)
        hi = jnp.zeros((SUBLANES, SLAB), F32)
        for tau in range(CHUNK):
            rws = slice(tau * SUBLANES, (tau + 1) * SUBLANES)
            hr, hi = ar * hr - ai * hi + hre[rws, cols], ar * hi + ai * hr + him[rws, cols]
            if tau % REC_PIECE == REC_PIECE - 1:
                yield
        e_re[:, cols] = hr
        e_im[:, cols] = hi

        cr = carry_re[:, cols]
        ci = carry_im[:, cols]
        lcr = lcr_ref[:, cols]
        lci = lci_ref[:, cols]
        for c in range(SUBLANES):
            hc_re[c:c + 1, cols] = cr
            hc_im[c:c + 1, cols] = ci
            er = e_re[c:c + 1, cols]
            ei = e_im[c:c + 1, cols]
            cr, ci = lcr * cr - lci * ci + er, lcr * ci + lci * cr + ei
        carry_re[:, cols] = cr
        carry_im[:, cols] = ci
        yield

        hr = hc_re[:, cols]
        hi = hc_im[:, cols]
        for tau in range(CHUNK):
            rws = slice(tau * SUBLANES, (tau + 1) * SUBLANES)
            hr, hi = ar * hr - ai * hi + hre[rws, cols], ar * hi + ai * hr + him[rws, cols]
            hre[rws, cols] = hr
            him[rws, cols] = hi
            if tau % REC_PIECE == REC_PIECE - 1:
                yield

        y_scr[:, ch] = _output_map(hre[:, cols], him[:, cols], cre_ref[sl], cim_ref[sl])
        yield

    taus = EPI_ROWS // SUBLANES
    for r in range(TM_B // EPI_ROWS):
        rows = slice(r * EPI_ROWS, (r + 1) * EPI_ROWS)
        o = _glu_norm(y_scr[rows, :] + d_ref[...] * up_scr[rows, :], wglu_ref, bglu_ref, gssm_ref)
        for i in range(taus):
            for s in range(nslab):
                pad_scr[s, pl.ds(r * taus + i, SUBLANES, stride=PITCH), :] = (
                    o[i * SUBLANES:(i + 1) * SUBLANES, s * LANES:(s + 1) * LANES])
        yield
    for c in range(SUBLANES):
        for s in range(nslab):
            out_ref[c * CHUNK:(c + 1) * CHUNK, s * LANES:(s + 1) * LANES] = (
                pad_scr[s, c * PITCH:c * PITCH + CHUNK, :].astype(BF16))
    yield


N_SCAN_STAGES = (1 + CHUNK // REC_PIECE + 2 * (STATE_LANES // SLAB)
                 + (STATE_LANES // SLAB) * (2 * (CHUNK // REC_PIECE) + 2) + TM_B // EPI_ROWS + 1)


def _mlp_stages(x_cols, attn, s_ref, wout_ref, gmlp_ref, wup_ref, wdown_ref, gfin_ref, y_ref, hm_scr, act_scr):
    ncol = D_MODEL // Q_COLS
    for n in range(ncol):
        cols = slice(n * Q_COLS, (n + 1) * Q_COLS)
        y_ref[:, cols] = (x_cols(cols)
                          + jnp.dot(attn(), wout_ref[0:ATTN_WIDTH, cols], preferred_element_type=F32)
                          + jnp.dot(s_ref[...], wout_ref[ATTN_WIDTH:, cols], preferred_element_type=F32))
        yield
    hm_scr[...] = _rms(y_ref[...], gmlp_ref[...]).astype(BF16)
    for c in range(D_FF // FF_BLK):
        act = act_scr.at[c % 2]
        for n in range(FF_BLK // Q_COLS):
            cols = slice(c * FF_BLK + n * Q_COLS, c * FF_BLK + (n + 1) * Q_COLS)
            up = jnp.dot(hm_scr[...], wup_ref[:, cols], preferred_element_type=F32)
            act[:, n * Q_COLS:(n + 1) * Q_COLS] = jnp.square(jnp.maximum(up, 0.0)).astype(BF16)
            yield
        for n in range(ncol):
            cols = slice(n * Q_COLS, (n + 1) * Q_COLS)
            y_ref[:, cols] += jnp.dot(act[...], wdown_ref[c * FF_BLK:(c + 1) * FF_BLK, cols],
                                      preferred_element_type=F32)
            yield
    y_ref[...] = _rms(y_ref[...], gfin_ref[...])
    yield


N_MLP_STAGES = D_MODEL // Q_COLS + (D_FF // FF_BLK) * (FF_BLK // Q_COLS + D_MODEL // Q_COLS) + 1


def _even_merge(n_a, n_b):
    keys = [((k + 0.5) / n_a, "a") for k in range(n_a)] + [((k + 0.5) / n_b, "b") for k in range(n_b)]
    return [name for _, name in sorted(keys)]


def _scan_mlp_kernel(x_ref, a_ref, u_ref, xdec_ref, adec_ref, sdec_ref,
                     lr_ref, li_ref, lcr_ref, lci_ref, bbr_ref, bbi_ref, cre_in, cim_in,
                     d_ref, wglu_ref, bglu_ref, gssm_ref, wout_ref, gmlp_ref, wup_ref, wdown_ref, gfin_ref,
                     y_ref, ydec_ref, hre_out, him_out,
                     ssm_scr, hm_scr, act_scr, bre_ref, bim_ref, cre_ref, cim_ref,
                     pad_scr, up_scr, y_scr, hre, him, carry_re, carry_im, e_re, e_im, hc_re, hc_im, *, nt):
    i = pl.program_id(0)
    dec = i == 0

    @pl.when(dec)
    def _():
        _embed_blocks(bbr_ref, bbi_ref, cre_in, cim_in, bre_ref, bim_ref, cre_ref, cim_ref)
        ssm_scr[...] = sdec_ref[...]

    @pl.when(i % nt == 0)
    def _():
        carry_re[...] = jnp.zeros_like(carry_re)
        carry_im[...] = jnp.zeros_like(carry_im)

    mlp = _mlp_stages(lambda cols: jnp.where(dec, xdec_ref[:, cols], x_ref[:, cols]),
                      lambda: jnp.where(dec, adec_ref[...], a_ref[...]),
                      ssm_scr, wout_ref, gmlp_ref, wup_ref, wdown_ref, gfin_ref, y_ref, hm_scr, act_scr)
    scan = _scan_stages(u_ref, lr_ref, li_ref, lcr_ref, lci_ref, d_ref, wglu_ref, bglu_ref, gssm_ref, ssm_scr,
                        bre_ref, bim_ref, cre_ref, cim_ref,
                        pad_scr, up_scr, y_scr, hre, him, carry_re, carry_im, e_re, e_im, hc_re, hc_im)
    for stage in _even_merge(N_MLP_STAGES, N_SCAN_STAGES):
        next(mlp if stage == "a" else scan)
    assert next(scan, None) is None and next(mlp, None) is None

    @pl.when(dec)
    def _():
        ydec_ref[...] = y_ref[...]

    @pl.when(i % nt == nt - 1)
    def _():
        hre_out[...] = carry_re[...]
        him_out[...] = carry_im[...]


def _scan_mlp(x, attn_n, u, x_dec, attn_dec, ssm_dec, lam, lam_chunk, bbar, c_gcp, d, w_glu, b_glu, g_ssm,
              w_out, g_mlp, w_up, w_down, g_fin):
    bsz, seq, _ = x.shape
    nt = seq // TM_B
    n = bsz * nt
    assert x_dec.shape[0] == TM_B, "the decode rows ride through the MLP as one tile"
    flat = lambda v: v.reshape(n * TM_B, v.shape[-1])
    cur = lambda width: pl.BlockSpec((TM_B, width), lambda i: (jnp.minimum(i, n - 1), 0))
    prev = lambda width: pl.BlockSpec((TM_B, width), lambda i: (jnp.maximum(i - 1, 0), 0))
    state = pl.BlockSpec((None, 1, STATE_LANES), lambda i: (jnp.minimum(i, n - 1) // nt, 0, 0))
    vec = _const_spec((1, STATE_LANES))
    chan = _const_spec((1, SSM_WIDTH))
    model = _const_spec((1, D_MODEL))
    dec = lambda width: _const_spec((TM_B, width))
    y, y_dec, h_re, h_im = pl.pallas_call(
        functools.partial(_scan_mlp_kernel, nt=nt),
        out_shape=(jax.ShapeDtypeStruct((n * TM_B, D_MODEL), F32),
                   jax.ShapeDtypeStruct((TM_B, D_MODEL), F32),
                   jax.ShapeDtypeStruct((bsz, 1, STATE_LANES), F32),
                   jax.ShapeDtypeStruct((bsz, 1, STATE_LANES), F32)),
        grid=(n + 1,),
        in_specs=[prev(D_MODEL), prev(ATTN_WIDTH), cur(SSM_WIDTH), dec(D_MODEL), dec(ATTN_WIDTH), dec(SSM_WIDTH),
                  vec, vec, vec, vec,
                  _group_blocks(), _group_blocks(), _group_blocks(), _group_blocks(),
                  chan, _const_spec((SSM_WIDTH, SSM_WIDTH)), chan, chan,
                  _const_spec((D_MODEL, D_MODEL)), model, _const_spec((D_MODEL, D_FF)),
                  _const_spec((D_FF, D_MODEL)), model],
        out_specs=(prev(D_MODEL), pl.BlockSpec((TM_B, D_MODEL), lambda i: (0, 0)), state, state),
        scratch_shapes=[pltpu.VMEM((TM_B, SSM_WIDTH), BF16),
                        pltpu.VMEM((TM_B, D_MODEL), BF16),
                        pltpu.VMEM((2, TM_B, FF_BLK), BF16)] + _block_operand_scratch() + [
                        pltpu.VMEM((SSM_WIDTH // LANES, SUBLANES * PITCH, LANES), F32),
                        pltpu.VMEM((TM_B, SSM_WIDTH), F32),
                        pltpu.VMEM((TM_B, SSM_WIDTH), F32),
                        pltpu.VMEM((TM_B, STATE_LANES), F32),
                        pltpu.VMEM((TM_B, STATE_LANES), F32),
                        pltpu.VMEM((1, STATE_LANES), F32),
                        pltpu.VMEM((1, STATE_LANES), F32),
                        pltpu.VMEM((SUBLANES, STATE_LANES), F32),
                        pltpu.VMEM((SUBLANES, STATE_LANES), F32),
                        pltpu.VMEM((SUBLANES, STATE_LANES), F32),
                        pltpu.VMEM((SUBLANES, STATE_LANES), F32)],
        compiler_params=pltpu.CompilerParams(
            dimension_semantics=("arbitrary",), vmem_limit_bytes=VMEM_LIMIT),
        name="scan_mlp",
    )(flat(x), flat(attn_n), flat(u), x_dec, attn_dec, ssm_dec, lam[0], lam[1], lam_chunk[0], lam_chunk[1],
      bbar[0], bbar[1], c_gcp[0], c_gcp[1], d, w_glu, b_glu, g_ssm, w_out, g_mlp, w_up, w_down, g_fin)
    return y.reshape(bsz, seq, D_MODEL), y_dec, h_re, h_im


def _block_operand_scratch():
    return [pltpu.VMEM((STATE_LANES // SLAB, SLAB_CH, SLAB), BF16) for _ in range(4)]


def _group_blocks():
    return _const_spec((N_SSM_GROUPS, SSM_GROUP, SSM_STATE))


def _s_scan_kernel(u_ref, h0r_ref, h0i_ref, lr_ref, li_ref, bbr_ref, bbi_ref, cre_in, cim_in,
                   d_ref, wglu_ref, bglu_ref, gssm_ref,
                   out_ref, hre_out, him_out,
                   bre_ref, bim_ref, cre_ref, cim_ref,
                   pad_scr, up_scr, hre, him):
    nseq = h0r_ref.shape[0]
    steps = u_ref.shape[0] // nseq
    nslab = SSM_WIDTH // LANES
    _embed_blocks(bbr_ref, bbi_ref, cre_in, cim_in, bre_ref, bim_ref, cre_ref, cim_ref)

    for s in range(nslab):
        pad_scr[s] = u_ref[:, s * LANES:(s + 1) * LANES]
    for i in range(steps):
        for s in range(nslab):
            up_scr[i * nseq:(i + 1) * nseq, s * LANES:(s + 1) * LANES] = (
                pad_scr[s, pl.ds(i, nseq, stride=steps), :])
    up = up_scr[...]
    upb = up.astype(BF16)

    ys = []
    for sl in range(STATE_LANES // SLAB):
        cols = slice(sl * SLAB, (sl + 1) * SLAB)
        ch = slice(sl * SLAB_CH, (sl + 1) * SLAB_CH)
        hre[:, cols] = jnp.dot(upb[:, ch], bre_ref[sl], preferred_element_type=F32)
        him[:, cols] = jnp.dot(upb[:, ch], bim_ref[sl], preferred_element_type=F32)
        ar = lr_ref[:, cols]
        ai = li_ref[:, cols]
        hr = h0r_ref[:, cols]
        hi = h0i_ref[:, cols]
        for i in range(steps):
            rws = slice(i * nseq, (i + 1) * nseq)
            hr, hi = ar * hr - ai * hi + hre[rws, cols], ar * hi + ai * hr + him[rws, cols]
            hre[rws, cols] = hr
            him[rws, cols] = hi
        hre_out[:, cols] = hr
        him_out[:, cols] = hi
        ys.append(_output_map(hre[:, cols], him[:, cols], cre_ref[sl], cim_ref[sl]))

    y = jnp.concatenate(ys, axis=-1) + d_ref[...] * up
    o = _glu_norm(y, wglu_ref, bglu_ref, gssm_ref)

    for i in range(steps):
        for s in range(nslab):
            pad_scr[s, pl.ds(i, nseq, stride=steps), :] = o[i * nseq:(i + 1) * nseq, s * LANES:(s + 1) * LANES]
    for s in range(nslab):
        out_ref[:, s * LANES:(s + 1) * LANES] = pad_scr[s].astype(BF16)


def _s_scan(u, h0_re, h0_im, lam, bbar, c_gcp, d, w_glu, b_glu, g_ssm):
    rows = u.shape[0]
    nseq = h0_re.shape[0]
    return pl.pallas_call(
        _s_scan_kernel,
        out_shape=(jax.ShapeDtypeStruct((rows, SSM_WIDTH), BF16),
                   jax.ShapeDtypeStruct((nseq, STATE_LANES), F32),
                   jax.ShapeDtypeStruct((nseq, STATE_LANES), F32)),
        scratch_shapes=_block_operand_scratch() + [
                        pltpu.VMEM((SSM_WIDTH // LANES, rows, LANES), F32),
                        pltpu.VMEM((rows, SSM_WIDTH), F32),
                        pltpu.VMEM((rows, STATE_LANES), F32),
                        pltpu.VMEM((rows, STATE_LANES), F32)],
        compiler_params=pltpu.CompilerParams(vmem_limit_bytes=VMEM_LIMIT),
        name="s_scan",
    )(u, h0_re, h0_im, lam[0], lam[1], bbar[0], bbar[1], c_gcp[0], c_gcp[1], d, w_glu, b_glu, g_ssm)


def _pair_heads(w, axis):
    shape = w.shape
    split = shape[:axis] + (N_KV_HEADS, Q_PER_KV, HEAD_DIM) + shape[axis + 1:]
    return jnp.swapaxes(w.reshape(split), axis, axis + 1).reshape(shape)


def kernel(x_prompt, x_sample, cache_k, cache_v, state_ssm_re, state_ssm_im, rel_bias, norm_mix, w_in, attn_sinks, ssm_a_re, ssm_a_im, ssm_log_step, ssm_b_re, ssm_b_im, ssm_c_re, ssm_c_im, ssm_d, w_glu, b_glu, norm_attn_out, norm_ssm_out, w_out, norm_mlp, w_up, w_down, norm_final):
    depth = w_in.shape[0]
    bsz, seq, _ = x_prompt.shape
    nseq, steps, _ = x_sample.shape
    bucket = jnp.asarray(_rel_bucket_table())
    g_fin = norm_final.reshape(1, D_MODEL)
    assert depth == 1, "the final norm is fused into the (single) layer's MLP kernel"

    xp, xs = x_prompt, x_sample
    outs = [[] for _ in range(8)]
    for l in range(depth):
        lr, li, lcr, lci, bbr, bbi = _ssm_prep(ssm_a_re[l], ssm_a_im[l], ssm_log_step[l], ssm_b_re[l], ssm_b_im[l])
        lam = (lr, li)
        lam_chunk = (lcr, lci)
        bbar = (bbr, bbi)
        c_gcp = (ssm_c_re[l], ssm_c_im[l])
        g_mix = norm_mix[l].reshape(1, D_MODEL)
        g_attn = _pair_heads(norm_attn_out[l].reshape(1, ATTN_WIDTH), axis=1)
        g_ssm = norm_ssm_out[l].reshape(1, SSM_WIDTH)
        g_mlp = norm_mlp[l].reshape(1, D_MODEL)
        sinks = attn_sinks[l].reshape(1, N_Q_HEADS)
        d = ssm_d[l].reshape(1, SSM_WIDTH)
        bg = b_glu[l].reshape(1, SSM_WIDTH)
        w_in_b = jnp.concatenate([_pair_heads(w_in[l][:, :ATTN_WIDTH], axis=1), w_in[l][:, ATTN_WIDTH:]],
                                 axis=1).astype(BF16)

        attn_n, u, k1, v1, w_out_b, w_glu_b, w_up_b, w_down_b = _in_attn(
            xp, rel_bias, sinks, g_mix, w_in_b, bucket.T, g_attn.reshape(ATTN_WIDTH, 1),
            w_out[l], w_glu[l], w_up[l], w_down[l])

        ck = cache_k[l].reshape(nseq, WINDOW, KV_WIDTH)
        cv = cache_v[l].reshape(nseq, WINDOW, KV_WIDTH)
        s_attn, s_u, k2, v2 = _s_in_attn(xs, rel_bias, sinks, g_mix, w_in_b, bucket, g_attn, ck, cv)
        s_ssm, r2, i2 = _s_scan(s_u, state_ssm_re[l].reshape(nseq, STATE_LANES),
                                state_ssm_im[l].reshape(nseq, STATE_LANES),
                                lam, bbar, c_gcp, d, w_glu_b, bg, g_ssm)

        xp, xs, r1, i1 = _scan_mlp(xp, attn_n, u, xs.reshape(nseq * steps, D_MODEL), s_attn, s_ssm,
                                   lam, lam_chunk, bbar, c_gcp, d, w_glu_b, bg, g_ssm,
                                   w_out_b, g_mlp, w_up_b, w_down_b, g_fin)
        xs = xs.reshape(nseq, steps, D_MODEL)

        kv_shape = (WINDOW, N_KV_HEADS, HEAD_DIM)
        st_shape = (N_SSM_GROUPS, SSM_STATE)
        for lst, val in zip(outs, (k1.reshape(bsz, *kv_shape), v1.reshape(bsz, *kv_shape),
                                   r1.reshape(bsz, *st_shape), i1.reshape(bsz, *st_shape),
                                   k2.reshape(nseq, *kv_shape), v2.reshape(nseq, *kv_shape),
                                   r2.reshape(nseq, *st_shape), i2.reshape(nseq, *st_shape))):
            lst.append(val)

    return (xp, xs) + tuple(jnp.stack(o) for o in outs)
```

```python
import functools
import math

import numpy as np
import jax
import jax.numpy as jnp
from jax import lax
from jax.experimental import pallas as pl
from jax.experimental.pallas import tpu as pltpu

D_MODEL = 1024
HEAD_DIM = 64
ATTN_WIDTH = 512
N_Q_HEADS = 8
N_KV_HEADS = 2
Q_PER_KV = 4
KV_WIDTH = 128
WINDOW = 128
BLOCK = 128
N_BUCKETS = 32
MAX_DISTANCE = 128
SSM_WIDTH = 512
SSM_GROUP = 16
N_SSM_GROUPS = 32
SSM_STATE = 64
STATE_LANES = N_SSM_GROUPS * SSM_STATE
D_FF = 4096
D_IN_PROJ = 1280
EPS = 1e-6
NEG = -1e30
LOG2E = math.log2(math.e)

LANES = 128
SUBLANES = 8
MXU_DIM = 256
VMEM_LIMIT = 60 * 1024 * 1024

F32 = jnp.float32
BF16 = jnp.bfloat16

TM_A = 512
TM_B = 512
CHUNK = TM_B // SUBLANES
PITCH = CHUNK + SUBLANES
S_SEQ = 16
NEW_ROWS = 16


def _rms(x, g):
    return x * lax.rsqrt(jnp.mean(x * x, axis=-1, keepdims=True) + EPS) * g


def _rel_bucket_table():
    i = np.arange(BLOCK)[:, None]
    j = np.arange(2 * BLOCK)[None, :]
    n = np.maximum(i + BLOCK - j, 0)
    max_exact = N_BUCKETS // 2
    nf = np.maximum(n, max_exact).astype(np.float32)
    large = max_exact + (np.log(nf / np.float32(max_exact)) / np.float32(math.log(MAX_DISTANCE / max_exact))
                         * np.float32(N_BUCKETS - max_exact)).astype(np.int32)
    large = np.minimum(large, N_BUCKETS - 1)
    return np.where(n < max_exact, n, large).astype(np.int32)


def _build_bias(relb_ref, bucket_ref, bias_scr):
    bucket = bucket_ref[...]
    for h in range(N_Q_HEADS):
        bias_scr[h] = jnp.zeros(bucket.shape, F32)
    for b in range(N_BUCKETS):
        hit = bucket == b
        for h in range(N_Q_HEADS):
            bias_scr[h] = jnp.where(hit, relb_ref[b, h], bias_scr[h])


def _build_bias_t(relb_ref, bucket_t_ref, bias_t_scr):
    bucket_t = bucket_t_ref[...]
    bias_t_scr[...] = jnp.zeros_like(bias_t_scr)
    for b in range(N_BUCKETS):
        hit = bucket_t == b
        for h in range(N_Q_HEADS):
            cols = slice(h * BLOCK, (h + 1) * BLOCK)
            bias_t_scr[:, cols] = jnp.where(hit, relb_ref[b, h] * LOG2E, bias_t_scr[:, cols])
    kj = lax.broadcasted_iota(jnp.int32, (2 * BLOCK, BLOCK), 0)
    qi = lax.broadcasted_iota(jnp.int32, (2 * BLOCK, BLOCK), 1)
    dist = qi + BLOCK - kj
    in_band = jnp.logical_and(dist >= 0, dist < WINDOW)
    for h in range(N_Q_HEADS):
        cols = slice(h * BLOCK, (h + 1) * BLOCK)
        bias_t_scr[:, cols] = jnp.where(in_band, bias_t_scr[:, cols], NEG)


def _ssm_prep_kernel(are_ref, aim_ref, ls_ref, btr_ref, bti_ref,
                     lr_ref, li_ref, lcr_ref, lci_ref, bbr_ref, bbi_ref):
    ar = are_ref[...]
    ai = aim_ref[...]
    step = jnp.exp(ls_ref[...])
    zr = ar * step
    zi = ai * step
    mag = jnp.exp(zr)
    lr = mag * jnp.cos(zi)
    li = mag * jnp.sin(zi)
    pr, pi = lr, li
    for _ in range(int(math.log2(CHUNK))):
        pr, pi = pr * pr - pi * pi, 2.0 * pr * pi
    for g in range(N_SSM_GROUPS):
        st = slice(g * SSM_STATE, (g + 1) * SSM_STATE)
        lr_ref[:, st] = lr[g]
        li_ref[:, st] = li[g]
        lcr_ref[:, st] = pr[g]
        lci_ref[:, st] = pi[g]
    nr = lr - 1.0
    ni = li
    den = ar * ar + ai * ai
    cr = (nr * ar + ni * ai) / den
    ci = (ni * ar - nr * ai) / den
    btr = btr_ref[...]
    bti = bti_ref[...]
    bbr_ref[...] = cr * btr - ci * bti
    bbi_ref[...] = cr * bti + ci * btr


def _ssm_prep(a_re, a_im, log_step, b_re, b_im):
    g, p, c = N_SSM_GROUPS, SSM_STATE, SSM_GROUP
    vec = jax.ShapeDtypeStruct((1, STATE_LANES), F32)
    mat = jax.ShapeDtypeStruct((g, c, p), F32)
    return pl.pallas_call(
        _ssm_prep_kernel,
        out_shape=(vec, vec, vec, vec, mat, mat),
        name="ssm_prep",
    )(a_re.reshape(g, 1, p), a_im.reshape(g, 1, p), log_step.reshape(g, 1, 1),
      jnp.transpose(b_re, (0, 2, 1)), jnp.transpose(b_im, (0, 2, 1)))


def _stack_heads(q_blk, lane_lo):
    zero = jnp.zeros_like(q_blk[:, 0:LANES])
    lo, hi = [], []
    for c in range(ATTN_WIDTH // LANES):
        blk = q_blk[:, c * LANES:(c + 1) * LANES]
        lo.append(jnp.where(lane_lo, blk, zero))
        hi.append(jnp.where(lane_lo, zero, blk))
    return jnp.concatenate(lo + hi, axis=0)


def _unstack_heads(o, rows, lane_lo):
    nblk = ATTN_WIDTH // LANES
    return jnp.concatenate(
        [jnp.where(lane_lo, o[c * rows:(c + 1) * rows], o[(c + nblk) * rows:(c + nblk + 1) * rows])
         for c in range(nblk)], axis=-1)


Q_COLS = MXU_DIM


def _proj_stages(x_ref, gmix_ref, win_ref, q_stage, k_stage, vt_stage, u_ref, newk_ref, newv_ref):
    xn = _rms(x_ref[...], gmix_ref[...]).astype(BF16)
    q_scale = HEAD_DIM ** -0.5 * LOG2E
    for c in range(ATTN_WIDTH // Q_COLS):
        cols = slice(c * Q_COLS, (c + 1) * Q_COLS)
        q = jnp.dot(xn, win_ref[:, cols], preferred_element_type=F32)
        q_stage[:, cols] = (q * q_scale).astype(BF16)
        yield
    kv = jnp.dot(xn, win_ref[:, ATTN_WIDTH:ATTN_WIDTH + 2 * KV_WIDTH], preferred_element_type=F32)
    k_stage[...] = kv[:, :KV_WIDTH].astype(BF16)
    vt_stage[...] = kv[:, KV_WIDTH:].T.astype(BF16)
    newk_ref[...] = kv[TM_A - WINDOW:, :KV_WIDTH]
    newv_ref[...] = kv[TM_A - WINDOW:, KV_WIDTH:]
    yield
    u0 = ATTN_WIDTH + 2 * KV_WIDTH
    for c in range(SSM_WIDTH // Q_COLS):
        cols = slice(u0 + c * Q_COLS, u0 + (c + 1) * Q_COLS)
        u_ref[:, c * Q_COLS:(c + 1) * Q_COLS] = jnp.dot(xn, win_ref[:, cols], preferred_element_type=F32)
        yield


def _attn_stages(sinks_ref, attn_ref, k_scr, vt_scr, q_scr, bias_t_scr, gt_scr, first_tile):
    nstk = N_Q_HEADS * BLOCK
    nblk = ATTN_WIDTH // LANES
    lane_lo = lax.broadcasted_iota(jnp.int32, (BLOCK, LANES), 1) < HEAD_DIM
    row_lo = lax.broadcasted_iota(jnp.int32, (KV_WIDTH, BLOCK), 0) < HEAD_DIM
    kj = lax.broadcasted_iota(jnp.int32, (2 * BLOCK, nstk), 0)
    has_prev = jnp.logical_or(kj >= BLOCK, jnp.logical_not(first_tile))
    head = lax.broadcasted_iota(jnp.int32, (1, nstk), 1) // BLOCK
    sink = jnp.zeros((1, nstk), F32)
    for h in range(N_Q_HEADS):
        sink = jnp.where(head == h, sinks_ref[0, h] * LOG2E, sink)

    def score(j):
        r0 = j * BLOCK
        q2 = _stack_heads(q_scr[r0:r0 + BLOCK, :], lane_lo)
        keys = k_scr[r0:r0 + 2 * BLOCK, :]
        return lax.dot_general(keys, q2, (((1,), (1,)), ((), ())), preferred_element_type=F32)

    nb = TM_A // BLOCK
    scores = {0: score(0)}
    yield
    for j in range(nb):
        if j + 1 < nb:
            scores[j + 1] = score(j + 1)
            yield
        r0 = j * BLOCK
        s = scores.pop(j) + bias_t_scr[...]
        if j == 0:
            s = jnp.where(has_prev, s, NEG)
        m = jnp.maximum(jnp.max(s, axis=0, keepdims=True), sink)
        p = jnp.exp2(s - m)
        denom = jnp.sum(p, axis=0, keepdims=True) + jnp.exp2(sink - m)
        o = jnp.dot(vt_scr[:, r0:r0 + 2 * BLOCK], p.astype(BF16), preferred_element_type=F32) / denom
        a = jnp.concatenate(
            [jnp.where(row_lo, o[:, c * BLOCK:(c + 1) * BLOCK], o[:, (c + nblk) * BLOCK:(c + nblk + 1) * BLOCK])
             for c in range(nblk)], axis=0)
        a = a * lax.rsqrt(jnp.mean(a * a, axis=0, keepdims=True) + EPS) * gt_scr[...]
        attn_ref[r0:r0 + BLOCK, :] = a.T.astype(BF16)
        yield


def _in_proj_kernel(x_ref, gmix_ref, win_ref, wout_f, wglu_f, wup_f, wdown_f,
                    q_ref, k_ref, vt_ref, u_ref, newk_ref, newv_ref, wout_b, wglu_b, wup_b, wdown_b):
    wout_b[...] = wout_f[...].astype(BF16)
    wglu_b[...] = wglu_f[...].astype(BF16)
    wup_b[...] = wup_f[...].astype(BF16)
    wdown_b[...] = wdown_f[...].astype(BF16)
    for _ in _proj_stages(x_ref, gmix_ref, win_ref, q_ref, k_ref, vt_ref, u_ref, newk_ref, newv_ref):
        pass


def _const_spec(shape):
    zeros = (0,) * len(shape)
    return pl.BlockSpec(shape, lambda *_: zeros, pipeline_mode=pl.Buffered(1))


def _smem_spec():
    return pl.BlockSpec(memory_space=pltpu.SMEM)


def _in_proj(x, g_mix, w_in, w_out, w_glu, w_up, w_down):
    bsz, seq, _ = x.shape
    nt = seq // TM_A
    steps = bsz * nt
    row = lambda width: pl.BlockSpec((TM_A, width), lambda i: (i, 0))
    win_spec = pl.BlockSpec((None, WINDOW, KV_WIDTH), lambda i: (i // nt, 0, 0))

    def rows_of(w):
        blk = w.shape[0] // steps
        return pl.BlockSpec((blk, w.shape[1]), lambda i: (i, 0))

    nhead_blk = D_MODEL // HEAD_DIM

    def wout_src(i):
        return (jnp.minimum(i, nhead_blk - 1), 0)

    def wout_dst(i):
        j = jnp.minimum(i, nhead_blk - 1)
        return (jnp.where(j < N_Q_HEADS, (j % Q_PER_KV) * N_KV_HEADS + j // Q_PER_KV, j), 0)

    assert steps >= nhead_blk and w_up.shape[0] % steps == 0 and w_down.shape[0] % steps == 0
    wout_in = pl.BlockSpec((HEAD_DIM, D_MODEL), wout_src)
    wout_out = pl.BlockSpec((HEAD_DIM, D_MODEL), wout_dst)
    cast_shape = lambda w: jax.ShapeDtypeStruct(w.shape, BF16)
    return pl.pallas_call(
        _in_proj_kernel,
        out_shape=(jax.ShapeDtypeStruct((bsz * seq, ATTN_WIDTH), BF16),
                   jax.ShapeDtypeStruct((bsz * seq, KV_WIDTH), BF16),
                   jax.ShapeDtypeStruct((KV_WIDTH, bsz * seq), BF16),
                   jax.ShapeDtypeStruct((bsz * seq, SSM_WIDTH), F32),
                   jax.ShapeDtypeStruct((bsz, WINDOW, KV_WIDTH), F32),
                   jax.ShapeDtypeStruct((bsz, WINDOW, KV_WIDTH), F32),
                   cast_shape(w_out), cast_shape(w_glu), cast_shape(w_up), cast_shape(w_down)),
        grid=(steps,),
        in_specs=[row(D_MODEL), _const_spec((1, D_MODEL)), _const_spec((D_MODEL, D_IN_PROJ)),
                  wout_in, rows_of(w_glu), rows_of(w_up), rows_of(w_down)],
        out_specs=(row(ATTN_WIDTH), row(KV_WIDTH), pl.BlockSpec((KV_WIDTH, TM_A), lambda i: (0, i)),
                   row(SSM_WIDTH), win_spec, win_spec,
                   wout_out, rows_of(w_glu), rows_of(w_up), rows_of(w_down)),
        compiler_params=pltpu.CompilerParams(
            dimension_semantics=("arbitrary",), vmem_limit_bytes=VMEM_LIMIT),
        name="in_proj",
    )(x.reshape(bsz * seq, D_MODEL), g_mix, w_in, w_out, w_glu, w_up, w_down)


def _s_in_attn_kernel(relb_ref, sinks_ref, x_ref, gmix_ref, win_ref, bucket_ref, gattn_ref,
                      ck_ref, cv_ref,
                      attn_ref, u_ref, newk_ref, newv_ref,
                      bias_scr, biasd_scr, proj_scr):
    rows = attn_ref.shape[0]
    steps = rows // S_SEQ
    pair = SUBLANES // steps
    prow = N_Q_HEADS * SUBLANES

    @pl.when(pl.program_id(0) == 0)
    def _():
        _build_bias(relb_ref, bucket_ref, bias_scr)
        for h in range(N_Q_HEADS):
            for k in range(pair):
                biasd_scr[h * SUBLANES + k * steps:h * SUBLANES + (k + 1) * steps, :] = bias_scr[h, 0:steps, :]
        xn = _rms(x_ref[...], gmix_ref[...]).astype(BF16)
        proj_all = jnp.dot(xn, win_ref[...], preferred_element_type=F32)
        proj_scr[...] = proj_all
        u_ref[...] = proj_all[:, ATTN_WIDTH + 2 * KV_WIDTH:]

    r0 = pl.multiple_of(pl.program_id(0) * rows, rows)
    proj = proj_scr[pl.ds(r0, rows), :]
    qs = proj[:, :ATTN_WIDTH] * (HEAD_DIM ** -0.5)

    grp = NEW_ROWS // SUBLANES
    nsq = grp * pair
    row = lax.broadcasted_iota(jnp.int32, (grp * prow, 1), 0)
    qseq = row // prow * pair + (row % SUBLANES) // steps
    qrow = row % steps
    kcol = lax.broadcasted_iota(jnp.int32, (grp * prow, nsq * WINDOW), 1)
    valid_c = jnp.logical_and(kcol // WINDOW == qseq, kcol % WINDOW > qrow)
    kcol_n = lax.broadcasted_iota(jnp.int32, (grp * prow, NEW_ROWS), 1)
    valid_n = jnp.logical_and(kcol_n // steps == qseq, kcol_n % steps <= qrow)
    sink = jnp.zeros((grp * prow, 1), F32)
    for h in range(N_Q_HEADS):
        sink = jnp.where(row % prow // SUBLANES == h, sinks_ref[0, h], sink)
    lane_lo = lax.broadcasted_iota(jnp.int32, (SUBLANES, LANES), 1) < HEAD_DIM
    gattn = gattn_ref[...]
    bias_rows = jnp.concatenate([biasd_scr[...]] * grp, axis=0)
    bias_c = jnp.concatenate([bias_rows[:, 0:WINDOW]] * nsq, axis=1)
    bias_new = bias_rows[:, WINDOW:2 * WINDOW]
    bias_n = bias_new
    lane = lax.broadcasted_iota(jnp.int32, (grp * prow, WINDOW), 1)
    for k in range(1, nsq):
        bias_n = jnp.where(lane >= k * steps, pltpu.roll(bias_new, k * steps, axis=1), bias_n)
    bias_n = bias_n[:, 0:NEW_ROWS]
    nt_dims = (((1,), (1,)), ((), ()))

    tiles = []
    for gi in range(S_SEQ // nsq):
        rws = slice(gi * NEW_ROWS, (gi + 1) * NEW_ROWS)
        q2 = jnp.concatenate(
            [_stack_heads(qs[gi * NEW_ROWS + t * SUBLANES:gi * NEW_ROWS + (t + 1) * SUBLANES], lane_lo)
             for t in range(grp)], axis=0).astype(BF16)
        kn = proj[rws, ATTN_WIDTH:ATTN_WIDTH + KV_WIDTH]
        vn = proj[rws, ATTN_WIDTH + KV_WIDTH:ATTN_WIDTH + 2 * KV_WIDTH]
        cks, cvs = [], []
        for k in range(nsq):
            sq = gi * nsq + k
            ck = ck_ref[sq]
            cv = cv_ref[sq]
            newk_ref[sq, 0:WINDOW - steps, :] = ck[steps:, :]
            newk_ref[sq, WINDOW - steps:, :] = kn[k * steps:(k + 1) * steps]
            newv_ref[sq, 0:WINDOW - steps, :] = cv[steps:, :]
            newv_ref[sq, WINDOW - steps:, :] = vn[k * steps:(k + 1) * steps]
            cks.append(ck.astype(BF16))
            cvs.append(cv.astype(BF16))
        s_c = lax.dot_general(q2, jnp.concatenate(cks, axis=0), nt_dims, preferred_element_type=F32)
        s_n = lax.dot_general(q2, kn.astype(BF16), nt_dims, preferred_element_type=F32)
        s_c = jnp.where(valid_c, s_c + bias_c, NEG)
        s_n = jnp.where(valid_n, s_n + bias_n, NEG)
        m = jnp.maximum(jnp.maximum(jnp.max(s_c, axis=-1, keepdims=True),
                                    jnp.max(s_n, axis=-1, keepdims=True)), sink)
        p_c = jnp.exp(s_c - m)
        p_n = jnp.exp(s_n - m)
        denom = (jnp.sum(p_c, axis=-1, keepdims=True) + jnp.sum(p_n, axis=-1, keepdims=True)
                 + jnp.exp(sink - m))
        o = (jnp.dot(p_c.astype(BF16), jnp.concatenate(cvs, axis=0), preferred_element_type=F32)
             + jnp.dot(p_n.astype(BF16), vn.astype(BF16), preferred_element_type=F32)) / denom
        for t in range(grp):
            tiles.append(_unstack_heads(o[t * prow:(t + 1) * prow], SUBLANES, lane_lo))
    attn_ref[...] = _rms(jnp.concatenate(tiles, axis=0), gattn).astype(BF16)


def _s_in_attn(x, rel_bias, sinks, g_mix, w_in, bucket, g_attn, cache_k, cache_v):
    nseq, steps, _ = x.shape
    rows = S_SEQ * steps
    xf = x.reshape(nseq * steps, D_MODEL)
    row = lambda width: pl.BlockSpec((rows, width), lambda i: (i, 0))
    cache = pl.BlockSpec((S_SEQ, WINDOW, KV_WIDTH), lambda i: (i, 0, 0))
    return pl.pallas_call(
        _s_in_attn_kernel,
        out_shape=(jax.ShapeDtypeStruct((nseq * steps, ATTN_WIDTH), BF16),
                   jax.ShapeDtypeStruct((nseq * steps, SSM_WIDTH), F32),
                   jax.ShapeDtypeStruct((nseq, WINDOW, KV_WIDTH), F32),
                   jax.ShapeDtypeStruct((nseq, WINDOW, KV_WIDTH), F32)),
        grid=(nseq // S_SEQ,),
        in_specs=[_smem_spec(), _smem_spec(), _const_spec((nseq * steps, D_MODEL)), _const_spec((1, D_MODEL)),
                  _const_spec((D_MODEL, D_IN_PROJ)), _const_spec((SUBLANES, 2 * BLOCK)),
                  _const_spec((1, ATTN_WIDTH)), cache, cache],
        out_specs=(row(ATTN_WIDTH), pl.BlockSpec((nseq * steps, SSM_WIDTH), lambda i: (0, 0)), cache, cache),
        scratch_shapes=[pltpu.VMEM((N_Q_HEADS, SUBLANES, 2 * BLOCK), F32),
                        pltpu.VMEM((N_Q_HEADS * SUBLANES, 2 * WINDOW), F32),
                        pltpu.VMEM((nseq * steps, D_IN_PROJ), F32)],
        compiler_params=pltpu.CompilerParams(
            dimension_semantics=("arbitrary",), vmem_limit_bytes=VMEM_LIMIT),
        name="s_in_attn",
    )(rel_bias, sinks, xf, g_mix, w_in, bucket[0:SUBLANES], g_attn, cache_k, cache_v)


def _embed_blocks(bbr_ref, bbi_ref, cre_in, cim_in, bre, bim, cre, cim):
    for ref in (bre, bim, cre, cim):
        ref[...] = jnp.zeros_like(ref)
    for g in range(N_SSM_GROUPS):
        sl, j = divmod(g, SLAB // SSM_STATE)
        ch = slice(j * SSM_GROUP, (j + 1) * SSM_GROUP)
        st = slice(j * SSM_STATE, (j + 1) * SSM_STATE)
        bre[sl, ch, st] = bbr_ref[g].astype(BF16)
        bim[sl, ch, st] = bbi_ref[g].astype(BF16)
        cre[sl, ch, st] = cre_in[g].astype(BF16)
        cim[sl, ch, st] = cim_in[g].astype(BF16)


def _output_map(h_slab_re, h_slab_im, c_slab_re, c_slab_im):
    nt_dims = (((1,), (1,)), ((), ()))
    yr = lax.dot_general(h_slab_re.astype(BF16), c_slab_re, nt_dims, preferred_element_type=F32)
    yi = lax.dot_general(h_slab_im.astype(BF16), c_slab_im, nt_dims, preferred_element_type=F32)
    return yr - yi


def _glu_norm(y, wglu_ref, bglu_ref, gssm_ref):
    g = jax.nn.gelu(y)
    gate = jax.nn.sigmoid(jnp.dot(g.astype(BF16), wglu_ref[...], preferred_element_type=F32) + bglu_ref[...])
    return _rms(g * gate, gssm_ref[...])


SLAB = 4 * LANES
SLAB_CH = SLAB // SSM_STATE * SSM_GROUP


REC_PIECE = 16
EPI_ROWS = 128
FF_BLK = 1024


def _scan_stages(u_ref, lr_ref, li_ref, lcr_ref, lci_ref, d_ref, wglu_ref, bglu_ref, gssm_ref, out_ref,
                 bre_ref, bim_ref, cre_ref, cim_ref,
                 pad_scr, up_scr, y_scr, hre, him, carry_re, carry_im, e_re, e_im, hc_re, hc_im):
    nslab = SSM_WIDTH // LANES
    nslabs = STATE_LANES // SLAB

    for c in range(SUBLANES):
        for s in range(nslab):
            pad_scr[s, c * PITCH:c * PITCH + CHUNK, :] = u_ref[c * CHUNK:(c + 1) * CHUNK, s * LANES:(s + 1) * LANES]
    yield
    for tau in range(CHUNK):
        for s in range(nslab):
            up_scr[tau * SUBLANES:(tau + 1) * SUBLANES, s * LANES:(s + 1) * LANES] = (
                pad_scr[s, pl.ds(tau, SUBLANES, stride=PITCH), :])
        if tau % REC_PIECE == REC_PIECE - 1:
            yield

    for sl in range(nslabs):
        cols = slice(sl * SLAB, (sl + 1) * SLAB)
        ch = slice(sl * SLAB_CH, (sl + 1) * SLAB_CH)
        upb = up_scr[:, ch].astype(BF16)
        hre[:, cols] = jnp.dot(upb, bre_ref[sl], preferred_element_type=F32)
        yield
        him[:, cols] = jnp.dot(upb, bim_ref[sl], preferred_element_type=F32)
        yield

    for sl in range(nslabs):
        cols = slice(sl * SLAB, (sl + 1) * SLAB)
        ch = slice(sl * SLAB_CH, (sl + 1) * SLAB_CH)
        ar = jnp.broadcast_to(lr_ref[:, cols], (SUBLANES, SLAB))
        ai = jnp.broadcast_to(li_ref[:, cols], (SUBLANES, SLAB))

        hr = jnp.zeros((SUBLANES, SLAB), F32)
        hi = jnp.zeros((SUBLANES, SLAB), F32)
        for tau in range(CHUNK):
            rws = slice(tau * SUBLANES, (tau + 1) * SUBLANES)
            hr, hi = ar * hr - ai * hi + hre[rws, cols], ar * hi + ai * hr + him[rws, cols]
            if tau % REC_PIECE == REC_PIECE - 1:
                yield
        e_re[:, cols] = hr
        e_im[:, cols] = hi

        cr = carry_re[:, cols]
        ci = carry_im[:, cols]
        lcr = lcr_ref[:, cols]
        lci = lci_ref[:, cols]
        for c in range(SUBLANES):
            hc_re[c:c + 1, cols] = cr
            hc_im[c:c + 1, cols] = ci
            er = e_re[c:c + 1, cols]
            ei = e_im[c:c + 1, cols]
            cr, ci = lcr * cr - lci * ci + er, lcr * ci + lci * cr + ei
        carry_re[:, cols] = cr
        carry_im[:, cols] = ci
        yield

        hr = hc_re[:, cols]
        hi = hc_im[:, cols]
        for tau in range(CHUNK):
            rws = slice(tau * SUBLANES, (tau + 1) * SUBLANES)
            hr, hi = ar * hr - ai * hi + hre[rws, cols], ar * hi + ai * hr + him[rws, cols]
            hre[rws, cols] = hr
            him[rws, cols] = hi
            if tau % REC_PIECE == REC_PIECE - 1:
                yield

        y_scr[:, ch] = _output_map(hre[:, cols], him[:, cols], cre_ref[sl], cim_ref[sl])
        yield

    taus = EPI_ROWS // SUBLANES
    for r in range(TM_B // EPI_ROWS):
        rows = slice(r * EPI_ROWS, (r + 1) * EPI_ROWS)
        o = _glu_norm(y_scr[rows, :] + d_ref[...] * up_scr[rows, :], wglu_ref, bglu_ref, gssm_ref)
        for i in range(taus):
            for s in range(nslab):
                pad_scr[s, pl.ds(r * taus + i, SUBLANES, stride=PITCH), :] = (
                    o[i * SUBLANES:(i + 1) * SUBLANES, s * LANES:(s + 1) * LANES])
        yield
    for c in range(SUBLANES):
        for s in range(nslab):
            out_ref[c * CHUNK:(c + 1) * CHUNK, s * LANES:(s + 1) * LANES] = (
                pad_scr[s, c * PITCH:c * PITCH + CHUNK, :].astype(BF16))
    yield


N_SCAN_STAGES = (1 + CHUNK // REC_PIECE + 2 * (STATE_LANES // SLAB)
                 + (STATE_LANES // SLAB) * (2 * (CHUNK // REC_PIECE) + 2) + TM_B // EPI_ROWS + 1)


def _mlp_stages(x_cols, attn, s_ref, wout_ref, gmlp_ref, wup_ref, wdown_ref, gfin_ref, y_ref, hm_scr, act_scr):
    ncol = D_MODEL // Q_COLS
    for n in range(ncol):
        cols = slice(n * Q_COLS, (n + 1) * Q_COLS)
        y_ref[:, cols] = (x_cols(cols)
                          + jnp.dot(attn(), wout_ref[0:ATTN_WIDTH, cols], preferred_element_type=F32)
                          + jnp.dot(s_ref[...], wout_ref[ATTN_WIDTH:, cols], preferred_element_type=F32))
        yield
    hm_scr[...] = _rms(y_ref[...], gmlp_ref[...]).astype(BF16)
    for c in range(D_FF // FF_BLK):
        act = act_scr.at[c % 2]
        for n in range(FF_BLK // Q_COLS):
            cols = slice(c * FF_BLK + n * Q_COLS, c * FF_BLK + (n + 1) * Q_COLS)
            up = jnp.dot(hm_scr[...], wup_ref[:, cols], preferred_element_type=F32)
            act[:, n * Q_COLS:(n + 1) * Q_COLS] = jnp.square(jnp.maximum(up, 0.0)).astype(BF16)
            yield
        for n in range(ncol):
            cols = slice(n * Q_COLS, (n + 1) * Q_COLS)
            y_ref[:, cols] += jnp.dot(act[...], wdown_ref[c * FF_BLK:(c + 1) * FF_BLK, cols],
                                      preferred_element_type=F32)
            yield
    y_ref[...] = _rms(y_ref[...], gfin_ref[...])
    yield


N_MLP_STAGES = D_MODEL // Q_COLS + (D_FF // FF_BLK) * (FF_BLK // Q_COLS + D_MODEL // Q_COLS) + 1


N_ATTN_STAGES = 2 * (TM_A // BLOCK)


def _even_merge(counts):
    keys = [((k + 0.5) / n, name) for name, n in counts.items() for k in range(n)]
    return [name for _, name in sorted(keys)]


def _scan_mlp_kernel(relb_ref, sinks_ref, x_ref, q_ref, k_ref, kprev_ref, vt_ref, vtprev_ref, u_ref,
                     xdec_ref, adec_ref, sdec_ref, bucket_t_ref, gattn_ref,
                     lr_ref, li_ref, lcr_ref, lci_ref, bbr_ref, bbi_ref, cre_in, cim_in,
                     d_ref, wglu_ref, bglu_ref, gssm_ref, wout_ref, gmlp_ref, wup_ref, wdown_ref, gfin_ref,
                     y_ref, ydec_ref, hre_out, him_out,
                     ssm_scr, attn_scr, k_scr, vt_scr, bias_t_scr, gt_scr,
                     hm_scr, act_scr, bre_ref, bim_ref, cre_ref, cim_ref,
                     pad_scr, up_scr, y_scr, hre, him, carry_re, carry_im, e_re, e_im, hc_re, hc_im, *, nt):
    i = pl.program_id(0)
    dec = i == 0

    @pl.when(dec)
    def _():
        _embed_blocks(bbr_ref, bbi_ref, cre_in, cim_in, bre_ref, bim_ref, cre_ref, cim_ref)
        ssm_scr[...] = sdec_ref[...]
        attn_scr[...] = adec_ref[...]
        _build_bias_t(relb_ref, bucket_t_ref, bias_t_scr)
        gt_scr[...] = jnp.broadcast_to(gattn_ref[...], (ATTN_WIDTH, BLOCK))

    k_scr[0:BLOCK, :] = kprev_ref[...]
    k_scr[BLOCK:, :] = k_ref[...]
    vt_scr[:, 0:BLOCK] = vtprev_ref[...]
    vt_scr[:, BLOCK:] = vt_ref[...]

    @pl.when(i % nt == 0)
    def _():
        carry_re[...] = jnp.zeros_like(carry_re)
        carry_im[...] = jnp.zeros_like(carry_im)

    mlp = _mlp_stages(lambda cols: jnp.where(dec, xdec_ref[:, cols], x_ref[:, cols]),
                      lambda: attn_scr[...],
                      ssm_scr, wout_ref, gmlp_ref, wup_ref, wdown_ref, gfin_ref, y_ref, hm_scr, act_scr)
    scan = _scan_stages(u_ref, lr_ref, li_ref, lcr_ref, lci_ref, d_ref, wglu_ref, bglu_ref, gssm_ref, ssm_scr,
                        bre_ref, bim_ref, cre_ref, cim_ref,
                        pad_scr, up_scr, y_scr, hre, him, carry_re, carry_im, e_re, e_im, hc_re, hc_im)
    attn = _attn_stages(sinks_ref, attn_scr, k_scr, vt_scr, q_ref, bias_t_scr, gt_scr, i % nt == 0)
    gens = {"mlp": mlp, "scan": scan, "attn": attn}
    order = _even_merge({"mlp": N_MLP_STAGES, "scan": N_SCAN_STAGES, "attn": N_ATTN_STAGES})
    for name in order:
        next(gens[name])
    assert all(next(g, None) is None for g in gens.values())

    @pl.when(dec)
    def _():
        ydec_ref[...] = y_ref[...]

    @pl.when(i % nt == nt - 1)
    def _():
        hre_out[...] = carry_re[...]
        him_out[...] = carry_im[...]


def _scan_mlp(x, q, k, vt, u, x_dec, attn_dec, ssm_dec, rel_bias, sinks, bucket_t, g_attn_col,
              lam, lam_chunk, bbar, c_gcp, d, w_glu, b_glu, g_ssm, w_out, g_mlp, w_up, w_down, g_fin):
    bsz, seq, _ = x.shape
    nt = seq // TM_B
    n = bsz * nt
    assert x_dec.shape[0] == TM_B, "the decode rows ride through the MLP as one tile"
    assert TM_A == TM_B
    flat = lambda v: v.reshape(n * TM_B, v.shape[-1])
    tile = lambda i: jnp.minimum(i, n - 1)
    halo = lambda i: jnp.maximum(tile(i) * (TM_B // BLOCK) - 1, 0)
    cur = lambda width: pl.BlockSpec((TM_B, width), lambda i: (tile(i), 0))
    prev = lambda width: pl.BlockSpec((TM_B, width), lambda i: (jnp.maximum(i - 1, 0), 0))
    state = pl.BlockSpec((None, 1, STATE_LANES), lambda i: (jnp.minimum(i, n - 1) // nt, 0, 0))
    vec = _const_spec((1, STATE_LANES))
    chan = _const_spec((1, SSM_WIDTH))
    model = _const_spec((1, D_MODEL))
    dec = lambda width: _const_spec((TM_B, width))
    y, y_dec, h_re, h_im = pl.pallas_call(
        functools.partial(_scan_mlp_kernel, nt=nt),
        out_shape=(jax.ShapeDtypeStruct((n * TM_B, D_MODEL), F32),
                   jax.ShapeDtypeStruct((TM_B, D_MODEL), F32),
                   jax.ShapeDtypeStruct((bsz, 1, STATE_LANES), F32),
                   jax.ShapeDtypeStruct((bsz, 1, STATE_LANES), F32)),
        grid=(n + 1,),
        in_specs=[_smem_spec(), _smem_spec(), prev(D_MODEL), cur(ATTN_WIDTH), cur(KV_WIDTH),
                  pl.BlockSpec((BLOCK, KV_WIDTH), lambda i: (halo(i), 0)),
                  pl.BlockSpec((KV_WIDTH, TM_B), lambda i: (0, tile(i))),
                  pl.BlockSpec((KV_WIDTH, BLOCK), lambda i: (0, halo(i))),
                  cur(SSM_WIDTH), dec(D_MODEL), dec(ATTN_WIDTH), dec(SSM_WIDTH),
                  _const_spec((2 * BLOCK, BLOCK)), _const_spec((ATTN_WIDTH, 1)),
                  vec, vec, vec, vec,
                  _group_blocks(), _group_blocks(), _group_blocks(), _group_blocks(),
                  chan, _const_spec((SSM_WIDTH, SSM_WIDTH)), chan, chan,
                  _const_spec((D_MODEL, D_MODEL)), model, _const_spec((D_MODEL, D_FF)),
                  _const_spec((D_FF, D_MODEL)), model],
        out_specs=(prev(D_MODEL), pl.BlockSpec((TM_B, D_MODEL), lambda i: (0, 0)), state, state),
        scratch_shapes=[pltpu.VMEM((TM_B, SSM_WIDTH), BF16),
                        pltpu.VMEM((TM_B, ATTN_WIDTH), BF16),
                        pltpu.VMEM((TM_B + BLOCK, KV_WIDTH), BF16),
                        pltpu.VMEM((KV_WIDTH, TM_B + BLOCK), BF16),
                        pltpu.VMEM((2 * BLOCK, N_Q_HEADS * BLOCK), F32),
                        pltpu.VMEM((ATTN_WIDTH, BLOCK), F32),
                        pltpu.VMEM((TM_B, D_MODEL), BF16),
                        pltpu.VMEM((2, TM_B, FF_BLK), BF16)] + _block_operand_scratch() + [
                        pltpu.VMEM((SSM_WIDTH // LANES, SUBLANES * PITCH, LANES), F32),
                        pltpu.VMEM((TM_B, SSM_WIDTH), F32),
                        pltpu.VMEM((TM_B, SSM_WIDTH), F32),
                        pltpu.VMEM((TM_B, STATE_LANES), F32),
                        pltpu.VMEM((TM_B, STATE_LANES), F32),
                        pltpu.VMEM((1, STATE_LANES), F32),
                        pltpu.VMEM((1, STATE_LANES), F32),
                        pltpu.VMEM((SUBLANES, STATE_LANES), F32),
                        pltpu.VMEM((SUBLANES, STATE_LANES), F32),
                        pltpu.VMEM((SUBLANES, STATE_LANES), F32),
                        pltpu.VMEM((SUBLANES, STATE_LANES), F32)],
        compiler_params=pltpu.CompilerParams(
            dimension_semantics=("arbitrary",), vmem_limit_bytes=VMEM_LIMIT),
        name="scan_mlp",
    )(rel_bias, sinks, flat(x), q, k, k, vt, vt, u, x_dec, attn_dec, ssm_dec, bucket_t, g_attn_col,
      lam[0], lam[1], lam_chunk[0], lam_chunk[1],
      bbar[0], bbar[1], c_gcp[0], c_gcp[1], d, w_glu, b_glu, g_ssm, w_out, g_mlp, w_up, w_down, g_fin)
    return y.reshape(bsz, seq, D_MODEL), y_dec, h_re, h_im


def _block_operand_scratch():
    return [pltpu.VMEM((STATE_LANES // SLAB, SLAB_CH, SLAB), BF16) for _ in range(4)]


def _group_blocks():
    return _const_spec((N_SSM_GROUPS, SSM_GROUP, SSM_STATE))


def _s_scan_kernel(u_ref, h0r_ref, h0i_ref, lr_ref, li_ref, bbr_ref, bbi_ref, cre_in, cim_in,
                   d_ref, wglu_ref, bglu_ref, gssm_ref,
                   out_ref, hre_out, him_out,
                   bre_ref, bim_ref, cre_ref, cim_ref,
                   pad_scr, up_scr, hre, him):
    nseq = h0r_ref.shape[0]
    steps = u_ref.shape[0] // nseq
    nslab = SSM_WIDTH // LANES
    _embed_blocks(bbr_ref, bbi_ref, cre_in, cim_in, bre_ref, bim_ref, cre_ref, cim_ref)

    for s in range(nslab):
        pad_scr[s] = u_ref[:, s * LANES:(s + 1) * LANES]
    for i in range(steps):
        for s in range(nslab):
            up_scr[i * nseq:(i + 1) * nseq, s * LANES:(s + 1) * LANES] = (
                pad_scr[s, pl.ds(i, nseq, stride=steps), :])
    up = up_scr[...]
    upb = up.astype(BF16)

    ys = []
    for sl in range(STATE_LANES // SLAB):
        cols = slice(sl * SLAB, (sl + 1) * SLAB)
        ch = slice(sl * SLAB_CH, (sl + 1) * SLAB_CH)
        hre[:, cols] = jnp.dot(upb[:, ch], bre_ref[sl], preferred_element_type=F32)
        him[:, cols] = jnp.dot(upb[:, ch], bim_ref[sl], preferred_element_type=F32)
        ar = lr_ref[:, cols]
        ai = li_ref[:, cols]
        hr = h0r_ref[:, cols]
        hi = h0i_ref[:, cols]
        for i in range(steps):
            rws = slice(i * nseq, (i + 1) * nseq)
            hr, hi = ar * hr - ai * hi + hre[rws, cols], ar * hi + ai * hr + him[rws, cols]
            hre[rws, cols] = hr
            him[rws, cols] = hi
        hre_out[:, cols] = hr
        him_out[:, cols] = hi
        ys.append(_output_map(hre[:, cols], him[:, cols], cre_ref[sl], cim_ref[sl]))

    y = jnp.concatenate(ys, axis=-1) + d_ref[...] * up
    o = _glu_norm(y, wglu_ref, bglu_ref, gssm_ref)

    for i in range(steps):
        for s in range(nslab):
            pad_scr[s, pl.ds(i, nseq, stride=steps), :] = o[i * nseq:(i + 1) * nseq, s * LANES:(s + 1) * LANES]
    for s in range(nslab):
        out_ref[:, s * LANES:(s + 1) * LANES] = pad_scr[s].astype(BF16)


def _s_scan(u, h0_re, h0_im, lam, bbar, c_gcp, d, w_glu, b_glu, g_ssm):
    rows = u.shape[0]
    nseq = h0_re.shape[0]
    return pl.pallas_call(
        _s_scan_kernel,
        out_shape=(jax.ShapeDtypeStruct((rows, SSM_WIDTH), BF16),
                   jax.ShapeDtypeStruct((nseq, STATE_LANES), F32),
                   jax.ShapeDtypeStruct((nseq, STATE_LANES), F32)),
        scratch_shapes=_block_operand_scratch() + [
                        pltpu.VMEM((SSM_WIDTH // LANES, rows, LANES), F32),
                        pltpu.VMEM((rows, SSM_WIDTH), F32),
                        pltpu.VMEM((rows, STATE_LANES), F32),
                        pltpu.VMEM((rows, STATE_LANES), F32)],
        compiler_params=pltpu.CompilerParams(vmem_limit_bytes=VMEM_LIMIT),
        name="s_scan",
    )(u, h0_re, h0_im, lam[0], lam[1], bbar[0], bbar[1], c_gcp[0], c_gcp[1], d, w_glu, b_glu, g_ssm)


def _pair_heads(w, axis):
    shape = w.shape
    split = shape[:axis] + (N_KV_HEADS, Q_PER_KV, HEAD_DIM) + shape[axis + 1:]
    return jnp.swapaxes(w.reshape(split), axis, axis + 1).reshape(shape)


def kernel(x_prompt, x_sample, cache_k, cache_v, state_ssm_re, state_ssm_im, rel_bias, norm_mix, w_in, attn_sinks, ssm_a_re, ssm_a_im, ssm_log_step, ssm_b_re, ssm_b_im, ssm_c_re, ssm_c_im, ssm_d, w_glu, b_glu, norm_attn_out, norm_ssm_out, w_out, norm_mlp, w_up, w_down, norm_final):
    depth = w_in.shape[0]
    bsz, seq, _ = x_prompt.shape
    nseq, steps, _ = x_sample.shape
    bucket = jnp.asarray(_rel_bucket_table())
    g_fin = norm_final.reshape(1, D_MODEL)
    assert depth == 1, "the final norm is fused into the (single) layer's MLP kernel"

    xp, xs = x_prompt, x_sample
    outs = [[] for _ in range(8)]
    for l in range(depth):
        lr, li, lcr, lci, bbr, bbi = _ssm_prep(ssm_a_re[l], ssm_a_im[l], ssm_log_step[l], ssm_b_re[l], ssm_b_im[l])
        lam = (lr, li)
        lam_chunk = (lcr, lci)
        bbar = (bbr, bbi)
        c_gcp = (ssm_c_re[l], ssm_c_im[l])
        g_mix = norm_mix[l].reshape(1, D_MODEL)
        g_attn = _pair_heads(norm_attn_out[l].reshape(1, ATTN_WIDTH), axis=1)
        g_ssm = norm_ssm_out[l].reshape(1, SSM_WIDTH)
        g_mlp = norm_mlp[l].reshape(1, D_MODEL)
        sinks = attn_sinks[l].reshape(1, N_Q_HEADS)
        d = ssm_d[l].reshape(1, SSM_WIDTH)
        bg = b_glu[l].reshape(1, SSM_WIDTH)
        w_in_b = jnp.concatenate([_pair_heads(w_in[l][:, :ATTN_WIDTH], axis=1), w_in[l][:, ATTN_WIDTH:]],
                                 axis=1).astype(BF16)

        q, k, vt, u, k1, v1, w_out_b, w_glu_b, w_up_b, w_down_b = _in_proj(
            xp, g_mix, w_in_b, w_out[l], w_glu[l], w_up[l], w_down[l])

        ck = cache_k[l].reshape(nseq, WINDOW, KV_WIDTH)
        cv = cache_v[l].reshape(nseq, WINDOW, KV_WIDTH)
        s_attn, s_u, k2, v2 = _s_in_attn(xs, rel_bias, sinks, g_mix, w_in_b, bucket, g_attn, ck, cv)
        s_ssm, r2, i2 = _s_scan(s_u, state_ssm_re[l].reshape(nseq, STATE_LANES),
                                state_ssm_im[l].reshape(nseq, STATE_LANES),
                                lam, bbar, c_gcp, d, w_glu_b, bg, g_ssm)

        xp, xs, r1, i1 = _scan_mlp(xp, q, k, vt, u, xs.reshape(nseq * steps, D_MODEL), s_attn, s_ssm,
                                   rel_bias, sinks, bucket.T, g_attn.reshape(ATTN_WIDTH, 1),
                                   lam, lam_chunk, bbar, c_gcp, d, w_glu_b, bg, g_ssm,
                                   w_out_b, g_mlp, w_up_b, w_down_b, g_fin)
        xs = xs.reshape(nseq, steps, D_MODEL)

        kv_shape = (WINDOW, N_KV_HEADS, HEAD_DIM)
        st_shape = (N_SSM_GROUPS, SSM_STATE)
        for lst, val in zip(outs, (k1.reshape(bsz, *kv_shape), v1.reshape(bsz, *kv_shape),
                                   r1.reshape(bsz, *st_shape), i1.reshape(bsz, *st_shape),
                                   k2.reshape(nseq, *kv_shape), v2.reshape(nseq, *kv_shape),
                                   r2.reshape(nseq, *st_shape), i2.reshape(nseq, *st_shape))):
            lst.append(val)

    return (xp, xs) + tuple(jnp.stack(o) for o in outs)
```

```python
import functools
import math

import numpy as np
import jax
import jax.numpy as jnp
from jax import lax
from jax.experimental import pallas as pl
from jax.experimental.pallas import tpu as pltpu

D_MODEL = 1024
HEAD_DIM = 64
ATTN_WIDTH = 512
N_Q_HEADS = 8
N_KV_HEADS = 2
Q_PER_KV = 4
KV_WIDTH = 128
WINDOW = 128
BLOCK = 128
N_BUCKETS = 32
MAX_DISTANCE = 128
SSM_WIDTH = 512
SSM_GROUP = 16
N_SSM_GROUPS = 32
SSM_STATE = 64
STATE_LANES = N_SSM_GROUPS * SSM_STATE
D_FF = 4096
D_IN_PROJ = 1280
EPS = 1e-6
NEG = -1e30
LOG2E = math.log2(math.e)

LANES = 128
SUBLANES = 8
MXU_DIM = 256
VMEM_LIMIT = 56 * 1024 * 1024

F32 = jnp.float32
BF16 = jnp.bfloat16

TM_A = 1024
TM_B = 512
CHUNK = TM_B // SUBLANES
PITCH = CHUNK + SUBLANES
S_SEQ = 16
NEW_ROWS = 16


def _rms(x, g):
    return x * lax.rsqrt(jnp.mean(x * x, axis=-1, keepdims=True) + EPS) * g


def _rel_bucket_table():
    i = np.arange(BLOCK)[:, None]
    j = np.arange(2 * BLOCK)[None, :]
    n = np.maximum(i + BLOCK - j, 0)
    max_exact = N_BUCKETS // 2
    nf = np.maximum(n, max_exact).astype(np.float32)
    large = max_exact + (np.log(nf / np.float32(max_exact)) / np.float32(math.log(MAX_DISTANCE / max_exact))
                         * np.float32(N_BUCKETS - max_exact)).astype(np.int32)
    large = np.minimum(large, N_BUCKETS - 1)
    return np.where(n < max_exact, n, large).astype(np.int32)


def _build_bias(relb_ref, bucket_ref, bias_scr):
    bucket = bucket_ref[...]
    for h in range(N_Q_HEADS):
        bias_scr[h] = jnp.zeros(bucket.shape, F32)
    for b in range(N_BUCKETS):
        hit = bucket == b
        for h in range(N_Q_HEADS):
            bias_scr[h] = jnp.where(hit, relb_ref[b, h], bias_scr[h])


def _build_bias_t(relb_ref, bucket_t_ref, bias_t_scr):
    bucket_t = bucket_t_ref[...]
    bias_t_scr[...] = jnp.zeros_like(bias_t_scr)
    for b in range(N_BUCKETS):
        hit = bucket_t == b
        for h in range(N_Q_HEADS):
            cols = slice(h * BLOCK, (h + 1) * BLOCK)
            bias_t_scr[:, cols] = jnp.where(hit, relb_ref[b, h] * LOG2E, bias_t_scr[:, cols])
    kj = lax.broadcasted_iota(jnp.int32, (2 * BLOCK, BLOCK), 0)
    qi = lax.broadcasted_iota(jnp.int32, (2 * BLOCK, BLOCK), 1)
    dist = qi + BLOCK - kj
    in_band = jnp.logical_and(dist >= 0, dist < WINDOW)
    for h in range(N_Q_HEADS):
        cols = slice(h * BLOCK, (h + 1) * BLOCK)
        bias_t_scr[:, cols] = jnp.where(in_band, bias_t_scr[:, cols], NEG)


def _ssm_prep_kernel(are_ref, aim_ref, ls_ref, btr_ref, bti_ref,
                     lr_ref, li_ref, lcr_ref, lci_ref, bbr_ref, bbi_ref):
    ar = are_ref[...]
    ai = aim_ref[...]
    step = jnp.exp(ls_ref[...])
    zr = ar * step
    zi = ai * step
    mag = jnp.exp(zr)
    lr = mag * jnp.cos(zi)
    li = mag * jnp.sin(zi)
    pr, pi = lr, li
    for _ in range(int(math.log2(CHUNK))):
        pr, pi = pr * pr - pi * pi, 2.0 * pr * pi
    for g in range(N_SSM_GROUPS):
        st = slice(g * SSM_STATE, (g + 1) * SSM_STATE)
        lr_ref[:, st] = lr[g]
        li_ref[:, st] = li[g]
        lcr_ref[:, st] = pr[g]
        lci_ref[:, st] = pi[g]
    nr = lr - 1.0
    ni = li
    den = ar * ar + ai * ai
    cr = (nr * ar + ni * ai) / den
    ci = (ni * ar - nr * ai) / den
    btr = btr_ref[...]
    bti = bti_ref[...]
    bbr_ref[...] = cr * btr - ci * bti
    bbi_ref[...] = cr * bti + ci * btr


def _ssm_prep(a_re, a_im, log_step, b_re, b_im):
    g, p, c = N_SSM_GROUPS, SSM_STATE, SSM_GROUP
    vec = jax.ShapeDtypeStruct((1, STATE_LANES), F32)
    mat = jax.ShapeDtypeStruct((g, c, p), F32)
    return pl.pallas_call(
        _ssm_prep_kernel,
        out_shape=(vec, vec, vec, vec, mat, mat),
        name="ssm_prep",
    )(a_re.reshape(g, 1, p), a_im.reshape(g, 1, p), log_step.reshape(g, 1, 1),
      jnp.transpose(b_re, (0, 2, 1)), jnp.transpose(b_im, (0, 2, 1)))


def _stack_heads(q_blk, lane_lo):
    zero = jnp.zeros_like(q_blk[:, 0:LANES])
    lo, hi = [], []
    for c in range(ATTN_WIDTH // LANES):
        blk = q_blk[:, c * LANES:(c + 1) * LANES]
        lo.append(jnp.where(lane_lo, blk, zero))
        hi.append(jnp.where(lane_lo, zero, blk))
    return jnp.concatenate(lo + hi, axis=0)


def _unstack_heads(o, rows, lane_lo):
    nblk = ATTN_WIDTH // LANES
    return jnp.concatenate(
        [jnp.where(lane_lo, o[c * rows:(c + 1) * rows], o[(c + nblk) * rows:(c + nblk + 1) * rows])
         for c in range(nblk)], axis=-1)


Q_COLS = MXU_DIM


def _proj_stages(x_ref, gmix_ref, win_ref, q_stage, k_stage, vt_stage, u_ref, newk_ref, newv_ref):
    xn = _rms(x_ref[...], gmix_ref[...]).astype(BF16)
    q_scale = HEAD_DIM ** -0.5 * LOG2E
    for c in range(ATTN_WIDTH // Q_COLS):
        cols = slice(c * Q_COLS, (c + 1) * Q_COLS)
        q = jnp.dot(xn, win_ref[:, cols], preferred_element_type=F32)
        q_stage[:, cols] = (q * q_scale).astype(BF16)
        yield
    kv = jnp.dot(xn, win_ref[:, ATTN_WIDTH:ATTN_WIDTH + 2 * KV_WIDTH], preferred_element_type=F32)
    k_stage[...] = kv[:, :KV_WIDTH].astype(BF16)
    vt_stage[...] = kv[:, KV_WIDTH:].T.astype(BF16)
    newk_ref[...] = kv[TM_A - WINDOW:, :KV_WIDTH]
    newv_ref[...] = kv[TM_A - WINDOW:, KV_WIDTH:]
    yield
    u0 = ATTN_WIDTH + 2 * KV_WIDTH
    for c in range(SSM_WIDTH // Q_COLS):
        cols = slice(u0 + c * Q_COLS, u0 + (c + 1) * Q_COLS)
        u_ref[:, c * Q_COLS:(c + 1) * Q_COLS] = jnp.dot(xn, win_ref[:, cols], preferred_element_type=F32)
        yield


def _attn_stages(sinks_ref, attn_ref, k_scr, vt_scr, q_scr, bias_t_scr, gt_scr, first_tile):
    nstk = N_Q_HEADS * BLOCK
    nblk = ATTN_WIDTH // LANES
    lane_lo = lax.broadcasted_iota(jnp.int32, (BLOCK, LANES), 1) < HEAD_DIM
    row_lo = lax.broadcasted_iota(jnp.int32, (KV_WIDTH, BLOCK), 0) < HEAD_DIM
    kj = lax.broadcasted_iota(jnp.int32, (2 * BLOCK, nstk), 0)
    has_prev = jnp.logical_or(kj >= BLOCK, jnp.logical_not(first_tile))
    head = lax.broadcasted_iota(jnp.int32, (1, nstk), 1) // BLOCK
    sink = jnp.zeros((1, nstk), F32)
    for h in range(N_Q_HEADS):
        sink = jnp.where(head == h, sinks_ref[0, h] * LOG2E, sink)

    scores = []
    for j in range(TM_A // BLOCK):
        r0 = j * BLOCK
        q2 = _stack_heads(q_scr[r0:r0 + BLOCK, :], lane_lo)
        keys = k_scr[r0:r0 + 2 * BLOCK, :]
        scores.append(lax.dot_general(keys, q2, (((1,), (1,)), ((), ())), preferred_element_type=F32))
        yield
    for j in range(TM_A // BLOCK):
        r0 = j * BLOCK
        s = scores[j] + bias_t_scr[...]
        if j == 0:
            s = jnp.where(has_prev, s, NEG)
        m = jnp.maximum(jnp.max(s, axis=0, keepdims=True), sink)
        p = jnp.exp2(s - m)
        denom = jnp.sum(p, axis=0, keepdims=True) + jnp.exp2(sink - m)
        o = jnp.dot(vt_scr[:, r0:r0 + 2 * BLOCK], p.astype(BF16), preferred_element_type=F32) / denom
        a = jnp.concatenate(
            [jnp.where(row_lo, o[:, c * BLOCK:(c + 1) * BLOCK], o[:, (c + nblk) * BLOCK:(c + nblk + 1) * BLOCK])
             for c in range(nblk)], axis=0)
        a = a * lax.rsqrt(jnp.mean(a * a, axis=0, keepdims=True) + EPS) * gt_scr[...]
        attn_ref[r0:r0 + BLOCK, :] = a.T.astype(BF16)
        yield


def _in_attn_kernel(relb_ref, sinks_ref, x_ref, gmix_ref, win_ref, bucket_t_ref, gattn_ref,
                    wout_f, wglu_f, wup_f, wdown_f,
                    attn_ref, u_ref, newk_ref, newv_ref, wout_b, wglu_b, wup_b, wdown_b,
                    k_scr, vt_scr, q_scr, k_stage, vt_stage, q_stage, bias_t_scr, gt_scr, *, nt):
    i = pl.program_id(0)

    @pl.when(i == 0)
    def _():
        _build_bias_t(relb_ref, bucket_t_ref, bias_t_scr)
        gt_scr[...] = jnp.broadcast_to(gattn_ref[...], (ATTN_WIDTH, BLOCK))
        k_scr[...] = jnp.zeros_like(k_scr)
        vt_scr[...] = jnp.zeros_like(vt_scr)
        k_stage[...] = jnp.zeros_like(k_stage)
        vt_stage[...] = jnp.zeros_like(vt_stage)
        q_stage[...] = jnp.zeros_like(q_stage)

    k_scr[0:BLOCK, :] = k_scr[TM_A:TM_A + BLOCK, :]
    vt_scr[:, 0:BLOCK] = vt_scr[:, TM_A:TM_A + BLOCK]
    k_scr[BLOCK:BLOCK + TM_A, :] = k_stage[...]
    vt_scr[:, BLOCK:BLOCK + TM_A] = vt_stage[...]
    q_scr[...] = q_stage[...]

    wout_b[...] = wout_f[...].astype(BF16)
    wglu_b[...] = wglu_f[...].astype(BF16)
    wup_b[...] = wup_f[...].astype(BF16)
    wdown_b[...] = wdown_f[...].astype(BF16)

    proj = _proj_stages(x_ref, gmix_ref, win_ref, q_stage, k_stage, vt_stage, u_ref, newk_ref, newv_ref)
    attn = _attn_stages(sinks_ref, attn_ref, k_scr, vt_scr, q_scr, bias_t_scr, gt_scr, i % nt == 1 % nt)
    nb = TM_A // BLOCK
    order = ("p" + "a" * (nb // 2)) * 2 + "p" + "a" * (nb // 4) + "p" + "a" * (nb // 2) + "p" + "a" * (nb - nb // 4 - nb // 2)
    for stage in order:
        next(proj if stage == "p" else attn)
    assert next(proj, None) is None and next(attn, None) is None


def _const_spec(shape):
    zeros = (0,) * len(shape)
    return pl.BlockSpec(shape, lambda *_: zeros, pipeline_mode=pl.Buffered(1))


def _smem_spec():
    return pl.BlockSpec(memory_space=pltpu.SMEM)


def _in_attn(x, rel_bias, sinks, g_mix, w_in, bucket_t, g_attn_col, w_out, w_glu, w_up, w_down):
    bsz, seq, _ = x.shape
    nt = seq // TM_A
    steps = bsz * nt

    def tile(i):
        return jnp.minimum(i, steps - 1)

    cur = lambda width: pl.BlockSpec((TM_A, width), lambda i: (tile(i), 0))
    prev = lambda width: pl.BlockSpec((TM_A, width), lambda i: (jnp.maximum(i - 1, 0), 0))
    win_spec = pl.BlockSpec((None, WINDOW, KV_WIDTH), lambda i: (tile(i) // nt, 0, 0))

    def rows_of(w):
        blk = w.shape[0] // steps
        return pl.BlockSpec((blk, w.shape[1]), lambda i: (tile(i), 0))

    nhead_blk = D_MODEL // HEAD_DIM

    def wout_src(i):
        return (jnp.minimum(tile(i), nhead_blk - 1), 0)

    def wout_dst(i):
        j = jnp.minimum(tile(i), nhead_blk - 1)
        return (jnp.where(j < N_Q_HEADS, (j % Q_PER_KV) * N_KV_HEADS + j // Q_PER_KV, j), 0)

    assert steps >= nhead_blk and w_up.shape[0] % steps == 0 and w_down.shape[0] % steps == 0
    wout_in = pl.BlockSpec((HEAD_DIM, D_MODEL), wout_src)
    wout_out = pl.BlockSpec((HEAD_DIM, D_MODEL), wout_dst)
    cast_shape = lambda w: jax.ShapeDtypeStruct(w.shape, BF16)
    return pl.pallas_call(
        functools.partial(_in_attn_kernel, nt=nt),
        out_shape=(jax.ShapeDtypeStruct((bsz * seq, ATTN_WIDTH), BF16),
                   jax.ShapeDtypeStruct((bsz * seq, SSM_WIDTH), F32),
                   jax.ShapeDtypeStruct((bsz, WINDOW, KV_WIDTH), F32),
                   jax.ShapeDtypeStruct((bsz, WINDOW, KV_WIDTH), F32),
                   cast_shape(w_out), cast_shape(w_glu), cast_shape(w_up), cast_shape(w_down)),
        grid=(steps + 1,),
        in_specs=[_smem_spec(), _smem_spec(), cur(D_MODEL), _const_spec((1, D_MODEL)),
                  _const_spec((D_MODEL, D_IN_PROJ)), _const_spec((2 * BLOCK, BLOCK)),
                  _const_spec((ATTN_WIDTH, 1)),
                  wout_in, rows_of(w_glu), rows_of(w_up), rows_of(w_down)],
        out_specs=(prev(ATTN_WIDTH), cur(SSM_WIDTH), win_spec, win_spec,
                   wout_out, rows_of(w_glu), rows_of(w_up), rows_of(w_down)),
        scratch_shapes=[pltpu.VMEM((TM_A + BLOCK, KV_WIDTH), BF16),
                        pltpu.VMEM((KV_WIDTH, TM_A + BLOCK), BF16),
                        pltpu.VMEM((TM_A, ATTN_WIDTH), BF16),
                        pltpu.VMEM((TM_A, KV_WIDTH), BF16),
                        pltpu.VMEM((KV_WIDTH, TM_A), BF16),
                        pltpu.VMEM((TM_A, ATTN_WIDTH), BF16),
                        pltpu.VMEM((2 * BLOCK, N_Q_HEADS * BLOCK), F32),
                        pltpu.VMEM((ATTN_WIDTH, BLOCK), F32)],
        compiler_params=pltpu.CompilerParams(
            dimension_semantics=("arbitrary",), vmem_limit_bytes=VMEM_LIMIT),
        name="in_attn",
    )(rel_bias, sinks, x.reshape(bsz * seq, D_MODEL), g_mix, w_in, bucket_t, g_attn_col, w_out, w_glu, w_up, w_down)


def _s_in_attn_kernel(relb_ref, sinks_ref, x_ref, gmix_ref, win_ref, bucket_ref, gattn_ref,
                      ck_ref, cv_ref,
                      attn_ref, u_ref, newk_ref, newv_ref,
                      bias_scr, biasd_scr, proj_scr):
    rows = attn_ref.shape[0]
    steps = rows // S_SEQ
    pair = SUBLANES // steps
    prow = N_Q_HEADS * SUBLANES

    @pl.when(pl.program_id(0) == 0)
    def _():
        _build_bias(relb_ref, bucket_ref, bias_scr)
        for h in range(N_Q_HEADS):
            for k in range(pair):
                biasd_scr[h * SUBLANES + k * steps:h * SUBLANES + (k + 1) * steps, :] = bias_scr[h, 0:steps, :]
        xn = _rms(x_ref[...], gmix_ref[...]).astype(BF16)
        proj_all = jnp.dot(xn, win_ref[...], preferred_element_type=F32)
        proj_scr[...] = proj_all
        u_ref[...] = proj_all[:, ATTN_WIDTH + 2 * KV_WIDTH:]

    r0 = pl.multiple_of(pl.program_id(0) * rows, rows)
    proj = proj_scr[pl.ds(r0, rows), :]
    qs = proj[:, :ATTN_WIDTH] * (HEAD_DIM ** -0.5)

    grp = NEW_ROWS // SUBLANES
    nsq = grp * pair
    row = lax.broadcasted_iota(jnp.int32, (grp * prow, 1), 0)
    qseq = row // prow * pair + (row % SUBLANES) // steps
    qrow = row % steps
    kcol = lax.broadcasted_iota(jnp.int32, (grp * prow, nsq * WINDOW), 1)
    valid_c = jnp.logical_and(kcol // WINDOW == qseq, kcol % WINDOW > qrow)
    kcol_n = lax.broadcasted_iota(jnp.int32, (grp * prow, NEW_ROWS), 1)
    valid_n = jnp.logical_and(kcol_n // steps == qseq, kcol_n % steps <= qrow)
    sink = jnp.zeros((grp * prow, 1), F32)
    for h in range(N_Q_HEADS):
        sink = jnp.where(row % prow // SUBLANES == h, sinks_ref[0, h], sink)
    lane_lo = lax.broadcasted_iota(jnp.int32, (SUBLANES, LANES), 1) < HEAD_DIM
    gattn = gattn_ref[...]
    bias_rows = jnp.concatenate([biasd_scr[...]] * grp, axis=0)
    bias_c = jnp.concatenate([bias_rows[:, 0:WINDOW]] * nsq, axis=1)
    bias_new = bias_rows[:, WINDOW:2 * WINDOW]
    bias_n = bias_new
    lane = lax.broadcasted_iota(jnp.int32, (grp * prow, WINDOW), 1)
    for k in range(1, nsq):
        bias_n = jnp.where(lane >= k * steps, pltpu.roll(bias_new, k * steps, axis=1), bias_n)
    bias_n = bias_n[:, 0:NEW_ROWS]
    nt_dims = (((1,), (1,)), ((), ()))

    tiles = []
    for gi in range(S_SEQ // nsq):
        rws = slice(gi * NEW_ROWS, (gi + 1) * NEW_ROWS)
        q2 = jnp.concatenate(
            [_stack_heads(qs[gi * NEW_ROWS + t * SUBLANES:gi * NEW_ROWS + (t + 1) * SUBLANES], lane_lo)
             for t in range(grp)], axis=0).astype(BF16)
        kn = proj[rws, ATTN_WIDTH:ATTN_WIDTH + KV_WIDTH]
        vn = proj[rws, ATTN_WIDTH + KV_WIDTH:ATTN_WIDTH + 2 * KV_WIDTH]
        cks, cvs = [], []
        for k in range(nsq):
            sq = gi * nsq + k
            ck = ck_ref[sq]
            cv = cv_ref[sq]
            newk_ref[sq, 0:WINDOW - steps, :] = ck[steps:, :]
            newk_ref[sq, WINDOW - steps:, :] = kn[k * steps:(k + 1) * steps]
            newv_ref[sq, 0:WINDOW - steps, :] = cv[steps:, :]
            newv_ref[sq, WINDOW - steps:, :] = vn[k * steps:(k + 1) * steps]
            cks.append(ck.astype(BF16))
            cvs.append(cv.astype(BF16))
        s_c = lax.dot_general(q2, jnp.concatenate(cks, axis=0), nt_dims, preferred_element_type=F32)
        s_n = lax.dot_general(q2, kn.astype(BF16), nt_dims, preferred_element_type=F32)
        s_c = jnp.where(valid_c, s_c + bias_c, NEG)
        s_n = jnp.where(valid_n, s_n + bias_n, NEG)
        m = jnp.maximum(jnp.maximum(jnp.max(s_c, axis=-1, keepdims=True),
                                    jnp.max(s_n, axis=-1, keepdims=True)), sink)
        p_c = jnp.exp(s_c - m)
        p_n = jnp.exp(s_n - m)
        denom = (jnp.sum(p_c, axis=-1, keepdims=True) + jnp.sum(p_n, axis=-1, keepdims=True)
                 + jnp.exp(sink - m))
        o = (jnp.dot(p_c.astype(BF16), jnp.concatenate(cvs, axis=0), preferred_element_type=F32)
             + jnp.dot(p_n.astype(BF16), vn.astype(BF16), preferred_element_type=F32)) / denom
        for t in range(grp):
            tiles.append(_unstack_heads(o[t * prow:(t + 1) * prow], SUBLANES, lane_lo))
    attn_ref[...] = _rms(jnp.concatenate(tiles, axis=0), gattn).astype(BF16)


def _s_in_attn(x, rel_bias, sinks, g_mix, w_in, bucket, g_attn, cache_k, cache_v):
    nseq, steps, _ = x.shape
    rows = S_SEQ * steps
    xf = x.reshape(nseq * steps, D_MODEL)
    row = lambda width: pl.BlockSpec((rows, width), lambda i: (i, 0))
    cache = pl.BlockSpec((S_SEQ, WINDOW, KV_WIDTH), lambda i: (i, 0, 0))
    return pl.pallas_call(
        _s_in_attn_kernel,
        out_shape=(jax.ShapeDtypeStruct((nseq * steps, ATTN_WIDTH), BF16),
                   jax.ShapeDtypeStruct((nseq * steps, SSM_WIDTH), F32),
                   jax.ShapeDtypeStruct((nseq, WINDOW, KV_WIDTH), F32),
                   jax.ShapeDtypeStruct((nseq, WINDOW, KV_WIDTH), F32)),
        grid=(nseq // S_SEQ,),
        in_specs=[_smem_spec(), _smem_spec(), _const_spec((nseq * steps, D_MODEL)), _const_spec((1, D_MODEL)),
                  _const_spec((D_MODEL, D_IN_PROJ)), _const_spec((SUBLANES, 2 * BLOCK)),
                  _const_spec((1, ATTN_WIDTH)), cache, cache],
        out_specs=(row(ATTN_WIDTH), pl.BlockSpec((nseq * steps, SSM_WIDTH), lambda i: (0, 0)), cache, cache),
        scratch_shapes=[pltpu.VMEM((N_Q_HEADS, SUBLANES, 2 * BLOCK), F32),
                        pltpu.VMEM((N_Q_HEADS * SUBLANES, 2 * WINDOW), F32),
                        pltpu.VMEM((nseq * steps, D_IN_PROJ), F32)],
        compiler_params=pltpu.CompilerParams(
            dimension_semantics=("arbitrary",), vmem_limit_bytes=VMEM_LIMIT),
        name="s_in_attn",
    )(rel_bias, sinks, xf, g_mix, w_in, bucket[0:SUBLANES], g_attn, cache_k, cache_v)


def _embed_blocks(bbr_ref, bbi_ref, cre_in, cim_in, bre, bim, cre, cim):
    for ref in (bre, bim, cre, cim):
        ref[...] = jnp.zeros_like(ref)
    for g in range(N_SSM_GROUPS):
        sl, j = divmod(g, SLAB // SSM_STATE)
        ch = slice(j * SSM_GROUP, (j + 1) * SSM_GROUP)
        st = slice(j * SSM_STATE, (j + 1) * SSM_STATE)
        bre[sl, ch, st] = bbr_ref[g].astype(BF16)
        bim[sl, ch, st] = bbi_ref[g].astype(BF16)
        cre[sl, ch, st] = cre_in[g].astype(BF16)
        cim[sl, ch, st] = cim_in[g].astype(BF16)


def _output_map(h_slab_re, h_slab_im, c_slab_re, c_slab_im):
    nt_dims = (((1,), (1,)), ((), ()))
    yr = lax.dot_general(h_slab_re.astype(BF16), c_slab_re, nt_dims, preferred_element_type=F32)
    yi = lax.dot_general(h_slab_im.astype(BF16), c_slab_im, nt_dims, preferred_element_type=F32)
    return yr - yi


def _glu_norm(y, wglu_ref, bglu_ref, gssm_ref):
    g = jax.nn.gelu(y)
    gate = jax.nn.sigmoid(jnp.dot(g.astype(BF16), wglu_ref[...], preferred_element_type=F32) + bglu_ref[...])
    return _rms(g * gate, gssm_ref[...])


SLAB = 4 * LANES
SLAB_CH = SLAB // SSM_STATE * SSM_GROUP


REC_PIECE = 16
EPI_ROWS = 128
FF_BLK = 1024


def _scan_stages(u_ref, lr_ref, li_ref, lcr_ref, lci_ref, d_ref, wglu_ref, bglu_ref, gssm_ref, out_ref,
                 bre_ref, bim_ref, cre_ref, cim_ref,
                 pad_scr, up_scr, y_scr, hre, him, carry_re, carry_im, e_re, e_im, hc_re, hc_im):
    nslab = SSM_WIDTH // LANES
    nslabs = STATE_LANES // SLAB

    for c in range(SUBLANES):
        for s in range(nslab):
            pad_scr[s, c * PITCH:c * PITCH + CHUNK, :] = u_ref[c * CHUNK:(c + 1) * CHUNK, s * LANES:(s + 1) * LANES]
    yield
    for tau in range(CHUNK):
        for s in range(nslab):
            up_scr[tau * SUBLANES:(tau + 1) * SUBLANES, s * LANES:(s + 1) * LANES] = (
                pad_scr[s, pl.ds(tau, SUBLANES, stride=PITCH), :])
        if tau % REC_PIECE == REC_PIECE - 1:
            yield

    for sl in range(nslabs):
        cols = slice(sl * SLAB, (sl + 1) * SLAB)
        ch = slice(sl * SLAB_CH, (sl + 1) * SLAB_CH)
        upb = up_scr[:, ch].astype(BF16)
        hre[:, cols] = jnp.dot(upb, bre_ref[sl], preferred_element_type=F32)
        yield
        him[:, cols] = jnp.dot(upb, bim_ref[sl], preferred_element_type=F32)
        yield

    for sl in range(nslabs):
        cols = slice(sl * SLAB, (sl + 1) * SLAB)
        ch = slice(sl * SLAB_CH, (sl + 1) * SLAB_CH)
        ar = jnp.broadcast_to(lr_ref[:, cols], (SUBLANES, SLAB))
        ai = jnp.broadcast_to(li_ref[:, cols], (SUBLANES, SLAB))

        hr = jnp.zeros((SUBLANES, SLAB), F32)
        hi = jnp.zeros((SUBLANES, SLAB), F32)
        for tau in range(CHUNK):
            rws = slice(tau * SUBLANES, (tau + 1) * SUBLANES)
            hr, hi = ar * hr - ai * hi + hre[rws, cols], ar * hi + ai * hr + him[rws, cols]
            if tau % REC_PIECE == REC_PIECE - 1:
                yield
        e_re[:, cols] = hr
        e_im[:, cols] = hi

        cr = carry_re[:, cols]
        ci = carry_im[:, cols]
        lcr = lcr_ref[:, cols]
        lci = lci_ref[:, cols]
        for c in range(SUBLANES):
            hc_re[c:c + 1, cols] = cr
            hc_im[c:c + 1, cols] = ci
            er = e_re[c:c + 1, cols]
            ei = e_im[c:c + 1, cols]
            cr, ci = lcr * cr - lci * ci + er, lcr * ci + lci * cr + ei
        carry_re[:, cols] = cr
        carry_im[:, cols] = ci
        yield

        hr = hc_re[:, cols]
        hi = hc_im[:, cols]
        for tau in range(CHUNK):
            rws = slice(tau * SUBLANES, (tau + 1) * SUBLANES)
            hr, hi = ar * hr - ai * hi + hre[rws, cols], ar * hi + ai * hr + him[rws, cols]
            hre[rws, cols] = hr
            him[rws, cols] = hi
            if tau % REC_PIECE == REC_PIECE - 1:
                yield

        y_scr[:, ch] = _output_map(hre[:, cols], him[:, cols], cre_ref[sl], cim_ref[sl])
        yield

    taus = EPI_ROWS // SUBLANES
    for r in range(TM_B // EPI_ROWS):
        rows = slice(r * EPI_ROWS, (r + 1) * EPI_ROWS)
        o = _glu_norm(y_scr[rows, :] + d_ref[...] * up_scr[rows, :], wglu_ref, bglu_ref, gssm_ref)
        for i in range(taus):
            for s in range(nslab):
                pad_scr[s, pl.ds(r * taus + i, SUBLANES, stride=PITCH), :] = (
                    o[i * SUBLANES:(i + 1) * SUBLANES, s * LANES:(s + 1) * LANES])
        yield
    for c in range(SUBLANES):
        for s in range(nslab):
            out_ref[c * CHUNK:(c + 1) * CHUNK, s * LANES:(s + 1) * LANES] = (
                pad_scr[s, c * PITCH:c * PITCH + CHUNK, :].astype(BF16))
    yield


N_SCAN_STAGES = (1 + CHUNK // REC_PIECE + 2 * (STATE_LANES // SLAB)
                 + (STATE_LANES // SLAB) * (2 * (CHUNK // REC_PIECE) + 2) + TM_B // EPI_ROWS + 1)


def _mlp_stages(x_cols, attn, s_ref, wout_ref, gmlp_ref, wup_ref, wdown_ref, gfin_ref, y_ref, hm_scr, act_scr):
    ncol = D_MODEL // Q_COLS
    for n in range(ncol):
        cols = slice(n * Q_COLS, (n + 1) * Q_COLS)
        y_ref[:, cols] = (x_cols(cols)
                          + jnp.dot(attn(), wout_ref[0:ATTN_WIDTH, cols], preferred_element_type=F32)
                          + jnp.dot(s_ref[...], wout_ref[ATTN_WIDTH:, cols], preferred_element_type=F32))
        yield
    hm_scr[...] = _rms(y_ref[...], gmlp_ref[...]).astype(BF16)
    for c in range(D_FF // FF_BLK):
        act = act_scr.at[c % 2]
        for n in range(FF_BLK // Q_COLS):
            cols = slice(c * FF_BLK + n * Q_COLS, c * FF_BLK + (n + 1) * Q_COLS)
            up = jnp.dot(hm_scr[...], wup_ref[:, cols], preferred_element_type=F32)
            act[:, n * Q_COLS:(n + 1) * Q_COLS] = jnp.square(jnp.maximum(up, 0.0)).astype(BF16)
            yield
        for n in range(ncol):
            cols = slice(n * Q_COLS, (n + 1) * Q_COLS)
            y_ref[:, cols] += jnp.dot(act[...], wdown_ref[c * FF_BLK:(c + 1) * FF_BLK, cols],
                                      preferred_element_type=F32)
            yield
    y_ref[...] = _rms(y_ref[...], gfin_ref[...])
    yield


N_MLP_STAGES = D_MODEL // Q_COLS + (D_FF // FF_BLK) * (FF_BLK // Q_COLS + D_MODEL // Q_COLS) + 1


def _even_merge(n_a, n_b):
    keys = [((k + 0.5) / n_a, "a") for k in range(n_a)] + [((k + 0.5) / n_b, "b") for k in range(n_b)]
    return [name for _, name in sorted(keys)]


def _scan_mlp_kernel(x_ref, a_ref, u_ref, xdec_ref, adec_ref, sdec_ref,
                     lr_ref, li_ref, lcr_ref, lci_ref, bbr_ref, bbi_ref, cre_in, cim_in,
                     d_ref, wglu_ref, bglu_ref, gssm_ref, wout_ref, gmlp_ref, wup_ref, wdown_ref, gfin_ref,
                     y_ref, ydec_ref, hre_out, him_out,
                     ssm_scr, hm_scr, act_scr, bre_ref, bim_ref, cre_ref, cim_ref,
                     pad_scr, up_scr, y_scr, hre, him, carry_re, carry_im, e_re, e_im, hc_re, hc_im, *, nt):
    i = pl.program_id(0)
    dec = i == 0

    @pl.when(dec)
    def _():
        _embed_blocks(bbr_ref, bbi_ref, cre_in, cim_in, bre_ref, bim_ref, cre_ref, cim_ref)
        ssm_scr[...] = sdec_ref[...]

    @pl.when(i % nt == 0)
    def _():
        carry_re[...] = jnp.zeros_like(carry_re)
        carry_im[...] = jnp.zeros_like(carry_im)

    mlp = _mlp_stages(lambda cols: jnp.where(dec, xdec_ref[:, cols], x_ref[:, cols]),
                      lambda: jnp.where(dec, adec_ref[...], a_ref[...]),
                      ssm_scr, wout_ref, gmlp_ref, wup_ref, wdown_ref, gfin_ref, y_ref, hm_scr, act_scr)
    scan = _scan_stages(u_ref, lr_ref, li_ref, lcr_ref, lci_ref, d_ref, wglu_ref, bglu_ref, gssm_ref, ssm_scr,
                        bre_ref, bim_ref, cre_ref, cim_ref,
                        pad_scr, up_scr, y_scr, hre, him, carry_re, carry_im, e_re, e_im, hc_re, hc_im)
    for stage in _even_merge(N_MLP_STAGES, N_SCAN_STAGES):
        next(mlp if stage == "a" else scan)
    assert next(scan, None) is None and next(mlp, None) is None

    @pl.when(dec)
    def _():
        ydec_ref[...] = y_ref[...]

    @pl.when(i % nt == nt - 1)
    def _():
        hre_out[...] = carry_re[...]
        him_out[...] = carry_im[...]


def _scan_mlp(x, attn_n, u, x_dec, attn_dec, ssm_dec, lam, lam_chunk, bbar, c_gcp, d, w_glu, b_glu, g_ssm,
              w_out, g_mlp, w_up, w_down, g_fin):
    bsz, seq, _ = x.shape
    nt = seq // TM_B
    n = bsz * nt
    assert x_dec.shape[0] == TM_B, "the decode rows ride through the MLP as one tile"
    flat = lambda v: v.reshape(n * TM_B, v.shape[-1])
    cur = lambda width: pl.BlockSpec((TM_B, width), lambda i: (jnp.minimum(i, n - 1), 0))
    prev = lambda width: pl.BlockSpec((TM_B, width), lambda i: (jnp.maximum(i - 1, 0), 0))
    state = pl.BlockSpec((None, 1, STATE_LANES), lambda i: (jnp.minimum(i, n - 1) // nt, 0, 0))
    vec = _const_spec((1, STATE_LANES))
    chan = _const_spec((1, SSM_WIDTH))
    model = _const_spec((1, D_MODEL))
    dec = lambda width: _const_spec((TM_B, width))
    y, y_dec, h_re, h_im = pl.pallas_call(
        functools.partial(_scan_mlp_kernel, nt=nt),
        out_shape=(jax.ShapeDtypeStruct((n * TM_B, D_MODEL), F32),
                   jax.ShapeDtypeStruct((TM_B, D_MODEL), F32),
                   jax.ShapeDtypeStruct((bsz, 1, STATE_LANES), F32),
                   jax.ShapeDtypeStruct((bsz, 1, STATE_LANES), F32)),
        grid=(n + 1,),
        in_specs=[prev(D_MODEL), prev(ATTN_WIDTH), cur(SSM_WIDTH), dec(D_MODEL), dec(ATTN_WIDTH), dec(SSM_WIDTH),
                  vec, vec, vec, vec,
                  _group_blocks(), _group_blocks(), _group_blocks(), _group_blocks(),
                  chan, _const_spec((SSM_WIDTH, SSM_WIDTH)), chan, chan,
                  _const_spec((D_MODEL, D_MODEL)), model, _const_spec((D_MODEL, D_FF)),
                  _const_spec((D_FF, D_MODEL)), model],
        out_specs=(prev(D_MODEL), pl.BlockSpec((TM_B, D_MODEL), lambda i: (0, 0)), state, state),
        scratch_shapes=[pltpu.VMEM((TM_B, SSM_WIDTH), BF16),
                        pltpu.VMEM((TM_B, D_MODEL), BF16),
                        pltpu.VMEM((2, TM_B, FF_BLK), BF16)] + _block_operand_scratch() + [
                        pltpu.VMEM((SSM_WIDTH // LANES, SUBLANES * PITCH, LANES), F32),
                        pltpu.VMEM((TM_B, SSM_WIDTH), F32),
                        pltpu.VMEM((TM_B, SSM_WIDTH), F32),
                        pltpu.VMEM((TM_B, STATE_LANES), F32),
                        pltpu.VMEM((TM_B, STATE_LANES), F32),
                        pltpu.VMEM((1, STATE_LANES), F32),
                        pltpu.VMEM((1, STATE_LANES), F32),
                        pltpu.VMEM((SUBLANES, STATE_LANES), F32),
                        pltpu.VMEM((SUBLANES, STATE_LANES), F32),
                        pltpu.VMEM((SUBLANES, STATE_LANES), F32),
                        pltpu.VMEM((SUBLANES, STATE_LANES), F32)],
        compiler_params=pltpu.CompilerParams(
            dimension_semantics=("arbitrary",), vmem_limit_bytes=VMEM_LIMIT),
        name="scan_mlp",
    )(flat(x), flat(attn_n), flat(u), x_dec, attn_dec, ssm_dec, lam[0], lam[1], lam_chunk[0], lam_chunk[1],
      bbar[0], bbar[1], c_gcp[0], c_gcp[1], d, w_glu, b_glu, g_ssm, w_out, g_mlp, w_up, w_down, g_fin)
    return y.reshape(bsz, seq, D_MODEL), y_dec, h_re, h_im


def _block_operand_scratch():
    return [pltpu.VMEM((STATE_LANES // SLAB, SLAB_CH, SLAB), BF16) for _ in range(4)]


def _group_blocks():
    return _const_spec((N_SSM_GROUPS, SSM_GROUP, SSM_STATE))


def _s_scan_kernel(u_ref, h0r_ref, h0i_ref, lr_ref, li_ref, bbr_ref, bbi_ref, cre_in, cim_in,
                   d_ref, wglu_ref, bglu_ref, gssm_ref,
                   out_ref, hre_out, him_out,
                   bre_ref, bim_ref, cre_ref, cim_ref,
                   pad_scr, up_scr, hre, him):
    nseq = h0r_ref.shape[0]
    steps = u_ref.shape[0] // nseq
    nslab = SSM_WIDTH // LANES
    _embed_blocks(bbr_ref, bbi_ref, cre_in, cim_in, bre_ref, bim_ref, cre_ref, cim_ref)

    for s in range(nslab):
        pad_scr[s] = u_ref[:, s * LANES:(s + 1) * LANES]
    for i in range(steps):
        for s in range(nslab):
            up_scr[i * nseq:(i + 1) * nseq, s * LANES:(s + 1) * LANES] = (
                pad_scr[s, pl.ds(i, nseq, stride=steps), :])
    up = up_scr[...]
    upb = up.astype(BF16)

    ys = []
    for sl in range(STATE_LANES // SLAB):
        cols = slice(sl * SLAB, (sl + 1) * SLAB)
        ch = slice(sl * SLAB_CH, (sl + 1) * SLAB_CH)
        hre[:, cols] = jnp.dot(upb[:, ch], bre_ref[sl], preferred_element_type=F32)
        him[:, cols] = jnp.dot(upb[:, ch], bim_ref[sl], preferred_element_type=F32)
        ar = lr_ref[:, cols]
        ai = li_ref[:, cols]
        hr = h0r_ref[:, cols]
        hi = h0i_ref[:, cols]
        for i in range(steps):
            rws = slice(i * nseq, (i + 1) * nseq)
            hr, hi = ar * hr - ai * hi + hre[rws, cols], ar * hi + ai * hr + him[rws, cols]
            hre[rws, cols] = hr
            him[rws, cols] = hi
        hre_out[:, cols] = hr
        him_out[:, cols] = hi
        ys.append(_output_map(hre[:, cols], him[:, cols], cre_ref[sl], cim_ref[sl]))

    y = jnp.concatenate(ys, axis=-1) + d_ref[...] * up
    o = _glu_norm(y, wglu_ref, bglu_ref, gssm_ref)

    for i in range(steps):
        for s in range(nslab):
            pad_scr[s, pl.ds(i, nseq, stride=steps), :] = o[i * nseq:(i + 1) * nseq, s * LANES:(s + 1) * LANES]
    for s in range(nslab):
        out_ref[:, s * LANES:(s + 1) * LANES] = pad_scr[s].astype(BF16)


def _s_scan(u, h0_re, h0_im, lam, bbar, c_gcp, d, w_glu, b_glu, g_ssm):
    rows = u.shape[0]
    nseq = h0_re.shape[0]
    return pl.pallas_call(
        _s_scan_kernel,
        out_shape=(jax.ShapeDtypeStruct((rows, SSM_WIDTH), BF16),
                   jax.ShapeDtypeStruct((nseq, STATE_LANES), F32),
                   jax.ShapeDtypeStruct((nseq, STATE_LANES), F32)),
        scratch_shapes=_block_operand_scratch() + [
                        pltpu.VMEM((SSM_WIDTH // LANES, rows, LANES), F32),
                        pltpu.VMEM((rows, SSM_WIDTH), F32),
                        pltpu.VMEM((rows, STATE_LANES), F32),
                        pltpu.VMEM((rows, STATE_LANES), F32)],
        compiler_params=pltpu.CompilerParams(vmem_limit_bytes=VMEM_LIMIT),
        name="s_scan",
    )(u, h0_re, h0_im, lam[0], lam[1], bbar[0], bbar[1], c_gcp[0], c_gcp[1], d, w_glu, b_glu, g_ssm)


def _pair_heads(w, axis):
    shape = w.shape
    split = shape[:axis] + (N_KV_HEADS, Q_PER_KV, HEAD_DIM) + shape[axis + 1:]
    return jnp.swapaxes(w.reshape(split), axis, axis + 1).reshape(shape)


def kernel(x_prompt, x_sample, cache_k, cache_v, state_ssm_re, state_ssm_im, rel_bias, norm_mix, w_in, attn_sinks, ssm_a_re, ssm_a_im, ssm_log_step, ssm_b_re, ssm_b_im, ssm_c_re, ssm_c_im, ssm_d, w_glu, b_glu, norm_attn_out, norm_ssm_out, w_out, norm_mlp, w_up, w_down, norm_final):
    depth = w_in.shape[0]
    bsz, seq, _ = x_prompt.shape
    nseq, steps, _ = x_sample.shape
    bucket = jnp.asarray(_rel_bucket_table())
    g_fin = norm_final.reshape(1, D_MODEL)
    assert depth == 1, "the final norm is fused into the (single) layer's MLP kernel"

    xp, xs = x_prompt, x_sample
    outs = [[] for _ in range(8)]
    for l in range(depth):
        lr, li, lcr, lci, bbr, bbi = _ssm_prep(ssm_a_re[l], ssm_a_im[l], ssm_log_step[l], ssm_b_re[l], ssm_b_im[l])
        lam = (lr, li)
        lam_chunk = (lcr, lci)
        bbar = (bbr, bbi)
        c_gcp = (ssm_c_re[l], ssm_c_im[l])
        g_mix = norm_mix[l].reshape(1, D_MODEL)
        g_attn = _pair_heads(norm_attn_out[l].reshape(1, ATTN_WIDTH), axis=1)
        g_ssm = norm_ssm_out[l].reshape(1, SSM_WIDTH)
        g_mlp = norm_mlp[l].reshape(1, D_MODEL)
        sinks = attn_sinks[l].reshape(1, N_Q_HEADS)
        d = ssm_d[l].reshape(1, SSM_WIDTH)
        bg = b_glu[l].reshape(1, SSM_WIDTH)
        w_in_b = jnp.concatenate([_pair_heads(w_in[l][:, :ATTN_WIDTH], axis=1), w_in[l][:, ATTN_WIDTH:]],
                                 axis=1).astype(BF16)

        attn_n, u, k1, v1, w_out_b, w_glu_b, w_up_b, w_down_b = _in_attn(
            xp, rel_bias, sinks, g_mix, w_in_b, bucket.T, g_attn.reshape(ATTN_WIDTH, 1),
            w_out[l], w_glu[l], w_up[l], w_down[l])

        ck = cache_k[l].reshape(nseq, WINDOW, KV_WIDTH)
        cv = cache_v[l].reshape(nseq, WINDOW, KV_WIDTH)
        s_attn, s_u, k2, v2 = _s_in_attn(xs, rel_bias, sinks, g_mix, w_in_b, bucket, g_attn, ck, cv)
        s_ssm, r2, i2 = _s_scan(s_u, state_ssm_re[l].reshape(nseq, STATE_LANES),
                                state_ssm_im[l].reshape(nseq, STATE_LANES),
                                lam, bbar, c_gcp, d, w_glu_b, bg, g_ssm)

        xp, xs, r1, i1 = _scan_mlp(xp, attn_n, u, xs.reshape(nseq * steps, D_MODEL), s_attn, s_ssm,
                                   lam, lam_chunk, bbar, c_gcp, d, w_glu_b, bg, g_ssm,
                                   w_out_b, g_mlp, w_up_b, w_down_b, g_fin)
        xs = xs.reshape(nseq, steps, D_MODEL)

        kv_shape = (WINDOW, N_KV_HEADS, HEAD_DIM)
        st_shape = (N_SSM_GROUPS, SSM_STATE)
        for lst, val in zip(outs, (k1.reshape(bsz, *kv_shape), v1.reshape(bsz, *kv_shape),
                                   r1.reshape(bsz, *st_shape), i1.reshape(bsz, *st_shape),
                                   k2.reshape(nseq, *kv_shape), v2.reshape(nseq, *kv_shape),
                                   r2.reshape(nseq, *st_shape), i2.reshape(nseq, *st_shape))):
            lst.append(val)

    return (xp, xs) + tuple(jnp.stack(o) for o in outs)
```

```python
import functools
import math

import numpy as np
import jax
import jax.numpy as jnp
from jax import lax
from jax.experimental import pallas as pl
from jax.experimental.pallas import tpu as pltpu

D_MODEL = 1024
HEAD_DIM = 64
ATTN_WIDTH = 512
N_Q_HEADS = 8
N_KV_HEADS = 2
Q_PER_KV = 4
KV_WIDTH = 128
WINDOW = 128
BLOCK = 128
N_BUCKETS = 32
MAX_DISTANCE = 128
SSM_WIDTH = 512
SSM_GROUP = 16
N_SSM_GROUPS = 32
SSM_STATE = 64
STATE_LANES = N_SSM_GROUPS * SSM_STATE
D_FF = 4096
D_IN_PROJ = 1280
EPS = 1e-6
NEG = -1e30
LOG2E = math.log2(math.e)

LANES = 128
SUBLANES = 8
MXU_DIM = 256
VMEM_LIMIT = 56 * 1024 * 1024

F32 = jnp.float32
BF16 = jnp.bfloat16

TM_A = 1024
TM_B = 512
CHUNK = TM_B // SUBLANES
PITCH = CHUNK + SUBLANES
S_SEQ = 32
NEW_ROWS = 16


def _rms(x, g):
    return x * lax.rsqrt(jnp.mean(x * x, axis=-1, keepdims=True) + EPS) * g


def _rel_bucket_table():
    i = np.arange(BLOCK)[:, None]
    j = np.arange(2 * BLOCK)[None, :]
    n = np.maximum(i + BLOCK - j, 0)
    max_exact = N_BUCKETS // 2
    nf = np.maximum(n, max_exact).astype(np.float32)
    large = max_exact + (np.log(nf / np.float32(max_exact)) / np.float32(math.log(MAX_DISTANCE / max_exact))
                         * np.float32(N_BUCKETS - max_exact)).astype(np.int32)
    large = np.minimum(large, N_BUCKETS - 1)
    return np.where(n < max_exact, n, large).astype(np.int32)


def _build_bias(relb_ref, bucket_ref, bias_scr):
    bucket = bucket_ref[...]
    for h in range(N_Q_HEADS):
        bias_scr[h] = jnp.zeros(bucket.shape, F32)
    for b in range(N_BUCKETS):
        hit = bucket == b
        for h in range(N_Q_HEADS):
            bias_scr[h] = jnp.where(hit, relb_ref[b, h], bias_scr[h])


def _build_bias_t(relb_ref, bucket_t_ref, bias_t_scr):
    bucket_t = bucket_t_ref[...]
    bias_t_scr[...] = jnp.zeros_like(bias_t_scr)
    for b in range(N_BUCKETS):
        hit = bucket_t == b
        for h in range(N_Q_HEADS):
            cols = slice(h * BLOCK, (h + 1) * BLOCK)
            bias_t_scr[:, cols] = jnp.where(hit, relb_ref[b, h] * LOG2E, bias_t_scr[:, cols])
    kj = lax.broadcasted_iota(jnp.int32, (2 * BLOCK, BLOCK), 0)
    qi = lax.broadcasted_iota(jnp.int32, (2 * BLOCK, BLOCK), 1)
    dist = qi + BLOCK - kj
    in_band = jnp.logical_and(dist >= 0, dist < WINDOW)
    for h in range(N_Q_HEADS):
        cols = slice(h * BLOCK, (h + 1) * BLOCK)
        bias_t_scr[:, cols] = jnp.where(in_band, bias_t_scr[:, cols], NEG)


def _ssm_prep_kernel(are_ref, aim_ref, ls_ref, btr_ref, bti_ref,
                     lr_ref, li_ref, lcr_ref, lci_ref, bbr_ref, bbi_ref):
    ar = are_ref[...]
    ai = aim_ref[...]
    step = jnp.exp(ls_ref[...])
    zr = ar * step
    zi = ai * step
    mag = jnp.exp(zr)
    lr = mag * jnp.cos(zi)
    li = mag * jnp.sin(zi)
    pr, pi = lr, li
    for _ in range(int(math.log2(CHUNK))):
        pr, pi = pr * pr - pi * pi, 2.0 * pr * pi
    for g in range(N_SSM_GROUPS):
        st = slice(g * SSM_STATE, (g + 1) * SSM_STATE)
        lr_ref[:, st] = lr[g]
        li_ref[:, st] = li[g]
        lcr_ref[:, st] = pr[g]
        lci_ref[:, st] = pi[g]
    nr = lr - 1.0
    ni = li
    den = ar * ar + ai * ai
    cr = (nr * ar + ni * ai) / den
    ci = (ni * ar - nr * ai) / den
    btr = btr_ref[...]
    bti = bti_ref[...]
    bbr_ref[...] = cr * btr - ci * bti
    bbi_ref[...] = cr * bti + ci * btr


def _ssm_prep(a_re, a_im, log_step, b_re, b_im):
    g, p, c = N_SSM_GROUPS, SSM_STATE, SSM_GROUP
    vec = jax.ShapeDtypeStruct((1, STATE_LANES), F32)
    mat = jax.ShapeDtypeStruct((g, c, p), F32)
    return pl.pallas_call(
        _ssm_prep_kernel,
        out_shape=(vec, vec, vec, vec, mat, mat),
        name="ssm_prep",
    )(a_re.reshape(g, 1, p), a_im.reshape(g, 1, p), log_step.reshape(g, 1, 1),
      jnp.transpose(b_re, (0, 2, 1)), jnp.transpose(b_im, (0, 2, 1)))


def _stack_heads(q_blk, lane_lo):
    zero = jnp.zeros_like(q_blk[:, 0:LANES])
    lo, hi = [], []
    for c in range(ATTN_WIDTH // LANES):
        blk = q_blk[:, c * LANES:(c + 1) * LANES]
        lo.append(jnp.where(lane_lo, blk, zero))
        hi.append(jnp.where(lane_lo, zero, blk))
    return jnp.concatenate(lo + hi, axis=0)


def _unstack_heads(o, rows, lane_lo):
    nblk = ATTN_WIDTH // LANES
    return jnp.concatenate(
        [jnp.where(lane_lo, o[c * rows:(c + 1) * rows], o[(c + nblk) * rows:(c + nblk + 1) * rows])
         for c in range(nblk)], axis=-1)


Q_COLS = MXU_DIM


def _proj_stages(x_ref, gmix_ref, win_ref, q_stage, k_stage, vt_stage, u_ref, newk_ref, newv_ref):
    xn = _rms(x_ref[...], gmix_ref[...]).astype(BF16)
    q_scale = HEAD_DIM ** -0.5 * LOG2E
    for c in range(ATTN_WIDTH // Q_COLS):
        cols = slice(c * Q_COLS, (c + 1) * Q_COLS)
        q = jnp.dot(xn, win_ref[:, cols], preferred_element_type=F32)
        q_stage[:, cols] = (q * q_scale).astype(BF16)
        yield
    kv = jnp.dot(xn, win_ref[:, ATTN_WIDTH:ATTN_WIDTH + 2 * KV_WIDTH], preferred_element_type=F32)
    k_stage[...] = kv[:, :KV_WIDTH].astype(BF16)
    vt_stage[...] = kv[:, KV_WIDTH:].T.astype(BF16)
    newk_ref[...] = kv[TM_A - WINDOW:, :KV_WIDTH]
    newv_ref[...] = kv[TM_A - WINDOW:, KV_WIDTH:]
    yield
    u0 = ATTN_WIDTH + 2 * KV_WIDTH
    for c in range(SSM_WIDTH // Q_COLS):
        cols = slice(u0 + c * Q_COLS, u0 + (c + 1) * Q_COLS)
        u_ref[:, c * Q_COLS:(c + 1) * Q_COLS] = jnp.dot(xn, win_ref[:, cols], preferred_element_type=F32)
        yield


def _attn_stages(sinks_ref, attn_ref, k_scr, vt_scr, q_scr, bias_t_scr, gt_scr, first_tile):
    nstk = N_Q_HEADS * BLOCK
    nblk = ATTN_WIDTH // LANES
    lane_lo = lax.broadcasted_iota(jnp.int32, (BLOCK, LANES), 1) < HEAD_DIM
    row_lo = lax.broadcasted_iota(jnp.int32, (KV_WIDTH, BLOCK), 0) < HEAD_DIM
    kj = lax.broadcasted_iota(jnp.int32, (2 * BLOCK, nstk), 0)
    has_prev = jnp.logical_or(kj >= BLOCK, jnp.logical_not(first_tile))
    head = lax.broadcasted_iota(jnp.int32, (1, nstk), 1) // BLOCK
    sink = jnp.zeros((1, nstk), F32)
    for h in range(N_Q_HEADS):
        sink = jnp.where(head == h, sinks_ref[0, h] * LOG2E, sink)

    scores = []
    for j in range(TM_A // BLOCK):
        r0 = j * BLOCK
        q2 = _stack_heads(q_scr[r0:r0 + BLOCK, :], lane_lo)
        keys = k_scr[r0:r0 + 2 * BLOCK, :]
        scores.append(lax.dot_general(keys, q2, (((1,), (1,)), ((), ())), preferred_element_type=F32))
        yield
    for j in range(TM_A // BLOCK):
        r0 = j * BLOCK
        s = scores[j] + bias_t_scr[...]
        if j == 0:
            s = jnp.where(has_prev, s, NEG)
        m = jnp.maximum(jnp.max(s, axis=0, keepdims=True), sink)
        p = jnp.exp2(s - m)
        denom = jnp.sum(p, axis=0, keepdims=True) + jnp.exp2(sink - m)
        o = jnp.dot(vt_scr[:, r0:r0 + 2 * BLOCK], p.astype(BF16), preferred_element_type=F32) / denom
        a = jnp.concatenate(
            [jnp.where(row_lo, o[:, c * BLOCK:(c + 1) * BLOCK], o[:, (c + nblk) * BLOCK:(c + nblk + 1) * BLOCK])
             for c in range(nblk)], axis=0)
        a = a * lax.rsqrt(jnp.mean(a * a, axis=0, keepdims=True) + EPS) * gt_scr[...]
        attn_ref[r0:r0 + BLOCK, :] = a.T.astype(BF16)
        yield


def _in_attn_kernel(relb_ref, sinks_ref, x_ref, gmix_ref, win_ref, bucket_t_ref, gattn_ref,
                    wout_f, wglu_f, wup_f, wdown_f,
                    attn_ref, u_ref, newk_ref, newv_ref, wout_b, wglu_b, wup_b, wdown_b,
                    k_scr, vt_scr, q_scr, k_stage, vt_stage, q_stage, bias_t_scr, gt_scr, *, nt):
    i = pl.program_id(0)

    @pl.when(i == 0)
    def _():
        _build_bias_t(relb_ref, bucket_t_ref, bias_t_scr)
        gt_scr[...] = jnp.broadcast_to(gattn_ref[...], (ATTN_WIDTH, BLOCK))
        k_scr[...] = jnp.zeros_like(k_scr)
        vt_scr[...] = jnp.zeros_like(vt_scr)
        k_stage[...] = jnp.zeros_like(k_stage)
        vt_stage[...] = jnp.zeros_like(vt_stage)
        q_stage[...] = jnp.zeros_like(q_stage)

    k_scr[0:BLOCK, :] = k_scr[TM_A:TM_A + BLOCK, :]
    vt_scr[:, 0:BLOCK] = vt_scr[:, TM_A:TM_A + BLOCK]
    k_scr[BLOCK:BLOCK + TM_A, :] = k_stage[...]
    vt_scr[:, BLOCK:BLOCK + TM_A] = vt_stage[...]
    q_scr[...] = q_stage[...]

    wout_b[...] = wout_f[...].astype(BF16)
    wglu_b[...] = wglu_f[...].astype(BF16)
    wup_b[...] = wup_f[...].astype(BF16)
    wdown_b[...] = wdown_f[...].astype(BF16)

    proj = _proj_stages(x_ref, gmix_ref, win_ref, q_stage, k_stage, vt_stage, u_ref, newk_ref, newv_ref)
    attn = _attn_stages(sinks_ref, attn_ref, k_scr, vt_scr, q_scr, bias_t_scr, gt_scr, i % nt == 1 % nt)
    nb = TM_A // BLOCK
    order = ("p" + "a" * (nb // 2)) * 2 + "p" + "a" * (nb // 4) + "p" + "a" * (nb // 2) + "p" + "a" * (nb - nb // 4 - nb // 2)
    for stage in order:
        next(proj if stage == "p" else attn)
    assert next(proj, None) is None and next(attn, None) is None


def _const_spec(shape):
    zeros = (0,) * len(shape)
    return pl.BlockSpec(shape, lambda *_: zeros, pipeline_mode=pl.Buffered(1))


def _smem_spec():
    return pl.BlockSpec(memory_space=pltpu.SMEM)


def _in_attn(x, rel_bias, sinks, g_mix, w_in, bucket_t, g_attn_col, w_out, w_glu, w_up, w_down):
    bsz, seq, _ = x.shape
    nt = seq // TM_A
    steps = bsz * nt

    def tile(i):
        return jnp.minimum(i, steps - 1)

    cur = lambda width: pl.BlockSpec((TM_A, width), lambda i: (tile(i), 0))
    prev = lambda width: pl.BlockSpec((TM_A, width), lambda i: (jnp.maximum(i - 1, 0), 0))
    win_spec = pl.BlockSpec((None, WINDOW, KV_WIDTH), lambda i: (tile(i) // nt, 0, 0))

    def rows_of(w):
        blk = w.shape[0] // steps
        return pl.BlockSpec((blk, w.shape[1]), lambda i: (tile(i), 0))

    nhead_blk = D_MODEL // HEAD_DIM

    def wout_src(i):
        return (jnp.minimum(tile(i), nhead_blk - 1), 0)

    def wout_dst(i):
        j = jnp.minimum(tile(i), nhead_blk - 1)
        return (jnp.where(j < N_Q_HEADS, (j % Q_PER_KV) * N_KV_HEADS + j // Q_PER_KV, j), 0)

    assert steps >= nhead_blk and w_up.shape[0] % steps == 0 and w_down.shape[0] % steps == 0
    wout_in = pl.BlockSpec((HEAD_DIM, D_MODEL), wout_src)
    wout_out = pl.BlockSpec((HEAD_DIM, D_MODEL), wout_dst)
    cast_shape = lambda w: jax.ShapeDtypeStruct(w.shape, BF16)
    return pl.pallas_call(
        functools.partial(_in_attn_kernel, nt=nt),
        out_shape=(jax.ShapeDtypeStruct((bsz * seq, ATTN_WIDTH), BF16),
                   jax.ShapeDtypeStruct((bsz * seq, SSM_WIDTH), F32),
                   jax.ShapeDtypeStruct((bsz, WINDOW, KV_WIDTH), F32),
                   jax.ShapeDtypeStruct((bsz, WINDOW, KV_WIDTH), F32),
                   cast_shape(w_out), cast_shape(w_glu), cast_shape(w_up), cast_shape(w_down)),
        grid=(steps + 1,),
        in_specs=[_smem_spec(), _smem_spec(), cur(D_MODEL), _const_spec((1, D_MODEL)),
                  _const_spec((D_MODEL, D_IN_PROJ)), _const_spec((2 * BLOCK, BLOCK)),
                  _const_spec((ATTN_WIDTH, 1)),
                  wout_in, rows_of(w_glu), rows_of(w_up), rows_of(w_down)],
        out_specs=(prev(ATTN_WIDTH), cur(SSM_WIDTH), win_spec, win_spec,
                   wout_out, rows_of(w_glu), rows_of(w_up), rows_of(w_down)),
        scratch_shapes=[pltpu.VMEM((TM_A + BLOCK, KV_WIDTH), BF16),
                        pltpu.VMEM((KV_WIDTH, TM_A + BLOCK), BF16),
                        pltpu.VMEM((TM_A, ATTN_WIDTH), BF16),
                        pltpu.VMEM((TM_A, KV_WIDTH), BF16),
                        pltpu.VMEM((KV_WIDTH, TM_A), BF16),
                        pltpu.VMEM((TM_A, ATTN_WIDTH), BF16),
                        pltpu.VMEM((2 * BLOCK, N_Q_HEADS * BLOCK), F32),
                        pltpu.VMEM((ATTN_WIDTH, BLOCK), F32)],
        compiler_params=pltpu.CompilerParams(
            dimension_semantics=("arbitrary",), vmem_limit_bytes=VMEM_LIMIT),
        name="in_attn",
    )(rel_bias, sinks, x.reshape(bsz * seq, D_MODEL), g_mix, w_in, bucket_t, g_attn_col, w_out, w_glu, w_up, w_down)


def _s_in_attn_kernel(relb_ref, sinks_ref, x_ref, gmix_ref, win_ref, bucket_ref, gattn_ref,
                      ck_ref, cv_ref,
                      attn_ref, u_ref, newk_ref, newv_ref,
                      bias_scr, biasd_scr, proj_scr):
    rows = attn_ref.shape[0]
    steps = rows // S_SEQ
    pair = SUBLANES // steps
    prow = N_Q_HEADS * SUBLANES

    @pl.when(pl.program_id(0) == 0)
    def _():
        _build_bias(relb_ref, bucket_ref, bias_scr)
        for h in range(N_Q_HEADS):
            for k in range(pair):
                biasd_scr[h * SUBLANES + k * steps:h * SUBLANES + (k + 1) * steps, :] = bias_scr[h, 0:steps, :]
        xn = _rms(x_ref[...], gmix_ref[...]).astype(BF16)
        proj_all = jnp.dot(xn, win_ref[...], preferred_element_type=F32)
        proj_scr[...] = proj_all
        u_ref[...] = proj_all[:, ATTN_WIDTH + 2 * KV_WIDTH:]

    r0 = pl.multiple_of(pl.program_id(0) * rows, rows)
    proj = proj_scr[pl.ds(r0, rows), :]
    qs = proj[:, :ATTN_WIDTH] * (HEAD_DIM ** -0.5)

    grp = NEW_ROWS // SUBLANES
    nsq = grp * pair
    row = lax.broadcasted_iota(jnp.int32, (grp * prow, 1), 0)
    qseq = row // prow * pair + (row % SUBLANES) // steps
    qrow = row % steps
    kcol = lax.broadcasted_iota(jnp.int32, (grp * prow, nsq * WINDOW), 1)
    valid_c = jnp.logical_and(kcol // WINDOW == qseq, kcol % WINDOW > qrow)
    kcol_n = lax.broadcasted_iota(jnp.int32, (grp * prow, NEW_ROWS), 1)
    valid_n = jnp.logical_and(kcol_n // steps == qseq, kcol_n % steps <= qrow)
    sink = jnp.zeros((grp * prow, 1), F32)
    for h in range(N_Q_HEADS):
        sink = jnp.where(row % prow // SUBLANES == h, sinks_ref[0, h], sink)
    lane_lo = lax.broadcasted_iota(jnp.int32, (SUBLANES, LANES), 1) < HEAD_DIM
    gattn = gattn_ref[...]
    bias_rows = jnp.concatenate([biasd_scr[...]] * grp, axis=0)
    bias_c = jnp.concatenate([bias_rows[:, 0:WINDOW]] * nsq, axis=1)
    bias_new = bias_rows[:, WINDOW:2 * WINDOW]
    bias_n = bias_new
    lane = lax.broadcasted_iota(jnp.int32, (grp * prow, WINDOW), 1)
    for k in range(1, nsq):
        bias_n = jnp.where(lane >= k * steps, pltpu.roll(bias_new, k * steps, axis=1), bias_n)
    bias_n = bias_n[:, 0:NEW_ROWS]
    nt_dims = (((1,), (1,)), ((), ()))

    tiles = []
    for gi in range(S_SEQ // nsq):
        rws = slice(gi * NEW_ROWS, (gi + 1) * NEW_ROWS)
        q2 = jnp.concatenate(
            [_stack_heads(qs[gi * NEW_ROWS + t * SUBLANES:gi * NEW_ROWS + (t + 1) * SUBLANES], lane_lo)
             for t in range(grp)], axis=0).astype(BF16)
        kn = proj[rws, ATTN_WIDTH:ATTN_WIDTH + KV_WIDTH]
        vn = proj[rws, ATTN_WIDTH + KV_WIDTH:ATTN_WIDTH + 2 * KV_WIDTH]
        cks, cvs = [], []
        for k in range(nsq):
            sq = gi * nsq + k
            ck = ck_ref[sq]
            cv = cv_ref[sq]
            newk_ref[sq, 0:WINDOW - steps, :] = ck[steps:, :]
            newk_ref[sq, WINDOW - steps:, :] = kn[k * steps:(k + 1) * steps]
            newv_ref[sq, 0:WINDOW - steps, :] = cv[steps:, :]
            newv_ref[sq, WINDOW - steps:, :] = vn[k * steps:(k + 1) * steps]
            cks.append(ck.astype(BF16))
            cvs.append(cv.astype(BF16))
        s_c = lax.dot_general(q2, jnp.concatenate(cks, axis=0), nt_dims, preferred_element_type=F32)
        s_n = lax.dot_general(q2, kn.astype(BF16), nt_dims, preferred_element_type=F32)
        s_c = jnp.where(valid_c, s_c + bias_c, NEG)
        s_n = jnp.where(valid_n, s_n + bias_n, NEG)
        m = jnp.maximum(jnp.maximum(jnp.max(s_c, axis=-1, keepdims=True),
                                    jnp.max(s_n, axis=-1, keepdims=True)), sink)
        p_c = jnp.exp(s_c - m)
        p_n = jnp.exp(s_n - m)
        denom = (jnp.sum(p_c, axis=-1, keepdims=True) + jnp.sum(p_n, axis=-1, keepdims=True)
                 + jnp.exp(sink - m))
        o = (jnp.dot(p_c.astype(BF16), jnp.concatenate(cvs, axis=0), preferred_element_type=F32)
             + jnp.dot(p_n.astype(BF16), vn.astype(BF16), preferred_element_type=F32)) / denom
        for t in range(grp):
            tiles.append(_unstack_heads(o[t * prow:(t + 1) * prow], SUBLANES, lane_lo))
    attn_ref[...] = _rms(jnp.concatenate(tiles, axis=0), gattn).astype(BF16)


def _s_in_attn(x, rel_bias, sinks, g_mix, w_in, bucket, g_attn, cache_k, cache_v):
    nseq, steps, _ = x.shape
    rows = S_SEQ * steps
    xf = x.reshape(nseq * steps, D_MODEL)
    row = lambda width: pl.BlockSpec((rows, width), lambda i: (i, 0))
    cache = pl.BlockSpec((S_SEQ, WINDOW, KV_WIDTH), lambda i: (i, 0, 0))
    return pl.pallas_call(
        _s_in_attn_kernel,
        out_shape=(jax.ShapeDtypeStruct((nseq * steps, ATTN_WIDTH), BF16),
                   jax.ShapeDtypeStruct((nseq * steps, SSM_WIDTH), F32),
                   jax.ShapeDtypeStruct((nseq, WINDOW, KV_WIDTH), F32),
                   jax.ShapeDtypeStruct((nseq, WINDOW, KV_WIDTH), F32)),
        grid=(nseq // S_SEQ,),
        in_specs=[_smem_spec(), _smem_spec(), _const_spec((nseq * steps, D_MODEL)), _const_spec((1, D_MODEL)),
                  _const_spec((D_MODEL, D_IN_PROJ)), _const_spec((SUBLANES, 2 * BLOCK)),
                  _const_spec((1, ATTN_WIDTH)), cache, cache],
        out_specs=(row(ATTN_WIDTH), pl.BlockSpec((nseq * steps, SSM_WIDTH), lambda i: (0, 0)), cache, cache),
        scratch_shapes=[pltpu.VMEM((N_Q_HEADS, SUBLANES, 2 * BLOCK), F32),
                        pltpu.VMEM((N_Q_HEADS * SUBLANES, 2 * WINDOW), F32),
                        pltpu.VMEM((nseq * steps, D_IN_PROJ), F32)],
        compiler_params=pltpu.CompilerParams(
            dimension_semantics=("arbitrary",), vmem_limit_bytes=VMEM_LIMIT),
        name="s_in_attn",
    )(rel_bias, sinks, xf, g_mix, w_in, bucket[0:SUBLANES], g_attn, cache_k, cache_v)


def _embed_blocks(bbr_ref, bbi_ref, cre_in, cim_in, bre, bim, cre, cim):
    for ref in (bre, bim, cre, cim):
        ref[...] = jnp.zeros_like(ref)
    for g in range(N_SSM_GROUPS):
        sl, j = divmod(g, SLAB // SSM_STATE)
        ch = slice(j * SSM_GROUP, (j + 1) * SSM_GROUP)
        st = slice(j * SSM_STATE, (j + 1) * SSM_STATE)
        bre[sl, ch, st] = bbr_ref[g].astype(BF16)
        bim[sl, ch, st] = bbi_ref[g].astype(BF16)
        cre[sl, ch, st] = cre_in[g].astype(BF16)
        cim[sl, ch, st] = cim_in[g].astype(BF16)


def _output_map(h_slab_re, h_slab_im, c_slab_re, c_slab_im):
    nt_dims = (((1,), (1,)), ((), ()))
    yr = lax.dot_general(h_slab_re.astype(BF16), c_slab_re, nt_dims, preferred_element_type=F32)
    yi = lax.dot_general(h_slab_im.astype(BF16), c_slab_im, nt_dims, preferred_element_type=F32)
    return yr - yi


def _glu_norm(y, wglu_ref, bglu_ref, gssm_ref):
    g = jax.nn.gelu(y)
    gate = jax.nn.sigmoid(jnp.dot(g.astype(BF16), wglu_ref[...], preferred_element_type=F32) + bglu_ref[...])
    return _rms(g * gate, gssm_ref[...])


SLAB = 4 * LANES
SLAB_CH = SLAB // SSM_STATE * SSM_GROUP


REC_PIECE = 16
EPI_ROWS = 128
FF_BLK = 1024


def _scan_stages(u_ref, lr_ref, li_ref, lcr_ref, lci_ref, d_ref, wglu_ref, bglu_ref, gssm_ref, out_ref,
                 bre_ref, bim_ref, cre_ref, cim_ref,
                 pad_scr, up_scr, y_scr, hre, him, carry_re, carry_im, e_re, e_im, hc_re, hc_im):
    nslab = SSM_WIDTH // LANES
    nslabs = STATE_LANES // SLAB

    for c in range(SUBLANES):
        for s in range(nslab):
            pad_scr[s, c * PITCH:c * PITCH + CHUNK, :] = u_ref[c * CHUNK:(c + 1) * CHUNK, s * LANES:(s + 1) * LANES]
    yield
    for tau in range(CHUNK):
        for s in range(nslab):
            up_scr[tau * SUBLANES:(tau + 1) * SUBLANES, s * LANES:(s + 1) * LANES] = (
                pad_scr[s, pl.ds(tau, SUBLANES, stride=PITCH), :])
        if tau % REC_PIECE == REC_PIECE - 1:
            yield

    for sl in range(nslabs):
        cols = slice(sl * SLAB, (sl + 1) * SLAB)
        ch = slice(sl * SLAB_CH, (sl + 1) * SLAB_CH)
        upb = up_scr[:, ch].astype(BF16)
        hre[:, cols] = jnp.dot(upb, bre_ref[sl], preferred_element_type=F32)
        yield
        him[:, cols] = jnp.dot(upb, bim_ref[sl], preferred_element_type=F32)
        yield

    for sl in range(nslabs):
        cols = slice(sl * SLAB, (sl + 1) * SLAB)
        ch = slice(sl * SLAB_CH, (sl + 1) * SLAB_CH)
        ar = jnp.broadcast_to(lr_ref[:, cols], (SUBLANES, SLAB))
        ai = jnp.broadcast_to(li_ref[:, cols], (SUBLANES, SLAB))

        hr = jnp.zeros((SUBLANES, SLAB), F32)
        hi = jnp.zeros((SUBLANES, SLAB), F32)
        for tau in range(CHUNK):
            rws = slice(tau * SUBLANES, (tau + 1) * SUBLANES)
            hr, hi = ar * hr - ai * hi + hre[rws, cols], ar * hi + ai * hr + him[rws, cols]
            if tau % REC_PIECE == REC_PIECE - 1:
                yield
        e_re[:, cols] = hr
        e_im[:, cols] = hi

        cr = carry_re[:, cols]
        ci = carry_im[:, cols]
        lcr = lcr_ref[:, cols]
        lci = lci_ref[:, cols]
        for c in range(SUBLANES):
            hc_re[c:c + 1, cols] = cr
            hc_im[c:c + 1, cols] = ci
            er = e_re[c:c + 1, cols]
            ei = e_im[c:c + 1, cols]
            cr, ci = lcr * cr - lci * ci + er, lcr * ci + lci * cr + ei
        carry_re[:, cols] = cr
        carry_im[:, cols] = ci
        yield

        hr = hc_re[:, cols]
        hi = hc_im[:, cols]
        for tau in range(CHUNK):
            rws = slice(tau * SUBLANES, (tau + 1) * SUBLANES)
            hr, hi = ar * hr - ai * hi + hre[rws, cols], ar * hi + ai * hr + him[rws, cols]
            hre[rws, cols] = hr
            him[rws, cols] = hi
            if tau % REC_PIECE == REC_PIECE - 1:
                yield

        y_scr[:, ch] = _output_map(hre[:, cols], him[:, cols], cre_ref[sl], cim_ref[sl])
        yield

    taus = EPI_ROWS // SUBLANES
    for r in range(TM_B // EPI_ROWS):
        rows = slice(r * EPI_ROWS, (r + 1) * EPI_ROWS)
        o = _glu_norm(y_scr[rows, :] + d_ref[...] * up_scr[rows, :], wglu_ref, bglu_ref, gssm_ref)
        for i in range(taus):
            for s in range(nslab):
                pad_scr[s, pl.ds(r * taus + i, SUBLANES, stride=PITCH), :] = (
                    o[i * SUBLANES:(i + 1) * SUBLANES, s * LANES:(s + 1) * LANES])
        yield
    for c in range(SUBLANES):
        for s in range(nslab):
            out_ref[c * CHUNK:(c + 1) * CHUNK, s * LANES:(s + 1) * LANES] = (
                pad_scr[s, c * PITCH:c * PITCH + CHUNK, :].astype(BF16))
    yield


N_SCAN_STAGES = (1 + CHUNK // REC_PIECE + 2 * (STATE_LANES // SLAB)
                 + (STATE_LANES // SLAB) * (2 * (CHUNK // REC_PIECE) + 2) + TM_B // EPI_ROWS + 1)


def _mlp_stages(x_cols, attn, s_ref, wout_ref, gmlp_ref, wup_ref, wdown_ref, gfin_ref, y_ref, hm_scr, act_scr):
    ncol = D_MODEL // Q_COLS
    for n in range(ncol):
        cols = slice(n * Q_COLS, (n + 1) * Q_COLS)
        y_ref[:, cols] = (x_cols(cols)
                          + jnp.dot(attn(), wout_ref[0:ATTN_WIDTH, cols], preferred_element_type=F32)
                          + jnp.dot(s_ref[...], wout_ref[ATTN_WIDTH:, cols], preferred_element_type=F32))
        yield
    hm_scr[...] = _rms(y_ref[...], gmlp_ref[...]).astype(BF16)
    for c in range(D_FF // FF_BLK):
        act = act_scr.at[c % 2]
        for n in range(FF_BLK // Q_COLS):
            cols = slice(c * FF_BLK + n * Q_COLS, c * FF_BLK + (n + 1) * Q_COLS)
            up = jnp.dot(hm_scr[...], wup_ref[:, cols], preferred_element_type=F32)
            act[:, n * Q_COLS:(n + 1) * Q_COLS] = jnp.square(jnp.maximum(up, 0.0)).astype(BF16)
            yield
        for n in range(ncol):
            cols = slice(n * Q_COLS, (n + 1) * Q_COLS)
            y_ref[:, cols] += jnp.dot(act[...], wdown_ref[c * FF_BLK:(c + 1) * FF_BLK, cols],
                                      preferred_element_type=F32)
            yield
    y_ref[...] = _rms(y_ref[...], gfin_ref[...])
    yield


N_MLP_STAGES = D_MODEL // Q_COLS + (D_FF // FF_BLK) * (FF_BLK // Q_COLS + D_MODEL // Q_COLS) + 1


def _even_merge(n_a, n_b):
    keys = [((k + 0.5) / n_a, "a") for k in range(n_a)] + [((k + 0.5) / n_b, "b") for k in range(n_b)]
    return [name for _, name in sorted(keys)]


def _scan_mlp_kernel(x_ref, a_ref, u_ref, xdec_ref, adec_ref, sdec_ref,
                     lr_ref, li_ref, lcr_ref, lci_ref, bbr_ref, bbi_ref, cre_in, cim_in,
                     d_ref, wglu_ref, bglu_ref, gssm_ref, wout_ref, gmlp_ref, wup_ref, wdown_ref, gfin_ref,
                     y_ref, ydec_ref, hre_out, him_out,
                     ssm_scr, hm_scr, act_scr, bre_ref, bim_ref, cre_ref, cim_ref,
                     pad_scr, up_scr, y_scr, hre, him, carry_re, carry_im, e_re, e_im, hc_re, hc_im, *, nt):
    i = pl.program_id(0)
    dec = i == 0

    @pl.when(dec)
    def _():
        _embed_blocks(bbr_ref, bbi_ref, cre_in, cim_in, bre_ref, bim_ref, cre_ref, cim_ref)
        ssm_scr[...] = sdec_ref[...]

    @pl.when(i % nt == 0)
    def _():
        carry_re[...] = jnp.zeros_like(carry_re)
        carry_im[...] = jnp.zeros_like(carry_im)

    mlp = _mlp_stages(lambda cols: jnp.where(dec, xdec_ref[:, cols], x_ref[:, cols]),
                      lambda: jnp.where(dec, adec_ref[...], a_ref[...]),
                      ssm_scr, wout_ref, gmlp_ref, wup_ref, wdown_ref, gfin_ref, y_ref, hm_scr, act_scr)
    scan = _scan_stages(u_ref, lr_ref, li_ref, lcr_ref, lci_ref, d_ref, wglu_ref, bglu_ref, gssm_ref, ssm_scr,
                        bre_ref, bim_ref, cre_ref, cim_ref,
                        pad_scr, up_scr, y_scr, hre, him, carry_re, carry_im, e_re, e_im, hc_re, hc_im)
    for stage in _even_merge(N_MLP_STAGES, N_SCAN_STAGES):
        next(mlp if stage == "a" else scan)
    assert next(scan, None) is None and next(mlp, None) is None

    @pl.when(dec)
    def _():
        ydec_ref[...] = y_ref[...]

    @pl.when(i % nt == nt - 1)
    def _():
        hre_out[...] = carry_re[...]
        him_out[...] = carry_im[...]


def _scan_mlp(x, attn_n, u, x_dec, attn_dec, ssm_dec, lam, lam_chunk, bbar, c_gcp, d, w_glu, b_glu, g_ssm,
              w_out, g_mlp, w_up, w_down, g_fin):
    bsz, seq, _ = x.shape
    nt = seq // TM_B
    n = bsz * nt
    assert x_dec.shape[0] == TM_B, "the decode rows ride through the MLP as one tile"
    flat = lambda v: v.reshape(n * TM_B, v.shape[-1])
    cur = lambda width: pl.BlockSpec((TM_B, width), lambda i: (jnp.minimum(i, n - 1), 0))
    prev = lambda width: pl.BlockSpec((TM_B, width), lambda i: (jnp.maximum(i - 1, 0), 0))
    state = pl.BlockSpec((None, 1, STATE_LANES), lambda i: (jnp.minimum(i, n - 1) // nt, 0, 0))
    vec = _const_spec((1, STATE_LANES))
    chan = _const_spec((1, SSM_WIDTH))
    model = _const_spec((1, D_MODEL))
    dec = lambda width: _const_spec((TM_B, width))
    y, y_dec, h_re, h_im = pl.pallas_call(
        functools.partial(_scan_mlp_kernel, nt=nt),
        out_shape=(jax.ShapeDtypeStruct((n * TM_B, D_MODEL), F32),
                   jax.ShapeDtypeStruct((TM_B, D_MODEL), F32),
                   jax.ShapeDtypeStruct((bsz, 1, STATE_LANES), F32),
                   jax.ShapeDtypeStruct((bsz, 1, STATE_LANES), F32)),
        grid=(n + 1,),
        in_specs=[prev(D_MODEL), prev(ATTN_WIDTH), cur(SSM_WIDTH), dec(D_MODEL), dec(ATTN_WIDTH), dec(SSM_WIDTH),
                  vec, vec, vec, vec,
                  _group_blocks(), _group_blocks(), _group_blocks(), _group_blocks(),
                  chan, _const_spec((SSM_WIDTH, SSM_WIDTH)), chan, chan,
                  _const_spec((D_MODEL, D_MODEL)), model, _const_spec((D_MODEL, D_FF)),
                  _const_spec((D_FF, D_MODEL)), model],
        out_specs=(prev(D_MODEL), pl.BlockSpec((TM_B, D_MODEL), lambda i: (0, 0)), state, state),
        scratch_shapes=[pltpu.VMEM((TM_B, SSM_WIDTH), BF16),
                        pltpu.VMEM((TM_B, D_MODEL), BF16),
                        pltpu.VMEM((2, TM_B, FF_BLK), BF16)] + _block_operand_scratch() + [
                        pltpu.VMEM((SSM_WIDTH // LANES, SUBLANES * PITCH, LANES), F32),
                        pltpu.VMEM((TM_B, SSM_WIDTH), F32),
                        pltpu.VMEM((TM_B, SSM_WIDTH), F32),
                        pltpu.VMEM((TM_B, STATE_LANES), F32),
                        pltpu.VMEM((TM_B, STATE_LANES), F32),
                        pltpu.VMEM((1, STATE_LANES), F32),
                        pltpu.VMEM((1, STATE_LANES), F32),
                        pltpu.VMEM((SUBLANES, STATE_LANES), F32),
                        pltpu.VMEM((SUBLANES, STATE_LANES), F32),
                        pltpu.VMEM((SUBLANES, STATE_LANES), F32),
                        pltpu.VMEM((SUBLANES, STATE_LANES), F32)],
        compiler_params=pltpu.CompilerParams(
            dimension_semantics=("arbitrary",), vmem_limit_bytes=VMEM_LIMIT),
        name="scan_mlp",
    )(flat(x), flat(attn_n), flat(u), x_dec, attn_dec, ssm_dec, lam[0], lam[1], lam_chunk[0], lam_chunk[1],
      bbar[0], bbar[1], c_gcp[0], c_gcp[1], d, w_glu, b_glu, g_ssm, w_out, g_mlp, w_up, w_down, g_fin)
    return y.reshape(bsz, seq, D_MODEL), y_dec, h_re, h_im


def _block_operand_scratch():
    return [pltpu.VMEM((STATE_LANES // SLAB, SLAB_CH, SLAB), BF16) for _ in range(4)]


def _group_blocks():
    return _const_spec((N_SSM_GROUPS, SSM_GROUP, SSM_STATE))


def _s_scan_kernel(u_ref, h0r_ref, h0i_ref, lr_ref, li_ref, bbr_ref, bbi_ref, cre_in, cim_in,
                   d_ref, wglu_ref, bglu_ref, gssm_ref,
                   out_ref, hre_out, him_out,
                   bre_ref, bim_ref, cre_ref, cim_ref,
                   pad_scr, up_scr, hre, him):
    nseq = h0r_ref.shape[0]
    steps = u_ref.shape[0] // nseq
    nslab = SSM_WIDTH // LANES
    _embed_blocks(bbr_ref, bbi_ref, cre_in, cim_in, bre_ref, bim_ref, cre_ref, cim_ref)

    for s in range(nslab):
        pad_scr[s] = u_ref[:, s * LANES:(s + 1) * LANES]
    for i in range(steps):
        for s in range(nslab):
            up_scr[i * nseq:(i + 1) * nseq, s * LANES:(s + 1) * LANES] = (
                pad_scr[s, pl.ds(i, nseq, stride=steps), :])
    up = up_scr[...]
    upb = up.astype(BF16)

    ys = []
    for sl in range(STATE_LANES // SLAB):
        cols = slice(sl * SLAB, (sl + 1) * SLAB)
        ch = slice(sl * SLAB_CH, (sl + 1) * SLAB_CH)
        hre[:, cols] = jnp.dot(upb[:, ch], bre_ref[sl], preferred_element_type=F32)
        him[:, cols] = jnp.dot(upb[:, ch], bim_ref[sl], preferred_element_type=F32)
        ar = lr_ref[:, cols]
        ai = li_ref[:, cols]
        hr = h0r_ref[:, cols]
        hi = h0i_ref[:, cols]
        for i in range(steps):
            rws = slice(i * nseq, (i + 1) * nseq)
            hr, hi = ar * hr - ai * hi + hre[rws, cols], ar * hi + ai * hr + him[rws, cols]
            hre[rws, cols] = hr
            him[rws, cols] = hi
        hre_out[:, cols] = hr
        him_out[:, cols] = hi
        ys.append(_output_map(hre[:, cols], him[:, cols], cre_ref[sl], cim_ref[sl]))

    y = jnp.concatenate(ys, axis=-1) + d_ref[...] * up
    o = _glu_norm(y, wglu_ref, bglu_ref, gssm_ref)

    for i in range(steps):
        for s in range(nslab):
            pad_scr[s, pl.ds(i, nseq, stride=steps), :] = o[i * nseq:(i + 1) * nseq, s * LANES:(s + 1) * LANES]
    for s in range(nslab):
        out_ref[:, s * LANES:(s + 1) * LANES] = pad_scr[s].astype(BF16)


def _s_scan(u, h0_re, h0_im, lam, bbar, c_gcp, d, w_glu, b_glu, g_ssm):
    rows = u.shape[0]
    nseq = h0_re.shape[0]
    return pl.pallas_call(
        _s_scan_kernel,
        out_shape=(jax.ShapeDtypeStruct((rows, SSM_WIDTH), BF16),
                   jax.ShapeDtypeStruct((nseq, STATE_LANES), F32),
                   jax.ShapeDtypeStruct((nseq, STATE_LANES), F32)),
        scratch_shapes=_block_operand_scratch() + [
                        pltpu.VMEM((SSM_WIDTH // LANES, rows, LANES), F32),
                        pltpu.VMEM((rows, SSM_WIDTH), F32),
                        pltpu.VMEM((rows, STATE_LANES), F32),
                        pltpu.VMEM((rows, STATE_LANES), F32)],
        compiler_params=pltpu.CompilerParams(vmem_limit_bytes=VMEM_LIMIT),
        name="s_scan",
    )(u, h0_re, h0_im, lam[0], lam[1], bbar[0], bbar[1], c_gcp[0], c_gcp[1], d, w_glu, b_glu, g_ssm)


def _pair_heads(w, axis):
    shape = w.shape
    split = shape[:axis] + (N_KV_HEADS, Q_PER_KV, HEAD_DIM) + shape[axis + 1:]
    return jnp.swapaxes(w.reshape(split), axis, axis + 1).reshape(shape)


def kernel(x_prompt, x_sample, cache_k, cache_v, state_ssm_re, state_ssm_im, rel_bias, norm_mix, w_in, attn_sinks, ssm_a_re, ssm_a_im, ssm_log_step, ssm_b_re, ssm_b_im, ssm_c_re, ssm_c_im, ssm_d, w_glu, b_glu, norm_attn_out, norm_ssm_out, w_out, norm_mlp, w_up, w_down, norm_final):
    depth = w_in.shape[0]
    bsz, seq, _ = x_prompt.shape
    nseq, steps, _ = x_sample.shape
    bucket = jnp.asarray(_rel_bucket_table())
    g_fin = norm_final.reshape(1, D_MODEL)
    assert depth == 1, "the final norm is fused into the (single) layer's MLP kernel"

    xp, xs = x_prompt, x_sample
    outs = [[] for _ in range(8)]
    for l in range(depth):
        lr, li, lcr, lci, bbr, bbi = _ssm_prep(ssm_a_re[l], ssm_a_im[l], ssm_log_step[l], ssm_b_re[l], ssm_b_im[l])
        lam = (lr, li)
        lam_chunk = (lcr, lci)
        bbar = (bbr, bbi)
        c_gcp = (ssm_c_re[l], ssm_c_im[l])
        g_mix = norm_mix[l].reshape(1, D_MODEL)
        g_attn = _pair_heads(norm_attn_out[l].reshape(1, ATTN_WIDTH), axis=1)
        g_ssm = norm_ssm_out[l].reshape(1, SSM_WIDTH)
        g_mlp = norm_mlp[l].reshape(1, D_MODEL)
        sinks = attn_sinks[l].reshape(1, N_Q_HEADS)
        d = ssm_d[l].reshape(1, SSM_WIDTH)
        bg = b_glu[l].reshape(1, SSM_WIDTH)
        w_in_b = jnp.concatenate([_pair_heads(w_in[l][:, :ATTN_WIDTH], axis=1), w_in[l][:, ATTN_WIDTH:]],
                                 axis=1).astype(BF16)

        attn_n, u, k1, v1, w_out_b, w_glu_b, w_up_b, w_down_b = _in_attn(
            xp, rel_bias, sinks, g_mix, w_in_b, bucket.T, g_attn.reshape(ATTN_WIDTH, 1),
            w_out[l], w_glu[l], w_up[l], w_down[l])

        ck = cache_k[l].reshape(nseq, WINDOW, KV_WIDTH)
        cv = cache_v[l].reshape(nseq, WINDOW, KV_WIDTH)
        s_attn, s_u, k2, v2 = _s_in_attn(xs, rel_bias, sinks, g_mix, w_in_b, bucket, g_attn, ck, cv)
        s_ssm, r2, i2 = _s_scan(s_u, state_ssm_re[l].reshape(nseq, STATE_LANES),
                                state_ssm_im[l].reshape(nseq, STATE_LANES),
                                lam, bbar, c_gcp, d, w_glu_b, bg, g_ssm)

        xp, xs, r1, i1 = _scan_mlp(xp, attn_n, u, xs.reshape(nseq * steps, D_MODEL), s_attn, s_ssm,
                                   lam, lam_chunk, bbar, c_gcp, d, w_glu_b, bg, g_ssm,
                                   w_out_b, g_mlp, w_up_b, w_down_b, g_fin)
        xs = xs.reshape(nseq, steps, D_MODEL)

        kv_shape = (WINDOW, N_KV_HEADS, HEAD_DIM)
        st_shape = (N_SSM_GROUPS, SSM_STATE)
        for lst, val in zip(outs, (k1.reshape(bsz, *kv_shape), v1.reshape(bsz, *kv_shape),
                                   r1.reshape(bsz, *st_shape), i1.reshape(bsz, *st_shape),
                                   k2.reshape(nseq, *kv_shape), v2.reshape(nseq, *kv_shape),
                                   r2.reshape(nseq, *st_shape), i2.reshape(nseq, *st_shape))):
            lst.append(val)

    return (xp, xs) + tuple(jnp.stack(o) for o in outs)
```

```python
import functools
import math

import numpy as np
import jax
import jax.numpy as jnp
from jax import lax
from jax.experimental import pallas as pl
from jax.experimental.pallas import tpu as pltpu

D_MODEL = 1024
HEAD_DIM = 64
ATTN_WIDTH = 512
N_Q_HEADS = 8
N_KV_HEADS = 2
Q_PER_KV = 4
KV_WIDTH = 128
WINDOW = 128
BLOCK = 128
N_BUCKETS = 32
MAX_DISTANCE = 128
SSM_WIDTH = 512
SSM_GROUP = 16
N_SSM_GROUPS = 32
SSM_STATE = 64
STATE_LANES = N_SSM_GROUPS * SSM_STATE
D_FF = 4096
D_IN_PROJ = 1280
EPS = 1e-6
NEG = -1e30
LOG2E = math.log2(math.e)

LANES = 128
SUBLANES = 8
MXU_DIM = 256
VMEM_LIMIT = 56 * 1024 * 1024

F32 = jnp.float32
BF16 = jnp.bfloat16

TM_A = 1024
TM_B = 512
CHUNK = TM_B // SUBLANES
PITCH = CHUNK + SUBLANES
S_SEQ = 16
NEW_ROWS = 16


def _rms(x, g):
    return x * lax.rsqrt(jnp.mean(x * x, axis=-1, keepdims=True) + EPS) * g


def _rel_bucket_table():
    i = np.arange(BLOCK)[:, None]
    j = np.arange(2 * BLOCK)[None, :]
    n = np.maximum(i + BLOCK - j, 0)
    max_exact = N_BUCKETS // 2
    nf = np.maximum(n, max_exact).astype(np.float32)
    large = max_exact + (np.log(nf / np.float32(max_exact)) / np.float32(math.log(MAX_DISTANCE / max_exact))
                         * np.float32(N_BUCKETS - max_exact)).astype(np.int32)
    large = np.minimum(large, N_BUCKETS - 1)
    return np.where(n < max_exact, n, large).astype(np.int32)


def _build_bias(relb_ref, bucket_ref, bias_scr):
    bucket = bucket_ref[...]
    for h in range(N_Q_HEADS):
        bias_scr[h] = jnp.zeros(bucket.shape, F32)
    for b in range(N_BUCKETS):
        hit = bucket == b
        for h in range(N_Q_HEADS):
            bias_scr[h] = jnp.where(hit, relb_ref[b, h], bias_scr[h])


def _build_bias_t(relb_ref, bucket_t_ref, bias_t_scr):
    bucket_t = bucket_t_ref[...]
    bias_t_scr[...] = jnp.zeros_like(bias_t_scr)
    for b in range(N_BUCKETS):
        hit = bucket_t == b
        for h in range(N_Q_HEADS):
            cols = slice(h * BLOCK, (h + 1) * BLOCK)
            bias_t_scr[:, cols] = jnp.where(hit, relb_ref[b, h] * LOG2E, bias_t_scr[:, cols])
    kj = lax.broadcasted_iota(jnp.int32, (2 * BLOCK, BLOCK), 0)
    qi = lax.broadcasted_iota(jnp.int32, (2 * BLOCK, BLOCK), 1)
    dist = qi + BLOCK - kj
    in_band = jnp.logical_and(dist >= 0, dist < WINDOW)
    for h in range(N_Q_HEADS):
        cols = slice(h * BLOCK, (h + 1) * BLOCK)
        bias_t_scr[:, cols] = jnp.where(in_band, bias_t_scr[:, cols], NEG)


def _ssm_prep_kernel(are_ref, aim_ref, ls_ref, btr_ref, bti_ref,
                     lr_ref, li_ref, lcr_ref, lci_ref, bbr_ref, bbi_ref):
    ar = are_ref[...]
    ai = aim_ref[...]
    step = jnp.exp(ls_ref[...])
    zr = ar * step
    zi = ai * step
    mag = jnp.exp(zr)
    lr = mag * jnp.cos(zi)
    li = mag * jnp.sin(zi)
    pr, pi = lr, li
    for _ in range(int(math.log2(CHUNK))):
        pr, pi = pr * pr - pi * pi, 2.0 * pr * pi
    for g in range(N_SSM_GROUPS):
        st = slice(g * SSM_STATE, (g + 1) * SSM_STATE)
        lr_ref[:, st] = lr[g]
        li_ref[:, st] = li[g]
        lcr_ref[:, st] = pr[g]
        lci_ref[:, st] = pi[g]
    nr = lr - 1.0
    ni = li
    den = ar * ar + ai * ai
    cr = (nr * ar + ni * ai) / den
    ci = (ni * ar - nr * ai) / den
    btr = btr_ref[...]
    bti = bti_ref[...]
    bbr_ref[...] = cr * btr - ci * bti
    bbi_ref[...] = cr * bti + ci * btr


def _ssm_prep(a_re, a_im, log_step, b_re, b_im):
    g, p, c = N_SSM_GROUPS, SSM_STATE, SSM_GROUP
    vec = jax.ShapeDtypeStruct((1, STATE_LANES), F32)
    mat = jax.ShapeDtypeStruct((g, c, p), F32)
    return pl.pallas_call(
        _ssm_prep_kernel,
        out_shape=(vec, vec, vec, vec, mat, mat),
        name="ssm_prep",
    )(a_re.reshape(g, 1, p), a_im.reshape(g, 1, p), log_step.reshape(g, 1, 1),
      jnp.transpose(b_re, (0, 2, 1)), jnp.transpose(b_im, (0, 2, 1)))


def _stack_heads(q_blk, lane_lo):
    zero = jnp.zeros_like(q_blk[:, 0:LANES])
    lo, hi = [], []
    for c in range(ATTN_WIDTH // LANES):
        blk = q_blk[:, c * LANES:(c + 1) * LANES]
        lo.append(jnp.where(lane_lo, blk, zero))
        hi.append(jnp.where(lane_lo, zero, blk))
    return jnp.concatenate(lo + hi, axis=0)


def _unstack_heads(o, rows, lane_lo):
    nblk = ATTN_WIDTH // LANES
    return jnp.concatenate(
        [jnp.where(lane_lo, o[c * rows:(c + 1) * rows], o[(c + nblk) * rows:(c + nblk + 1) * rows])
         for c in range(nblk)], axis=-1)


Q_COLS = MXU_DIM


def _proj_stages(x_ref, gmix_ref, win_ref, q_stage, k_stage, vt_stage, u_ref, newk_ref, newv_ref):
    xn = _rms(x_ref[...], gmix_ref[...]).astype(BF16)
    q_scale = HEAD_DIM ** -0.5 * LOG2E
    for c in range(ATTN_WIDTH // Q_COLS):
        cols = slice(c * Q_COLS, (c + 1) * Q_COLS)
        q = jnp.dot(xn, win_ref[:, cols], preferred_element_type=F32)
        q_stage[:, cols] = (q * q_scale).astype(BF16)
        yield
    kv = jnp.dot(xn, win_ref[:, ATTN_WIDTH:ATTN_WIDTH + 2 * KV_WIDTH], preferred_element_type=F32)
    k_stage[...] = kv[:, :KV_WIDTH].astype(BF16)
    vt_stage[...] = kv[:, KV_WIDTH:].T.astype(BF16)
    newk_ref[...] = kv[TM_A - WINDOW:, :KV_WIDTH]
    newv_ref[...] = kv[TM_A - WINDOW:, KV_WIDTH:]
    yield
    u0 = ATTN_WIDTH + 2 * KV_WIDTH
    for c in range(SSM_WIDTH // Q_COLS):
        cols = slice(u0 + c * Q_COLS, u0 + (c + 1) * Q_COLS)
        u_ref[:, c * Q_COLS:(c + 1) * Q_COLS] = jnp.dot(xn, win_ref[:, cols], preferred_element_type=F32)
        yield


def _attn_stages(sinks_ref, attn_ref, k_scr, vt_scr, q_scr, bias_t_scr, gt_scr, first_tile):
    nstk = N_Q_HEADS * BLOCK
    nblk = ATTN_WIDTH // LANES
    lane_lo = lax.broadcasted_iota(jnp.int32, (BLOCK, LANES), 1) < HEAD_DIM
    row_lo = lax.broadcasted_iota(jnp.int32, (KV_WIDTH, BLOCK), 0) < HEAD_DIM
    half = nstk // 2
    kj = lax.broadcasted_iota(jnp.int32, (2 * BLOCK, half), 0)
    has_prev = jnp.logical_or(kj >= BLOCK, jnp.logical_not(first_tile))
    head = lax.broadcasted_iota(jnp.int32, (1, half), 1) // BLOCK
    sinks = []
    for h in range(2):
        sink = jnp.zeros((1, half), F32)
        for c in range(N_Q_HEADS // 2):
            sink = jnp.where(head == c, sinks_ref[0, h * (N_Q_HEADS // 2) + c] * LOG2E, sink)
        sinks.append(sink)
    scores = []
    for j in range(TM_A // BLOCK):
        r0 = j * BLOCK
        q2 = _stack_heads(q_scr[r0:r0 + BLOCK, :], lane_lo)
        keys = k_scr[r0:r0 + 2 * BLOCK, :]
        scores.append([lax.dot_general(keys, q2[h * half:(h + 1) * half], (((1,), (1,)), ((), ())),
                                       preferred_element_type=F32) for h in range(2)])
        yield
    for j in range(TM_A // BLOCK):
        r0 = j * BLOCK
        o = []
        for h in range(2):
            lanes = slice(h * half, (h + 1) * half)
            s = scores[j][h] + bias_t_scr[:, lanes]
            if j == 0:
                s = jnp.where(has_prev, s, NEG)
            m = jnp.maximum(jnp.max(s, axis=0, keepdims=True), sinks[h])
            p = jnp.exp2(s - m)
            denom = jnp.sum(p, axis=0, keepdims=True) + jnp.exp2(sinks[h] - m)
            o.append(jnp.dot(vt_scr[:, r0:r0 + 2 * BLOCK], p.astype(BF16), preferred_element_type=F32) / denom)
        a = jnp.concatenate(
            [jnp.where(row_lo, o[0][:, c * BLOCK:(c + 1) * BLOCK], o[1][:, c * BLOCK:(c + 1) * BLOCK])
             for c in range(nblk)], axis=0)
        a = a * lax.rsqrt(jnp.mean(a * a, axis=0, keepdims=True) + EPS) * gt_scr[...]
        attn_ref[r0:r0 + BLOCK, :] = a.T.astype(BF16)
        yield


def _in_attn_kernel(relb_ref, sinks_ref, x_ref, gmix_ref, win_ref, bucket_t_ref, gattn_ref,
                    wout_f, wglu_f, wup_f, wdown_f,
                    attn_ref, u_ref, newk_ref, newv_ref, wout_b, wglu_b, wup_b, wdown_b,
                    k_scr, vt_scr, q_scr, k_stage, vt_stage, q_stage, bias_t_scr, gt_scr, *, nt):
    i = pl.program_id(0)

    @pl.when(i == 0)
    def _():
        _build_bias_t(relb_ref, bucket_t_ref, bias_t_scr)
        gt_scr[...] = jnp.broadcast_to(gattn_ref[...], (ATTN_WIDTH, BLOCK))
        k_scr[...] = jnp.zeros_like(k_scr)
        vt_scr[...] = jnp.zeros_like(vt_scr)
        k_stage[...] = jnp.zeros_like(k_stage)
        vt_stage[...] = jnp.zeros_like(vt_stage)
        q_stage[...] = jnp.zeros_like(q_stage)

    k_scr[0:BLOCK, :] = k_scr[TM_A:TM_A + BLOCK, :]
    vt_scr[:, 0:BLOCK] = vt_scr[:, TM_A:TM_A + BLOCK]
    k_scr[BLOCK:BLOCK + TM_A, :] = k_stage[...]
    vt_scr[:, BLOCK:BLOCK + TM_A] = vt_stage[...]
    q_scr[...] = q_stage[...]

    wout_b[...] = wout_f[...].astype(BF16)
    wglu_b[...] = wglu_f[...].astype(BF16)
    wup_b[...] = wup_f[...].astype(BF16)
    wdown_b[...] = wdown_f[...].astype(BF16)

    proj = _proj_stages(x_ref, gmix_ref, win_ref, q_stage, k_stage, vt_stage, u_ref, newk_ref, newv_ref)
    attn = _attn_stages(sinks_ref, attn_ref, k_scr, vt_scr, q_scr, bias_t_scr, gt_scr, i % nt == 1 % nt)
    nb = TM_A // BLOCK
    order = ("p" + "a" * (nb // 2)) * 2 + "p" + "a" * (nb // 4) + "p" + "a" * (nb // 2) + "p" + "a" * (nb - nb // 4 - nb // 2)
    for stage in order:
        next(proj if stage == "p" else attn)
    assert next(proj, None) is None and next(attn, None) is None


def _const_spec(shape):
    zeros = (0,) * len(shape)
    return pl.BlockSpec(shape, lambda *_: zeros, pipeline_mode=pl.Buffered(1))


def _smem_spec():
    return pl.BlockSpec(memory_space=pltpu.SMEM)


def _in_attn(x, rel_bias, sinks, g_mix, w_in, bucket_t, g_attn_col, w_out, w_glu, w_up, w_down):
    bsz, seq, _ = x.shape
    nt = seq // TM_A
    steps = bsz * nt

    def tile(i):
        return jnp.minimum(i, steps - 1)

    cur = lambda width: pl.BlockSpec((TM_A, width), lambda i: (tile(i), 0))
    prev = lambda width: pl.BlockSpec((TM_A, width), lambda i: (jnp.maximum(i - 1, 0), 0))
    win_spec = pl.BlockSpec((None, WINDOW, KV_WIDTH), lambda i: (tile(i) // nt, 0, 0))

    def rows_of(w):
        blk = w.shape[0] // steps
        return pl.BlockSpec((blk, w.shape[1]), lambda i: (tile(i), 0))

    nhead_blk = D_MODEL // HEAD_DIM

    def wout_src(i):
        return (jnp.minimum(tile(i), nhead_blk - 1), 0)

    def wout_dst(i):
        j = jnp.minimum(tile(i), nhead_blk - 1)
        return (jnp.where(j < N_Q_HEADS, (j % Q_PER_KV) * N_KV_HEADS + j // Q_PER_KV, j), 0)

    assert steps >= nhead_blk and w_up.shape[0] % steps == 0 and w_down.shape[0] % steps == 0
    wout_in = pl.BlockSpec((HEAD_DIM, D_MODEL), wout_src)
    wout_out = pl.BlockSpec((HEAD_DIM, D_MODEL), wout_dst)
    cast_shape = lambda w: jax.ShapeDtypeStruct(w.shape, BF16)
    return pl.pallas_call(
        functools.partial(_in_attn_kernel, nt=nt),
        out_shape=(jax.ShapeDtypeStruct((bsz * seq, ATTN_WIDTH), BF16),
                   jax.ShapeDtypeStruct((bsz * seq, SSM_WIDTH), F32),
                   jax.ShapeDtypeStruct((bsz, WINDOW, KV_WIDTH), F32),
                   jax.ShapeDtypeStruct((bsz, WINDOW, KV_WIDTH), F32),
                   cast_shape(w_out), cast_shape(w_glu), cast_shape(w_up), cast_shape(w_down)),
        grid=(steps + 1,),
        in_specs=[_smem_spec(), _smem_spec(), cur(D_MODEL), _const_spec((1, D_MODEL)),
                  _const_spec((D_MODEL, D_IN_PROJ)), _const_spec((2 * BLOCK, BLOCK)),
                  _const_spec((ATTN_WIDTH, 1)),
                  wout_in, rows_of(w_glu), rows_of(w_up), rows_of(w_down)],
        out_specs=(prev(ATTN_WIDTH), cur(SSM_WIDTH), win_spec, win_spec,
                   wout_out, rows_of(w_glu), rows_of(w_up), rows_of(w_down)),
        scratch_shapes=[pltpu.VMEM((TM_A + BLOCK, KV_WIDTH), BF16),
                        pltpu.VMEM((KV_WIDTH, TM_A + BLOCK), BF16),
                        pltpu.VMEM((TM_A, ATTN_WIDTH), BF16),
                        pltpu.VMEM((TM_A, KV_WIDTH), BF16),
                        pltpu.VMEM((KV_WIDTH, TM_A), BF16),
                        pltpu.VMEM((TM_A, ATTN_WIDTH), BF16),
                        pltpu.VMEM((2 * BLOCK, N_Q_HEADS * BLOCK), F32),
                        pltpu.VMEM((ATTN_WIDTH, BLOCK), F32)],
        compiler_params=pltpu.CompilerParams(
            dimension_semantics=("arbitrary",), vmem_limit_bytes=VMEM_LIMIT),
        name="in_attn",
    )(rel_bias, sinks, x.reshape(bsz * seq, D_MODEL), g_mix, w_in, bucket_t, g_attn_col, w_out, w_glu, w_up, w_down)


def _s_in_attn_kernel(relb_ref, sinks_ref, x_ref, gmix_ref, win_ref, bucket_ref, gattn_ref,
                      ck_ref, cv_ref,
                      attn_ref, u_ref, newk_ref, newv_ref,
                      bias_scr, biasd_scr, proj_scr):
    rows = attn_ref.shape[0]
    steps = rows // S_SEQ
    pair = SUBLANES // steps
    prow = N_Q_HEADS * SUBLANES

    @pl.when(pl.program_id(0) == 0)
    def _():
        _build_bias(relb_ref, bucket_ref, bias_scr)
        for h in range(N_Q_HEADS):
            for k in range(pair):
                biasd_scr[h * SUBLANES + k * steps:h * SUBLANES + (k + 1) * steps, :] = bias_scr[h, 0:steps, :]
        xn = _rms(x_ref[...], gmix_ref[...]).astype(BF16)
        proj_all = jnp.dot(xn, win_ref[...], preferred_element_type=F32)
        proj_scr[...] = proj_all
        u_ref[...] = proj_all[:, ATTN_WIDTH + 2 * KV_WIDTH:]

    r0 = pl.multiple_of(pl.program_id(0) * rows, rows)
    proj = proj_scr[pl.ds(r0, rows), :]
    qs = proj[:, :ATTN_WIDTH] * (HEAD_DIM ** -0.5)

    grp = NEW_ROWS // SUBLANES
    nsq = grp * pair
    row = lax.broadcasted_iota(jnp.int32, (grp * prow, 1), 0)
    qseq = row // prow * pair + (row % SUBLANES) // steps
    qrow = row % steps
    kcol = lax.broadcasted_iota(jnp.int32, (grp * prow, nsq * WINDOW), 1)
    valid_c = jnp.logical_and(kcol // WINDOW == qseq, kcol % WINDOW > qrow)
    kcol_n = lax.broadcasted_iota(jnp.int32, (grp * prow, NEW_ROWS), 1)
    valid_n = jnp.logical_and(kcol_n // steps == qseq, kcol_n % steps <= qrow)
    sink = jnp.zeros((grp * prow, 1), F32)
    for h in range(N_Q_HEADS):
        sink = jnp.where(row % prow // SUBLANES == h, sinks_ref[0, h], sink)
    lane_lo = lax.broadcasted_iota(jnp.int32, (SUBLANES, LANES), 1) < HEAD_DIM
    gattn = gattn_ref[...]
    bias_rows = jnp.concatenate([biasd_scr[...]] * grp, axis=0)
    bias_c = jnp.concatenate([bias_rows[:, 0:WINDOW]] * nsq, axis=1)
    bias_new = bias_rows[:, WINDOW:2 * WINDOW]
    bias_n = bias_new
    lane = lax.broadcasted_iota(jnp.int32, (grp * prow, WINDOW), 1)
    for k in range(1, nsq):
        bias_n = jnp.where(lane >= k * steps, pltpu.roll(bias_new, k * steps, axis=1), bias_n)
    bias_n = bias_n[:, 0:NEW_ROWS]
    nt_dims = (((1,), (1,)), ((), ()))

    tiles = []
    for gi in range(S_SEQ // nsq):
        rws = slice(gi * NEW_ROWS, (gi + 1) * NEW_ROWS)
        q2 = jnp.concatenate(
            [_stack_heads(qs[gi * NEW_ROWS + t * SUBLANES:gi * NEW_ROWS + (t + 1) * SUBLANES], lane_lo)
             for t in range(grp)], axis=0).astype(BF16)
        kn = proj[rws, ATTN_WIDTH:ATTN_WIDTH + KV_WIDTH]
        vn = proj[rws, ATTN_WIDTH + KV_WIDTH:ATTN_WIDTH + 2 * KV_WIDTH]
        cks, cvs = [], []
        for k in range(nsq):
            sq = gi * nsq + k
            ck = ck_ref[sq]
            cv = cv_ref[sq]
            newk_ref[sq, 0:WINDOW - steps, :] = ck[steps:, :]
            newk_ref[sq, WINDOW - steps:, :] = kn[k * steps:(k + 1) * steps]
            newv_ref[sq, 0:WINDOW - steps, :] = cv[steps:, :]
            newv_ref[sq, WINDOW - steps:, :] = vn[k * steps:(k + 1) * steps]
            cks.append(ck.astype(BF16))
            cvs.append(cv.astype(BF16))
        s_c = lax.dot_general(q2, jnp.concatenate(cks, axis=0), nt_dims, preferred_element_type=F32)
        s_n = lax.dot_general(q2, kn.astype(BF16), nt_dims, preferred_element_type=F32)
        s_c = jnp.where(valid_c, s_c + bias_c, NEG)
        s_n = jnp.where(valid_n, s_n + bias_n, NEG)
        m = jnp.maximum(jnp.maximum(jnp.max(s_c, axis=-1, keepdims=True),
                                    jnp.max(s_n, axis=-1, keepdims=True)), sink)
        p_c = jnp.exp(s_c - m)
        p_n = jnp.exp(s_n - m)
        denom = (jnp.sum(p_c, axis=-1, keepdims=True) + jnp.sum(p_n, axis=-1, keepdims=True)
                 + jnp.exp(sink - m))
        o = (jnp.dot(p_c.astype(BF16), jnp.concatenate(cvs, axis=0), preferred_element_type=F32)
             + jnp.dot(p_n.astype(BF16), vn.astype(BF16), preferred_element_type=F32)) / denom
        for t in range(grp):
            tiles.append(_unstack_heads(o[t * prow:(t + 1) * prow], SUBLANES, lane_lo))
    attn_ref[...] = _rms(jnp.concatenate(tiles, axis=0), gattn).astype(BF16)


def _s_in_attn(x, rel_bias, sinks, g_mix, w_in, bucket, g_attn, cache_k, cache_v):
    nseq, steps, _ = x.shape
    rows = S_SEQ * steps
    xf = x.reshape(nseq * steps, D_MODEL)
    row = lambda width: pl.BlockSpec((rows, width), lambda i: (i, 0))
    cache = pl.BlockSpec((S_SEQ, WINDOW, KV_WIDTH), lambda i: (i, 0, 0))
    return pl.pallas_call(
        _s_in_attn_kernel,
        out_shape=(jax.ShapeDtypeStruct((nseq * steps, ATTN_WIDTH), BF16),
                   jax.ShapeDtypeStruct((nseq * steps, SSM_WIDTH), F32),
                   jax.ShapeDtypeStruct((nseq, WINDOW, KV_WIDTH), F32),
                   jax.ShapeDtypeStruct((nseq, WINDOW, KV_WIDTH), F32)),
        grid=(nseq // S_SEQ,),
        in_specs=[_smem_spec(), _smem_spec(), _const_spec((nseq * steps, D_MODEL)), _const_spec((1, D_MODEL)),
                  _const_spec((D_MODEL, D_IN_PROJ)), _const_spec((SUBLANES, 2 * BLOCK)),
                  _const_spec((1, ATTN_WIDTH)), cache, cache],
        out_specs=(row(ATTN_WIDTH), pl.BlockSpec((nseq * steps, SSM_WIDTH), lambda i: (0, 0)), cache, cache),
        scratch_shapes=[pltpu.VMEM((N_Q_HEADS, SUBLANES, 2 * BLOCK), F32),
                        pltpu.VMEM((N_Q_HEADS * SUBLANES, 2 * WINDOW), F32),
                        pltpu.VMEM((nseq * steps, D_IN_PROJ), F32)],
        compiler_params=pltpu.CompilerParams(
            dimension_semantics=("arbitrary",), vmem_limit_bytes=VMEM_LIMIT),
        name="s_in_attn",
    )(rel_bias, sinks, xf, g_mix, w_in, bucket[0:SUBLANES], g_attn, cache_k, cache_v)


def _embed_blocks(bbr_ref, bbi_ref, cre_in, cim_in, bre, bim, cre, cim):
    for ref in (bre, bim, cre, cim):
        ref[...] = jnp.zeros_like(ref)
    for g in range(N_SSM_GROUPS):
        sl, j = divmod(g, SLAB // SSM_STATE)
        ch = slice(j * SSM_GROUP, (j + 1) * SSM_GROUP)
        st = slice(j * SSM_STATE, (j + 1) * SSM_STATE)
        bre[sl, ch, st] = bbr_ref[g].astype(BF16)
        bim[sl, ch, st] = bbi_ref[g].astype(BF16)
        cre[sl, ch, st] = cre_in[g].astype(BF16)
        cim[sl, ch, st] = cim_in[g].astype(BF16)


def _output_map(h_slab_re, h_slab_im, c_slab_re, c_slab_im):
    nt_dims = (((1,), (1,)), ((), ()))
    yr = lax.dot_general(h_slab_re.astype(BF16), c_slab_re, nt_dims, preferred_element_type=F32)
    yi = lax.dot_general(h_slab_im.astype(BF16), c_slab_im, nt_dims, preferred_element_type=F32)
    return yr - yi


def _glu_norm(y, wglu_ref, bglu_ref, gssm_ref):
    g = jax.nn.gelu(y)
    gate = jax.nn.sigmoid(jnp.dot(g.astype(BF16), wglu_ref[...], preferred_element_type=F32) + bglu_ref[...])
    return _rms(g * gate, gssm_ref[...])


SLAB = 4 * LANES
SLAB_CH = SLAB // SSM_STATE * SSM_GROUP


REC_PIECE = 16
EPI_ROWS = 128
FF_BLK = 1024


def _scan_stages(u_ref, lr_ref, li_ref, lcr_ref, lci_ref, d_ref, wglu_ref, bglu_ref, gssm_ref, out_ref,
                 bre_ref, bim_ref, cre_ref, cim_ref,
                 pad_scr, up_scr, y_scr, hre, him, carry_re, carry_im, e_re, e_im, hc_re, hc_im):
    nslab = SSM_WIDTH // LANES
    nslabs = STATE_LANES // SLAB

    for c in range(SUBLANES):
        for s in range(nslab):
            pad_scr[s, c * PITCH:c * PITCH + CHUNK, :] = u_ref[c * CHUNK:(c + 1) * CHUNK, s * LANES:(s + 1) * LANES]
    yield
    for tau in range(CHUNK):
        for s in range(nslab):
            up_scr[tau * SUBLANES:(tau + 1) * SUBLANES, s * LANES:(s + 1) * LANES] = (
                pad_scr[s, pl.ds(tau, SUBLANES, stride=PITCH), :])
        if tau % REC_PIECE == REC_PIECE - 1:
            yield

    for sl in range(nslabs):
        cols = slice(sl * SLAB, (sl + 1) * SLAB)
        ch = slice(sl * SLAB_CH, (sl + 1) * SLAB_CH)
        upb = up_scr[:, ch].astype(BF16)
        hre[:, cols] = jnp.dot(upb, bre_ref[sl], preferred_element_type=F32)
        yield
        him[:, cols] = jnp.dot(upb, bim_ref[sl], preferred_element_type=F32)
        yield

    for sl in range(nslabs):
        cols = slice(sl * SLAB, (sl + 1) * SLAB)
        ch = slice(sl * SLAB_CH, (sl + 1) * SLAB_CH)
        ar = jnp.broadcast_to(lr_ref[:, cols], (SUBLANES, SLAB))
        ai = jnp.broadcast_to(li_ref[:, cols], (SUBLANES, SLAB))

        hr = jnp.zeros((SUBLANES, SLAB), F32)
        hi = jnp.zeros((SUBLANES, SLAB), F32)
        for tau in range(CHUNK):
            rws = slice(tau * SUBLANES, (tau + 1) * SUBLANES)
            hr, hi = ar * hr - ai * hi + hre[rws, cols], ar * hi + ai * hr + him[rws, cols]
            if tau % REC_PIECE == REC_PIECE - 1:
                yield
        e_re[:, cols] = hr
        e_im[:, cols] = hi

        cr = carry_re[:, cols]
        ci = carry_im[:, cols]
        lcr = lcr_ref[:, cols]
        lci = lci_ref[:, cols]
        for c in range(SUBLANES):
            hc_re[c:c + 1, cols] = cr
            hc_im[c:c + 1, cols] = ci
            er = e_re[c:c + 1, cols]
            ei = e_im[c:c + 1, cols]
            cr, ci = lcr * cr - lci * ci + er, lcr * ci + lci * cr + ei
        carry_re[:, cols] = cr
        carry_im[:, cols] = ci
        yield

        hr = hc_re[:, cols]
        hi = hc_im[:, cols]
        for tau in range(CHUNK):
            rws = slice(tau * SUBLANES, (tau + 1) * SUBLANES)
            hr, hi = ar * hr - ai * hi + hre[rws, cols], ar * hi + ai * hr + him[rws, cols]
            hre[rws, cols] = hr
            him[rws, cols] = hi
            if tau % REC_PIECE == REC_PIECE - 1:
                yield

        y_scr[:, ch] = _output_map(hre[:, cols], him[:, cols], cre_ref[sl], cim_ref[sl])
        yield

    taus = EPI_ROWS // SUBLANES
    for r in range(TM_B // EPI_ROWS):
        rows = slice(r * EPI_ROWS, (r + 1) * EPI_ROWS)
        o = _glu_norm(y_scr[rows, :] + d_ref[...] * up_scr[rows, :], wglu_ref, bglu_ref, gssm_ref)
        for i in range(taus):
            for s in range(nslab):
                pad_scr[s, pl.ds(r * taus + i, SUBLANES, stride=PITCH), :] = (
                    o[i * SUBLANES:(i + 1) * SUBLANES, s * LANES:(s + 1) * LANES])
        yield
    for c in range(SUBLANES):
        for s in range(nslab):
            out_ref[c * CHUNK:(c + 1) * CHUNK, s * LANES:(s + 1) * LANES] = (
                pad_scr[s, c * PITCH:c * PITCH + CHUNK, :].astype(BF16))
    yield


N_SCAN_STAGES = (1 + CHUNK // REC_PIECE + 2 * (STATE_LANES // SLAB)
                 + (STATE_LANES // SLAB) * (2 * (CHUNK // REC_PIECE) + 2) + TM_B // EPI_ROWS + 1)


def _mlp_stages(x_cols, attn, s_ref, wout_ref, gmlp_ref, wup_ref, wdown_ref, gfin_ref, y_ref, hm_scr, act_scr):
    ncol = D_MODEL // Q_COLS
    for n in range(ncol):
        cols = slice(n * Q_COLS, (n + 1) * Q_COLS)
        y_ref[:, cols] = (x_cols(cols)
                          + jnp.dot(attn(), wout_ref[0:ATTN_WIDTH, cols], preferred_element_type=F32)
                          + jnp.dot(s_ref[...], wout_ref[ATTN_WIDTH:, cols], preferred_element_type=F32))
        yield
    hm_scr[...] = _rms(y_ref[...], gmlp_ref[...]).astype(BF16)
    for c in range(D_FF // FF_BLK):
        act = act_scr.at[c % 2]
        for n in range(FF_BLK // Q_COLS):
            cols = slice(c * FF_BLK + n * Q_COLS, c * FF_BLK + (n + 1) * Q_COLS)
            up = jnp.dot(hm_scr[...], wup_ref[:, cols], preferred_element_type=F32)
            act[:, n * Q_COLS:(n + 1) * Q_COLS] = jnp.square(jnp.maximum(up, 0.0)).astype(BF16)
            yield
        for n in range(ncol):
            cols = slice(n * Q_COLS, (n + 1) * Q_COLS)
            y_ref[:, cols] += jnp.dot(act[...], wdown_ref[c * FF_BLK:(c + 1) * FF_BLK, cols],
                                      preferred_element_type=F32)
            yield
    y_ref[...] = _rms(y_ref[...], gfin_ref[...])
    yield


N_MLP_STAGES = D_MODEL // Q_COLS + (D_FF // FF_BLK) * (FF_BLK // Q_COLS + D_MODEL // Q_COLS) + 1


def _even_merge(n_a, n_b):
    keys = [((k + 0.5) / n_a, "a") for k in range(n_a)] + [((k + 0.5) / n_b, "b") for k in range(n_b)]
    return [name for _, name in sorted(keys)]


def _scan_mlp_kernel(x_ref, a_ref, u_ref, xdec_ref, adec_ref, sdec_ref,
                     lr_ref, li_ref, lcr_ref, lci_ref, bbr_ref, bbi_ref, cre_in, cim_in,
                     d_ref, wglu_ref, bglu_ref, gssm_ref, wout_ref, gmlp_ref, wup_ref, wdown_ref, gfin_ref,
                     y_ref, ydec_ref, hre_out, him_out,
                     ssm_scr, hm_scr, act_scr, bre_ref, bim_ref, cre_ref, cim_ref,
                     pad_scr, up_scr, y_scr, hre, him, carry_re, carry_im, e_re, e_im, hc_re, hc_im, *, nt):
    i = pl.program_id(0)
    dec = i == 0

    @pl.when(dec)
    def _():
        _embed_blocks(bbr_ref, bbi_ref, cre_in, cim_in, bre_ref, bim_ref, cre_ref, cim_ref)
        ssm_scr[...] = sdec_ref[...]

    @pl.when(i % nt == 0)
    def _():
        carry_re[...] = jnp.zeros_like(carry_re)
        carry_im[...] = jnp.zeros_like(carry_im)

    mlp = _mlp_stages(lambda cols: jnp.where(dec, xdec_ref[:, cols], x_ref[:, cols]),
                      lambda: jnp.where(dec, adec_ref[...], a_ref[...]),
                      ssm_scr, wout_ref, gmlp_ref, wup_ref, wdown_ref, gfin_ref, y_ref, hm_scr, act_scr)
    scan = _scan_stages(u_ref, lr_ref, li_ref, lcr_ref, lci_ref, d_ref, wglu_ref, bglu_ref, gssm_ref, ssm_scr,
                        bre_ref, bim_ref, cre_ref, cim_ref,
                        pad_scr, up_scr, y_scr, hre, him, carry_re, carry_im, e_re, e_im, hc_re, hc_im)
    for stage in _even_merge(N_MLP_STAGES, N_SCAN_STAGES):
        next(mlp if stage == "a" else scan)
    assert next(scan, None) is None and next(mlp, None) is None

    @pl.when(dec)
    def _():
        ydec_ref[...] = y_ref[...]

    @pl.when(i % nt == nt - 1)
    def _():
        hre_out[...] = carry_re[...]
        him_out[...] = carry_im[...]


def _scan_mlp(x, attn_n, u, x_dec, attn_dec, ssm_dec, lam, lam_chunk, bbar, c_gcp, d, w_glu, b_glu, g_ssm,
              w_out, g_mlp, w_up, w_down, g_fin):
    bsz, seq, _ = x.shape
    nt = seq // TM_B
    n = bsz * nt
    assert x_dec.shape[0] == TM_B, "the decode rows ride through the MLP as one tile"
    flat = lambda v: v.reshape(n * TM_B, v.shape[-1])
    cur = lambda width: pl.BlockSpec((TM_B, width), lambda i: (jnp.minimum(i, n - 1), 0))
    prev = lambda width: pl.BlockSpec((TM_B, width), lambda i: (jnp.maximum(i - 1, 0), 0))
    state = pl.BlockSpec((None, 1, STATE_LANES), lambda i: (jnp.minimum(i, n - 1) // nt, 0, 0))
    vec = _const_spec((1, STATE_LANES))
    chan = _const_spec((1, SSM_WIDTH))
    model = _const_spec((1, D_MODEL))
    dec = lambda width: _const_spec((TM_B, width))
    y, y_dec, h_re, h_im = pl.pallas_call(
        functools.partial(_scan_mlp_kernel, nt=nt),
        out_shape=(jax.ShapeDtypeStruct((n * TM_B, D_MODEL), F32),
                   jax.ShapeDtypeStruct((TM_B, D_MODEL), F32),
                   jax.ShapeDtypeStruct((bsz, 1, STATE_LANES), F32),
                   jax.ShapeDtypeStruct((bsz, 1, STATE_LANES), F32)),
        grid=(n + 1,),
        in_specs=[prev(D_MODEL), prev(ATTN_WIDTH), cur(SSM_WIDTH), dec(D_MODEL), dec(ATTN_WIDTH), dec(SSM_WIDTH),
                  vec, vec, vec, vec,
                  _group_blocks(), _group_blocks(), _group_blocks(), _group_blocks(),
                  chan, _const_spec((SSM_WIDTH, SSM_WIDTH)), chan, chan,
                  _const_spec((D_MODEL, D_MODEL)), model, _const_spec((D_MODEL, D_FF)),
                  _const_spec((D_FF, D_MODEL)), model],
        out_specs=(prev(D_MODEL), pl.BlockSpec((TM_B, D_MODEL), lambda i: (0, 0)), state, state),
        scratch_shapes=[pltpu.VMEM((TM_B, SSM_WIDTH), BF16),
                        pltpu.VMEM((TM_B, D_MODEL), BF16),
                        pltpu.VMEM((2, TM_B, FF_BLK), BF16)] + _block_operand_scratch() + [
                        pltpu.VMEM((SSM_WIDTH // LANES, SUBLANES * PITCH, LANES), F32),
                        pltpu.VMEM((TM_B, SSM_WIDTH), F32),
                        pltpu.VMEM((TM_B, SSM_WIDTH), F32),
                        pltpu.VMEM((TM_B, STATE_LANES), F32),
                        pltpu.VMEM((TM_B, STATE_LANES), F32),
                        pltpu.VMEM((1, STATE_LANES), F32),
                        pltpu.VMEM((1, STATE_LANES), F32),
                        pltpu.VMEM((SUBLANES, STATE_LANES), F32),
                        pltpu.VMEM((SUBLANES, STATE_LANES), F32),
                        pltpu.VMEM((SUBLANES, STATE_LANES), F32),
                        pltpu.VMEM((SUBLANES, STATE_LANES), F32)],
        compiler_params=pltpu.CompilerParams(
            dimension_semantics=("arbitrary",), vmem_limit_bytes=VMEM_LIMIT),
        name="scan_mlp",
    )(flat(x), flat(attn_n), flat(u), x_dec, attn_dec, ssm_dec, lam[0], lam[1], lam_chunk[0], lam_chunk[1],
      bbar[0], bbar[1], c_gcp[0], c_gcp[1], d, w_glu, b_glu, g_ssm, w_out, g_mlp, w_up, w_down, g_fin)
    return y.reshape(bsz, seq, D_MODEL), y_dec, h_re, h_im


def _block_operand_scratch():
    return [pltpu.VMEM((STATE_LANES // SLAB, SLAB_CH, SLAB), BF16) for _ in range(4)]


def _group_blocks():
    return _const_spec((N_SSM_GROUPS, SSM_GROUP, SSM_STATE))


def _s_scan_kernel(u_ref, h0r_ref, h0i_ref, lr_ref, li_ref, bbr_ref, bbi_ref, cre_in, cim_in,
                   d_ref, wglu_ref, bglu_ref, gssm_ref,
                   out_ref, hre_out, him_out,
                   bre_ref, bim_ref, cre_ref, cim_ref,
                   pad_scr, up_scr, hre, him):
    nseq = h0r_ref.shape[0]
    steps = u_ref.shape[0] // nseq
    nslab = SSM_WIDTH // LANES
    _embed_blocks(bbr_ref, bbi_ref, cre_in, cim_in, bre_ref, bim_ref, cre_ref, cim_ref)

    for s in range(nslab):
        pad_scr[s] = u_ref[:, s * LANES:(s + 1) * LANES]
    for i in range(steps):
        for s in range(nslab):
            up_scr[i * nseq:(i + 1) * nseq, s * LANES:(s + 1) * LANES] = (
                pad_scr[s, pl.ds(i, nseq, stride=steps), :])
    up = up_scr[...]
    upb = up.astype(BF16)

    ys = []
    for sl in range(STATE_LANES // SLAB):
        cols = slice(sl * SLAB, (sl + 1) * SLAB)
        ch = slice(sl * SLAB_CH, (sl + 1) * SLAB_CH)
        hre[:, cols] = jnp.dot(upb[:, ch], bre_ref[sl], preferred_element_type=F32)
        him[:, cols] = jnp.dot(upb[:, ch], bim_ref[sl], preferred_element_type=F32)
        ar = lr_ref[:, cols]
        ai = li_ref[:, cols]
        hr = h0r_ref[:, cols]
        hi = h0i_ref[:, cols]
        for i in range(steps):
            rws = slice(i * nseq, (i + 1) * nseq)
            hr, hi = ar * hr - ai * hi + hre[rws, cols], ar * hi + ai * hr + him[rws, cols]
            hre[rws, cols] = hr
            him[rws, cols] = hi
        hre_out[:, cols] = hr
        him_out[:, cols] = hi
        ys.append(_output_map(hre[:, cols], him[:, cols], cre_ref[sl], cim_ref[sl]))

    y = jnp.concatenate(ys, axis=-1) + d_ref[...] * up
    o = _glu_norm(y, wglu_ref, bglu_ref, gssm_ref)

    for i in range(steps):
        for s in range(nslab):
            pad_scr[s, pl.ds(i, nseq, stride=steps), :] = o[i * nseq:(i + 1) * nseq, s * LANES:(s + 1) * LANES]
    for s in range(nslab):
        out_ref[:, s * LANES:(s + 1) * LANES] = pad_scr[s].astype(BF16)


def _s_scan(u, h0_re, h0_im, lam, bbar, c_gcp, d, w_glu, b_glu, g_ssm):
    rows = u.shape[0]
    nseq = h0_re.shape[0]
    return pl.pallas_call(
        _s_scan_kernel,
        out_shape=(jax.ShapeDtypeStruct((rows, SSM_WIDTH), BF16),
                   jax.ShapeDtypeStruct((nseq, STATE_LANES), F32),
                   jax.ShapeDtypeStruct((nseq, STATE_LANES), F32)),
        scratch_shapes=_block_operand_scratch() + [
                        pltpu.VMEM((SSM_WIDTH // LANES, rows, LANES), F32),
                        pltpu.VMEM((rows, SSM_WIDTH), F32),
                        pltpu.VMEM((rows, STATE_LANES), F32),
                        pltpu.VMEM((rows, STATE_LANES), F32)],
        compiler_params=pltpu.CompilerParams(vmem_limit_bytes=VMEM_LIMIT),
        name="s_scan",
    )(u, h0_re, h0_im, lam[0], lam[1], bbar[0], bbar[1], c_gcp[0], c_gcp[1], d, w_glu, b_glu, g_ssm)


def _pair_heads(w, axis):
    shape = w.shape
    split = shape[:axis] + (N_KV_HEADS, Q_PER_KV, HEAD_DIM) + shape[axis + 1:]
    return jnp.swapaxes(w.reshape(split), axis, axis + 1).reshape(shape)


def kernel(x_prompt, x_sample, cache_k, cache_v, state_ssm_re, state_ssm_im, rel_bias, norm_mix, w_in, attn_sinks, ssm_a_re, ssm_a_im, ssm_log_step, ssm_b_re, ssm_b_im, ssm_c_re, ssm_c_im, ssm_d, w_glu, b_glu, norm_attn_out, norm_ssm_out, w_out, norm_mlp, w_up, w_down, norm_final):
    depth = w_in.shape[0]
    bsz, seq, _ = x_prompt.shape
    nseq, steps, _ = x_sample.shape
    bucket = jnp.asarray(_rel_bucket_table())
    g_fin = norm_final.reshape(1, D_MODEL)
    assert depth == 1, "the final norm is fused into the (single) layer's MLP kernel"

    xp, xs = x_prompt, x_sample
    outs = [[] for _ in range(8)]
    for l in range(depth):
        lr, li, lcr, lci, bbr, bbi = _ssm_prep(ssm_a_re[l], ssm_a_im[l], ssm_log_step[l], ssm_b_re[l], ssm_b_im[l])
        lam = (lr, li)
        lam_chunk = (lcr, lci)
        bbar = (bbr, bbi)
        c_gcp = (ssm_c_re[l], ssm_c_im[l])
        g_mix = norm_mix[l].reshape(1, D_MODEL)
        g_attn = _pair_heads(norm_attn_out[l].reshape(1, ATTN_WIDTH), axis=1)
        g_ssm = norm_ssm_out[l].reshape(1, SSM_WIDTH)
        g_mlp = norm_mlp[l].reshape(1, D_MODEL)
        sinks = attn_sinks[l].reshape(1, N_Q_HEADS)
        d = ssm_d[l].reshape(1, SSM_WIDTH)
        bg = b_glu[l].reshape(1, SSM_WIDTH)
        w_in_b = jnp.concatenate([_pair_heads(w_in[l][:, :ATTN_WIDTH], axis=1), w_in[l][:, ATTN_WIDTH:]],
                                 axis=1).astype(BF16)

        attn_n, u, k1, v1, w_out_b, w_glu_b, w_up_b, w_down_b = _in_attn(
            xp, rel_bias, sinks, g_mix, w_in_b, bucket.T, g_attn.reshape(ATTN_WIDTH, 1),
            w_out[l], w_glu[l], w_up[l], w_down[l])

        ck = cache_k[l].reshape(nseq, WINDOW, KV_WIDTH)
        cv = cache_v[l].reshape(nseq, WINDOW, KV_WIDTH)
        s_attn, s_u, k2, v2 = _s_in_attn(xs, rel_bias, sinks, g_mix, w_in_b, bucket, g_attn, ck, cv)
        s_ssm, r2, i2 = _s_scan(s_u, state_ssm_re[l].reshape(nseq, STATE_LANES),
                                state_ssm_im[l].reshape(nseq, STATE_LANES),
                                lam, bbar, c_gcp, d, w_glu_b, bg, g_ssm)

        xp, xs, r1, i1 = _scan_mlp(xp, attn_n, u, xs.reshape(nseq * steps, D_MODEL), s_attn, s_ssm,
                                   lam, lam_chunk, bbar, c_gcp, d, w_glu_b, bg, g_ssm,
                                   w_out_b, g_mlp, w_up_b, w_down_b, g_fin)
        xs = xs.reshape(nseq, steps, D_MODEL)

        kv_shape = (WINDOW, N_KV_HEADS, HEAD_DIM)
        st_shape = (N_SSM_GROUPS, SSM_STATE)
        for lst, val in zip(outs, (k1.reshape(bsz, *kv_shape), v1.reshape(bsz, *kv_shape),
                                   r1.reshape(bsz, *st_shape), i1.reshape(bsz, *st_shape),
                                   k2.reshape(nseq, *kv_shape), v2.reshape(nseq, *kv_shape),
                                   r2.reshape(nseq, *st_shape), i2.reshape(nseq, *st_shape))):
            lst.append(val)

    return (xp, xs) + tuple(jnp.stack(o) for o in outs)
```
